```python
import math
import jax
import jax.numpy as jnp
from jax import lax
import numpy as np

D_MODEL = 1024
BATCH = 4
SEQ = 4096
DEPTH = 2

RWKV_HEADS = 8
RWKV_HEAD_DIM = 64
RWKV_WIDTH = RWKV_HEADS * RWKV_HEAD_DIM
DECAY_LORA = 64
ICLR_LORA = 64
VRES_LORA = 32
GATE_LORA = 128
RWKV_GN_EPS = 64e-5
MLA_HEADS = 8
QK_NOPE_DIM = 64
QK_ROPE_DIM = 32
MLA_QK_DIM = QK_NOPE_DIM + QK_ROPE_DIM
MLA_V_DIM = 64
MLA_WIDTH = MLA_HEADS * MLA_V_DIM
Q_LORA_RANK = 256
KV_LORA_RANK = 128
ROPE_THETA = 10000.0
Q_BLOCK = 128
GDN_HEADS = 8
GDN_K_DIM = 64
GDN_V_DIM = 64
GDN_WIDTH = GDN_HEADS * GDN_V_DIM
GDN_QKV = 2 * GDN_HEADS * GDN_K_DIM + GDN_HEADS * GDN_V_DIM
GDN_CONV = 4
GDN_CHUNK = 64
N_BRANCH = 3
FF_DENSE = 2816
N_EXPERTS = 8
TOP_K = 2
FF_EXPERT = 3584
MOE_BLOCK = 128
PLE_DIM = 256
NORM_EPS = 1e-6

RWKV_IN_SIZES = (RWKV_WIDTH, RWKV_WIDTH, RWKV_WIDTH, DECAY_LORA, ICLR_LORA, GATE_LORA)
MLA_IN_SIZES = (Q_LORA_RANK, KV_LORA_RANK, QK_ROPE_DIM)
GDN_IN_SIZES = (GDN_QKV, GDN_HEADS, GDN_HEADS, GDN_WIDTH)
RWKV_IN = sum(RWKV_IN_SIZES)
MLA_IN = sum(MLA_IN_SIZES)
GDN_IN = sum(GDN_IN_SIZES)
GATE_IN = N_BRANCH * D_MODEL
N_IN = RWKV_IN + MLA_IN + GDN_IN + GATE_IN

kernel_name = 'hybrid_rwkv7_mla_gdn_moe_block'


def rms_norm(x, g, eps=NORM_EPS):
    xf = x.astype(jnp.float32)
    y = xf * lax.rsqrt(jnp.mean(jnp.square(xf), axis=-1, keepdims=True) + eps)
    return (y * g.astype(jnp.float32)).astype(x.dtype)


def l2_normalize(x, eps=1e-12):
    xf = x.astype(jnp.float32)
    return xf * lax.rsqrt(jnp.sum(jnp.square(xf), axis=-1, keepdims=True) + eps)


def split_cols(z, sizes):
    return jnp.split(z, [int(c) for c in np.cumsum(sizes)[:-1]], axis=-1)


def heads(t, n):
    return t.reshape(t.shape[:-1] + (n, t.shape[-1] // n))


def token_shift(z):
    return jnp.pad(z, ((0, 0), (1, 0), (0, 0)))[:, :-1]


def rope_tables(positions):
    inv_freq = 1.0 / (ROPE_THETA ** (jnp.arange(0, QK_ROPE_DIM, 2, dtype=jnp.float32) / QK_ROPE_DIM))
    ang = positions.astype(jnp.float32)[..., None] * inv_freq
    return jnp.cos(ang)[:, :, None, :], jnp.sin(ang)[:, :, None, :]


def apply_rope(x, cos, sin):
    x1, x2 = jnp.split(x.astype(jnp.float32), 2, axis=-1)
    return jnp.concatenate([x1 * cos - x2 * sin, x2 * cos + x1 * sin], axis=-1).astype(x.dtype)


def rwkv7_recurrence(r, w, k, v, a, b):
    bsz, _, h, n = r.shape

    def step(state, xs):
        r_t, w_t, k_t, v_t, a_t, b_t = xs
        sa = jnp.einsum('bhij,bhj->bhi', state, a_t)
        state = (state * w_t[:, :, None, :] + sa[..., None] * b_t[:, :, None, :]
                 + v_t[..., None] * k_t[:, :, None, :])
        return state, jnp.einsum('bhij,bhj->bhi', state, r_t)

    xs = tuple(jnp.moveaxis(t, 1, 0) for t in (r, w, k, v, a, b))
    _, y = lax.scan(step, jnp.zeros((bsz, h, n, n), jnp.float32), xs)
    return jnp.moveaxis(y, 0, 1)


def rwkv7_branch(z, u, v_first, vres, mu, w0, w_up, a0, a_up, g_up, k_k, k_a, r_k, ln_g, ln_b):
    b, s, _ = z.shape
    z = z + (token_shift(z) - z) * mu
    r, k, v, w_lo, a_lo, g_lo = split_cols(z, RWKV_IN_SIZES)
    log_w = -jax.nn.softplus(-(w0 + jnp.tanh(w_lo) @ w_up)) - 0.5
    decay = jnp.exp(-jnp.exp(log_w.astype(jnp.float32)))
    iclr = jax.nn.sigmoid(a0 + a_lo @ a_up)
    gate = jax.nn.sigmoid(g_lo) @ g_up
    if vres is None:
        v_first = v
    else:
        v_mu, v_down, v_up, v_bias = vres
        xv = u + (token_shift(u) - u) * v_mu
        v = v + (v_first - v) * jax.nn.sigmoid(v_bias + (xv @ v_down) @ v_up)
    kk = l2_normalize(heads(k * k_k, RWKV_HEADS))
    k = k * (1.0 + (iclr - 1.0) * k_a)
    rh, kh, vh, ah, wh = (heads(t, RWKV_HEADS).astype(jnp.float32) for t in (r, k, v, iclr, decay))
    y = rwkv7_recurrence(rh, wh, kh, vh, -kk, kk * ah)
    mean = jnp.mean(y, axis=-1, keepdims=True)
    var = jnp.mean(jnp.square(y - mean), axis=-1, keepdims=True)
    y = ((y - mean) * lax.rsqrt(var + RWKV_GN_EPS)).reshape(b, s, RWKV_WIDTH) * ln_g + ln_b
    bonus = jnp.sum(rh * kh * r_k, axis=-1, keepdims=True) * vh
    y = y + bonus.reshape(b, s, RWKV_WIDTH)
    return (y * gate).astype(z.dtype), v_first


def causal_block_attention(q, k, v, scale):
    seq = q.shape[1]
    outs = []
    for start in range(0, seq, Q_BLOCK):
        stop = start + Q_BLOCK
        s = jnp.einsum('bqhd,bkhd->bhqk', q[:, start:stop], k[:, :stop]).astype(jnp.float32) * scale
        mask = (start + jnp.arange(Q_BLOCK))[:, None] >= jnp.arange(stop)[None, :]
        prob = jax.nn.softmax(jnp.where(mask, s, -jnp.inf), axis=-1)
        outs.append(jnp.einsum('bhqk,bkhd->bqhd', prob.astype(v.dtype), v[:, :stop]))
    return jnp.concatenate(outs, axis=1)


def mla_branch(z, cos, sin, q_norm_g, kv_norm_g, w_uq, w_ukv, qk_g_q, qk_g_k):
    b, s, _ = z.shape
    c_q, c_kv, k_pe = split_cols(z, MLA_IN_SIZES)
    q = heads(rms_norm(c_q, q_norm_g) @ w_uq, MLA_HEADS)
    kv = heads(rms_norm(c_kv, kv_norm_g) @ w_ukv, MLA_HEADS)
    k_nope, v = kv[..., :QK_NOPE_DIM], kv[..., QK_NOPE_DIM:]
    k_pe = jnp.broadcast_to(k_pe[:, :, None, :], k_nope.shape[:-1] + (QK_ROPE_DIM,))
    k = jnp.concatenate([k_nope, k_pe], axis=-1)
    q = rms_norm(q, qk_g_q)
    k = rms_norm(k, qk_g_k)
    q = jnp.concatenate([q[..., :QK_NOPE_DIM], apply_rope(q[..., QK_NOPE_DIM:], cos, sin)], axis=-1)
    k = jnp.concatenate([k[..., :QK_NOPE_DIM], apply_rope(k[..., QK_NOPE_DIM:], cos, sin)], axis=-1)
    o = causal_block_attention(q, k, v, MLA_QK_DIM ** -0.5)
    return o.reshape(b, s, MLA_WIDTH)


def causal_depthwise_conv(x, w):
    xp = jnp.pad(x, ((0, 0), (GDN_CONV - 1, 0), (0, 0)))
    return lax.conv_general_dilated(xp, w[:, None, :].astype(x.dtype), (1,), 'VALID',
                                    dimension_numbers=('NWC', 'WIO', 'NWC'),
                                    feature_group_count=x.shape[-1])


def chunk_gated_delta_rule(q, k, v, g, beta):
    b, s, h, dk = q.shape
    dv = v.shape[-1]
    c = GDN_CHUNK
    n = s // c

    def chunks(t):
        return jnp.moveaxis(t.reshape((b, n, c, h) + t.shape[3:]), 3, 2)

    q, k, v, g, beta = (chunks(t) for t in (q, k, v, g, beta))
    gc = jnp.cumsum(g, axis=-1)
    causal = jnp.tril(jnp.ones((c, c), dtype=bool))
    strict = jnp.tril(jnp.ones((c, c), dtype=bool), -1)
    gamma = jnp.exp(jnp.where(causal, gc[..., :, None] - gc[..., None, :], -jnp.inf))
    kb = k * beta[..., None]
    m = jnp.where(strict, jnp.einsum('bnhid,bnhjd->bnhij', kb, k) * gamma, 0.0) + jnp.eye(c, dtype=jnp.float32)
    rhs = jnp.concatenate([v * beta[..., None], kb * jnp.exp(gc)[..., None]], axis=-1)
    sol = lax.linalg.triangular_solve(m, rhs, left_side=True, lower=True, unit_diagonal=True)
    u_c, w_c = sol[..., :dv], sol[..., dv:]
    a_in = jnp.einsum('bnhid,bnhjd->bnhij', q, k) * gamma
    q_dec = q * jnp.exp(gc)[..., None]
    k_dec = k * jnp.exp(gc[..., -1:] - gc)[..., None]
    chunk_dec = jnp.exp(gc[..., -1])

    def step(state, xs):
        u_i, w_i, a_i, qd_i, kd_i, d_i = xs
        v_new = u_i - jnp.einsum('bhck,bhkv->bhcv', w_i, state)
        o_i = jnp.einsum('bhck,bhkv->bhcv', qd_i, state) + jnp.einsum('bhcj,bhjv->bhcv', a_i, v_new)
        state = state * d_i[..., None, None] + jnp.einsum('bhck,bhcv->bhkv', kd_i, v_new)
        return state, o_i

    xs = tuple(jnp.moveaxis(t, 1, 0) for t in (u_c, w_c, a_in, q_dec, k_dec, chunk_dec))
    _, o = lax.scan(step, jnp.zeros((b, h, dk, dv), jnp.float32), xs)
    return jnp.moveaxis(o, 0, 1).swapaxes(2, 3).reshape(b, s, h, dv)


def gdn_branch(z, conv_w, a_log, dt_bias, norm_g):
    b, s, _ = z.shape
    qkv, b_logit, a_logit, gate = split_cols(z, GDN_IN_SIZES)
    qkv = jax.nn.silu(causal_depthwise_conv(qkv, conv_w))
    q, k, v = split_cols(qkv, (GDN_HEADS * GDN_K_DIM, GDN_HEADS * GDN_K_DIM, GDN_HEADS * GDN_V_DIM))
    q = l2_normalize(heads(q, GDN_HEADS)) * (GDN_K_DIM ** -0.5)
    k = l2_normalize(heads(k, GDN_HEADS))
    v = heads(v, GDN_HEADS).astype(jnp.float32)
    beta = jax.nn.sigmoid(b_logit.astype(jnp.float32))
    g = -jnp.exp(a_log.astype(jnp.float32)) * jax.nn.softplus(a_logit.astype(jnp.float32) + dt_bias)
    o = chunk_gated_delta_rule(q, k, v, g, beta)
    o = rms_norm(o, norm_g) * jax.nn.silu(heads(gate, GDN_HEADS).astype(jnp.float32))
    return o.reshape(b, s, GDN_WIDTH).astype(z.dtype)


def swiglu(x, w_gate, w_up, w_down):
    return (jax.nn.silu(x @ w_gate) * (x @ w_up)) @ w_down


def moe_swiglu(x2d, w_router, w_gate, w_up, w_down):
    n_tok, d = x2d.shape
    n_assign = n_tok * TOP_K
    n_blocks = -(-(n_assign + N_EXPERTS * (MOE_BLOCK - 1)) // MOE_BLOCK)
    logits = jnp.dot(x2d, w_router).astype(jnp.float32)
    top_logits, top_e = lax.top_k(logits, TOP_K)
    top_w = jax.nn.softmax(top_logits, axis=-1).reshape(-1)
    flat_e = top_e.reshape(-1)
    order = jnp.argsort(flat_e)
    e_sorted = flat_e[order]
    tok_sorted = order // TOP_K
    w_sorted = top_w[order]
    counts = jnp.bincount(flat_e, length=N_EXPERTS)
    padded = (counts + MOE_BLOCK - 1) // MOE_BLOCK * MOE_BLOCK
    pad_end = jnp.cumsum(padded)
    rank = jnp.arange(n_assign) - (jnp.cumsum(counts) - counts)[e_sorted]
    dest = (pad_end - padded)[e_sorted] + rank
    rows = jnp.zeros((n_blocks * MOE_BLOCK, d), x2d.dtype).at[dest].set(x2d[tok_sorted])
    block_e = jnp.minimum(jnp.searchsorted(pad_end, jnp.arange(n_blocks) * MOE_BLOCK, side='right'), N_EXPERTS - 1)

    def expert_block(args):
        xb, e = args
        return (jax.nn.silu(xb @ w_gate[e]) * (xb @ w_up[e])) @ w_down[e]

    y_rows = lax.map(expert_block, (rows.reshape(n_blocks, MOE_BLOCK, d), block_e)).reshape(n_blocks * MOE_BLOCK, d)
    y = jax.ops.segment_sum(y_rows[dest] * w_sorted[:, None], tok_sorted, num_segments=n_tok)
    return y.astype(x2d.dtype)


def setup_inputs(seed: int = 0) -> dict:
    key = jax.random.key(seed)
    ks = iter(jax.random.split(key, 64))
    f32 = jnp.float32
    L = DEPTH
    LD = (DEPTH + 1) // 2
    LM = DEPTH // 2
    LV = DEPTH - 1

    def nrm(shape, scale):
        return jax.random.normal(next(ks), shape, f32) * scale

    def unif(shape, lo, hi):
        return jax.random.uniform(next(ks), shape, f32, lo, hi)

    def gain(shape):
        return 1.0 + nrm(shape, 0.02)

    x = nrm((BATCH, SEQ, D_MODEL), 1.0)
    p = nrm((DEPTH, BATCH, SEQ, PLE_DIM), 1.0)
    positions = (jnp.arange(SEQ, dtype=jnp.int32)[None, :]
                 + jax.random.randint(next(ks), (BATCH, 1), 0, 1024, dtype=jnp.int32))
    dt = jnp.exp(unif((L, GDN_HEADS), math.log(1e-3), math.log(1e-1)))
    gdn_dt_bias = dt + jnp.log(-jnp.expm1(-dt))
    return {
        'x': x,
        'p': p,
        'positions': positions,
        'norm_mix_g': gain((L, D_MODEL)),
        'w_in': nrm((L, D_MODEL, N_IN), D_MODEL ** -0.5),
        'rwkv_mu': unif((L, RWKV_IN), 0.0, 1.0),
        'rwkv_w0': unif((L, RWKV_WIDTH), -6.0, 1.0),
        'rwkv_w_up': nrm((L, DECAY_LORA, RWKV_WIDTH), DECAY_LORA ** -0.5),
        'rwkv_a0': unif((L, RWKV_WIDTH), -1.0, 1.0),
        'rwkv_a_up': nrm((L, ICLR_LORA, RWKV_WIDTH), ICLR_LORA ** -0.5),
        'rwkv_g_up': nrm((L, GATE_LORA, RWKV_WIDTH), GATE_LORA ** -0.5),
        'rwkv_k_k': 0.85 + nrm((L, RWKV_WIDTH), 0.05),
        'rwkv_k_a': 1.0 + nrm((L, RWKV_WIDTH), 0.05),
        'rwkv_r_k': nrm((L, RWKV_HEADS, RWKV_HEAD_DIM), 0.1),
        'rwkv_ln_g': gain((L, RWKV_WIDTH)),
        'rwkv_ln_b': nrm((L, RWKV_WIDTH), 0.02),
        'vres_mu': unif((LV, D_MODEL), 0.0, 1.0),
        'vres_down': nrm((LV, D_MODEL, VRES_LORA), D_MODEL ** -0.5),
        'vres_up': nrm((LV, VRES_LORA, RWKV_WIDTH), VRES_LORA ** -0.5),
        'vres_b': 1.0 + nrm((LV, RWKV_WIDTH), 0.1),
        'mla_q_norm_g': gain((L, Q_LORA_RANK)),
        'mla_kv_norm_g': gain((L, KV_LORA_RANK)),
        'mla_w_uq': nrm((L, Q_LORA_RANK, MLA_HEADS * MLA_QK_DIM), Q_LORA_RANK ** -0.5),
        'mla_w_ukv': nrm((L, KV_LORA_RANK, MLA_HEADS * (QK_NOPE_DIM + MLA_V_DIM)), KV_LORA_RANK ** -0.5),
        'mla_qk_norm_q': gain((L, MLA_QK_DIM)),
        'mla_qk_norm_k': gain((L, MLA_QK_DIM)),
        'gdn_conv_w': nrm((L, GDN_CONV, GDN_QKV), GDN_CONV ** -0.5),
        'gdn_a_log': jnp.log(unif((L, GDN_HEADS), 1.0, 16.0)),
        'gdn_dt_bias': gdn_dt_bias,
        'gdn_norm_g': gain((L, GDN_V_DIM)),
        'w_br_rwkv': nrm((L, RWKV_WIDTH, D_MODEL), RWKV_WIDTH ** -0.5),
        'w_br_mla': nrm((L, MLA_WIDTH, D_MODEL), MLA_WIDTH ** -0.5),
        'w_br_gdn': nrm((L, GDN_WIDTH, D_MODEL), GDN_WIDTH ** -0.5),
        'w_out': nrm((L, D_MODEL, D_MODEL), D_MODEL ** -0.5),
        'norm_ffn_g': gain((L, D_MODEL)),
        'ffn_wg': nrm((LD, D_MODEL, FF_DENSE), D_MODEL ** -0.5),
        'ffn_wu': nrm((LD, D_MODEL, FF_DENSE), D_MODEL ** -0.5),
        'ffn_wd': nrm((LD, FF_DENSE, D_MODEL), FF_DENSE ** -0.5),
        'moe_router': nrm((LM, D_MODEL, N_EXPERTS), D_MODEL ** -0.5),
        'moe_wg': nrm((LM, N_EXPERTS, D_MODEL, FF_EXPERT), D_MODEL ** -0.5),
        'moe_wu': nrm((LM, N_EXPERTS, D_MODEL, FF_EXPERT), D_MODEL ** -0.5),
        'moe_wd': nrm((LM, N_EXPERTS, FF_EXPERT, D_MODEL), FF_EXPERT ** -0.5),
        'ple_proj': nrm((L, PLE_DIM, D_MODEL), PLE_DIM ** -0.5),
        'ple_gate': nrm((L, D_MODEL, D_MODEL), D_MODEL ** -0.5),
        'ple_norm_g': gain((L, D_MODEL)),
    }


def reference(x, p, positions, norm_mix_g, w_in, rwkv_mu, rwkv_w0, rwkv_w_up, rwkv_a0, rwkv_a_up,
              rwkv_g_up, rwkv_k_k, rwkv_k_a, rwkv_r_k, rwkv_ln_g, rwkv_ln_b, vres_mu, vres_down,
              vres_up, vres_b, mla_q_norm_g, mla_kv_norm_g, mla_w_uq, mla_w_ukv, mla_qk_norm_q,
              mla_qk_norm_k, gdn_conv_w, gdn_a_log, gdn_dt_bias, gdn_norm_g, w_br_rwkv, w_br_mla,
              w_br_gdn, w_out, norm_ffn_g, ffn_wg, ffn_wu, ffn_wd, moe_router, moe_wg, moe_wu, moe_wd,
              ple_proj, ple_gate, ple_norm_g):
    b, s, d = x.shape
    cos, sin = rope_tables(positions)
    h = x
    v_first = None
    for i in range(DEPTH):
        u = rms_norm(h, norm_mix_g[i])
        z = u @ w_in[i]
        z_rwkv, z_mla, z_gdn, z_gate = split_cols(z, (RWKV_IN, MLA_IN, GDN_IN, GATE_IN))
        vres = None if i == 0 else (vres_mu[i - 1], vres_down[i - 1], vres_up[i - 1], vres_b[i - 1])
        o_a, v_first = rwkv7_branch(z_rwkv, u, v_first, vres, rwkv_mu[i], rwkv_w0[i], rwkv_w_up[i],
                                    rwkv_a0[i], rwkv_a_up[i], rwkv_g_up[i], rwkv_k_k[i], rwkv_k_a[i],
                                    rwkv_r_k[i], rwkv_ln_g[i], rwkv_ln_b[i])
        o_b = mla_branch(z_mla, cos, sin, mla_q_norm_g[i], mla_kv_norm_g[i], mla_w_uq[i], mla_w_ukv[i],
                         mla_qk_norm_q[i], mla_qk_norm_k[i])
        o_c = gdn_branch(z_gdn, gdn_conv_w[i], gdn_a_log[i], gdn_dt_bias[i], gdn_norm_g[i])
        g_a, g_b, g_c = jnp.split(jax.nn.sigmoid(z_gate), N_BRANCH, axis=-1)
        merged = (g_a * (o_a @ w_br_rwkv[i]) + g_b * (o_b @ w_br_mla[i])
                  + g_c * (o_c @ w_br_gdn[i]))
        h = h + merged @ w_out[i]
        u2 = rms_norm(h, norm_ffn_g[i])
        if i % 2 == 0:
            f = swiglu(u2, ffn_wg[i // 2], ffn_wu[i // 2], ffn_wd[i // 2])
        else:
            f = moe_swiglu(u2.reshape(b * s, d), moe_router[i // 2], moe_wg[i // 2],
                           moe_wu[i // 2], moe_wd[i // 2]).reshape(b, s, d)
        h = h + f
        e = rms_norm(p[i] @ ple_proj[i], ple_norm_g[i])
        h = h + jax.nn.sigmoid(h @ ple_gate[i]) * e
    return h
```

```python
import functools
import math

import jax
import jax.numpy as jnp
import numpy as np
from jax import lax
from jax.experimental import pallas as pl
from jax.experimental.pallas import tpu as pltpu

F32 = jnp.float32
BF16 = jnp.bfloat16
HI = lax.Precision.HIGHEST

NORM_EPS = 1e-6
RWKV_GN_EPS = 64e-5
N_HEADS = 8
HEAD_DIM = 64
WIDTH = N_HEADS * HEAD_DIM
QK_NOPE = 64
QK_ROPE = 32
QK_DIM = QK_NOPE + QK_ROPE
Q_LORA = 256
KV_LORA = 128
ROPE_THETA = 10000.0
GDN_CONV = 4
N_EXPERTS = 8
TOP_K = 2
LANES = 128
CHUNK = 64
VMEM_LIMIT = 48 * 1024 * 1024


def _cparams(sem):
    return pltpu.CompilerParams(dimension_semantics=sem, vmem_limit_bytes=VMEM_LIMIT)


def _mm_kernel(*refs, norm, hi):
    if norm:
        x_ref, g_ref, w_ref, o_ref, xs_ref = refs
    else:
        x_ref, w_ref, o_ref, xs_ref = refs

    @pl.when(pl.program_id(1) == 0)
    def _():
        x = x_ref[...].astype(F32)
        if norm:
            ms = jnp.mean(x * x, axis=-1, keepdims=True)
            x = x * lax.rsqrt(ms + NORM_EPS) * g_ref[...]
        xs_ref[...] = x.astype(xs_ref.dtype)

    o_ref[...] = jnp.dot(xs_ref[...], w_ref[...], preferred_element_type=F32,
                         precision=HI if hi else None).astype(o_ref.dtype)


def _mm(x, w, g=None, *, bm=512, bn=512, hi=False, out_dtype=F32):
    m, k = x.shape
    n = w.shape[1]
    bm = min(bm, m)
    bn = min(bn, -(-n // LANES) * LANES)
    n_pad = -(-n // bn) * bn
    wdt = F32 if hi else BF16
    w = w.astype(wdt)
    if n_pad != n:
        w = jnp.pad(w, ((0, 0), (0, n_pad - n)))
    norm = g is not None
    in_specs = [pl.BlockSpec((bm, k), lambda i, j: (i, 0))]
    args = [x]
    if norm:
        in_specs.append(pl.BlockSpec((1, k), lambda i, j: (0, 0)))
        args.append(g.reshape(1, k).astype(F32))
    in_specs.append(pl.BlockSpec((k, bn), lambda i, j: (0, j)))
    args.append(w)
    out = pl.pallas_call(
        functools.partial(_mm_kernel, norm=norm, hi=hi),
        grid=(m // bm, n_pad // bn),
        in_specs=in_specs,
        out_specs=pl.BlockSpec((bm, bn), lambda i, j: (i, j)),
        out_shape=jax.ShapeDtypeStruct((m, n_pad), out_dtype),
        scratch_shapes=[pltpu.VMEM((bm, k), wdt)],
        compiler_params=_cparams(("arbitrary", "arbitrary")),
        name="mm",
    )(*args)
    return out if n_pad == n else out[:, :n]


def _flash_kernel(qi_ref, kj_ref, q_ref, k_ref, v_ref, o_ref, m_ref, l_ref, acc_ref, *, blk):
    n = pl.program_id(2)
    i = qi_ref[n]
    j = kj_ref[n]

    @pl.when(j == 0)
    def _():
        m_ref[...] = jnp.full_like(m_ref, -jnp.inf)
        l_ref[...] = jnp.zeros_like(l_ref)
        acc_ref[...] = jnp.zeros_like(acc_ref)

    s = lax.dot_general(q_ref[...], k_ref[...], (((1,), (1,)), ((), ())), preferred_element_type=F32)
    row = lax.broadcasted_iota(jnp.int32, (blk, blk), 0) + i * blk
    col = lax.broadcasted_iota(jnp.int32, (blk, blk), 1) + j * blk
    s = jnp.where(row >= col, s, -jnp.inf)
    m_old = m_ref[...]
    m_new = jnp.maximum(m_old, jnp.max(s, axis=-1, keepdims=True))
    alpha = jnp.exp(m_old - m_new)
    p = jnp.exp(s - m_new)
    l_ref[...] = alpha * l_ref[...] + jnp.sum(p, axis=-1, keepdims=True)
    acc_ref[...] = alpha * acc_ref[...] + jnp.dot(p.astype(BF16), v_ref[...], preferred_element_type=F32)
    m_ref[...] = m_new

    @pl.when(j == i)
    def _():
        o_ref[...] = (acc_ref[...] / l_ref[...]).astype(o_ref.dtype)


def _flash_attention(q, k, v, *, blk=512):
    b, h, s, d = q.shape
    blk = min(blk, s)
    nb = s // blk
    qi = np.array([i for i in range(nb) for j in range(i + 1)], np.int32)
    kj = np.array([j for i in range(nb) for j in range(i + 1)], np.int32)
    q_spec = pl.BlockSpec((None, None, blk, d), lambda bb, hh, n, qi_r, kj_r: (bb, hh, qi_r[n], 0))
    kv_spec = pl.BlockSpec((None, None, blk, d), lambda bb, hh, n, qi_r, kj_r: (bb, hh, kj_r[n], 0))
    return pl.pallas_call(
        functools.partial(_flash_kernel, blk=blk),
        grid_spec=pltpu.PrefetchScalarGridSpec(
            num_scalar_prefetch=2,
            grid=(b, h, len(qi)),
            in_specs=[q_spec, kv_spec, kv_spec],
            out_specs=q_spec,
            scratch_shapes=[pltpu.VMEM((blk, 1), F32), pltpu.VMEM((blk, 1), F32), pltpu.VMEM((blk, d), F32)],
        ),
        out_shape=jax.ShapeDtypeStruct((b, h, s, d), F32),
        compiler_params=_cparams(("arbitrary", "arbitrary", "arbitrary")),
        name="flash",
    )(jnp.asarray(qi), jnp.asarray(kj), q, k, v)


def _dot(a, b):
    return jnp.dot(a, b, preferred_element_type=F32, precision=HI)


def _dot_nt(a, b):
    return lax.dot_general(a, b, (((1,), (1,)), ((), ())), preferred_element_type=F32, precision=HI)


def _dot_tn(a, b):
    return jnp.dot(a.T, b, preferred_element_type=F32, precision=HI)


def _pair_masks():
    lane = lax.broadcasted_iota(jnp.int32, (CHUNK, LANES), 1)
    m0 = (lane < HEAD_DIM).astype(F32)
    return m0, 1.0 - m0


def _ext(x, m0, m1):
    return jnp.concatenate([x * m0, x * m1], axis=0)


def _neumann_inverse(a):
    n = a.shape[0]
    eye = (lax.broadcasted_iota(jnp.int32, (n, n), 0) == lax.broadcasted_iota(jnp.int32, (n, n), 1)).astype(F32)
    t = eye + a
    p = a
    for _ in range(int(math.log2(CHUNK)) - 1):
        p = _dot(p, p)
        t = t + _dot(p, t)
    return t


def _rwkv_kernel(r_ref, k_ref, v_ref, a_ref, b_ref, ld_ref, y_ref, st_ref):
    @pl.when(pl.program_id(1) == 0)
    def _():
        st_ref[...] = jnp.zeros_like(st_ref)

    c2 = 2 * CHUNK
    row = lax.broadcasted_iota(jnp.int32, (c2, c2), 0)
    col = lax.broadcasted_iota(jnp.int32, (c2, c2), 1)
    strict = row > col
    trow = lax.broadcasted_iota(jnp.int32, (CHUNK, c2), 0)
    tcol = lax.broadcasted_iota(jnp.int32, (CHUNK, c2), 1) & (CHUNK - 1)
    incl_w = tcol <= trow
    tri = (lax.broadcasted_iota(jnp.int32, (CHUNK, CHUNK), 0)
           >= lax.broadcasted_iota(jnp.int32, (CHUNK, CHUNK), 1)).astype(F32)
    m0, m1 = _pair_masks()

    for p in range(WIDTH // LANES):
        sl = slice(p * LANES, (p + 1) * LANES)
        r, k, v, a, b, ld = (ref[:, sl] for ref in (r_ref, k_ref, v_ref, a_ref, b_ref, ld_ref))
        cum = _dot(tri, ld)
        e_pos = jnp.exp(cum)
        e_neg = jnp.exp(-cum)
        e_end = jnp.exp(cum[CHUNK - 1:CHUNK, :] - cum)
        a_ext = _ext(a * jnp.exp(cum - ld), m0, m1)
        b_ext = _ext(b * e_neg, m0, m1)
        k_ext = _ext(k * e_neg, m0, m1)
        v_ext = _ext(v, m0, m1)
        r_dec = r * e_pos
        a_ab = jnp.where(strict, _dot_nt(a_ext, b_ext), 0.0)
        a_ak = jnp.where(strict, _dot_nt(a_ext, k_ext), 0.0)
        t_inv = _neumann_inverse(a_ab)
        w_ext = _dot(t_inv, a_ext)
        u0_ext = _dot(t_inv, _dot(a_ak, v_ext))
        a_rb = jnp.where(incl_w, _dot_nt(r_dec, b_ext), 0.0)
        a_rk = jnp.where(incl_w, _dot_nt(r_dec, k_ext), 0.0)
        st = st_ref[p]
        u_ext = u0_ext + _dot(w_ext, st)
        y = _dot(r_dec, st) + _dot(jnp.concatenate([a_rb, a_rk], axis=1),
                                   jnp.concatenate([u_ext, v_ext], axis=0))
        y_ref[:, sl] = y
        e_last = jnp.broadcast_to(e_pos[CHUNK - 1:CHUNK, :], (LANES, LANES))
        upd = _dot_tn(jnp.concatenate([_ext(b * e_end, m0, m1), _ext(k * e_end, m0, m1)], axis=0),
                      jnp.concatenate([u_ext, v_ext], axis=0))
        st_ref[p] = st * e_last.T + upd


def _gdn_kernel(q_ref, k_ref, v_ref, g_ref, beta_ref, o_ref, st_ref):
    @pl.when(pl.program_id(1) == 0)
    def _():
        st_ref[...] = jnp.zeros_like(st_ref)

    c2 = 2 * CHUNK
    row = lax.broadcasted_iota(jnp.int32, (c2, c2), 0)
    col = lax.broadcasted_iota(jnp.int32, (c2, c2), 1)
    same_head = (row >= CHUNK) == (col >= CHUNK)
    causal = same_head & (row >= col)
    strict = row > col
    tri = (lax.broadcasted_iota(jnp.int32, (CHUNK, CHUNK), 0)
           >= lax.broadcasted_iota(jnp.int32, (CHUNK, CHUNK), 1)).astype(F32)
    m0, m1 = _pair_masks()

    for p in range(WIDTH // LANES):
        sl = slice(p * LANES, (p + 1) * LANES)
        q, k, v, g, beta = (ref[:, sl] for ref in (q_ref, k_ref, v_ref, g_ref, beta_ref))
        gc = _dot(tri, g)
        g_rows = jnp.concatenate([jnp.broadcast_to(gc[:, 0:1], (CHUNK, LANES)),
                                  jnp.broadcast_to(gc[:, HEAD_DIM:HEAD_DIM + 1], (CHUNK, LANES))], axis=0)
        gamma = jnp.exp(jnp.where(causal, g_rows - g_rows.T, -jnp.inf))
        e_pos = jnp.exp(gc)
        e_last = e_pos[CHUNK - 1:CHUNK, :]
        kb = k * beta
        k_ext = _ext(k, m0, m1)
        a_neg = -jnp.where(strict, _dot_nt(_ext(kb, m0, m1), k_ext) * gamma, 0.0)
        t_inv = _neumann_inverse(a_neg)
        u_ext = _dot(t_inv, _ext(v * beta, m0, m1))
        w_ext = _dot(t_inv, _ext(kb * e_pos, m0, m1))
        a_in = _dot_nt(q, k_ext) * (gamma[:CHUNK] + gamma[CHUNK:])
        st = st_ref[p]
        v_new = u_ext - _dot(w_ext, st)
        o_ref[:, sl] = _dot(q * e_pos, st) + _dot(a_in, v_new)
        k_dec = _ext(k * jnp.exp(gc[CHUNK - 1:CHUNK, :] - gc), m0, m1)
        st_ref[p] = st * e_last + _dot_tn(k_dec, v_new)


def _chunk_scan(kernel, args, b, s, name):
    spec = pl.BlockSpec((CHUNK, WIDTH), lambda bb, c: (bb * (s // CHUNK) + c, 0))
    return pl.pallas_call(
        kernel,
        grid=(b, s // CHUNK),
        in_specs=[spec] * len(args),
        out_specs=spec,
        out_shape=jax.ShapeDtypeStruct((b * s, WIDTH), F32),
        scratch_shapes=[pltpu.VMEM((WIDTH // LANES, LANES, LANES), F32)],
        compiler_params=_cparams(("arbitrary", "arbitrary")),
        name=name,
    )(*args)


def _merge_kernel(h_ref, zg_ref, oa_ref, ob_ref, oc_ref, wa_ref, wb_ref, wc_ref, wo_ref, out_ref):
    d = h_ref.shape[1]
    merged = jnp.zeros(h_ref.shape, F32)
    for n, (o_ref, w_ref) in enumerate(((oa_ref, wa_ref), (ob_ref, wb_ref), (oc_ref, wc_ref))):
        proj = jnp.dot(o_ref[...].astype(BF16), w_ref[...], preferred_element_type=F32)
        merged = merged + jax.nn.sigmoid(zg_ref[:, n * d:(n + 1) * d]) * proj
    out_ref[...] = h_ref[...] + jnp.dot(merged.astype(BF16), wo_ref[...], preferred_element_type=F32)


def _merge(h, zg, oa, ob, oc, wa, wb, wc, wo, *, bm=512):
    m, d = h.shape
    bm = min(bm, m)
    row = lambda width: pl.BlockSpec((bm, width), lambda i: (i, 0))
    full = lambda w: pl.BlockSpec(w.shape, lambda i: (0, 0))
    ws = [w.astype(BF16) for w in (wa, wb, wc, wo)]
    return pl.pallas_call(
        _merge_kernel,
        grid=(m // bm,),
        in_specs=[row(d), row(3 * d), row(WIDTH), row(WIDTH), row(WIDTH)] + [full(w) for w in ws],
        out_specs=row(d),
        out_shape=jax.ShapeDtypeStruct((m, d), F32),
        compiler_params=_cparams(("arbitrary",)),
        name="merge",
    )(h, zg, oa, ob, oc, *ws)


def _ffn_kernel(be_ref, nu_ref, x_ref, g_ref, wg_ref, wu_ref, wd_ref, o_ref, xs_ref, acc_ref, *, nf):
    i = pl.program_id(0)
    f = pl.program_id(1)
    used = i < nu_ref[0]

    @pl.when(used & (f == 0))
    def _():
        x = x_ref[...]
        ms = jnp.mean(x * x, axis=-1, keepdims=True)
        xs_ref[...] = (x * lax.rsqrt(ms + NORM_EPS) * g_ref[...]).astype(BF16)

    @pl.when(f == 0)
    def _():
        acc_ref[...] = jnp.zeros_like(acc_ref)

    @pl.when(used)
    def _():
        xs = xs_ref[...]
        gate = jnp.dot(xs, wg_ref[...], preferred_element_type=F32)
        up = jnp.dot(xs, wu_ref[...], preferred_element_type=F32)
        act = (jax.nn.silu(gate) * up).astype(BF16)
        acc_ref[...] += jnp.dot(act, wd_ref[...], preferred_element_type=F32)

    @pl.when(f == nf - 1)
    def _():
        o_ref[...] = acc_ref[...]


def _ffn(x, g, wg, wu, wd, block_e, n_used, *, bm, tf):
    r, d = x.shape
    ff = wg.shape[2]
    nf = ff // tf

    def ff_idx(i, f, be, nu):
        return jnp.where(i < nu[0], f, nf - 1)

    return pl.pallas_call(
        functools.partial(_ffn_kernel, nf=nf),
        grid_spec=pltpu.PrefetchScalarGridSpec(
            num_scalar_prefetch=2,
            grid=(r // bm, nf),
            in_specs=[
                pl.BlockSpec((bm, d), lambda i, f, be, nu: (i, 0)),
                pl.BlockSpec((1, d), lambda i, f, be, nu: (0, 0)),
                pl.BlockSpec((None, d, tf), lambda i, f, be, nu: (be[i], 0, ff_idx(i, f, be, nu))),
                pl.BlockSpec((None, d, tf), lambda i, f, be, nu: (be[i], 0, ff_idx(i, f, be, nu))),
                pl.BlockSpec((None, tf, d), lambda i, f, be, nu: (be[i], ff_idx(i, f, be, nu), 0)),
            ],
            out_specs=pl.BlockSpec((bm, d), lambda i, f, be, nu: (i, 0)),
            scratch_shapes=[pltpu.VMEM((bm, d), BF16), pltpu.VMEM((bm, d), F32)],
        ),
        out_shape=jax.ShapeDtypeStruct((r, d), F32),
        compiler_params=_cparams(("arbitrary", "arbitrary")),
        name="ffn",
    )(block_e, n_used, x, g.reshape(1, d).astype(F32), wg, wu, wd)


def _ple_kernel(h_ref, p_ref, wg_ref, wp_ref, g_ref, o_ref):
    h = h_ref[...]
    gate = jnp.dot(h.astype(BF16), wg_ref[...], preferred_element_type=F32)
    e = jnp.dot(p_ref[...].astype(BF16), wp_ref[...], preferred_element_type=F32)
    ms = jnp.mean(e * e, axis=-1, keepdims=True)
    e = e * lax.rsqrt(ms + NORM_EPS) * g_ref[...]
    o_ref[...] = h + jax.nn.sigmoid(gate) * e


def _ple(h, p, w_gate, w_proj, g, *, bm=512):
    m, d = h.shape
    bm = min(bm, m)
    pd = p.shape[1]
    return pl.pallas_call(
        _ple_kernel,
        grid=(m // bm,),
        in_specs=[pl.BlockSpec((bm, d), lambda i: (i, 0)), pl.BlockSpec((bm, pd), lambda i: (i, 0)),
                  pl.BlockSpec((d, d), lambda i: (0, 0)), pl.BlockSpec((pd, d), lambda i: (0, 0)),
                  pl.BlockSpec((1, d), lambda i: (0, 0))],
        out_specs=pl.BlockSpec((bm, d), lambda i: (i, 0)),
        out_shape=jax.ShapeDtypeStruct((m, d), F32),
        compiler_params=_cparams(("arbitrary",)),
        name="ple",
    )(h, p, w_gate.astype(BF16), w_proj.astype(BF16), g.reshape(1, d).astype(F32))


def _shift(z, b, s):
    z3 = z.reshape(b, s, -1)
    return jnp.pad(z3, ((0, 0), (1, 0), (0, 0)))[:, :-1].reshape(z.shape)


def _per_head(fn, x):
    t = x.shape[0]
    return fn(x.reshape(t, N_HEADS, HEAD_DIM)).reshape(t, WIDTH)


def _l2n(x, eps=1e-12):
    return x * lax.rsqrt(jnp.sum(x * x, axis=-1, keepdims=True) + eps)


def _rwkv_branch(z, u_vres, v_first, b, s, mu, w0, w_up, a0, a_up, g_up, k_k, k_a, r_k, ln_g, ln_b, vres):
    z = z + (_shift(z, b, s) - z) * mu
    r, k, v = z[:, :WIDTH], z[:, WIDTH:2 * WIDTH], z[:, 2 * WIDTH:3 * WIDTH]
    w_lo = z[:, 3 * WIDTH:3 * WIDTH + 64]
    a_lo = z[:, 3 * WIDTH + 64:3 * WIDTH + 128]
    g_lo = z[:, 3 * WIDTH + 128:]
    log_w = -jax.nn.softplus(-(w0 + _mm(jnp.tanh(w_lo), w_up))) - 0.5
    ld = -jnp.exp(log_w)
    iclr = jax.nn.sigmoid(a0 + _mm(a_lo, a_up))
    gate = _mm(jax.nn.sigmoid(g_lo), g_up)
    if vres is None:
        v_first = v
    else:
        v_up, v_bias = vres
        lo = u_vres[:, :32] + _shift(u_vres[:, 32:64], b, s)
        v = v + (v_first - v) * jax.nn.sigmoid(v_bias + _mm(lo, v_up))
    kk = _per_head(_l2n, k * k_k)
    k = k * (1.0 + (iclr - 1.0) * k_a)
    y = _chunk_scan(_rwkv_kernel, (r, k, v, -kk, kk * iclr, ld), b, s, "rwkv")

    def gn(t):
        mean = jnp.mean(t, axis=-1, keepdims=True)
        var = jnp.mean(jnp.square(t - mean), axis=-1, keepdims=True)
        return (t - mean) * lax.rsqrt(var + RWKV_GN_EPS)

    y = _per_head(gn, y) * ln_g + ln_b
    t = r.shape[0]
    bonus = jnp.sum((r * k).reshape(t, N_HEADS, HEAD_DIM) * r_k, axis=-1, keepdims=True) * v.reshape(t, N_HEADS, HEAD_DIM)
    y = y + bonus.reshape(t, WIDTH)
    return y * gate, v_first


def _mla_branch(z, cos, sin, b, s, q_norm_g, kv_norm_g, w_uq, w_ukv, qk_g_q, qk_g_k):
    t = z.shape[0]
    c_q, c_kv, k_pe = z[:, :Q_LORA], z[:, Q_LORA:Q_LORA + KV_LORA], z[:, Q_LORA + KV_LORA:Q_LORA + KV_LORA + QK_ROPE]
    q = _mm(c_q, w_uq, q_norm_g).reshape(t, N_HEADS, QK_DIM)
    kv = _mm(c_kv, w_ukv, kv_norm_g).reshape(t, N_HEADS, QK_NOPE + HEAD_DIM)
    k_nope, v = kv[..., :QK_NOPE], kv[..., QK_NOPE:]
    k = jnp.concatenate([k_nope, jnp.broadcast_to(k_pe[:, None, :], (t, N_HEADS, QK_ROPE))], axis=-1)

    def rms(x, g):
        return x * lax.rsqrt(jnp.mean(x * x, axis=-1, keepdims=True) + NORM_EPS) * g

    def rope(x):
        x1, x2 = x[..., :QK_ROPE // 2], x[..., QK_ROPE // 2:]
        return jnp.concatenate([x1 * cos - x2 * sin, x2 * cos + x1 * sin], axis=-1)

    q = rms(q, qk_g_q)
    k = rms(k, qk_g_k)
    q = jnp.concatenate([q[..., :QK_NOPE], rope(q[..., QK_NOPE:])], axis=-1) * (QK_DIM ** -0.5)
    k = jnp.concatenate([k[..., :QK_NOPE], rope(k[..., QK_NOPE:])], axis=-1)

    def to_bhsd(x):
        x = jnp.pad(x, ((0, 0), (0, 0), (0, LANES - x.shape[-1])))
        return x.reshape(b, s, N_HEADS, LANES).transpose(0, 2, 1, 3).astype(BF16)

    o = _flash_attention(to_bhsd(q), to_bhsd(k), to_bhsd(v))
    return o[..., :HEAD_DIM].transpose(0, 2, 1, 3).reshape(t, WIDTH)


def _gdn_branch(z, b, s, conv_w, a_log, dt_bias, norm_g):
    t = z.shape[0]
    nq = 3 * WIDTH
    qkv, b_logit, a_logit, gate = z[:, :nq], z[:, nq:nq + 8], z[:, nq + 8:nq + 16], z[:, nq + 16:nq + 16 + WIDTH]
    x3 = jnp.pad(qkv.reshape(b, s, nq), ((0, 0), (GDN_CONV - 1, 0), (0, 0)))
    conv = sum(x3[:, j:j + s] * conv_w[j] for j in range(GDN_CONV)).reshape(t, nq)
    qkv = jax.nn.silu(conv)
    q = _per_head(_l2n, qkv[:, :WIDTH]) * (HEAD_DIM ** -0.5)
    k = _per_head(_l2n, qkv[:, WIDTH:2 * WIDTH])
    v = qkv[:, 2 * WIDTH:]
    beta = jax.nn.sigmoid(b_logit)
    g = -jnp.exp(a_log) * jax.nn.softplus(a_logit + dt_bias)
    o = _chunk_scan(_gdn_kernel, (q, k, v, jnp.repeat(g, HEAD_DIM, axis=1), jnp.repeat(beta, HEAD_DIM, axis=1)),
                    b, s, "gdn")

    def rms(x):
        return x * lax.rsqrt(jnp.mean(x * x, axis=-1, keepdims=True) + NORM_EPS) * norm_g

    return _per_head(rms, o) * jax.nn.silu(gate)


def _moe(h, g, w_router, wg, wu, wd, *, bm=512, tf=512):
    t, d = h.shape
    n_assign = t * TOP_K
    n_blocks = -(-(n_assign + N_EXPERTS * (bm - 1)) // bm)
    logits = _mm(h, w_router, g, hi=True)
    top_logits, top_e = lax.top_k(logits, TOP_K)
    top_w = jax.nn.softmax(top_logits, axis=-1).reshape(-1)
    flat_e = top_e.reshape(-1)
    onehot = (flat_e[:, None] == jnp.arange(N_EXPERTS)[None, :]).astype(jnp.int32)
    rank = jnp.take_along_axis(jnp.cumsum(onehot, axis=0) - onehot, flat_e[:, None], axis=1)[:, 0]
    counts = jnp.sum(onehot, axis=0)
    padded = (counts + bm - 1) // bm * bm
    pad_end = jnp.cumsum(padded)
    dest = (pad_end - padded)[flat_e] + rank
    src_tok = jnp.zeros((n_blocks * bm,), jnp.int32).at[dest].set(jnp.arange(n_assign, dtype=jnp.int32) // TOP_K)
    block_e = jnp.minimum(jnp.searchsorted(pad_end, jnp.arange(n_blocks) * bm, side='right'),
                          N_EXPERTS - 1).astype(jnp.int32)
    n_used = (pad_end[-1] // bm).astype(jnp.int32).reshape(1)
    y_rows = _ffn(h[src_tok], g, wg, wu, wd, block_e, n_used, bm=bm, tf=tf)
    y = (y_rows[dest] * top_w[:, None]).reshape(t, TOP_K, d)
    return y[:, 0] + y[:, 1]


def kernel(x, p, positions, norm_mix_g, w_in, rwkv_mu, rwkv_w0, rwkv_w_up, rwkv_a0, rwkv_a_up, rwkv_g_up, rwkv_k_k, rwkv_k_a, rwkv_r_k, rwkv_ln_g, rwkv_ln_b, vres_mu, vres_down, vres_up, vres_b, mla_q_norm_g, mla_kv_norm_g, mla_w_uq, mla_w_ukv, mla_qk_norm_q, mla_qk_norm_k, gdn_conv_w, gdn_a_log, gdn_dt_bias, gdn_norm_g, w_br_rwkv, w_br_mla, w_br_gdn, w_out, norm_ffn_g, ffn_wg, ffn_wu, ffn_wd, moe_router, moe_wg, moe_wu, moe_wd, ple_proj, ple_gate, ple_norm_g):
    b, s, d = x.shape
    t = b * s
    depth = w_in.shape[0]
    rwkv_in = 3 * WIDTH + 256
    mla_in = Q_LORA + KV_LORA + QK_ROPE
    gdn_in = 3 * WIDTH + 16 + WIDTH
    inv_freq = 1.0 / (ROPE_THETA ** (jnp.arange(0, QK_ROPE, 2, dtype=F32) / QK_ROPE))
    ang = positions.astype(F32).reshape(t, 1, 1) * inv_freq
    cos, sin = jnp.cos(ang), jnp.sin(ang)

    h = x.reshape(t, d)
    v_first = None
    for i in range(depth):
        w = w_in[i]
        g = norm_mix_g[i]
        w_rwkv = w[:, :rwkv_in]
        u_vres = None
        if i > 0:
            mu_v = vres_mu[i - 1][:, None]
            vd = vres_down[i - 1]
            w_rwkv = jnp.concatenate([w_rwkv, (1.0 - mu_v) * vd, mu_v * vd], axis=1)
        z_rwkv = _mm(h, w_rwkv, g)
        if i > 0:
            z_rwkv, u_vres = z_rwkv[:, :rwkv_in], z_rwkv[:, rwkv_in:]
        z_mla = _mm(h, w[:, rwkv_in:rwkv_in + mla_in], g)
        z_gdn = _mm(h, w[:, rwkv_in + mla_in:rwkv_in + mla_in + gdn_in], g)
        z_gate = _mm(h, w[:, rwkv_in + mla_in + gdn_in:], g)
        vres = None if i == 0 else (vres_up[i - 1], vres_b[i - 1])
        o_a, v_first = _rwkv_branch(z_rwkv, u_vres, v_first, b, s, rwkv_mu[i], rwkv_w0[i], rwkv_w_up[i],
                                    rwkv_a0[i], rwkv_a_up[i], rwkv_g_up[i], rwkv_k_k[i], rwkv_k_a[i],
                                    rwkv_r_k[i], rwkv_ln_g[i], rwkv_ln_b[i], vres)
        o_b = _mla_branch(z_mla, cos, sin, b, s, mla_q_norm_g[i], mla_kv_norm_g[i], mla_w_uq[i], mla_w_ukv[i],
                          mla_qk_norm_q[i], mla_qk_norm_k[i])
        o_c = _gdn_branch(z_gdn, b, s, gdn_conv_w[i], gdn_a_log[i], gdn_dt_bias[i], gdn_norm_g[i])
        h = _merge(h, z_gate, o_a, o_b, o_c, w_br_rwkv[i], w_br_mla[i], w_br_gdn[i], w_out[i])
        if i % 2 == 0:
            j = i // 2
            ff = ffn_wg.shape[2]
            f = _ffn(h, norm_ffn_g[i], ffn_wg[j:j + 1].astype(BF16), ffn_wu[j:j + 1].astype(BF16),
                     ffn_wd[j:j + 1].astype(BF16), jnp.zeros((t // 512,), jnp.int32),
                     jnp.full((1,), t // 512, jnp.int32), bm=512, tf=ff // 2)
        else:
            j = i // 2
            f = _moe(h, norm_ffn_g[i], moe_router[j], moe_wg[j].astype(BF16), moe_wu[j].astype(BF16),
                     moe_wd[j].astype(BF16))
        h = h + f
        h = _ple(h, p[i].reshape(t, -1), ple_gate[i], ple_proj[i], ple_norm_g[i])
    return h.reshape(b, s, d)
```

```python
import functools
import math

import jax
import jax.numpy as jnp
import numpy as np
from jax import lax
from jax.experimental import pallas as pl
from jax.experimental.pallas import tpu as pltpu

F32 = jnp.float32
BF16 = jnp.bfloat16
HI = lax.Precision.HIGHEST

NORM_EPS = 1e-6
RWKV_GN_EPS = 64e-5
N_HEADS = 8
HEAD_DIM = 64
WIDTH = N_HEADS * HEAD_DIM
QK_NOPE = 64
QK_ROPE = 32
QK_DIM = QK_NOPE + QK_ROPE
Q_LORA = 256
KV_LORA = 128
ROPE_THETA = 10000.0
GDN_CONV = 4
N_EXPERTS = 8
TOP_K = 2
LANES = 128
CHUNK = 64
VMEM_LIMIT = 48 * 1024 * 1024


def _cparams(sem):
    return pltpu.CompilerParams(dimension_semantics=sem, vmem_limit_bytes=VMEM_LIMIT)


def _mm_kernel(*refs, norm, hi):
    if norm:
        x_ref, g_ref, w_ref, o_ref, xs_ref = refs
    else:
        x_ref, w_ref, o_ref, xs_ref = refs

    @pl.when(pl.program_id(1) == 0)
    def _():
        x = x_ref[...].astype(F32)
        if norm:
            ms = jnp.mean(x * x, axis=-1, keepdims=True)
            x = x * lax.rsqrt(ms + NORM_EPS) * g_ref[...]
        xs_ref[...] = x.astype(xs_ref.dtype)

    o_ref[...] = jnp.dot(xs_ref[...], w_ref[...], preferred_element_type=F32,
                         precision=HI if hi else None).astype(o_ref.dtype)


def _mm(x, w, g=None, *, bm=512, bn=512, hi=False, out_dtype=F32):
    m, k = x.shape
    n = w.shape[1]
    bm = min(bm, m)
    bn = min(bn, -(-n // LANES) * LANES)
    n_pad = -(-n // bn) * bn
    wdt = F32 if hi else BF16
    w = w.astype(wdt)
    if n_pad != n:
        w = jnp.pad(w, ((0, 0), (0, n_pad - n)))
    norm = g is not None
    in_specs = [pl.BlockSpec((bm, k), lambda i, j: (i, 0))]
    args = [x]
    if norm:
        in_specs.append(pl.BlockSpec((1, k), lambda i, j: (0, 0)))
        args.append(g.reshape(1, k).astype(F32))
    in_specs.append(pl.BlockSpec((k, bn), lambda i, j: (0, j)))
    args.append(w)
    out = pl.pallas_call(
        functools.partial(_mm_kernel, norm=norm, hi=hi),
        grid=(m // bm, n_pad // bn),
        in_specs=in_specs,
        out_specs=pl.BlockSpec((bm, bn), lambda i, j: (i, j)),
        out_shape=jax.ShapeDtypeStruct((m, n_pad), out_dtype),
        scratch_shapes=[pltpu.VMEM((bm, k), wdt)],
        compiler_params=_cparams(("arbitrary", "arbitrary")),
        name="mm",
    )(*args)
    return out if n_pad == n else out[:, :n]


def _flash_kernel(qi_ref, kj_ref, q_ref, k_ref, v_ref, o_ref, m_ref, l_ref, acc_ref, *, blk):
    n = pl.program_id(2)
    i = qi_ref[n]
    j = kj_ref[n]

    @pl.when(j == 0)
    def _():
        m_ref[...] = jnp.full_like(m_ref, -jnp.inf)
        l_ref[...] = jnp.zeros_like(l_ref)
        acc_ref[...] = jnp.zeros_like(acc_ref)

    def update(masked):
        s = lax.dot_general(q_ref[...], k_ref[...], (((1,), (1,)), ((), ())), preferred_element_type=F32)
        if masked:
            row = lax.broadcasted_iota(jnp.int32, (blk, blk), 0)
            col = lax.broadcasted_iota(jnp.int32, (blk, blk), 1)
            s = jnp.where(row >= col, s, -jnp.inf)
        m_old = m_ref[...]
        m_new = jnp.maximum(m_old, jnp.max(s, axis=-1, keepdims=True))
        alpha = jnp.exp(m_old - m_new)
        p = jnp.exp(s - m_new[:, :1])
        l_ref[...] = alpha * l_ref[...] + jnp.sum(p, axis=-1, keepdims=True)
        acc_ref[...] = alpha * acc_ref[...] + jnp.dot(p.astype(BF16), v_ref[...], preferred_element_type=F32)
        m_ref[...] = m_new

    @pl.when(j < i)
    def _():
        update(False)

    @pl.when(j == i)
    def _():
        update(True)
        o_ref[...] = (acc_ref[...] / l_ref[...]).astype(o_ref.dtype)


def _flash_attention(q, k, v, *, blk=1024):
    b, h, s, d = q.shape
    blk = min(blk, s)
    nb = s // blk
    qi = np.array([i for i in range(nb) for j in range(i + 1)], np.int32)
    kj = np.array([j for i in range(nb) for j in range(i + 1)], np.int32)
    q_spec = pl.BlockSpec((None, None, blk, d), lambda bb, hh, n, qi_r, kj_r: (bb, hh, qi_r[n], 0))
    kv_spec = pl.BlockSpec((None, None, blk, d), lambda bb, hh, n, qi_r, kj_r: (bb, hh, kj_r[n], 0))
    return pl.pallas_call(
        functools.partial(_flash_kernel, blk=blk),
        grid_spec=pltpu.PrefetchScalarGridSpec(
            num_scalar_prefetch=2,
            grid=(b, h, len(qi)),
            in_specs=[q_spec, kv_spec, kv_spec],
            out_specs=q_spec,
            scratch_shapes=[pltpu.VMEM((blk, LANES), F32), pltpu.VMEM((blk, LANES), F32), pltpu.VMEM((blk, d), F32)],
        ),
        out_shape=jax.ShapeDtypeStruct((b, h, s, d), F32),
        compiler_params=_cparams(("arbitrary", "arbitrary", "arbitrary")),
        name="flash",
    )(jnp.asarray(qi), jnp.asarray(kj), q, k, v)


def _dot(a, b):
    return jnp.dot(a.astype(BF16), b.astype(BF16), preferred_element_type=F32)


def _dot_nt(a, b):
    return lax.dot_general(a.astype(BF16), b.astype(BF16), (((1,), (1,)), ((), ())), preferred_element_type=F32)


def _dot_tn(a, b):
    return jnp.dot(a.T.astype(BF16), b.astype(BF16), preferred_element_type=F32)


def _cumsum_rows(tri, x):
    hi = x.astype(BF16)
    r1 = x - hi.astype(F32)
    mid = r1.astype(BF16)
    lo = (r1 - mid.astype(F32)).astype(BF16)
    s = jnp.dot(tri, jnp.concatenate([hi, mid, lo], axis=1), preferred_element_type=F32)
    n = x.shape[1]
    return s[:, :n] + s[:, n:2 * n] + s[:, 2 * n:]


def _pair_masks():
    lane = lax.broadcasted_iota(jnp.int32, (CHUNK, LANES), 1)
    m0 = (lane < HEAD_DIM).astype(F32)
    return m0, 1.0 - m0


def _ext(x, m0, m1, dtype=BF16):
    return jnp.concatenate([x * m0, x * m1], axis=0).astype(dtype)


def _cat(xs, axis):
    return jnp.concatenate(xs, axis=axis)


def _neumann_inverse(mats):
    n = mats[0].shape[0]
    eye = (lax.broadcasted_iota(jnp.int32, (n, n), 0) == lax.broadcasted_iota(jnp.int32, (n, n), 1)).astype(F32)
    ts = [eye + a for a in mats]
    ps = [_dot(a, a) for a in mats]
    for _ in range(int(math.log2(CHUNK)) - 2):
        pps = [_dot(_cat([p, t], 0), p) for p, t in zip(ps, ts)]
        ps = [pp[:n] for pp in pps]
        ts = [t + pp[n:] for t, pp in zip(ts, pps)]
    return [t + _dot(t, p) for t, p in zip(ts, ps)]


def _work_items(ref):
    return [(slice(c * CHUNK, (c + 1) * CHUNK), slice(p * LANES, (p + 1) * LANES))
            for c in range(ref.shape[0] // CHUNK) for p in range(WIDTH // LANES)]


def _rwkv_kernel(r_ref, k_ref, v_ref, a_ref, b_ref, ld_ref, y_ref, st_ref):
    @pl.when(pl.program_id(1) == 0)
    def _():
        st_ref[...] = jnp.zeros_like(st_ref)

    c2 = 2 * CHUNK
    strict = lax.broadcasted_iota(jnp.int32, (c2, c2), 0) > lax.broadcasted_iota(jnp.int32, (c2, c2), 1)
    trow = lax.broadcasted_iota(jnp.int32, (CHUNK, c2), 0)
    tcol = lax.broadcasted_iota(jnp.int32, (CHUNK, c2), 1) & (CHUNK - 1)
    incl_w = tcol <= trow
    tri = (lax.broadcasted_iota(jnp.int32, (CHUNK, CHUNK), 0)
           >= lax.broadcasted_iota(jnp.int32, (CHUNK, CHUNK), 1)).astype(BF16)
    m0, m1 = _pair_masks()

    items = _work_items(r_ref)
    n_pairs = WIDTH // LANES
    r, k, v, a, b, ld = ([ref[rows, cols] for rows, cols in items]
                         for ref in (r_ref, k_ref, v_ref, a_ref, b_ref, ld_ref))
    cum = [_cumsum_rows(tri, x) for x in ld]
    e_pos = [jnp.exp(x) for x in cum]
    e_neg = [jnp.exp(-x) for x in cum]
    e_end = [jnp.exp(x[CHUNK - 1:CHUNK, :] - x) for x in cum]
    a_ext = [_ext(ai * jnp.exp(ci - li), m0, m1) for ai, ci, li in zip(a, cum, ld)]
    v_ext = [_ext(x, m0, m1) for x in v]
    r_dec = [(ri * ei).astype(BF16) for ri, ei in zip(r, e_pos)]
    s1 = [_dot_nt(_cat([ae, rd], 0), _cat([_ext(bi * en, m0, m1), _ext(ki * en, m0, m1)], 0))
          for ae, rd, bi, ki, en in zip(a_ext, r_dec, b, k, e_neg)]
    a_ab = [jnp.where(strict, s[:c2, :c2], 0.0) for s in s1]
    a_ak = [jnp.where(strict, s[:c2, c2:], 0.0) for s in s1]
    a_rbk = [_cat([jnp.where(incl_w, s[c2:, :c2], 0.0), jnp.where(incl_w, s[c2:, c2:], 0.0)], 1).astype(BF16)
             for s in s1]
    akv = [_dot(x, ve) for x, ve in zip(a_ak, v_ext)]
    t_inv = _neumann_inverse(a_ab)
    wu0 = [_dot(t, _cat([ae, av.astype(BF16)], 1)) for t, ae, av in zip(t_inv, a_ext, akv)]
    bk_end = [_cat([_ext(bi * ee, m0, m1, F32), _ext(ki * ee, m0, m1, F32)], 0) for bi, ki, ee in zip(b, k, e_end)]
    e_last_t = [jnp.broadcast_to(e[CHUNK - 1:CHUNK, :], (LANES, LANES)).T for e in e_pos]
    st = [st_ref[p] for p in range(n_pairs)]
    for c in range(len(items) // n_pairs):
        idx = range(c * n_pairs, (c + 1) * n_pairs)
        z = [_dot(_cat([wu0[i][:, :LANES].astype(BF16), r_dec[i]], 0), s) for i, s in zip(idx, st)]
        uv = [_cat([(wu0[i][:, LANES:] + zi[:c2]).astype(BF16), v_ext[i]], 0) for i, zi in zip(idx, z)]
        for i, zi, uvi in zip(idx, z, uv):
            rows, cols = items[i]
            y_ref[rows, cols] = zi[c2:] + _dot(a_rbk[i], uvi)
        st = [s * e_last_t[i] + _dot_tn(bk_end[i], uvi) for i, s, uvi in zip(idx, st, uv)]
    for p in range(n_pairs):
        st_ref[p] = st[p]


def _gdn_kernel(q_ref, k_ref, v_ref, g_ref, beta_ref, o_ref, st_ref):
    @pl.when(pl.program_id(1) == 0)
    def _():
        st_ref[...] = jnp.zeros_like(st_ref)

    c2 = 2 * CHUNK
    row = lax.broadcasted_iota(jnp.int32, (c2, c2), 0)
    col = lax.broadcasted_iota(jnp.int32, (c2, c2), 1)
    causal = ((row >= CHUNK) == (col >= CHUNK)) & (row >= col)
    strict = row > col
    tri = (lax.broadcasted_iota(jnp.int32, (CHUNK, CHUNK), 0)
           >= lax.broadcasted_iota(jnp.int32, (CHUNK, CHUNK), 1)).astype(BF16)
    m0, m1 = _pair_masks()

    items = _work_items(q_ref)
    n_pairs = WIDTH // LANES
    q, k, v, g, beta = ([ref[rows, cols] for rows, cols in items] for ref in (q_ref, k_ref, v_ref, g_ref, beta_ref))
    gc = [_cumsum_rows(tri, x) for x in g]
    g_rows = [_cat([jnp.broadcast_to(x[:, 0:1], (CHUNK, LANES)),
                    jnp.broadcast_to(x[:, HEAD_DIM:HEAD_DIM + 1], (CHUNK, LANES))], 0) for x in gc]
    gamma = [jnp.exp(jnp.where(causal, x - x.T, -jnp.inf)) for x in g_rows]
    e_pos = [jnp.exp(x) for x in gc]
    kb = [ki * bi for ki, bi in zip(k, beta)]
    s1 = [_dot_nt(_cat([_ext(kbi, m0, m1), qi.astype(BF16)], 0), _ext(ki, m0, m1)) for kbi, qi, ki in zip(kb, q, k)]
    a_neg = [-jnp.where(strict, s[:c2] * ga, 0.0) for s, ga in zip(s1, gamma)]
    a_in = [(s[c2:] * (ga[:CHUNK] + ga[CHUNK:])).astype(BF16) for s, ga in zip(s1, gamma)]
    t_inv = _neumann_inverse(a_neg)
    uw = [_dot(t, _cat([_ext(vi * bi, m0, m1), _ext(kbi * ei, m0, m1)], 1))
          for t, vi, bi, kbi, ei in zip(t_inv, v, beta, kb, e_pos)]
    q_dec = [(qi * ei).astype(BF16) for qi, ei in zip(q, e_pos)]
    k_dec = [_ext(ki * jnp.exp(x[CHUNK - 1:CHUNK, :] - x), m0, m1, F32) for ki, x in zip(k, gc)]
    st = [st_ref[p] for p in range(n_pairs)]
    for c in range(len(items) // n_pairs):
        idx = range(c * n_pairs, (c + 1) * n_pairs)
        z = [_dot(_cat([uw[i][:, LANES:].astype(BF16), q_dec[i]], 0), s) for i, s in zip(idx, st)]
        v_new = [(uw[i][:, :LANES] - zi[:c2]).astype(BF16) for i, zi in zip(idx, z)]
        for i, zi, vn in zip(idx, z, v_new):
            rows, cols = items[i]
            o_ref[rows, cols] = zi[c2:] + _dot(a_in[i], vn)
        st = [s * e_pos[i][CHUNK - 1:CHUNK, :] + _dot_tn(k_dec[i], vn) for i, s, vn in zip(idx, st, v_new)]
    for p in range(n_pairs):
        st_ref[p] = st[p]


def _chunk_scan(kernel, args, b, s, name, *, rows=2 * CHUNK):
    rows = min(rows, s)
    spec = pl.BlockSpec((rows, WIDTH), lambda bb, c: (bb * (s // rows) + c, 0))
    return pl.pallas_call(
        kernel,
        grid=(b, s // rows),
        in_specs=[spec] * len(args),
        out_specs=spec,
        out_shape=jax.ShapeDtypeStruct((b * s, WIDTH), F32),
        scratch_shapes=[pltpu.VMEM((WIDTH // LANES, LANES, LANES), F32)],
        compiler_params=_cparams(("arbitrary", "arbitrary")),
        name=name,
    )(*args)


def _merge_kernel(h_ref, zg_ref, oa_ref, ob_ref, oc_ref, wa_ref, wb_ref, wc_ref, wo_ref, out_ref):
    d = h_ref.shape[1]
    merged = jnp.zeros(h_ref.shape, F32)
    for n, (o_ref, w_ref) in enumerate(((oa_ref, wa_ref), (ob_ref, wb_ref), (oc_ref, wc_ref))):
        proj = jnp.dot(o_ref[...].astype(BF16), w_ref[...], preferred_element_type=F32)
        merged = merged + jax.nn.sigmoid(zg_ref[:, n * d:(n + 1) * d]) * proj
    out_ref[...] = h_ref[...] + jnp.dot(merged.astype(BF16), wo_ref[...], preferred_element_type=F32)


def _merge(h, zg, oa, ob, oc, wa, wb, wc, wo, *, bm=512):
    m, d = h.shape
    bm = min(bm, m)
    row = lambda width: pl.BlockSpec((bm, width), lambda i: (i, 0))
    full = lambda w: pl.BlockSpec(w.shape, lambda i: (0, 0))
    ws = [w.astype(BF16) for w in (wa, wb, wc, wo)]
    return pl.pallas_call(
        _merge_kernel,
        grid=(m // bm,),
        in_specs=[row(d), row(3 * d), row(WIDTH), row(WIDTH), row(WIDTH)] + [full(w) for w in ws],
        out_specs=row(d),
        out_shape=jax.ShapeDtypeStruct((m, d), F32),
        compiler_params=_cparams(("arbitrary",)),
        name="merge",
    )(h, zg, oa, ob, oc, *ws)


def _ffn_kernel(be_ref, nu_ref, x_ref, g_ref, wg_ref, wu_ref, wd_ref, o_ref, xs_ref, acc_ref, *, nf):
    i = pl.program_id(0)
    f = pl.program_id(1)
    used = i < nu_ref[0]

    @pl.when(used & (f == 0))
    def _():
        x = x_ref[...]
        ms = jnp.mean(x * x, axis=-1, keepdims=True)
        xs_ref[...] = (x * lax.rsqrt(ms + NORM_EPS) * g_ref[...]).astype(BF16)

    @pl.when(f == 0)
    def _():
        acc_ref[...] = jnp.zeros_like(acc_ref)

    @pl.when(used)
    def _():
        xs = xs_ref[...]
        gate = jnp.dot(xs, wg_ref[...], preferred_element_type=F32)
        up = jnp.dot(xs, wu_ref[...], preferred_element_type=F32)
        act = (jax.nn.silu(gate) * up).astype(BF16)
        acc_ref[...] += jnp.dot(act, wd_ref[...], preferred_element_type=F32)

    @pl.when(f == nf - 1)
    def _():
        o_ref[...] = acc_ref[...]


def _ffn(x, g, wg, wu, wd, block_e, n_used, *, bm, tf):
    r, d = x.shape
    ff = wg.shape[2]
    nf = ff // tf

    def ff_idx(i, f, be, nu):
        return jnp.where(i < nu[0], f, nf - 1)

    return pl.pallas_call(
        functools.partial(_ffn_kernel, nf=nf),
        grid_spec=pltpu.PrefetchScalarGridSpec(
            num_scalar_prefetch=2,
            grid=(r // bm, nf),
            in_specs=[
                pl.BlockSpec((bm, d), lambda i, f, be, nu: (i, 0)),
                pl.BlockSpec((1, d), lambda i, f, be, nu: (0, 0)),
                pl.BlockSpec((None, d, tf), lambda i, f, be, nu: (be[i], 0, ff_idx(i, f, be, nu))),
                pl.BlockSpec((None, d, tf), lambda i, f, be, nu: (be[i], 0, ff_idx(i, f, be, nu))),
                pl.BlockSpec((None, tf, d), lambda i, f, be, nu: (be[i], ff_idx(i, f, be, nu), 0)),
            ],
            out_specs=pl.BlockSpec((bm, d), lambda i, f, be, nu: (i, 0)),
            scratch_shapes=[pltpu.VMEM((bm, d), BF16), pltpu.VMEM((bm, d), F32)],
        ),
        out_shape=jax.ShapeDtypeStruct((r, d), F32),
        compiler_params=_cparams(("arbitrary", "arbitrary")),
        name="ffn",
    )(block_e, n_used, x, g.reshape(1, d).astype(F32), wg, wu, wd)


def _ple_kernel(h_ref, p_ref, wg_ref, wp_ref, g_ref, o_ref):
    h = h_ref[...]
    gate = jnp.dot(h.astype(BF16), wg_ref[...], preferred_element_type=F32)
    e = jnp.dot(p_ref[...].astype(BF16), wp_ref[...], preferred_element_type=F32)
    ms = jnp.mean(e * e, axis=-1, keepdims=True)
    e = e * lax.rsqrt(ms + NORM_EPS) * g_ref[...]
    o_ref[...] = h + jax.nn.sigmoid(gate) * e


def _ple(h, p, w_gate, w_proj, g, *, bm=512):
    m, d = h.shape
    bm = min(bm, m)
    pd = p.shape[1]
    return pl.pallas_call(
        _ple_kernel,
        grid=(m // bm,),
        in_specs=[pl.BlockSpec((bm, d), lambda i: (i, 0)), pl.BlockSpec((bm, pd), lambda i: (i, 0)),
                  pl.BlockSpec((d, d), lambda i: (0, 0)), pl.BlockSpec((pd, d), lambda i: (0, 0)),
                  pl.BlockSpec((1, d), lambda i: (0, 0))],
        out_specs=pl.BlockSpec((bm, d), lambda i: (i, 0)),
        out_shape=jax.ShapeDtypeStruct((m, d), F32),
        compiler_params=_cparams(("arbitrary",)),
        name="ple",
    )(h, p, w_gate.astype(BF16), w_proj.astype(BF16), g.reshape(1, d).astype(F32))


def _shift(z, b, s):
    z3 = z.reshape(b, s, -1)
    return jnp.pad(z3, ((0, 0), (1, 0), (0, 0)))[:, :-1].reshape(z.shape)


def _per_head(fn, x):
    t = x.shape[0]
    return fn(x.reshape(t, N_HEADS, HEAD_DIM)).reshape(t, WIDTH)


def _l2n(x, eps=1e-12):
    return x * lax.rsqrt(jnp.sum(x * x, axis=-1, keepdims=True) + eps)


def _rwkv_branch(z, u_vres, v_first, b, s, mu, w0, w_up, a0, a_up, g_up, k_k, k_a, r_k, ln_g, ln_b, vres):
    z = z + (_shift(z, b, s) - z) * mu
    r, k, v = z[:, :WIDTH], z[:, WIDTH:2 * WIDTH], z[:, 2 * WIDTH:3 * WIDTH]
    w_lo = z[:, 3 * WIDTH:3 * WIDTH + 64]
    a_lo = z[:, 3 * WIDTH + 64:3 * WIDTH + 128]
    g_lo = z[:, 3 * WIDTH + 128:]
    log_w = -jax.nn.softplus(-(w0 + _mm(jnp.tanh(w_lo), w_up))) - 0.5
    ld = -jnp.exp(log_w)
    iclr = jax.nn.sigmoid(a0 + _mm(a_lo, a_up))
    gate = _mm(jax.nn.sigmoid(g_lo), g_up)
    if vres is None:
        v_first = v
    else:
        v_up, v_bias = vres
        lo = u_vres[:, :32] + _shift(u_vres[:, 32:64], b, s)
        v = v + (v_first - v) * jax.nn.sigmoid(v_bias + _mm(lo, v_up))
    kk = _per_head(_l2n, k * k_k)
    k = k * (1.0 + (iclr - 1.0) * k_a)
    y = _chunk_scan(_rwkv_kernel, (r, k, v, -kk, kk * iclr, ld), b, s, "rwkv")

    def gn(t):
        mean = jnp.mean(t, axis=-1, keepdims=True)
        var = jnp.mean(jnp.square(t - mean), axis=-1, keepdims=True)
        return (t - mean) * lax.rsqrt(var + RWKV_GN_EPS)

    y = _per_head(gn, y) * ln_g + ln_b
    t = r.shape[0]
    bonus = jnp.sum((r * k).reshape(t, N_HEADS, HEAD_DIM) * r_k, axis=-1, keepdims=True) * v.reshape(t, N_HEADS, HEAD_DIM)
    y = y + bonus.reshape(t, WIDTH)
    return y * gate, v_first


def _mla_branch(z, cos, sin, b, s, q_norm_g, kv_norm_g, w_uq, w_ukv, qk_g_q, qk_g_k):
    t = z.shape[0]
    c_q, c_kv, k_pe = z[:, :Q_LORA], z[:, Q_LORA:Q_LORA + KV_LORA], z[:, Q_LORA + KV_LORA:Q_LORA + KV_LORA + QK_ROPE]
    q = _mm(c_q, w_uq, q_norm_g).reshape(t, N_HEADS, QK_DIM)
    kv = _mm(c_kv, w_ukv, kv_norm_g).reshape(t, N_HEADS, QK_NOPE + HEAD_DIM)
    k_nope, v = kv[..., :QK_NOPE], kv[..., QK_NOPE:]
    k = jnp.concatenate([k_nope, jnp.broadcast_to(k_pe[:, None, :], (t, N_HEADS, QK_ROPE))], axis=-1)

    def rms(x, g):
        return x * lax.rsqrt(jnp.mean(x * x, axis=-1, keepdims=True) + NORM_EPS) * g

    def rope(x):
        x1, x2 = x[..., :QK_ROPE // 2], x[..., QK_ROPE // 2:]
        return jnp.concatenate([x1 * cos - x2 * sin, x2 * cos + x1 * sin], axis=-1)

    q = rms(q, qk_g_q)
    k = rms(k, qk_g_k)
    q = jnp.concatenate([q[..., :QK_NOPE], rope(q[..., QK_NOPE:])], axis=-1) * (QK_DIM ** -0.5)
    k = jnp.concatenate([k[..., :QK_NOPE], rope(k[..., QK_NOPE:])], axis=-1)

    def to_bhsd(x):
        x = jnp.pad(x, ((0, 0), (0, 0), (0, LANES - x.shape[-1])))
        return x.reshape(b, s, N_HEADS, LANES).transpose(0, 2, 1, 3).astype(BF16)

    o = _flash_attention(to_bhsd(q), to_bhsd(k), to_bhsd(v))
    return o[..., :HEAD_DIM].transpose(0, 2, 1, 3).reshape(t, WIDTH)


def _gdn_branch(z, b, s, conv_w, a_log, dt_bias, norm_g):
    t = z.shape[0]
    nq = 3 * WIDTH
    qkv, b_logit, a_logit, gate = z[:, :nq], z[:, nq:nq + 8], z[:, nq + 8:nq + 16], z[:, nq + 16:nq + 16 + WIDTH]
    x3 = jnp.pad(qkv.reshape(b, s, nq), ((0, 0), (GDN_CONV - 1, 0), (0, 0)))
    conv = sum(x3[:, j:j + s] * conv_w[j] for j in range(GDN_CONV)).reshape(t, nq)
    qkv = jax.nn.silu(conv)
    q = _per_head(_l2n, qkv[:, :WIDTH]) * (HEAD_DIM ** -0.5)
    k = _per_head(_l2n, qkv[:, WIDTH:2 * WIDTH])
    v = qkv[:, 2 * WIDTH:]
    beta = jax.nn.sigmoid(b_logit)
    g = -jnp.exp(a_log) * jax.nn.softplus(a_logit + dt_bias)
    o = _chunk_scan(_gdn_kernel, (q, k, v, jnp.repeat(g, HEAD_DIM, axis=1), jnp.repeat(beta, HEAD_DIM, axis=1)),
                    b, s, "gdn")

    def rms(x):
        return x * lax.rsqrt(jnp.mean(x * x, axis=-1, keepdims=True) + NORM_EPS) * norm_g

    return _per_head(rms, o) * jax.nn.silu(gate)


def _moe(h, g, w_router, wg, wu, wd, *, bm=512, tf=512):
    t, d = h.shape
    n_assign = t * TOP_K
    n_blocks = -(-(n_assign + N_EXPERTS * (bm - 1)) // bm)
    logits = _mm(h, w_router, g, hi=True)
    top_logits, top_e = lax.top_k(logits, TOP_K)
    top_w = jax.nn.softmax(top_logits, axis=-1).reshape(-1)
    flat_e = top_e.reshape(-1)
    onehot = (flat_e[:, None] == jnp.arange(N_EXPERTS)[None, :]).astype(jnp.int32)
    rank = jnp.take_along_axis(jnp.cumsum(onehot, axis=0) - onehot, flat_e[:, None], axis=1)[:, 0]
    counts = jnp.sum(onehot, axis=0)
    padded = (counts + bm - 1) // bm * bm
    pad_end = jnp.cumsum(padded)
    dest = (pad_end - padded)[flat_e] + rank
    src_tok = jnp.zeros((n_blocks * bm,), jnp.int32).at[dest].set(jnp.arange(n_assign, dtype=jnp.int32) // TOP_K)
    block_e = jnp.minimum(jnp.searchsorted(pad_end, jnp.arange(n_blocks) * bm, side='right'),
                          N_EXPERTS - 1).astype(jnp.int32)
    n_used = (pad_end[-1] // bm).astype(jnp.int32).reshape(1)
    y_rows = _ffn(h[src_tok], g, wg, wu, wd, block_e, n_used, bm=bm, tf=tf)
    y = (y_rows[dest] * top_w[:, None]).reshape(t, TOP_K, d)
    return y[:, 0] + y[:, 1]


def kernel(x, p, positions, norm_mix_g, w_in, rwkv_mu, rwkv_w0, rwkv_w_up, rwkv_a0, rwkv_a_up, rwkv_g_up, rwkv_k_k, rwkv_k_a, rwkv_r_k, rwkv_ln_g, rwkv_ln_b, vres_mu, vres_down, vres_up, vres_b, mla_q_norm_g, mla_kv_norm_g, mla_w_uq, mla_w_ukv, mla_qk_norm_q, mla_qk_norm_k, gdn_conv_w, gdn_a_log, gdn_dt_bias, gdn_norm_g, w_br_rwkv, w_br_mla, w_br_gdn, w_out, norm_ffn_g, ffn_wg, ffn_wu, ffn_wd, moe_router, moe_wg, moe_wu, moe_wd, ple_proj, ple_gate, ple_norm_g):
    b, s, d = x.shape
    t = b * s
    depth = w_in.shape[0]
    rwkv_in = 3 * WIDTH + 256
    mla_in = Q_LORA + KV_LORA + QK_ROPE
    gdn_in = 3 * WIDTH + 16 + WIDTH
    inv_freq = 1.0 / (ROPE_THETA ** (jnp.arange(0, QK_ROPE, 2, dtype=F32) / QK_ROPE))
    ang = positions.astype(F32).reshape(t, 1, 1) * inv_freq
    cos, sin = jnp.cos(ang), jnp.sin(ang)

    h = x.reshape(t, d)
    v_first = None
    for i in range(depth):
        w = w_in[i]
        g = norm_mix_g[i]
        w_rwkv = w[:, :rwkv_in]
        u_vres = None
        if i > 0:
            mu_v = vres_mu[i - 1][:, None]
            vd = vres_down[i - 1]
            w_rwkv = jnp.concatenate([w_rwkv, (1.0 - mu_v) * vd, mu_v * vd], axis=1)
        z_rwkv = _mm(h, w_rwkv, g)
        if i > 0:
            z_rwkv, u_vres = z_rwkv[:, :rwkv_in], z_rwkv[:, rwkv_in:]
        z_mla = _mm(h, w[:, rwkv_in:rwkv_in + mla_in], g)
        z_gdn = _mm(h, w[:, rwkv_in + mla_in:rwkv_in + mla_in + gdn_in], g)
        z_gate = _mm(h, w[:, rwkv_in + mla_in + gdn_in:], g)
        vres = None if i == 0 else (vres_up[i - 1], vres_b[i - 1])
        o_a, v_first = _rwkv_branch(z_rwkv, u_vres, v_first, b, s, rwkv_mu[i], rwkv_w0[i], rwkv_w_up[i],
                                    rwkv_a0[i], rwkv_a_up[i], rwkv_g_up[i], rwkv_k_k[i], rwkv_k_a[i],
                                    rwkv_r_k[i], rwkv_ln_g[i], rwkv_ln_b[i], vres)
        o_b = _mla_branch(z_mla, cos, sin, b, s, mla_q_norm_g[i], mla_kv_norm_g[i], mla_w_uq[i], mla_w_ukv[i],
                          mla_qk_norm_q[i], mla_qk_norm_k[i])
        o_c = _gdn_branch(z_gdn, b, s, gdn_conv_w[i], gdn_a_log[i], gdn_dt_bias[i], gdn_norm_g[i])
        h = _merge(h, z_gate, o_a, o_b, o_c, w_br_rwkv[i], w_br_mla[i], w_br_gdn[i], w_out[i])
        if i % 2 == 0:
            j = i // 2
            ff = ffn_wg.shape[2]
            f = _ffn(h, norm_ffn_g[i], ffn_wg[j:j + 1].astype(BF16), ffn_wu[j:j + 1].astype(BF16),
                     ffn_wd[j:j + 1].astype(BF16), jnp.zeros((t // 512,), jnp.int32),
                     jnp.full((1,), t // 512, jnp.int32), bm=512, tf=ff // 2)
        else:
            j = i // 2
            f = _moe(h, norm_ffn_g[i], moe_router[j], moe_wg[j].astype(BF16), moe_wu[j].astype(BF16),
                     moe_wd[j].astype(BF16))
        h = h + f
        h = _ple(h, p[i].reshape(t, -1), ple_gate[i], ple_proj[i], ple_norm_g[i])
    return h.reshape(b, s, d)
```

```python
import functools
import math

import jax
import jax.numpy as jnp
import numpy as np
from jax import lax
from jax.experimental import pallas as pl
from jax.experimental.pallas import tpu as pltpu

F32 = jnp.float32
BF16 = jnp.bfloat16
HI = lax.Precision.HIGHEST

NORM_EPS = 1e-6
RWKV_GN_EPS = 64e-5
L2_EPS = 1e-12
N_HEADS = 8
HEAD_DIM = 64
WIDTH = N_HEADS * HEAD_DIM
RWKV_LORA = 256
RWKV_IN = 3 * WIDTH + RWKV_LORA
VRES_LORA = 32
QK_NOPE = 64
QK_ROPE = 32
QK_DIM = QK_NOPE + QK_ROPE
Q_LORA = 256
KV_LORA = 128
MLA_IN = Q_LORA + KV_LORA + QK_ROPE
ROPE_THETA = 10000.0
GDN_CONV = 4
GDN_QKV = 3 * WIDTH
GDN_IN = GDN_QKV + 2 * N_HEADS + WIDTH
N_EXPERTS = 8
TOP_K = 2
LANES = 128
SUBLANES = 8
CHUNK = 64
SCAN_ROWS = 2 * CHUNK
VMEM_LIMIT = 48 * 1024 * 1024


def _cparams(sem):
    return pltpu.CompilerParams(dimension_semantics=sem, vmem_limit_bytes=VMEM_LIMIT)


def _pick_bn(n, cap=2304):
    units = n // LANES
    best = 1
    for d in range(1, units + 1):
        if units % d == 0 and d * LANES <= cap:
            best = d
    return best * LANES


def _mm_kernel(*refs, norm, hi):
    if norm:
        x_ref, g_ref, w_ref, o_ref, xs_ref = refs
    else:
        x_ref, w_ref, o_ref, xs_ref = refs

    @pl.when(pl.program_id(1) == 0)
    def _():
        x = x_ref[...].astype(F32)
        if norm:
            ms = jnp.mean(x * x, axis=-1, keepdims=True)
            x = x * lax.rsqrt(ms + NORM_EPS) * g_ref[...]
        xs_ref[...] = x.astype(xs_ref.dtype)

    o_ref[...] = jnp.dot(xs_ref[...], w_ref[...], preferred_element_type=F32,
                         precision=HI if hi else None).astype(o_ref.dtype)


def _mm(x, w, g=None, *, bm=512, hi=False, out_dtype=F32):
    m, k = x.shape
    n = w.shape[1]
    bm = min(bm, m)
    bn = _pick_bn(n)
    wdt = F32 if hi else BF16
    norm = g is not None
    in_specs = [pl.BlockSpec((bm, k), lambda i, j: (i, 0))]
    args = [x]
    if norm:
        in_specs.append(pl.BlockSpec((1, k), lambda i, j: (0, 0)))
        args.append(g.reshape(1, k).astype(F32))
    in_specs.append(pl.BlockSpec((k, bn), lambda i, j: (0, j)))
    args.append(w.astype(wdt))
    return pl.pallas_call(
        functools.partial(_mm_kernel, norm=norm, hi=hi),
        grid=(m // bm, n // bn),
        in_specs=in_specs,
        out_specs=pl.BlockSpec((bm, bn), lambda i, j: (i, j)),
        out_shape=jax.ShapeDtypeStruct((m, n), out_dtype),
        scratch_shapes=[pltpu.VMEM((bm, k), wdt)],
        compiler_params=_cparams(("arbitrary", "arbitrary")),
        name="mm",
    )(*args)


def _rms(x, n, g):
    return x * lax.rsqrt(jnp.sum(x * x, axis=-1, keepdims=True) * (1.0 / n) + NORM_EPS) * g


def _mla_prep_kernel(z_ref, cos_ref, s1_ref, s2_ref, gq_ref, gkv_ref, gqh_ref, gkh_ref, wq_ref, wk_ref, wv_ref,
                     q_ref, k_ref, v_ref):
    z = z_ref[...]
    cq = _rms(z[:, :Q_LORA], Q_LORA, gq_ref[...]).astype(BF16)
    ckv = _rms(z[:, Q_LORA:Q_LORA + KV_LORA], KV_LORA, gkv_ref[...]).astype(BF16)
    k_pe = pltpu.roll(z[:, Q_LORA + KV_LORA:], QK_NOPE, 1)
    q_all = jnp.dot(cq, wq_ref[...], preferred_element_type=F32)
    k_all = jnp.dot(ckv, wk_ref[...], preferred_element_type=F32)
    v_all = jnp.dot(ckv, wv_ref[...], preferred_element_type=F32)
    cos, s1, s2 = cos_ref[...], s1_ref[...], s2_ref[...]

    def rope(x):
        return x * cos + pltpu.roll(x, LANES - QK_ROPE // 2, 1) * s1 + pltpu.roll(x, QK_ROPE // 2, 1) * s2

    for h in range(N_HEADS):
        sl = slice(h * LANES, (h + 1) * LANES)
        q = rope(_rms(q_all[:, sl], QK_DIM, gqh_ref[...])) * (QK_DIM ** -0.5)
        k = rope(_rms(k_all[:, sl] + k_pe, QK_DIM, gkh_ref[...]))
        q_ref[h] = q.astype(BF16)
        k_ref[h] = k.astype(BF16)
        v_ref[h] = v_all[:, sl].astype(BF16)


def _pad_heads(w, width):
    k = w.shape[0]
    w = w.reshape(k, N_HEADS, width)
    return jnp.pad(w, ((0, 0), (0, 0), (0, LANES - width))).reshape(k, N_HEADS * LANES)


def _mla_prep(z, rope_tabs, b, s, q_norm_g, kv_norm_g, w_uq, w_ukv, qk_g_q, qk_g_k, *, bm=512):
    bm = min(bm, s)
    w_ukv = w_ukv.reshape(KV_LORA, N_HEADS, QK_NOPE + HEAD_DIM)
    wq = _pad_heads(w_uq, QK_DIM).astype(BF16)
    wk = _pad_heads(w_ukv[..., :QK_NOPE].reshape(KV_LORA, -1), QK_NOPE).astype(BF16)
    wv = _pad_heads(w_ukv[..., QK_NOPE:].reshape(KV_LORA, -1), HEAD_DIM).astype(BF16)
    pad_g = lambda g: jnp.pad(g, (0, LANES - g.shape[0])).reshape(1, LANES)
    row = lambda width: pl.BlockSpec((bm, width), lambda bb, i: (bb * (s // bm) + i, 0))
    full = lambda a: pl.BlockSpec(a.shape, lambda bb, i: (0, 0))
    consts = [q_norm_g.reshape(1, -1), kv_norm_g.reshape(1, -1), pad_g(qk_g_q), pad_g(qk_g_k), wq, wk, wv]
    out_spec = pl.BlockSpec((None, N_HEADS, bm, LANES), lambda bb, i: (bb, 0, i, 0))
    out_shape = jax.ShapeDtypeStruct((b, N_HEADS, s, LANES), BF16)
    return pl.pallas_call(
        _mla_prep_kernel,
        grid=(b, s // bm),
        in_specs=[row(z.shape[1]), row(LANES), row(LANES), row(LANES)] + [full(a) for a in consts],
        out_specs=[out_spec] * 3,
        out_shape=[out_shape] * 3,
        compiler_params=_cparams(("arbitrary", "arbitrary")),
        name="mla_prep",
    )(z, *rope_tabs, *consts)


def _flash_kernel(qi_ref, kj_ref, q_ref, k_ref, v_ref, o_ref, m_ref, l_ref, acc_ref, *, blk):
    n = pl.program_id(2)
    i = qi_ref[n]
    j = kj_ref[n]

    @pl.when(j == 0)
    def _():
        m_ref[...] = jnp.full_like(m_ref, -jnp.inf)
        l_ref[...] = jnp.zeros_like(l_ref)
        acc_ref[...] = jnp.zeros_like(acc_ref)

    def update(masked):
        s = lax.dot_general(q_ref[...], k_ref[...], (((1,), (1,)), ((), ())), preferred_element_type=F32)
        if masked:
            row = lax.broadcasted_iota(jnp.int32, (blk, blk), 0)
            col = lax.broadcasted_iota(jnp.int32, (blk, blk), 1)
            s = jnp.where(row >= col, s, -jnp.inf)
        m_old = m_ref[...]
        m_new = jnp.maximum(m_old, jnp.max(s, axis=-1, keepdims=True))
        alpha = jnp.exp(m_old - m_new)
        p = jnp.exp(s - m_new[:, :1])
        l_ref[...] = alpha * l_ref[...] + jnp.sum(p, axis=-1, keepdims=True)
        acc_ref[...] = alpha * acc_ref[...] + jnp.dot(p.astype(BF16), v_ref[...], preferred_element_type=F32)
        m_ref[...] = m_new

    @pl.when(j < i)
    def _():
        update(False)

    @pl.when(j == i)
    def _():
        update(True)
        o_ref[...] = (acc_ref[...] / l_ref[...]).astype(o_ref.dtype)


def _flash_attention(q, k, v, *, blk=1024):
    b, h, s, d = q.shape
    blk = min(blk, s)
    nb = s // blk
    qi = np.array([i for i in range(nb) for j in range(i + 1)], np.int32)
    kj = np.array([j for i in range(nb) for j in range(i + 1)], np.int32)
    q_spec = pl.BlockSpec((None, None, blk, d), lambda bb, hh, n, qi_r, kj_r: (bb, hh, qi_r[n], 0))
    kv_spec = pl.BlockSpec((None, None, blk, d), lambda bb, hh, n, qi_r, kj_r: (bb, hh, kj_r[n], 0))
    return pl.pallas_call(
        functools.partial(_flash_kernel, blk=blk),
        grid_spec=pltpu.PrefetchScalarGridSpec(
            num_scalar_prefetch=2,
            grid=(b, h, len(qi)),
            in_specs=[q_spec, kv_spec, kv_spec],
            out_specs=q_spec,
            scratch_shapes=[pltpu.VMEM((blk, LANES), F32), pltpu.VMEM((blk, LANES), F32), pltpu.VMEM((blk, d), F32)],
        ),
        out_shape=jax.ShapeDtypeStruct((b, h, s, d), BF16),
        compiler_params=_cparams(("arbitrary", "arbitrary", "arbitrary")),
        name="flash",
    )(jnp.asarray(qi), jnp.asarray(kj), q, k, v)


def _dot(a, b):
    return jnp.dot(a.astype(BF16), b.astype(BF16), preferred_element_type=F32)


def _dot_nt(a, b):
    return lax.dot_general(a.astype(BF16), b.astype(BF16), (((1,), (1,)), ((), ())), preferred_element_type=F32)


def _dot_tn(a, b):
    return jnp.dot(a.T.astype(BF16), b.astype(BF16), preferred_element_type=F32)


def _split3(x):
    hi = x.astype(BF16)
    r1 = x - hi.astype(F32)
    mid = r1.astype(BF16)
    return hi, mid, (r1 - mid.astype(F32)).astype(BF16)


def _cumsum_rows(tri, x):
    s = jnp.dot(tri, jnp.concatenate(_split3(x), axis=1), preferred_element_type=F32)
    n = x.shape[1]
    return s[:, :n] + s[:, n:2 * n] + s[:, 2 * n:]


def _head_sum(x):
    blk = (lax.broadcasted_iota(jnp.int32, (LANES, LANES), 0) // HEAD_DIM
           == lax.broadcasted_iota(jnp.int32, (LANES, LANES), 1) // HEAD_DIM).astype(BF16)
    rows = x.shape[0]
    outs = []
    for p in range(WIDTH // LANES):
        hi, mid, _ = _split3(x[:, p * LANES:(p + 1) * LANES])
        s = jnp.dot(jnp.concatenate([hi, mid], axis=0), blk, preferred_element_type=F32)
        outs.append(s[:rows] + s[rows:])
    return jnp.concatenate(outs, axis=1)


def _shift_rows(x, tail, j):
    rolled = pltpu.roll(x, j, 0)
    top = jnp.where(lax.broadcasted_iota(jnp.int32, (SUBLANES, x.shape[1]), 0) < j,
                    pltpu.roll(tail, j, 0), rolled[:SUBLANES])
    return jnp.concatenate([top, rolled[SUBLANES:]], axis=0)


def _softplus(x):
    return jnp.maximum(x, 0.0) + jnp.log(1.0 + jnp.exp(-jnp.abs(x)))


def _pair_masks():
    lane = lax.broadcasted_iota(jnp.int32, (CHUNK, LANES), 1)
    m0 = (lane < HEAD_DIM).astype(F32)
    return m0, 1.0 - m0


def _ext(x, m0, m1, dtype=BF16):
    return jnp.concatenate([x * m0, x * m1], axis=0).astype(dtype)


def _cat(xs, axis):
    return jnp.concatenate(xs, axis=axis)


def _neumann_inverse(mats):
    n = mats[0].shape[0]
    eye = (lax.broadcasted_iota(jnp.int32, (n, n), 0) == lax.broadcasted_iota(jnp.int32, (n, n), 1)).astype(F32)
    ts = [eye + a for a in mats]
    ps = [_dot(a, a) for a in mats]
    for _ in range(int(math.log2(CHUNK)) - 2):
        pps = [_dot(_cat([p, t], 0), p) for p, t in zip(ps, ts)]
        ps = [pp[:n] for pp in pps]
        ts = [t + pp[n:] for t, pp in zip(ts, pps)]
    return [t + _dot(t, p) for t, p in zip(ts, ps)]


def _work_items(rows):
    return [(slice(c * CHUNK, (c + 1) * CHUNK), slice(p * LANES, (p + 1) * LANES))
            for c in range(rows // CHUNK) for p in range(WIDTH // LANES)]


def _tiles(x, items):
    return [x[rows, cols] for rows, cols in items]


def _rwkv_scan(r, k, v, a, b, ld, st_ref, y_ref):
    c2 = 2 * CHUNK
    strict = lax.broadcasted_iota(jnp.int32, (c2, c2), 0) > lax.broadcasted_iota(jnp.int32, (c2, c2), 1)
    trow = lax.broadcasted_iota(jnp.int32, (CHUNK, c2), 0)
    tcol = lax.broadcasted_iota(jnp.int32, (CHUNK, c2), 1) & (CHUNK - 1)
    incl_w = tcol <= trow
    tri = (lax.broadcasted_iota(jnp.int32, (CHUNK, CHUNK), 0)
           >= lax.broadcasted_iota(jnp.int32, (CHUNK, CHUNK), 1)).astype(BF16)
    m0, m1 = _pair_masks()
    items = _work_items(r.shape[0])
    n_pairs = WIDTH // LANES
    r, k, v, a, b, ld = (_tiles(x, items) for x in (r, k, v, a, b, ld))
    cum = [_cumsum_rows(tri, x) for x in ld]
    e_pos = [jnp.exp(x) for x in cum]
    e_neg = [jnp.exp(-x) for x in cum]
    e_end = [jnp.exp(x[CHUNK - 1:CHUNK, :] - x) for x in cum]
    a_ext = [_ext(ai * jnp.exp(ci - li), m0, m1) for ai, ci, li in zip(a, cum, ld)]
    v_ext = [_ext(x, m0, m1) for x in v]
    r_dec = [(ri * ei).astype(BF16) for ri, ei in zip(r, e_pos)]
    s1 = [_dot_nt(_cat([ae, rd], 0), _cat([_ext(bi * en, m0, m1), _ext(ki * en, m0, m1)], 0))
          for ae, rd, bi, ki, en in zip(a_ext, r_dec, b, k, e_neg)]
    a_ab = [jnp.where(strict, s[:c2, :c2], 0.0) for s in s1]
    a_ak = [jnp.where(strict, s[:c2, c2:], 0.0) for s in s1]
    a_rbk = [_cat([jnp.where(incl_w, s[c2:, :c2], 0.0), jnp.where(incl_w, s[c2:, c2:], 0.0)], 1).astype(BF16)
             for s in s1]
    akv = [_dot(x, ve) for x, ve in zip(a_ak, v_ext)]
    t_inv = _neumann_inverse(a_ab)
    wu0 = [_dot(t, _cat([ae, av.astype(BF16)], 1)) for t, ae, av in zip(t_inv, a_ext, akv)]
    bk_end = [_cat([_ext(bi * ee, m0, m1, F32), _ext(ki * ee, m0, m1, F32)], 0) for bi, ki, ee in zip(b, k, e_end)]
    e_last_t = [jnp.broadcast_to(e[CHUNK - 1:CHUNK, :], (LANES, LANES)).T for e in e_pos]
    st = [st_ref[p] for p in range(n_pairs)]
    for c in range(len(items) // n_pairs):
        idx = range(c * n_pairs, (c + 1) * n_pairs)
        z = [_dot(_cat([wu0[i][:, :LANES].astype(BF16), r_dec[i]], 0), s) for i, s in zip(idx, st)]
        uv = [_cat([(wu0[i][:, LANES:] + zi[:c2]).astype(BF16), v_ext[i]], 0) for i, zi in zip(idx, z)]
        for i, zi, uvi in zip(idx, z, uv):
            rows, cols = items[i]
            y_ref[rows, cols] = zi[c2:] + _dot(a_rbk[i], uvi)
        st = [s * e_last_t[i] + _dot_tn(bk_end[i], uvi) for i, s, uvi in zip(idx, st, uv)]
    for p in range(n_pairs):
        st_ref[p] = st[p]


def _gdn_scan(q, k, v, g, beta, st_ref, o_ref):
    c2 = 2 * CHUNK
    row = lax.broadcasted_iota(jnp.int32, (c2, c2), 0)
    col = lax.broadcasted_iota(jnp.int32, (c2, c2), 1)
    causal = ((row >= CHUNK) == (col >= CHUNK)) & (row >= col)
    strict = row > col
    tri = (lax.broadcasted_iota(jnp.int32, (CHUNK, CHUNK), 0)
           >= lax.broadcasted_iota(jnp.int32, (CHUNK, CHUNK), 1)).astype(BF16)
    m0, m1 = _pair_masks()
    items = _work_items(q.shape[0])
    n_pairs = WIDTH // LANES
    q, k, v, g, beta = (_tiles(x, items) for x in (q, k, v, g, beta))
    gc = [_cumsum_rows(tri, x) for x in g]
    g_rows = [_cat([jnp.broadcast_to(x[:, 0:1], (CHUNK, LANES)),
                    jnp.broadcast_to(x[:, HEAD_DIM:HEAD_DIM + 1], (CHUNK, LANES))], 0) for x in gc]
    gamma = [jnp.exp(jnp.where(causal, x - x.T, -jnp.inf)) for x in g_rows]
    e_pos = [jnp.exp(x) for x in gc]
    kb = [ki * bi for ki, bi in zip(k, beta)]
    s1 = [_dot_nt(_cat([_ext(kbi, m0, m1), qi.astype(BF16)], 0), _ext(ki, m0, m1)) for kbi, qi, ki in zip(kb, q, k)]
    a_neg = [-jnp.where(strict, s[:c2] * ga, 0.0) for s, ga in zip(s1, gamma)]
    a_in = [(s[c2:] * (ga[:CHUNK] + ga[CHUNK:])).astype(BF16) for s, ga in zip(s1, gamma)]
    t_inv = _neumann_inverse(a_neg)
    uw = [_dot(t, _cat([_ext(vi * bi, m0, m1), _ext(kbi * ei, m0, m1)], 1))
          for t, vi, bi, kbi, ei in zip(t_inv, v, beta, kb, e_pos)]
    q_dec = [(qi * ei).astype(BF16) for qi, ei in zip(q, e_pos)]
    k_dec = [_ext(ki * jnp.exp(x[CHUNK - 1:CHUNK, :] - x), m0, m1, F32) for ki, x in zip(k, gc)]
    st = [st_ref[p] for p in range(n_pairs)]
    for c in range(len(items) // n_pairs):
        idx = range(c * n_pairs, (c + 1) * n_pairs)
        z = [_dot(_cat([uw[i][:, LANES:].astype(BF16), q_dec[i]], 0), s) for i, s in zip(idx, st)]
        v_new = [(uw[i][:, :LANES] - zi[:c2]).astype(BF16) for i, zi in zip(idx, z)]
        for i, zi, vn in zip(idx, z, v_new):
            rows, cols = items[i]
            o_ref[rows, cols] = zi[c2:] + _dot(a_in[i], vn)
        st = [s * e_pos[i][CHUNK - 1:CHUNK, :] + _dot_tn(k_dec[i], vn) for i, s, vn in zip(idx, st, v_new)]
    for p in range(n_pairs):
        st_ref[p] = st[p]


def _rwkv_kernel(*refs, vres):
    if vres:
        z_ref, vf_ref, mu_ref, prm_ref, wup_ref, aup_ref, gup_ref, vup_ref, o_ref, st_ref, tail_ref, y_ref = refs
    else:
        z_ref, mu_ref, prm_ref, wup_ref, aup_ref, gup_ref, o_ref, vf_out_ref, st_ref, tail_ref, y_ref = refs

    @pl.when(pl.program_id(1) == 0)
    def _():
        st_ref[...] = jnp.zeros_like(st_ref)
        tail_ref[...] = jnp.zeros_like(tail_ref)

    z = z_ref[...]
    rows = z.shape[0]
    zs = _shift_rows(z, tail_ref[...], 1)
    tail_ref[...] = z[rows - SUBLANES:]
    zl = z[:, :RWKV_IN] + (zs[:, :RWKV_IN] - z[:, :RWKV_IN]) * mu_ref[...]
    r, k, v = zl[:, :WIDTH], zl[:, WIDTH:2 * WIDTH], zl[:, 2 * WIDTH:3 * WIDTH]
    lo = zl[:, 3 * WIDTH:3 * WIDTH + LANES]
    g_lo = zl[:, 3 * WIDTH + LANES:]
    w0, a0, k_k, k_a, ln_g, ln_b, r_k, v_bias = (prm_ref[i:i + 1, :] for i in range(8))
    log_w = -_softplus(-(w0 + _dot(jnp.tanh(lo), wup_ref[...]))) - 0.5
    ld = -jnp.exp(log_w)
    iclr = jax.nn.sigmoid(a0 + _dot(lo, aup_ref[...]))
    gate = _dot(jax.nn.sigmoid(g_lo), gup_ref[...])
    if vres:
        x = z[:, RWKV_IN:] + pltpu.roll(zs[:, RWKV_IN:], LANES - VRES_LORA, 1)
        v = v + (vf_ref[...] - v) * jax.nn.sigmoid(v_bias + _dot(x, vup_ref[...]))
    else:
        vf_out_ref[...] = v
    kk = k * k_k
    kk = kk * lax.rsqrt(_head_sum(kk * kk) + L2_EPS)
    k = k * (1.0 + (iclr - 1.0) * k_a)
    _rwkv_scan(r, k, v, -kk, kk * iclr, ld, st_ref, y_ref)
    y = y_ref[...]
    d = y - _head_sum(y) * (1.0 / HEAD_DIM)
    y = d * lax.rsqrt(_head_sum(d * d) * (1.0 / HEAD_DIM) + RWKV_GN_EPS) * ln_g + ln_b
    y = y + _head_sum(r * k * r_k) * v
    o_ref[...] = (y * gate).astype(o_ref.dtype)


def _rwkv(z, v_first, b, s, mu, prm, w_up, a_up, g_up, v_up):
    vres = v_first is not None
    rows = min(SCAN_ROWS, s)
    zw = z.shape[1]
    row = lambda width: pl.BlockSpec((rows, width), lambda bb, c: (bb * (s // rows) + c, 0))
    full = lambda a: pl.BlockSpec(a.shape, lambda bb, c: (0, 0))
    zero = jnp.zeros((LANES // 2, WIDTH), F32)
    consts = [mu.reshape(1, -1), prm, _cat([w_up, zero], 0).astype(BF16), _cat([zero, a_up], 0).astype(BF16),
              g_up.astype(BF16)]
    args, in_specs = [z], [row(zw)]
    if vres:
        args.append(v_first)
        in_specs.append(row(WIDTH))
        consts.append(jnp.pad(v_up, ((0, LANES - VRES_LORA), (0, 0))).astype(BF16))
    out_shape = [jax.ShapeDtypeStruct((b * s, WIDTH), BF16)]
    out_specs = [row(WIDTH)]
    if not vres:
        out_shape.append(jax.ShapeDtypeStruct((b * s, WIDTH), F32))
        out_specs.append(row(WIDTH))
    outs = pl.pallas_call(
        functools.partial(_rwkv_kernel, vres=vres),
        grid=(b, s // rows),
        in_specs=in_specs + [full(a) for a in consts],
        out_specs=out_specs,
        out_shape=out_shape,
        scratch_shapes=[pltpu.VMEM((WIDTH // LANES, LANES, LANES), F32), pltpu.VMEM((SUBLANES, zw), F32),
                        pltpu.VMEM((rows, WIDTH), F32)],
        compiler_params=_cparams(("arbitrary", "arbitrary")),
        name="rwkv",
    )(*args, *consts)
    return (outs[0], v_first) if vres else (outs[0], outs[1])


def _gdn_kernel(z_ref, cw_ref, prm_ref, exp_ref, o_ref, st_ref, tail_ref, y_ref):
    @pl.when(pl.program_id(1) == 0)
    def _():
        st_ref[...] = jnp.zeros_like(st_ref)
        tail_ref[...] = jnp.zeros_like(tail_ref)

    rows = z_ref.shape[0]
    x = z_ref[:, :GDN_QKV]
    tail = tail_ref[...]
    conv = x * cw_ref[GDN_CONV - 1:GDN_CONV, :]
    for j in range(1, GDN_CONV):
        conv = conv + _shift_rows(x, tail, j) * cw_ref[GDN_CONV - 1 - j:GDN_CONV - j, :]
    tail_ref[...] = x[rows - SUBLANES:]
    qkv = conv * jax.nn.sigmoid(conv)
    q, k, v = qkv[:, :WIDTH], qkv[:, WIDTH:2 * WIDTH], qkv[:, 2 * WIDTH:]
    q = q * lax.rsqrt(_head_sum(q * q) + L2_EPS) * (HEAD_DIM ** -0.5)
    k = k * lax.rsqrt(_head_sum(k * k) + L2_EPS)
    logits = jnp.dot(_cat(_split3(z_ref[:, GDN_QKV + WIDTH:]), 0), exp_ref[...], preferred_element_type=F32)
    logits = logits[:rows] + logits[rows:2 * rows] + logits[2 * rows:]
    neg_a, dt_bias, norm_g = (prm_ref[i:i + 1, :] for i in range(3))
    beta = jax.nn.sigmoid(logits[:, :WIDTH])
    g = neg_a * _softplus(logits[:, WIDTH:] + dt_bias)
    _gdn_scan(q, k, v, g, beta, st_ref, y_ref)
    o = y_ref[...]
    o = o * lax.rsqrt(_head_sum(o * o) * (1.0 / HEAD_DIM) + NORM_EPS) * norm_g
    gate = z_ref[:, GDN_QKV:GDN_QKV + WIDTH]
    o_ref[...] = (o * (gate * jax.nn.sigmoid(gate))).astype(o_ref.dtype)


def _gdn(z, b, s, conv_w, a_log, dt_bias, norm_g):
    rows = min(SCAN_ROWS, s)
    zw = z.shape[1]
    row = lambda width: pl.BlockSpec((rows, width), lambda bb, c: (bb * (s // rows) + c, 0))
    full = lambda a: pl.BlockSpec(a.shape, lambda bb, c: (0, 0))
    per_lane = lambda t: jnp.repeat(t, HEAD_DIM)
    prm = jnp.stack([per_lane(-jnp.exp(a_log)), per_lane(dt_bias), jnp.tile(norm_g, N_HEADS)])
    head_of_lane = np.arange(WIDTH) // HEAD_DIM
    expand = np.zeros((LANES, 2 * WIDTH), np.float32)
    expand[head_of_lane, np.arange(WIDTH)] = 1.0
    expand[N_HEADS + head_of_lane, WIDTH + np.arange(WIDTH)] = 1.0
    consts = [conv_w, prm, jnp.asarray(expand, BF16)]
    return pl.pallas_call(
        _gdn_kernel,
        grid=(b, s // rows),
        in_specs=[row(zw)] + [full(a) for a in consts],
        out_specs=row(WIDTH),
        out_shape=jax.ShapeDtypeStruct((b * s, WIDTH), BF16),
        scratch_shapes=[pltpu.VMEM((WIDTH // LANES, LANES, LANES), F32), pltpu.VMEM((SUBLANES, GDN_QKV), F32),
                        pltpu.VMEM((rows, WIDTH), F32)],
        compiler_params=_cparams(("arbitrary", "arbitrary")),
        name="gdn",
    )(z, *consts)


def _merge_kernel(h_ref, zg_ref, oa_ref, ob_ref, oc_ref, wa_ref, wb_ref, wc_ref, wo_ref, out_ref):
    d = h_ref.shape[1]
    ob = jnp.concatenate([ob_ref[h] for h in range(N_HEADS)], axis=1)
    merged = jnp.zeros(h_ref.shape, F32)
    for n, (o, w_ref) in enumerate(((oa_ref[...], wa_ref), (ob, wb_ref), (oc_ref[...], wc_ref))):
        proj = jnp.dot(o, w_ref[...], preferred_element_type=F32)
        merged = merged + jax.nn.sigmoid(zg_ref[:, n * d:(n + 1) * d]) * proj
    out_ref[...] = h_ref[...] + jnp.dot(merged.astype(BF16), wo_ref[...], preferred_element_type=F32)


def _merge(h, zg, oa, ob, oc, wa, wb, wc, wo, s, *, bm=512):
    m, d = h.shape
    bm = min(bm, s)
    row = lambda width: pl.BlockSpec((bm, width), lambda i: (i, 0))
    full = lambda w: pl.BlockSpec(w.shape, lambda i: (0, 0))
    wb = jnp.pad(wb.reshape(N_HEADS, HEAD_DIM, d), ((0, 0), (0, LANES - HEAD_DIM), (0, 0))).reshape(N_HEADS * LANES, d)
    ws = [w.astype(BF16) for w in (wa, wb, wc, wo)]
    ob_spec = pl.BlockSpec((None, N_HEADS, bm, LANES), lambda i: (i // (s // bm), 0, i % (s // bm), 0))
    return pl.pallas_call(
        _merge_kernel,
        grid=(m // bm,),
        in_specs=[row(d), row(3 * d), row(WIDTH), ob_spec, row(WIDTH)] + [full(w) for w in ws],
        out_specs=row(d),
        out_shape=jax.ShapeDtypeStruct((m, d), F32),
        compiler_params=_cparams(("arbitrary",)),
        name="merge",
    )(h, zg, oa, ob, oc, *ws)


def _ffn_kernel(be_ref, nu_ref, x_ref, g_ref, wg_ref, wu_ref, wd_ref, o_ref, xs_ref, acc_ref, *, nf):
    i = pl.program_id(0)
    f = pl.program_id(1)
    used = i < nu_ref[0]

    @pl.when(used & (f == 0))
    def _():
        x = x_ref[...]
        ms = jnp.mean(x * x, axis=-1, keepdims=True)
        xs_ref[...] = (x * lax.rsqrt(ms + NORM_EPS) * g_ref[...]).astype(BF16)

    @pl.when(f == 0)
    def _():
        acc_ref[...] = jnp.zeros_like(acc_ref)

    @pl.when(used)
    def _():
        xs = xs_ref[...]
        gate = jnp.dot(xs, wg_ref[...], preferred_element_type=F32)
        up = jnp.dot(xs, wu_ref[...], preferred_element_type=F32)
        act = (jax.nn.silu(gate) * up).astype(BF16)
        acc_ref[...] += jnp.dot(act, wd_ref[...], preferred_element_type=F32)

    @pl.when(f == nf - 1)
    def _():
        o_ref[...] = acc_ref[...]


def _ffn(x, g, wg, wu, wd, block_e, n_used, *, bm, tf):
    r, d = x.shape
    ff = wg.shape[2]
    nf = ff // tf

    def ff_idx(i, f, be, nu):
        return jnp.where(i < nu[0], f, nf - 1)

    return pl.pallas_call(
        functools.partial(_ffn_kernel, nf=nf),
        grid_spec=pltpu.PrefetchScalarGridSpec(
            num_scalar_prefetch=2,
            grid=(r // bm, nf),
            in_specs=[
                pl.BlockSpec((bm, d), lambda i, f, be, nu: (i, 0)),
                pl.BlockSpec((1, d), lambda i, f, be, nu: (0, 0)),
                pl.BlockSpec((None, d, tf), lambda i, f, be, nu: (be[i], 0, ff_idx(i, f, be, nu))),
                pl.BlockSpec((None, d, tf), lambda i, f, be, nu: (be[i], 0, ff_idx(i, f, be, nu))),
                pl.BlockSpec((None, tf, d), lambda i, f, be, nu: (be[i], ff_idx(i, f, be, nu), 0)),
            ],
            out_specs=pl.BlockSpec((bm, d), lambda i, f, be, nu: (i, 0)),
            scratch_shapes=[pltpu.VMEM((bm, d), BF16), pltpu.VMEM((bm, d), F32)],
        ),
        out_shape=jax.ShapeDtypeStruct((r, d), F32),
        compiler_params=_cparams(("arbitrary", "arbitrary")),
        name="ffn",
    )(block_e, n_used, x, g.reshape(1, d).astype(F32), wg, wu, wd)


def _ple_kernel(h_ref, p_ref, wg_ref, wp_ref, g_ref, o_ref):
    h = h_ref[...]
    gate = jnp.dot(h.astype(BF16), wg_ref[...], preferred_element_type=F32)
    e = jnp.dot(p_ref[...].astype(BF16), wp_ref[...], preferred_element_type=F32)
    ms = jnp.mean(e * e, axis=-1, keepdims=True)
    e = e * lax.rsqrt(ms + NORM_EPS) * g_ref[...]
    o_ref[...] = h + jax.nn.sigmoid(gate) * e


def _ple(h, p, w_gate, w_proj, g, *, bm=512):
    m, d = h.shape
    bm = min(bm, m)
    pd = p.shape[1]
    return pl.pallas_call(
        _ple_kernel,
        grid=(m // bm,),
        in_specs=[pl.BlockSpec((bm, d), lambda i: (i, 0)), pl.BlockSpec((bm, pd), lambda i: (i, 0)),
                  pl.BlockSpec((d, d), lambda i: (0, 0)), pl.BlockSpec((pd, d), lambda i: (0, 0)),
                  pl.BlockSpec((1, d), lambda i: (0, 0))],
        out_specs=pl.BlockSpec((bm, d), lambda i: (i, 0)),
        out_shape=jax.ShapeDtypeStruct((m, d), F32),
        compiler_params=_cparams(("arbitrary",)),
        name="ple",
    )(h, p, w_gate.astype(BF16), w_proj.astype(BF16), g.reshape(1, d).astype(F32))


def _pad_cols(w, n):
    return jnp.pad(w, ((0, 0), (0, n - w.shape[1])))


def _rope_tables(positions):
    t = positions.size
    inv_freq = 1.0 / (ROPE_THETA ** (jnp.arange(0, QK_ROPE, 2, dtype=F32) / QK_ROPE))
    ang = positions.astype(F32).reshape(t, 1) * inv_freq
    cos, sin = jnp.cos(ang), jnp.sin(ang)
    half = QK_ROPE // 2
    ones = jnp.ones((t, QK_NOPE), F32)
    zeros = jnp.zeros((t, QK_NOPE), F32)
    z16 = jnp.zeros((t, half), F32)
    tail1 = jnp.ones((t, LANES - QK_DIM), F32)
    tail0 = jnp.zeros((t, LANES - QK_DIM), F32)
    return (_cat([ones, cos, cos, tail1], 1), _cat([zeros, -sin, z16, tail0], 1), _cat([zeros, z16, sin, tail0], 1))


def _moe(h, g, w_router, wg, wu, wd, *, bm=512, tf=512):
    t, d = h.shape
    n_assign = t * TOP_K
    n_blocks = -(-(n_assign + N_EXPERTS * (bm - 1)) // bm)
    logits = _mm(h, _pad_cols(w_router, LANES), g, hi=True)[:, :N_EXPERTS]
    top_logits, top_e = lax.top_k(logits, TOP_K)
    top_w = jax.nn.softmax(top_logits, axis=-1).reshape(-1)
    flat_e = top_e.reshape(-1)
    onehot = (flat_e[:, None] == jnp.arange(N_EXPERTS)[None, :]).astype(jnp.int32)
    rank = jnp.take_along_axis(jnp.cumsum(onehot, axis=0) - onehot, flat_e[:, None], axis=1)[:, 0]
    counts = jnp.sum(onehot, axis=0)
    padded = (counts + bm - 1) // bm * bm
    pad_end = jnp.cumsum(padded)
    dest = (pad_end - padded)[flat_e] + rank
    src_tok = jnp.zeros((n_blocks * bm,), jnp.int32).at[dest].set(jnp.arange(n_assign, dtype=jnp.int32) // TOP_K)
    block_e = jnp.minimum(jnp.searchsorted(pad_end, jnp.arange(n_blocks) * bm, side='right'),
                          N_EXPERTS - 1).astype(jnp.int32)
    n_used = (pad_end[-1] // bm).astype(jnp.int32).reshape(1)
    y_rows = _ffn(h[src_tok], g, wg, wu, wd, block_e, n_used, bm=bm, tf=tf)
    y = (y_rows[dest] * top_w[:, None]).reshape(t, TOP_K, d)
    return y[:, 0] + y[:, 1]


def kernel(x, p, positions, norm_mix_g, w_in, rwkv_mu, rwkv_w0, rwkv_w_up, rwkv_a0, rwkv_a_up, rwkv_g_up, rwkv_k_k, rwkv_k_a, rwkv_r_k, rwkv_ln_g, rwkv_ln_b, vres_mu, vres_down, vres_up, vres_b, mla_q_norm_g, mla_kv_norm_g, mla_w_uq, mla_w_ukv, mla_qk_norm_q, mla_qk_norm_k, gdn_conv_w, gdn_a_log, gdn_dt_bias, gdn_norm_g, w_br_rwkv, w_br_mla, w_br_gdn, w_out, norm_ffn_g, ffn_wg, ffn_wu, ffn_wd, moe_router, moe_wg, moe_wu, moe_wd, ple_proj, ple_gate, ple_norm_g):
    b, s, d = x.shape
    t = b * s
    depth = w_in.shape[0]
    rope_tabs = _rope_tables(positions)

    h = x.reshape(t, d)
    v_first = None
    for i in range(depth):
        w = w_in[i]
        g = norm_mix_g[i]
        w_rwkv = w[:, :RWKV_IN]
        if i > 0:
            mu_v = vres_mu[i - 1][:, None]
            vd = vres_down[i - 1]
            w_rwkv = _pad_cols(_cat([w_rwkv, (1.0 - mu_v) * vd, mu_v * vd], 1), RWKV_IN + LANES)
        w_gdn = w[:, RWKV_IN + MLA_IN:RWKV_IN + MLA_IN + GDN_IN]
        w_gdn = _pad_cols(_cat([w_gdn[:, :GDN_QKV], w_gdn[:, GDN_QKV + 2 * N_HEADS:],
                                w_gdn[:, GDN_QKV:GDN_QKV + 2 * N_HEADS]], 1), GDN_QKV + WIDTH + LANES)
        z_rwkv = _mm(h, w_rwkv, g)
        z_mla = _mm(h, _pad_cols(w[:, RWKV_IN:RWKV_IN + MLA_IN], Q_LORA + KV_LORA + LANES), g)
        z_gdn = _mm(h, w_gdn, g)
        z_gate = _mm(h, w[:, RWKV_IN + MLA_IN + GDN_IN:], g)
        prm = jnp.stack([rwkv_w0[i], rwkv_a0[i], rwkv_k_k[i], rwkv_k_a[i], rwkv_ln_g[i], rwkv_ln_b[i],
                         rwkv_r_k[i].reshape(-1), vres_b[i - 1] if i > 0 else jnp.zeros((WIDTH,), F32)])
        o_a, v_first = _rwkv(z_rwkv, v_first, b, s, rwkv_mu[i], prm, rwkv_w_up[i], rwkv_a_up[i], rwkv_g_up[i],
                             vres_up[i - 1] if i > 0 else None)
        q, k, v = _mla_prep(z_mla, rope_tabs, b, s, mla_q_norm_g[i], mla_kv_norm_g[i], mla_w_uq[i], mla_w_ukv[i],
                            mla_qk_norm_q[i], mla_qk_norm_k[i])
        o_b = _flash_attention(q, k, v)
        o_c = _gdn(z_gdn, b, s, gdn_conv_w[i], gdn_a_log[i], gdn_dt_bias[i], gdn_norm_g[i])
        h = _merge(h, z_gate, o_a, o_b, o_c, w_br_rwkv[i], w_br_mla[i], w_br_gdn[i], w_out[i], s)
        if i % 2 == 0:
            j = i // 2
            ff = ffn_wg.shape[2]
            f = _ffn(h, norm_ffn_g[i], ffn_wg[j:j + 1].astype(BF16), ffn_wu[j:j + 1].astype(BF16),
                     ffn_wd[j:j + 1].astype(BF16), jnp.zeros((t // 512,), jnp.int32),
                     jnp.full((1,), t // 512, jnp.int32), bm=512, tf=ff // 2)
        else:
            j = i // 2
            f = _moe(h, norm_ffn_g[i], moe_router[j], moe_wg[j].astype(BF16), moe_wu[j].astype(BF16),
                     moe_wd[j].astype(BF16))
        h = h + f
        h = _ple(h, p[i].reshape(t, -1), ple_gate[i], ple_proj[i], ple_norm_g[i])
    return h.reshape(b, s, d)
```

```python
import functools
import math

import jax
import jax.numpy as jnp
import numpy as np
from jax import lax
from jax.experimental import pallas as pl
from jax.experimental.pallas import tpu as pltpu

F32 = jnp.float32
BF16 = jnp.bfloat16
HI = lax.Precision.HIGHEST

NORM_EPS = 1e-6
RWKV_GN_EPS = 64e-5
L2_EPS = 1e-12
N_HEADS = 8
HEAD_DIM = 64
WIDTH = N_HEADS * HEAD_DIM
RWKV_LORA = 256
RWKV_IN = 3 * WIDTH + RWKV_LORA
VRES_LORA = 32
QK_NOPE = 64
QK_ROPE = 32
QK_DIM = QK_NOPE + QK_ROPE
Q_LORA = 256
KV_LORA = 128
MLA_IN = Q_LORA + KV_LORA + QK_ROPE
ROPE_THETA = 10000.0
GDN_CONV = 4
GDN_QKV = 3 * WIDTH
GDN_IN = GDN_QKV + 2 * N_HEADS + WIDTH
N_EXPERTS = 8
TOP_K = 2
LANES = 128
SUBLANES = 8
CHUNK = 64
SCAN_ROWS = 2 * CHUNK
VMEM_LIMIT = 48 * 1024 * 1024


def _cparams(sem):
    return pltpu.CompilerParams(dimension_semantics=sem, vmem_limit_bytes=VMEM_LIMIT)


def _pick_bn(n, cap=2304):
    units = n // LANES
    best = 1
    for d in range(1, units + 1):
        if units % d == 0 and d * LANES <= cap:
            best = d
    return best * LANES


def _norm_kernel(x_ref, g_ref, u_ref, lg_ref, wr_ref):
    x = x_ref[...]
    u = x * lax.rsqrt(jnp.mean(x * x, axis=-1, keepdims=True) + NORM_EPS) * g_ref[...]
    u_ref[...] = u.astype(u_ref.dtype)
    if wr_ref is not None:
        lg_ref[...] = jnp.dot(u, wr_ref[...], preferred_element_type=F32, precision=HI)


def _norm(x, g, w_router=None, *, bm=1024):
    m, d = x.shape
    bm = min(bm, m)
    row = lambda width: pl.BlockSpec((bm, width), lambda i: (i, 0))
    full = lambda a: pl.BlockSpec(a.shape, lambda i: (0, 0))
    args = [x, g.reshape(1, d)]
    out_shape = [jax.ShapeDtypeStruct((m, d), BF16)]
    out_specs = [row(d)]
    if w_router is None:
        body = lambda x_ref, g_ref, u_ref: _norm_kernel(x_ref, g_ref, u_ref, None, None)
    else:
        body = lambda x_ref, g_ref, wr_ref, u_ref, lg_ref: _norm_kernel(x_ref, g_ref, u_ref, lg_ref, wr_ref)
        args.append(w_router)
        out_shape.append(jax.ShapeDtypeStruct((m, w_router.shape[1]), F32))
        out_specs.append(row(w_router.shape[1]))
    outs = pl.pallas_call(
        body,
        grid=(m // bm,),
        in_specs=[row(d)] + [full(a) for a in args[1:]],
        out_specs=out_specs,
        out_shape=out_shape,
        compiler_params=_cparams(("arbitrary",)),
        name="norm",
    )(*args)
    return outs[0] if w_router is None else outs


def _mm_kernel(x_ref, w_ref, o_ref, ws_ref):
    @pl.when(pl.program_id(1) == 0)
    def _():
        ws_ref[...] = w_ref[...].astype(BF16)

    o_ref[...] = jnp.dot(x_ref[...], ws_ref[...], preferred_element_type=F32).astype(o_ref.dtype)


def _mm(x, w, *, bm=1024, out_dtype=F32):
    m, k = x.shape
    n = w.shape[1]
    bm = min(bm, m)
    bn = _pick_bn(n, cap=1024)
    return pl.pallas_call(
        _mm_kernel,
        grid=(n // bn, m // bm),
        in_specs=[pl.BlockSpec((bm, k), lambda j, i: (i, 0)), pl.BlockSpec((k, bn), lambda j, i: (0, j))],
        out_specs=pl.BlockSpec((bm, bn), lambda j, i: (i, j)),
        out_shape=jax.ShapeDtypeStruct((m, n), out_dtype),
        scratch_shapes=[pltpu.VMEM((k, bn), BF16)],
        compiler_params=_cparams(("arbitrary", "arbitrary")),
        name="mm",
    )(x, w)


def _rms(x, n, g):
    return x * lax.rsqrt(jnp.sum(x * x, axis=-1, keepdims=True) * (1.0 / n) + NORM_EPS) * g


def _mla_prep_kernel(u_ref, cos_ref, s1_ref, s2_ref, win_ref, gq_ref, gkv_ref, gqh_ref, gkh_ref, wq_ref, wk_ref,
                     wv_ref, q_ref, k_ref, v_ref):
    z = jnp.dot(u_ref[...], win_ref[...], preferred_element_type=F32)
    cq = _rms(z[:, :Q_LORA], Q_LORA, gq_ref[...]).astype(BF16)
    ckv = _rms(z[:, Q_LORA:Q_LORA + KV_LORA], KV_LORA, gkv_ref[...]).astype(BF16)
    k_pe = pltpu.roll(z[:, Q_LORA + KV_LORA:], QK_NOPE, 1)
    q_all = jnp.dot(cq, wq_ref[...], preferred_element_type=F32)
    k_all = jnp.dot(ckv, wk_ref[...], preferred_element_type=F32)
    v_all = jnp.dot(ckv, wv_ref[...], preferred_element_type=F32)
    cos, s1, s2 = cos_ref[...], s1_ref[...], s2_ref[...]

    def rope(x):
        return x * cos + pltpu.roll(x, LANES - QK_ROPE // 2, 1) * s1 + pltpu.roll(x, QK_ROPE // 2, 1) * s2

    for h in range(N_HEADS):
        sl = slice(h * LANES, (h + 1) * LANES)
        q = rope(_rms(q_all[:, sl], QK_DIM, gqh_ref[...])) * (QK_DIM ** -0.5)
        k = rope(_rms(k_all[:, sl] + k_pe, QK_DIM, gkh_ref[...]))
        q_ref[h] = q.astype(BF16)
        k_ref[h] = k.astype(BF16)
        v_ref[h] = v_all[:, sl].astype(BF16)


def _pad_heads(w, width):
    k = w.shape[0]
    w = w.reshape(k, N_HEADS, width)
    return jnp.pad(w, ((0, 0), (0, 0), (0, LANES - width))).reshape(k, N_HEADS * LANES)


def _mla_prep(u, w_in, rope_tabs, b, s, q_norm_g, kv_norm_g, w_uq, w_ukv, qk_g_q, qk_g_k, *, bm=512):
    bm = min(bm, s)
    w_in = _pad_cols(w_in, Q_LORA + KV_LORA + LANES).astype(BF16)
    w_ukv = w_ukv.reshape(KV_LORA, N_HEADS, QK_NOPE + HEAD_DIM)
    wq = _pad_heads(w_uq, QK_DIM).astype(BF16)
    wk = _pad_heads(w_ukv[..., :QK_NOPE].reshape(KV_LORA, -1), QK_NOPE).astype(BF16)
    wv = _pad_heads(w_ukv[..., QK_NOPE:].reshape(KV_LORA, -1), HEAD_DIM).astype(BF16)
    pad_g = lambda g: jnp.pad(g, (0, LANES - g.shape[0])).reshape(1, LANES)
    row = lambda width: pl.BlockSpec((bm, width), lambda bb, i: (bb * (s // bm) + i, 0))
    full = lambda a: pl.BlockSpec(a.shape, lambda bb, i: (0, 0))
    consts = [w_in, q_norm_g.reshape(1, -1), kv_norm_g.reshape(1, -1), pad_g(qk_g_q), pad_g(qk_g_k), wq, wk, wv]
    out_spec = pl.BlockSpec((None, N_HEADS, bm, LANES), lambda bb, i: (bb, 0, i, 0))
    out_shape = jax.ShapeDtypeStruct((b, N_HEADS, s, LANES), BF16)
    return pl.pallas_call(
        _mla_prep_kernel,
        grid=(b, s // bm),
        in_specs=[row(u.shape[1]), row(LANES), row(LANES), row(LANES)] + [full(a) for a in consts],
        out_specs=[out_spec] * 3,
        out_shape=[out_shape] * 3,
        compiler_params=_cparams(("arbitrary", "arbitrary")),
        name="mla_prep",
    )(u, *rope_tabs, *consts)


def _flash_kernel(qi_ref, kj_ref, q_ref, k_ref, v_ref, o_ref, m_ref, l_ref, acc_ref, *, blk):
    n = pl.program_id(2)
    i = qi_ref[n]
    j = kj_ref[n]

    @pl.when(j == 0)
    def _():
        m_ref[...] = jnp.full_like(m_ref, -jnp.inf)
        l_ref[...] = jnp.zeros_like(l_ref)
        acc_ref[...] = jnp.zeros_like(acc_ref)

    def update(masked):
        s = lax.dot_general(q_ref[...], k_ref[...], (((1,), (1,)), ((), ())), preferred_element_type=F32)
        if masked:
            row = lax.broadcasted_iota(jnp.int32, (blk, blk), 0)
            col = lax.broadcasted_iota(jnp.int32, (blk, blk), 1)
            s = jnp.where(row >= col, s, -jnp.inf)
        m_old = m_ref[...]
        m_new = jnp.maximum(m_old, jnp.max(s, axis=-1, keepdims=True))
        alpha = jnp.exp(m_old - m_new)
        p = jnp.exp(s - m_new[:, :1])
        l_ref[...] = alpha * l_ref[...] + jnp.sum(p, axis=-1, keepdims=True)
        acc_ref[...] = alpha * acc_ref[...] + jnp.dot(p.astype(BF16), v_ref[...], preferred_element_type=F32)
        m_ref[...] = m_new

    @pl.when(j < i)
    def _():
        update(False)

    @pl.when(j == i)
    def _():
        update(True)
        o_ref[...] = (acc_ref[...] / l_ref[...]).astype(o_ref.dtype)


def _flash_attention(q, k, v, *, blk=1024):
    b, h, s, d = q.shape
    blk = min(blk, s)
    nb = s // blk
    qi = np.array([i for i in range(nb) for j in range(i + 1)], np.int32)
    kj = np.array([j for i in range(nb) for j in range(i + 1)], np.int32)
    q_spec = pl.BlockSpec((None, None, blk, d), lambda bb, hh, n, qi_r, kj_r: (bb, hh, qi_r[n], 0))
    kv_spec = pl.BlockSpec((None, None, blk, d), lambda bb, hh, n, qi_r, kj_r: (bb, hh, kj_r[n], 0))
    return pl.pallas_call(
        functools.partial(_flash_kernel, blk=blk),
        grid_spec=pltpu.PrefetchScalarGridSpec(
            num_scalar_prefetch=2,
            grid=(b, h, len(qi)),
            in_specs=[q_spec, kv_spec, kv_spec],
            out_specs=q_spec,
            scratch_shapes=[pltpu.VMEM((blk, LANES), F32), pltpu.VMEM((blk, LANES), F32), pltpu.VMEM((blk, d), F32)],
        ),
        out_shape=jax.ShapeDtypeStruct((b, h, s, d), BF16),
        compiler_params=_cparams(("arbitrary", "arbitrary", "arbitrary")),
        name="flash",
    )(jnp.asarray(qi), jnp.asarray(kj), q, k, v)


def _dot(a, b):
    return jnp.dot(a.astype(BF16), b.astype(BF16), preferred_element_type=F32)


def _dot_nt(a, b):
    return lax.dot_general(a.astype(BF16), b.astype(BF16), (((1,), (1,)), ((), ())), preferred_element_type=F32)


def _dot_tn(a, b):
    return jnp.dot(a.T.astype(BF16), b.astype(BF16), preferred_element_type=F32)


def _split3(x):
    hi = x.astype(BF16)
    r1 = x - hi.astype(F32)
    mid = r1.astype(BF16)
    return hi, mid, (r1 - mid.astype(F32)).astype(BF16)


def _cumsum_rows(tri, x):
    s = jnp.dot(tri, jnp.concatenate(_split3(x), axis=1), preferred_element_type=F32)
    n = x.shape[1]
    return s[:, :n] + s[:, n:2 * n] + s[:, 2 * n:]


def _head_sum(x):
    blk = (lax.broadcasted_iota(jnp.int32, (LANES, LANES), 0) // HEAD_DIM
           == lax.broadcasted_iota(jnp.int32, (LANES, LANES), 1) // HEAD_DIM).astype(BF16)
    rows = x.shape[0]
    outs = []
    for p in range(WIDTH // LANES):
        hi, mid, _ = _split3(x[:, p * LANES:(p + 1) * LANES])
        s = jnp.dot(jnp.concatenate([hi, mid], axis=0), blk, preferred_element_type=F32)
        outs.append(s[:rows] + s[rows:])
    return jnp.concatenate(outs, axis=1)


def _shift_rows(x, tail, j):
    rolled = pltpu.roll(x, j, 0)
    top = jnp.where(lax.broadcasted_iota(jnp.int32, (SUBLANES, x.shape[1]), 0) < j,
                    pltpu.roll(tail, j, 0), rolled[:SUBLANES])
    return jnp.concatenate([top, rolled[SUBLANES:]], axis=0)


def _softplus(x):
    return jnp.maximum(x, 0.0) + jnp.log(1.0 + jnp.exp(-jnp.abs(x)))


def _pair_masks():
    lane = lax.broadcasted_iota(jnp.int32, (CHUNK, LANES), 1)
    m0 = (lane < HEAD_DIM).astype(F32)
    return m0, 1.0 - m0


def _ext(x, m0, m1, dtype=BF16):
    return jnp.concatenate([x * m0, x * m1], axis=0).astype(dtype)


def _cat(xs, axis):
    return jnp.concatenate(xs, axis=axis)


def _neumann_inverse(mats):
    n = mats[0].shape[0]
    eye = (lax.broadcasted_iota(jnp.int32, (n, n), 0) == lax.broadcasted_iota(jnp.int32, (n, n), 1)).astype(F32)
    ts = [eye + a for a in mats]
    ps = [_dot(a, a) for a in mats]
    for _ in range(int(math.log2(CHUNK)) - 2):
        pps = [_dot(_cat([p, t], 0), p) for p, t in zip(ps, ts)]
        ps = [pp[:n] for pp in pps]
        ts = [t + pp[n:] for t, pp in zip(ts, pps)]
    return [t + _dot(t, p) for t, p in zip(ts, ps)]


def _work_items(rows):
    return [(slice(c * CHUNK, (c + 1) * CHUNK), slice(p * LANES, (p + 1) * LANES))
            for c in range(rows // CHUNK) for p in range(WIDTH // LANES)]


def _tiles(x, items):
    return [x[rows, cols] for rows, cols in items]


def _rwkv_scan(r, k, v, a, b, ld, st_ref, y_ref):
    c2 = 2 * CHUNK
    strict = lax.broadcasted_iota(jnp.int32, (c2, c2), 0) > lax.broadcasted_iota(jnp.int32, (c2, c2), 1)
    trow = lax.broadcasted_iota(jnp.int32, (CHUNK, c2), 0)
    tcol = lax.broadcasted_iota(jnp.int32, (CHUNK, c2), 1) & (CHUNK - 1)
    incl_w = tcol <= trow
    tri = (lax.broadcasted_iota(jnp.int32, (CHUNK, CHUNK), 0)
           >= lax.broadcasted_iota(jnp.int32, (CHUNK, CHUNK), 1)).astype(BF16)
    m0, m1 = _pair_masks()
    items = _work_items(r.shape[0])
    n_pairs = WIDTH // LANES
    r, k, v, a, b, ld = (_tiles(x, items) for x in (r, k, v, a, b, ld))
    cum = [_cumsum_rows(tri, x) for x in ld]
    e_pos = [jnp.exp(x) for x in cum]
    e_neg = [jnp.exp(-x) for x in cum]
    e_end = [jnp.exp(x[CHUNK - 1:CHUNK, :] - x) for x in cum]
    a_ext = [_ext(ai * jnp.exp(ci - li), m0, m1) for ai, ci, li in zip(a, cum, ld)]
    v_ext = [_ext(x, m0, m1) for x in v]
    r_dec = [(ri * ei).astype(BF16) for ri, ei in zip(r, e_pos)]
    s1 = [_dot_nt(_cat([ae, rd], 0), _cat([_ext(bi * en, m0, m1), _ext(ki * en, m0, m1)], 0))
          for ae, rd, bi, ki, en in zip(a_ext, r_dec, b, k, e_neg)]
    a_ab = [jnp.where(strict, s[:c2, :c2], 0.0) for s in s1]
    a_ak = [jnp.where(strict, s[:c2, c2:], 0.0) for s in s1]
    a_rbk = [_cat([jnp.where(incl_w, s[c2:, :c2], 0.0), jnp.where(incl_w, s[c2:, c2:], 0.0)], 1).astype(BF16)
             for s in s1]
    akv = [_dot(x, ve) for x, ve in zip(a_ak, v_ext)]
    t_inv = _neumann_inverse(a_ab)
    wu0 = [_dot(t, _cat([ae, av.astype(BF16)], 1)) for t, ae, av in zip(t_inv, a_ext, akv)]
    bk_end = [_cat([_ext(bi * ee, m0, m1, F32), _ext(ki * ee, m0, m1, F32)], 0) for bi, ki, ee in zip(b, k, e_end)]
    e_last_t = [jnp.broadcast_to(e[CHUNK - 1:CHUNK, :], (LANES, LANES)).T for e in e_pos]
    st = [st_ref[p] for p in range(n_pairs)]
    for c in range(len(items) // n_pairs):
        idx = range(c * n_pairs, (c + 1) * n_pairs)
        z = [_dot(_cat([wu0[i][:, :LANES].astype(BF16), r_dec[i]], 0), s) for i, s in zip(idx, st)]
        uv = [_cat([(wu0[i][:, LANES:] + zi[:c2]).astype(BF16), v_ext[i]], 0) for i, zi in zip(idx, z)]
        for i, zi, uvi in zip(idx, z, uv):
            rows, cols = items[i]
            y_ref[rows, cols] = zi[c2:] + _dot(a_rbk[i], uvi)
        st = [s * e_last_t[i] + _dot_tn(bk_end[i], uvi) for i, s, uvi in zip(idx, st, uv)]
    for p in range(n_pairs):
        st_ref[p] = st[p]


def _gdn_scan(q, k, v, g, beta, st_ref, o_ref):
    c2 = 2 * CHUNK
    row = lax.broadcasted_iota(jnp.int32, (c2, c2), 0)
    col = lax.broadcasted_iota(jnp.int32, (c2, c2), 1)
    causal = ((row >= CHUNK) == (col >= CHUNK)) & (row >= col)
    strict = row > col
    tri = (lax.broadcasted_iota(jnp.int32, (CHUNK, CHUNK), 0)
           >= lax.broadcasted_iota(jnp.int32, (CHUNK, CHUNK), 1)).astype(BF16)
    m0, m1 = _pair_masks()
    items = _work_items(q.shape[0])
    n_pairs = WIDTH // LANES
    q, k, v, g, beta = (_tiles(x, items) for x in (q, k, v, g, beta))
    gc = [_cumsum_rows(tri, x) for x in g]
    g_rows = [_cat([jnp.broadcast_to(x[:, 0:1], (CHUNK, LANES)),
                    jnp.broadcast_to(x[:, HEAD_DIM:HEAD_DIM + 1], (CHUNK, LANES))], 0) for x in gc]
    gamma = [jnp.exp(jnp.where(causal, x - x.T, -jnp.inf)) for x in g_rows]
    e_pos = [jnp.exp(x) for x in gc]
    kb = [ki * bi for ki, bi in zip(k, beta)]
    s1 = [_dot_nt(_cat([_ext(kbi, m0, m1), qi.astype(BF16)], 0), _ext(ki, m0, m1)) for kbi, qi, ki in zip(kb, q, k)]
    a_neg = [-jnp.where(strict, s[:c2] * ga, 0.0) for s, ga in zip(s1, gamma)]
    a_in = [(s[c2:] * (ga[:CHUNK] + ga[CHUNK:])).astype(BF16) for s, ga in zip(s1, gamma)]
    t_inv = _neumann_inverse(a_neg)
    uw = [_dot(t, _cat([_ext(vi * bi, m0, m1), _ext(kbi * ei, m0, m1)], 1))
          for t, vi, bi, kbi, ei in zip(t_inv, v, beta, kb, e_pos)]
    q_dec = [(qi * ei).astype(BF16) for qi, ei in zip(q, e_pos)]
    k_dec = [_ext(ki * jnp.exp(x[CHUNK - 1:CHUNK, :] - x), m0, m1, F32) for ki, x in zip(k, gc)]
    st = [st_ref[p] for p in range(n_pairs)]
    for c in range(len(items) // n_pairs):
        idx = range(c * n_pairs, (c + 1) * n_pairs)
        z = [_dot(_cat([uw[i][:, LANES:].astype(BF16), q_dec[i]], 0), s) for i, s in zip(idx, st)]
        v_new = [(uw[i][:, :LANES] - zi[:c2]).astype(BF16) for i, zi in zip(idx, z)]
        for i, zi, vn in zip(idx, z, v_new):
            rows, cols = items[i]
            o_ref[rows, cols] = zi[c2:] + _dot(a_in[i], vn)
        st = [s * e_pos[i][CHUNK - 1:CHUNK, :] + _dot_tn(k_dec[i], vn) for i, s, vn in zip(idx, st, v_new)]
    for p in range(n_pairs):
        st_ref[p] = st[p]


def _rwkv_kernel(*refs, vres):
    if vres:
        z_ref, vf_ref, mu_ref, prm_ref, wup_ref, aup_ref, gup_ref, vup_ref, o_ref, st_ref, tail_ref, y_ref = refs
    else:
        z_ref, mu_ref, prm_ref, wup_ref, aup_ref, gup_ref, o_ref, vf_out_ref, st_ref, tail_ref, y_ref = refs

    @pl.when(pl.program_id(1) == 0)
    def _():
        st_ref[...] = jnp.zeros_like(st_ref)
        tail_ref[...] = jnp.zeros_like(tail_ref)

    z = z_ref[...]
    rows = z.shape[0]
    zs = _shift_rows(z, tail_ref[...], 1)
    tail_ref[...] = z[rows - SUBLANES:]
    zl = z[:, :RWKV_IN] + (zs[:, :RWKV_IN] - z[:, :RWKV_IN]) * mu_ref[...]
    r, k, v = zl[:, :WIDTH], zl[:, WIDTH:2 * WIDTH], zl[:, 2 * WIDTH:3 * WIDTH]
    lo = zl[:, 3 * WIDTH:3 * WIDTH + LANES]
    g_lo = zl[:, 3 * WIDTH + LANES:]
    w0, a0, k_k, k_a, ln_g, ln_b, r_k, v_bias = (prm_ref[i:i + 1, :] for i in range(8))
    log_w = -_softplus(-(w0 + _dot(jnp.tanh(lo), wup_ref[...]))) - 0.5
    ld = -jnp.exp(log_w)
    iclr = jax.nn.sigmoid(a0 + _dot(lo, aup_ref[...]))
    gate = _dot(jax.nn.sigmoid(g_lo), gup_ref[...])
    if vres:
        x = z[:, RWKV_IN:] + pltpu.roll(zs[:, RWKV_IN:], LANES - VRES_LORA, 1)
        v = v + (vf_ref[...] - v) * jax.nn.sigmoid(v_bias + _dot(x, vup_ref[...]))
    else:
        vf_out_ref[...] = v
    kk = k * k_k
    kk = kk * lax.rsqrt(_head_sum(kk * kk) + L2_EPS)
    k = k * (1.0 + (iclr - 1.0) * k_a)
    _rwkv_scan(r, k, v, -kk, kk * iclr, ld, st_ref, y_ref)
    y = y_ref[...]
    d = y - _head_sum(y) * (1.0 / HEAD_DIM)
    y = d * lax.rsqrt(_head_sum(d * d) * (1.0 / HEAD_DIM) + RWKV_GN_EPS) * ln_g + ln_b
    y = y + _head_sum(r * k * r_k) * v
    o_ref[...] = (y * gate).astype(o_ref.dtype)


def _rwkv(z, v_first, b, s, mu, prm, w_up, a_up, g_up, v_up):
    vres = v_first is not None
    rows = min(SCAN_ROWS, s)
    zw = z.shape[1]
    row = lambda width: pl.BlockSpec((rows, width), lambda bb, c: (bb * (s // rows) + c, 0))
    full = lambda a: pl.BlockSpec(a.shape, lambda bb, c: (0, 0))
    zero = jnp.zeros((LANES // 2, WIDTH), F32)
    consts = [mu.reshape(1, -1), prm, _cat([w_up, zero], 0).astype(BF16), _cat([zero, a_up], 0).astype(BF16),
              g_up.astype(BF16)]
    args, in_specs = [z], [row(zw)]
    if vres:
        args.append(v_first)
        in_specs.append(row(WIDTH))
        consts.append(jnp.pad(v_up, ((0, LANES - VRES_LORA), (0, 0))).astype(BF16))
    out_shape = [jax.ShapeDtypeStruct((b * s, WIDTH), BF16)]
    out_specs = [row(WIDTH)]
    if not vres:
        out_shape.append(jax.ShapeDtypeStruct((b * s, WIDTH), F32))
        out_specs.append(row(WIDTH))
    outs = pl.pallas_call(
        functools.partial(_rwkv_kernel, vres=vres),
        grid=(b, s // rows),
        in_specs=in_specs + [full(a) for a in consts],
        out_specs=out_specs,
        out_shape=out_shape,
        scratch_shapes=[pltpu.VMEM((WIDTH // LANES, LANES, LANES), F32), pltpu.VMEM((SUBLANES, zw), F32),
                        pltpu.VMEM((rows, WIDTH), F32)],
        compiler_params=_cparams(("arbitrary", "arbitrary")),
        name="rwkv",
    )(*args, *consts)
    return (outs[0], v_first) if vres else (outs[0], outs[1])


def _gdn_kernel(z_ref, cw_ref, prm_ref, exp_ref, o_ref, st_ref, tail_ref, y_ref):
    @pl.when(pl.program_id(1) == 0)
    def _():
        st_ref[...] = jnp.zeros_like(st_ref)
        tail_ref[...] = jnp.zeros_like(tail_ref)

    rows = z_ref.shape[0]
    x = z_ref[:, :GDN_QKV]
    tail = tail_ref[...]
    conv = x * cw_ref[GDN_CONV - 1:GDN_CONV, :]
    for j in range(1, GDN_CONV):
        conv = conv + _shift_rows(x, tail, j) * cw_ref[GDN_CONV - 1 - j:GDN_CONV - j, :]
    tail_ref[...] = x[rows - SUBLANES:]
    qkv = conv * jax.nn.sigmoid(conv)
    q, k, v = qkv[:, :WIDTH], qkv[:, WIDTH:2 * WIDTH], qkv[:, 2 * WIDTH:]
    q = q * lax.rsqrt(_head_sum(q * q) + L2_EPS) * (HEAD_DIM ** -0.5)
    k = k * lax.rsqrt(_head_sum(k * k) + L2_EPS)
    logits = jnp.dot(_cat(_split3(z_ref[:, GDN_QKV + WIDTH:GDN_QKV + WIDTH + LANES]), 0), exp_ref[...],
                     preferred_element_type=F32)
    logits = logits[:rows] + logits[rows:2 * rows] + logits[2 * rows:]
    neg_a, dt_bias, norm_g = (prm_ref[i:i + 1, :] for i in range(3))
    beta = jax.nn.sigmoid(logits[:, :WIDTH])
    g = neg_a * _softplus(logits[:, WIDTH:] + dt_bias)
    _gdn_scan(q, k, v, g, beta, st_ref, y_ref)
    o = y_ref[...]
    o = o * lax.rsqrt(_head_sum(o * o) * (1.0 / HEAD_DIM) + NORM_EPS) * norm_g
    gate = z_ref[:, GDN_QKV:GDN_QKV + WIDTH]
    o_ref[...] = (o * (gate * jax.nn.sigmoid(gate))).astype(o_ref.dtype)


def _gdn(z, b, s, conv_w, a_log, dt_bias, norm_g):
    rows = min(SCAN_ROWS, s)
    zw = z.shape[1]
    row = lambda width: pl.BlockSpec((rows, width), lambda bb, c: (bb * (s // rows) + c, 0))
    full = lambda a: pl.BlockSpec(a.shape, lambda bb, c: (0, 0))
    per_lane = lambda t: jnp.repeat(t, HEAD_DIM)
    prm = jnp.stack([per_lane(-jnp.exp(a_log)), per_lane(dt_bias), jnp.tile(norm_g, N_HEADS)])
    head_of_lane = np.arange(WIDTH) // HEAD_DIM
    expand = np.zeros((LANES, 2 * WIDTH), np.float32)
    expand[head_of_lane, np.arange(WIDTH)] = 1.0
    expand[N_HEADS + head_of_lane, WIDTH + np.arange(WIDTH)] = 1.0
    consts = [conv_w, prm, jnp.asarray(expand, BF16)]
    return pl.pallas_call(
        _gdn_kernel,
        grid=(b, s // rows),
        in_specs=[row(zw)] + [full(a) for a in consts],
        out_specs=row(WIDTH),
        out_shape=jax.ShapeDtypeStruct((b * s, WIDTH), BF16),
        scratch_shapes=[pltpu.VMEM((WIDTH // LANES, LANES, LANES), F32), pltpu.VMEM((SUBLANES, GDN_QKV), F32),
                        pltpu.VMEM((rows, WIDTH), F32)],
        compiler_params=_cparams(("arbitrary", "arbitrary")),
        name="gdn",
    )(z, *consts)


def _merge_kernel(h_ref, u_ref, oa_ref, ob_ref, oc_ref, wg_ref, wa_ref, wb_ref, wc_ref, wo_ref, out_ref):
    d = h_ref.shape[1]
    u = u_ref[...]
    ob = jnp.concatenate([ob_ref[h] for h in range(N_HEADS)], axis=1)
    merged = jnp.zeros(h_ref.shape, F32)
    for n, (o, w_ref) in enumerate(((oa_ref[...], wa_ref), (ob, wb_ref), (oc_ref[...], wc_ref))):
        proj = jnp.dot(o, w_ref[...], preferred_element_type=F32)
        zg = jnp.dot(u, wg_ref[:, n * d:(n + 1) * d], preferred_element_type=F32)
        merged = merged + jax.nn.sigmoid(zg) * proj
    out_ref[...] = h_ref[...] + jnp.dot(merged.astype(BF16), wo_ref[...], preferred_element_type=F32)


def _merge(h, u, oa, ob, oc, wg, wa, wb, wc, wo, s, *, bm=512):
    m, d = h.shape
    bm = min(bm, s)
    row = lambda width: pl.BlockSpec((bm, width), lambda i: (i, 0))
    full = lambda w: pl.BlockSpec(w.shape, lambda i: (0, 0))
    wb = jnp.pad(wb.reshape(N_HEADS, HEAD_DIM, d), ((0, 0), (0, LANES - HEAD_DIM), (0, 0))).reshape(N_HEADS * LANES, d)
    ws = [w.astype(BF16) for w in (wg, wa, wb, wc, wo)]
    ob_spec = pl.BlockSpec((None, N_HEADS, bm, LANES), lambda i: (i // (s // bm), 0, i % (s // bm), 0))
    return pl.pallas_call(
        _merge_kernel,
        grid=(m // bm,),
        in_specs=[row(d), row(d), row(WIDTH), ob_spec, row(WIDTH)] + [full(w) for w in ws],
        out_specs=row(d),
        out_shape=jax.ShapeDtypeStruct((m, d), F32),
        compiler_params=_cparams(("arbitrary",)),
        name="merge",
    )(h, u, oa, ob, oc, *ws)


def _ffn_kernel(be_ref, nu_ref, x_ref, g_ref, wg_ref, wu_ref, wd_ref, o_ref, xs_ref, acc_ref, *, nf, dense):
    i = pl.program_id(0)
    f = pl.program_id(1)
    used = i < nu_ref[0]

    @pl.when(used & (f == 0))
    def _():
        if dense:
            x = x_ref[...]
            ms = jnp.mean(x * x, axis=-1, keepdims=True)
            xs_ref[...] = (x * lax.rsqrt(ms + NORM_EPS) * g_ref[...]).astype(BF16)
        else:
            xs_ref[...] = x_ref[...]

    @pl.when(f == 0)
    def _():
        acc_ref[...] = jnp.zeros_like(acc_ref)

    @pl.when(used)
    def _():
        xs = xs_ref[...]
        gate = jnp.dot(xs, wg_ref[...].astype(BF16), preferred_element_type=F32)
        up = jnp.dot(xs, wu_ref[...].astype(BF16), preferred_element_type=F32)
        act = (jax.nn.silu(gate) * up).astype(BF16)
        acc_ref[...] += jnp.dot(act, wd_ref[...].astype(BF16), preferred_element_type=F32)

    @pl.when(f == nf - 1)
    def _():
        o_ref[...] = (x_ref[...] + acc_ref[...]) if dense else acc_ref[...]


def _ffn(x, g, wg, wu, wd, block_e, n_used, *, bm, tf, dense):
    r, d = x.shape
    ff = wg.shape[2]
    nf = ff // tf

    def ff_idx(i, f, be, nu):
        return jnp.where(i < nu[0], f, nf - 1)

    return pl.pallas_call(
        functools.partial(_ffn_kernel, nf=nf, dense=dense),
        grid_spec=pltpu.PrefetchScalarGridSpec(
            num_scalar_prefetch=2,
            grid=(r // bm, nf),
            in_specs=[
                pl.BlockSpec((bm, d), lambda i, f, be, nu: (i, 0)),
                pl.BlockSpec((1, d), lambda i, f, be, nu: (0, 0)),
                pl.BlockSpec((None, d, tf), lambda i, f, be, nu: (be[i], 0, ff_idx(i, f, be, nu))),
                pl.BlockSpec((None, d, tf), lambda i, f, be, nu: (be[i], 0, ff_idx(i, f, be, nu))),
                pl.BlockSpec((None, tf, d), lambda i, f, be, nu: (be[i], ff_idx(i, f, be, nu), 0)),
            ],
            out_specs=pl.BlockSpec((bm, d), lambda i, f, be, nu: (i, 0)),
            scratch_shapes=[pltpu.VMEM((bm, d), BF16), pltpu.VMEM((bm, d), F32)],
        ),
        out_shape=jax.ShapeDtypeStruct((r, d), F32),
        compiler_params=_cparams(("arbitrary", "arbitrary")),
        name="ffn",
    )(block_e, n_used, x, g.reshape(1, d).astype(F32), wg, wu, wd)


def _ple_kernel(h_ref, p_ref, wg_ref, wp_ref, g_ref, o_ref):
    h = h_ref[...]
    gate = jnp.dot(h.astype(BF16), wg_ref[...], preferred_element_type=F32)
    e = jnp.dot(p_ref[...].astype(BF16), wp_ref[...], preferred_element_type=F32)
    ms = jnp.mean(e * e, axis=-1, keepdims=True)
    e = e * lax.rsqrt(ms + NORM_EPS) * g_ref[...]
    o_ref[...] = h + jax.nn.sigmoid(gate) * e


def _ple(h, p, w_gate, w_proj, g, *, bm=512):
    m, d = h.shape
    bm = min(bm, m)
    pd = p.shape[1]
    return pl.pallas_call(
        _ple_kernel,
        grid=(m // bm,),
        in_specs=[pl.BlockSpec((bm, d), lambda i: (i, 0)), pl.BlockSpec((bm, pd), lambda i: (i, 0)),
                  pl.BlockSpec((d, d), lambda i: (0, 0)), pl.BlockSpec((pd, d), lambda i: (0, 0)),
                  pl.BlockSpec((1, d), lambda i: (0, 0))],
        out_specs=pl.BlockSpec((bm, d), lambda i: (i, 0)),
        out_shape=jax.ShapeDtypeStruct((m, d), F32),
        compiler_params=_cparams(("arbitrary",)),
        name="ple",
    )(h, p, w_gate.astype(BF16), w_proj.astype(BF16), g.reshape(1, d).astype(F32))


def _pad_cols(w, n):
    return jnp.pad(w, ((0, 0), (0, n - w.shape[1])))


def _rope_tables(positions):
    t = positions.size
    inv_freq = 1.0 / (ROPE_THETA ** (jnp.arange(0, QK_ROPE, 2, dtype=F32) / QK_ROPE))
    ang = positions.astype(F32).reshape(t, 1) * inv_freq
    cos, sin = jnp.cos(ang), jnp.sin(ang)
    half = QK_ROPE // 2
    ones = jnp.ones((t, QK_NOPE), F32)
    zeros = jnp.zeros((t, QK_NOPE), F32)
    z16 = jnp.zeros((t, half), F32)
    tail1 = jnp.ones((t, LANES - QK_DIM), F32)
    tail0 = jnp.zeros((t, LANES - QK_DIM), F32)
    return (_cat([ones, cos, cos, tail1], 1), _cat([zeros, -sin, z16, tail0], 1), _cat([zeros, z16, sin, tail0], 1))


def _moe(h, g, w_router, wg, wu, wd, *, bm=768, tf=512):
    t, d = h.shape
    n_assign = t * TOP_K
    n_blocks = -(-(n_assign + N_EXPERTS * (bm - 1)) // bm)
    u, logits = _norm(h, g, _pad_cols(w_router, LANES))
    top_logits, top_e = lax.top_k(logits[:, :N_EXPERTS], TOP_K)
    top_w = jax.nn.softmax(top_logits, axis=-1).reshape(-1)
    flat_e = top_e.reshape(-1)
    onehot = (flat_e[:, None] == jnp.arange(N_EXPERTS)[None, :]).astype(jnp.int32)
    rank = jnp.take_along_axis(jnp.cumsum(onehot, axis=0) - onehot, flat_e[:, None], axis=1)[:, 0]
    counts = jnp.sum(onehot, axis=0)
    padded = (counts + bm - 1) // bm * bm
    pad_end = jnp.cumsum(padded)
    dest = (pad_end - padded)[flat_e] + rank
    src_tok = jnp.zeros((n_blocks * bm,), jnp.int32).at[dest].set(jnp.arange(n_assign, dtype=jnp.int32) // TOP_K)
    block_e = jnp.minimum(jnp.searchsorted(pad_end, jnp.arange(n_blocks) * bm, side='right'),
                          N_EXPERTS - 1).astype(jnp.int32)
    n_used = (pad_end[-1] // bm).astype(jnp.int32).reshape(1)
    y_rows = _ffn(u[src_tok], g, wg, wu, wd, block_e, n_used, bm=bm, tf=tf, dense=False)
    y = (y_rows[dest] * top_w[:, None]).reshape(t, TOP_K, d)
    return h + y[:, 0] + y[:, 1]


def kernel(x, p, positions, norm_mix_g, w_in, rwkv_mu, rwkv_w0, rwkv_w_up, rwkv_a0, rwkv_a_up, rwkv_g_up, rwkv_k_k, rwkv_k_a, rwkv_r_k, rwkv_ln_g, rwkv_ln_b, vres_mu, vres_down, vres_up, vres_b, mla_q_norm_g, mla_kv_norm_g, mla_w_uq, mla_w_ukv, mla_qk_norm_q, mla_qk_norm_k, gdn_conv_w, gdn_a_log, gdn_dt_bias, gdn_norm_g, w_br_rwkv, w_br_mla, w_br_gdn, w_out, norm_ffn_g, ffn_wg, ffn_wu, ffn_wd, moe_router, moe_wg, moe_wu, moe_wd, ple_proj, ple_gate, ple_norm_g):
    b, s, d = x.shape
    t = b * s
    depth = w_in.shape[0]
    rope_tabs = _rope_tables(positions)

    h = x.reshape(t, d)
    v_first = None
    for i in range(depth):
        w = w_in[i]
        g = norm_mix_g[i]
        w_rwkv = w[:, :RWKV_IN]
        if i > 0:
            mu_v = vres_mu[i - 1][:, None]
            vd = vres_down[i - 1]
            w_rwkv = _pad_cols(_cat([w_rwkv, (1.0 - mu_v) * vd, mu_v * vd], 1), RWKV_IN + LANES)
        w_gdn = w[:, RWKV_IN + MLA_IN:RWKV_IN + MLA_IN + GDN_IN]
        w_gdn = _pad_cols(_cat([w_gdn[:, :GDN_QKV], w_gdn[:, GDN_QKV + 2 * N_HEADS:],
                                w_gdn[:, GDN_QKV:GDN_QKV + 2 * N_HEADS]], 1), GDN_QKV + WIDTH + 2 * LANES)
        u = _norm(h, g)
        z_rwkv = _mm(u, w_rwkv)
        z_gdn = _mm(u, w_gdn)
        prm = jnp.stack([rwkv_w0[i], rwkv_a0[i], rwkv_k_k[i], rwkv_k_a[i], rwkv_ln_g[i], rwkv_ln_b[i],
                         rwkv_r_k[i].reshape(-1), vres_b[i - 1] if i > 0 else jnp.zeros((WIDTH,), F32)])
        o_a, v_first = _rwkv(z_rwkv, v_first, b, s, rwkv_mu[i], prm, rwkv_w_up[i], rwkv_a_up[i], rwkv_g_up[i],
                             vres_up[i - 1] if i > 0 else None)
        q, k, v = _mla_prep(u, w[:, RWKV_IN:RWKV_IN + MLA_IN], rope_tabs, b, s, mla_q_norm_g[i], mla_kv_norm_g[i],
                            mla_w_uq[i], mla_w_ukv[i], mla_qk_norm_q[i], mla_qk_norm_k[i])
        o_b = _flash_attention(q, k, v)
        o_c = _gdn(z_gdn, b, s, gdn_conv_w[i], gdn_a_log[i], gdn_dt_bias[i], gdn_norm_g[i])
        h = _merge(h, u, o_a, o_b, o_c, w[:, RWKV_IN + MLA_IN + GDN_IN:], w_br_rwkv[i], w_br_mla[i], w_br_gdn[i],
                   w_out[i], s)
        j = i // 2
        if i % 2 == 0:
            bm = min(1024, t)
            h = _ffn(h, norm_ffn_g[i], ffn_wg[j:j + 1], ffn_wu[j:j + 1], ffn_wd[j:j + 1],
                     jnp.zeros((t // bm,), jnp.int32), jnp.full((1,), t // bm, jnp.int32), bm=bm, tf=256, dense=True)
        else:
            h = _moe(h, norm_ffn_g[i], moe_router[j], moe_wg[j], moe_wu[j], moe_wd[j])
        h = _ple(h, p[i].reshape(t, -1), ple_gate[i], ple_proj[i], ple_norm_g[i])
    return h.reshape(b, s, d)
```

```python
import functools
import math

import jax
import jax.numpy as jnp
import numpy as np
from jax import lax
from jax.experimental import pallas as pl
from jax.experimental.pallas import tpu as pltpu

F32 = jnp.float32
BF16 = jnp.bfloat16
HI = lax.Precision.HIGHEST

NORM_EPS = 1e-6
RWKV_GN_EPS = 64e-5
L2_EPS = 1e-12
N_HEADS = 8
HEAD_DIM = 64
WIDTH = N_HEADS * HEAD_DIM
RWKV_LORA = 256
RWKV_IN = 3 * WIDTH + RWKV_LORA
VRES_LORA = 32
QK_NOPE = 64
QK_ROPE = 32
QK_DIM = QK_NOPE + QK_ROPE
Q_LORA = 256
KV_LORA = 128
MLA_IN = Q_LORA + KV_LORA + QK_ROPE
ROPE_THETA = 10000.0
GDN_CONV = 4
GDN_QKV = 3 * WIDTH
GDN_IN = GDN_QKV + 2 * N_HEADS + WIDTH
N_EXPERTS = 8
TOP_K = 2
LANES = 128
SUBLANES = 8
CHUNK = 64
SCAN_ROWS = 2 * CHUNK
VMEM_LIMIT = 48 * 1024 * 1024


def _cparams(sem):
    return pltpu.CompilerParams(dimension_semantics=sem, vmem_limit_bytes=VMEM_LIMIT)


def _pick_bn(n, cap=2304):
    units = n // LANES
    best = 1
    for d in range(1, units + 1):
        if units % d == 0 and d * LANES <= cap:
            best = d
    return best * LANES


def _norm_kernel(x_ref, g_ref, u_ref):
    x = x_ref[...]
    u_ref[...] = (x * lax.rsqrt(jnp.mean(x * x, axis=-1, keepdims=True) + NORM_EPS) * g_ref[...]).astype(u_ref.dtype)


def _norm(x, g, *, bm=1024):
    m, d = x.shape
    bm = min(bm, m)
    return pl.pallas_call(
        _norm_kernel,
        grid=(m // bm,),
        in_specs=[pl.BlockSpec((bm, d), lambda i: (i, 0)), pl.BlockSpec((1, d), lambda i: (0, 0))],
        out_specs=pl.BlockSpec((bm, d), lambda i: (i, 0)),
        out_shape=jax.ShapeDtypeStruct((m, d), BF16),
        compiler_params=_cparams(("arbitrary",)),
        name="norm",
    )(x, g.reshape(1, d))


def _router_kernel(x_ref, g_ref, wr_ref, route_ref, counts_ref, carry_ref):
    @pl.when(pl.program_id(0) == 0)
    def _():
        carry_ref[...] = jnp.zeros_like(carry_ref)

    x = x_ref[...]
    rows = x.shape[0]
    u = x * lax.rsqrt(jnp.mean(x * x, axis=-1, keepdims=True) + NORM_EPS) * g_ref[...]
    lane = lax.broadcasted_iota(jnp.int32, (rows, LANES), 1)
    logits = jnp.where(lane < N_EXPERTS, jnp.dot(u, wr_ref[...], preferred_element_type=F32, precision=HI), -jnp.inf)
    m1 = jnp.max(logits, axis=-1, keepdims=True)
    i1 = jnp.min(jnp.where(logits == m1, lane, LANES), axis=-1, keepdims=True)
    rest = jnp.where(lane == i1, -jnp.inf, logits)
    m2 = jnp.max(rest, axis=-1, keepdims=True)
    i2 = jnp.min(jnp.where(rest == m2, lane, LANES), axis=-1, keepdims=True)
    e2 = jnp.exp(m2 - m1)
    w1 = 1.0 / (1.0 + e2)
    w2 = e2 / (1.0 + e2)
    hit1 = lane == i1
    hit2 = lane == i2
    onehot = jnp.where(hit1 | hit2, 1.0, 0.0)
    before = (lax.broadcasted_iota(jnp.int32, (rows, rows), 0) > lax.broadcasted_iota(jnp.int32, (rows, rows), 1))
    seen = jnp.dot(before.astype(BF16), onehot.astype(BF16), preferred_element_type=F32) + carry_ref[0:1, :]
    r1 = jnp.sum(jnp.where(hit1, seen, 0.0), axis=-1, keepdims=True)
    r2 = jnp.sum(jnp.where(hit2, seen, 0.0), axis=-1, keepdims=True)
    carry_ref[...] = carry_ref[...] + jnp.sum(onehot, axis=0, keepdims=True)
    cols = (i1.astype(F32), i2.astype(F32), w1, w2, r1, r2)
    route = jnp.zeros((rows, LANES), F32)
    for n, c in enumerate(cols):
        route = jnp.where(lane == n, c, route)
    route_ref[...] = route
    counts_ref[...] = carry_ref[...]


def _router(h, g, w_router, *, bm=1024):
    m, d = h.shape
    bm = min(bm, m)
    return pl.pallas_call(
        _router_kernel,
        grid=(m // bm,),
        in_specs=[pl.BlockSpec((bm, d), lambda i: (i, 0)), pl.BlockSpec((1, d), lambda i: (0, 0)),
                  pl.BlockSpec((d, LANES), lambda i: (0, 0))],
        out_specs=[pl.BlockSpec((bm, LANES), lambda i: (i, 0)), pl.BlockSpec((SUBLANES, LANES), lambda i: (0, 0))],
        out_shape=[jax.ShapeDtypeStruct((m, LANES), F32), jax.ShapeDtypeStruct((SUBLANES, LANES), F32)],
        scratch_shapes=[pltpu.VMEM((SUBLANES, LANES), F32)],
        compiler_params=_cparams(("arbitrary",)),
        name="router",
    )(h, g.reshape(1, d), _pad_cols(w_router, LANES))


def _mm_kernel(x_ref, w_ref, o_ref, ws_ref):
    @pl.when(pl.program_id(1) == 0)
    def _():
        ws_ref[...] = w_ref[...].astype(BF16)

    o_ref[...] = jnp.dot(x_ref[...], ws_ref[...], preferred_element_type=F32).astype(o_ref.dtype)


def _mm(x, w, *, bm=1024, out_dtype=F32):
    m, k = x.shape
    n = w.shape[1]
    bm = min(bm, m)
    bn = _pick_bn(n, cap=1024)
    return pl.pallas_call(
        _mm_kernel,
        grid=(n // bn, m // bm),
        in_specs=[pl.BlockSpec((bm, k), lambda j, i: (i, 0)), pl.BlockSpec((k, bn), lambda j, i: (0, j))],
        out_specs=pl.BlockSpec((bm, bn), lambda j, i: (i, j)),
        out_shape=jax.ShapeDtypeStruct((m, n), out_dtype),
        scratch_shapes=[pltpu.VMEM((k, bn), BF16)],
        compiler_params=_cparams(("arbitrary", "arbitrary")),
        name="mm",
    )(x, w)


def _rms(x, n, g):
    return x * lax.rsqrt(jnp.sum(x * x, axis=-1, keepdims=True) * (1.0 / n) + NORM_EPS) * g


def _mla_prep_kernel(u_ref, cos_ref, s1_ref, s2_ref, win_ref, gq_ref, gkv_ref, gqh_ref, gkh_ref, wq_ref, wk_ref,
                     wv_ref, q_ref, k_ref, v_ref):
    z = jnp.dot(u_ref[...], win_ref[...], preferred_element_type=F32)
    cq = _rms(z[:, :Q_LORA], Q_LORA, gq_ref[...]).astype(BF16)
    ckv = _rms(z[:, Q_LORA:Q_LORA + KV_LORA], KV_LORA, gkv_ref[...]).astype(BF16)
    k_pe = pltpu.roll(z[:, Q_LORA + KV_LORA:], QK_NOPE, 1)
    q_all = jnp.dot(cq, wq_ref[...], preferred_element_type=F32)
    k_all = jnp.dot(ckv, wk_ref[...], preferred_element_type=F32)
    v_all = jnp.dot(ckv, wv_ref[...], preferred_element_type=F32)
    cos, s1, s2 = cos_ref[...], s1_ref[...], s2_ref[...]

    def rope(x):
        return x * cos + pltpu.roll(x, LANES - QK_ROPE // 2, 1) * s1 + pltpu.roll(x, QK_ROPE // 2, 1) * s2

    for h in range(N_HEADS):
        sl = slice(h * LANES, (h + 1) * LANES)
        q = rope(_rms(q_all[:, sl], QK_DIM, gqh_ref[...])) * (QK_DIM ** -0.5)
        k = rope(_rms(k_all[:, sl] + k_pe, QK_DIM, gkh_ref[...]))
        q_ref[h] = q.astype(BF16)
        k_ref[h] = k.astype(BF16)
        v_ref[h] = v_all[:, sl].astype(BF16)


def _pad_heads(w, width):
    k = w.shape[0]
    w = w.reshape(k, N_HEADS, width)
    return jnp.pad(w, ((0, 0), (0, 0), (0, LANES - width))).reshape(k, N_HEADS * LANES)


def _mla_prep(u, w_in, rope_tabs, b, s, q_norm_g, kv_norm_g, w_uq, w_ukv, qk_g_q, qk_g_k, *, bm=512):
    bm = min(bm, s)
    w_in = _pad_cols(w_in, Q_LORA + KV_LORA + LANES).astype(BF16)
    w_ukv = w_ukv.reshape(KV_LORA, N_HEADS, QK_NOPE + HEAD_DIM)
    wq = _pad_heads(w_uq, QK_DIM).astype(BF16)
    wk = _pad_heads(w_ukv[..., :QK_NOPE].reshape(KV_LORA, -1), QK_NOPE).astype(BF16)
    wv = _pad_heads(w_ukv[..., QK_NOPE:].reshape(KV_LORA, -1), HEAD_DIM).astype(BF16)
    pad_g = lambda g: jnp.pad(g, (0, LANES - g.shape[0])).reshape(1, LANES)
    row = lambda width: pl.BlockSpec((bm, width), lambda bb, i: (bb * (s // bm) + i, 0))
    full = lambda a: pl.BlockSpec(a.shape, lambda bb, i: (0, 0))
    consts = [w_in, q_norm_g.reshape(1, -1), kv_norm_g.reshape(1, -1), pad_g(qk_g_q), pad_g(qk_g_k), wq, wk, wv]
    out_spec = pl.BlockSpec((None, N_HEADS, bm, LANES), lambda bb, i: (bb, 0, i, 0))
    out_shape = jax.ShapeDtypeStruct((b, N_HEADS, s, LANES), BF16)
    return pl.pallas_call(
        _mla_prep_kernel,
        grid=(b, s // bm),
        in_specs=[row(u.shape[1]), row(LANES), row(LANES), row(LANES)] + [full(a) for a in consts],
        out_specs=[out_spec] * 3,
        out_shape=[out_shape] * 3,
        compiler_params=_cparams(("arbitrary", "arbitrary")),
        name="mla_prep",
    )(u, *rope_tabs, *consts)


def _flash_kernel(qi_ref, kj_ref, q_ref, k_ref, v_ref, o_ref, m_ref, l_ref, acc_ref, *, blk, sub):
    n = pl.program_id(2)
    i = qi_ref[n]
    j = kj_ref[n]
    hb = q_ref.shape[0]

    @pl.when(j == 0)
    def _():
        m_ref[...] = jnp.full_like(m_ref, -jnp.inf)
        l_ref[...] = jnp.zeros_like(l_ref)
        acc_ref[...] = jnp.zeros_like(acc_ref)

    def update(masked):
        chains = [(h, r) for h in range(hb) for r in range(blk // sub)]

        def qk(h, r):
            nk = (r + 1) * sub if masked else blk
            s = lax.dot_general(q_ref[h, r * sub:(r + 1) * sub, :], k_ref[h, :nk, :], (((1,), (1,)), ((), ())),
                                preferred_element_type=F32)
            if masked:
                row = lax.broadcasted_iota(jnp.int32, (sub, nk), 0) + r * sub
                s = jnp.where(row >= lax.broadcasted_iota(jnp.int32, (sub, nk), 1), s, -jnp.inf)
            return s

        def softmax_pv(h, r, s):
            rows = slice(r * sub, (r + 1) * sub)
            m_old = m_ref[h, rows, :]
            m_new = jnp.maximum(m_old, jnp.max(s, axis=-1, keepdims=True))
            alpha = jnp.exp(m_old - m_new)
            p = jnp.exp(s - m_new[:, :1])
            l_ref[h, rows, :] = alpha * l_ref[h, rows, :] + jnp.sum(p, axis=-1, keepdims=True)
            m_ref[h, rows, :] = m_new
            pv = jnp.dot(p.astype(BF16), v_ref[h, :s.shape[1], :], preferred_element_type=F32)
            acc_ref[h, rows, :] = alpha * acc_ref[h, rows, :] + pv

        s_prev = qk(*chains[0])
        for c in range(len(chains)):
            s_next = qk(*chains[c + 1]) if c + 1 < len(chains) else None
            softmax_pv(*chains[c], s_prev)
            s_prev = s_next

    @pl.when(j < i)
    def _():
        update(False)

    @pl.when(j == i)
    def _():
        update(True)
        o_ref[...] = (acc_ref[...] / l_ref[...]).astype(o_ref.dtype)


def _flash_attention(q, k, v, *, blk=1024, sub=256, hb=2):
    b, h, s, d = q.shape
    blk = min(blk, s)
    sub = min(sub, blk)
    nb = s // blk
    qi = np.array([i for i in range(nb) for j in range(i + 1)], np.int32)
    kj = np.array([j for i in range(nb) for j in range(i + 1)], np.int32)
    q_spec = pl.BlockSpec((None, hb, blk, d), lambda bb, hh, n, qi_r, kj_r: (bb, hh, qi_r[n], 0))
    kv_spec = pl.BlockSpec((None, hb, blk, d), lambda bb, hh, n, qi_r, kj_r: (bb, hh, kj_r[n], 0))
    return pl.pallas_call(
        functools.partial(_flash_kernel, blk=blk, sub=sub),
        grid_spec=pltpu.PrefetchScalarGridSpec(
            num_scalar_prefetch=2,
            grid=(b, h // hb, len(qi)),
            in_specs=[q_spec, kv_spec, kv_spec],
            out_specs=q_spec,
            scratch_shapes=[pltpu.VMEM((hb, blk, LANES), F32), pltpu.VMEM((hb, blk, LANES), F32),
                            pltpu.VMEM((hb, blk, d), F32)],
        ),
        out_shape=jax.ShapeDtypeStruct((b, h, s, d), BF16),
        compiler_params=_cparams(("arbitrary", "arbitrary", "arbitrary")),
        name="flash",
    )(jnp.asarray(qi), jnp.asarray(kj), q, k, v)


def _dot(a, b):
    return jnp.dot(a.astype(BF16), b.astype(BF16), preferred_element_type=F32)


def _dot_nt(a, b):
    return lax.dot_general(a.astype(BF16), b.astype(BF16), (((1,), (1,)), ((), ())), preferred_element_type=F32)


def _dot_tn(a, b):
    return jnp.dot(a.T.astype(BF16), b.astype(BF16), preferred_element_type=F32)


def _split3(x):
    hi = x.astype(BF16)
    r1 = x - hi.astype(F32)
    mid = r1.astype(BF16)
    return hi, mid, (r1 - mid.astype(F32)).astype(BF16)


def _cumsum_rows(tri, x):
    s = jnp.dot(tri, jnp.concatenate(_split3(x), axis=1), preferred_element_type=F32)
    n = x.shape[1]
    return s[:, :n] + s[:, n:2 * n] + s[:, 2 * n:]


def _head_sum(x):
    blk = (lax.broadcasted_iota(jnp.int32, (LANES, LANES), 0) // HEAD_DIM
           == lax.broadcasted_iota(jnp.int32, (LANES, LANES), 1) // HEAD_DIM).astype(BF16)
    rows = x.shape[0]
    outs = []
    for p in range(WIDTH // LANES):
        hi, mid, _ = _split3(x[:, p * LANES:(p + 1) * LANES])
        s = jnp.dot(jnp.concatenate([hi, mid], axis=0), blk, preferred_element_type=F32)
        outs.append(s[:rows] + s[rows:])
    return jnp.concatenate(outs, axis=1)


def _shift_rows(x, tail, j):
    rolled = pltpu.roll(x, j, 0)
    top = jnp.where(lax.broadcasted_iota(jnp.int32, (SUBLANES, x.shape[1]), 0) < j,
                    pltpu.roll(tail, j, 0), rolled[:SUBLANES])
    return jnp.concatenate([top, rolled[SUBLANES:]], axis=0)


def _softplus(x):
    return jnp.maximum(x, 0.0) + jnp.log(1.0 + jnp.exp(-jnp.abs(x)))


def _pair_masks():
    lane = lax.broadcasted_iota(jnp.int32, (CHUNK, LANES), 1)
    m0 = (lane < HEAD_DIM).astype(F32)
    return m0, 1.0 - m0


def _ext(x, m0, m1, dtype=BF16):
    return jnp.concatenate([x * m0, x * m1], axis=0).astype(dtype)


def _cat(xs, axis):
    return jnp.concatenate(xs, axis=axis)


def _neumann_inverse(mats):
    n = mats[0].shape[0]
    eye = (lax.broadcasted_iota(jnp.int32, (n, n), 0) == lax.broadcasted_iota(jnp.int32, (n, n), 1)).astype(F32)
    ts = [eye + a for a in mats]
    ps = [_dot(a, a) for a in mats]
    for _ in range(int(math.log2(CHUNK)) - 2):
        pps = [_dot(_cat([p, t], 0), p) for p, t in zip(ps, ts)]
        ps = [pp[:n] for pp in pps]
        ts = [t + pp[n:] for t, pp in zip(ts, pps)]
    return [t + _dot(t, p) for t, p in zip(ts, ps)]


def _work_items(rows):
    return [(slice(c * CHUNK, (c + 1) * CHUNK), slice(p * LANES, (p + 1) * LANES))
            for c in range(rows // CHUNK) for p in range(WIDTH // LANES)]


def _tiles(x, items):
    return [x[rows, cols] for rows, cols in items]


def _rwkv_scan(r, k, v, a, b, ld, st_ref, y_ref):
    c2 = 2 * CHUNK
    strict = lax.broadcasted_iota(jnp.int32, (c2, c2), 0) > lax.broadcasted_iota(jnp.int32, (c2, c2), 1)
    trow = lax.broadcasted_iota(jnp.int32, (CHUNK, c2), 0)
    tcol = lax.broadcasted_iota(jnp.int32, (CHUNK, c2), 1) & (CHUNK - 1)
    incl_w = tcol <= trow
    tri = (lax.broadcasted_iota(jnp.int32, (CHUNK, CHUNK), 0)
           >= lax.broadcasted_iota(jnp.int32, (CHUNK, CHUNK), 1)).astype(BF16)
    m0, m1 = _pair_masks()
    items = _work_items(r.shape[0])
    n_pairs = WIDTH // LANES
    r, k, v, a, b, ld = (_tiles(x, items) for x in (r, k, v, a, b, ld))
    cum = [_cumsum_rows(tri, x) for x in ld]
    e_pos = [jnp.exp(x) for x in cum]
    e_neg = [jnp.exp(-x) for x in cum]
    e_end = [jnp.exp(x[CHUNK - 1:CHUNK, :] - x) for x in cum]
    a_ext = [_ext(ai * jnp.exp(ci - li), m0, m1) for ai, ci, li in zip(a, cum, ld)]
    v_ext = [_ext(x, m0, m1) for x in v]
    r_dec = [(ri * ei).astype(BF16) for ri, ei in zip(r, e_pos)]
    s1 = [_dot_nt(_cat([ae, rd], 0), _cat([_ext(bi * en, m0, m1), _ext(ki * en, m0, m1)], 0))
          for ae, rd, bi, ki, en in zip(a_ext, r_dec, b, k, e_neg)]
    a_ab = [jnp.where(strict, s[:c2, :c2], 0.0) for s in s1]
    a_ak = [jnp.where(strict, s[:c2, c2:], 0.0) for s in s1]
    a_rbk = [_cat([jnp.where(incl_w, s[c2:, :c2], 0.0), jnp.where(incl_w, s[c2:, c2:], 0.0)], 1).astype(BF16)
             for s in s1]
    akv = [_dot(x, ve) for x, ve in zip(a_ak, v_ext)]
    t_inv = _neumann_inverse(a_ab)
    wu0 = [_dot(t, _cat([ae, av.astype(BF16)], 1)) for t, ae, av in zip(t_inv, a_ext, akv)]
    bk_end = [_cat([_ext(bi * ee, m0, m1, F32), _ext(ki * ee, m0, m1, F32)], 0) for bi, ki, ee in zip(b, k, e_end)]
    e_last_t = [jnp.broadcast_to(e[CHUNK - 1:CHUNK, :], (LANES, LANES)).T for e in e_pos]
    st = [st_ref[p] for p in range(n_pairs)]
    for c in range(len(items) // n_pairs):
        idx = range(c * n_pairs, (c + 1) * n_pairs)
        z = [_dot(_cat([wu0[i][:, :LANES].astype(BF16), r_dec[i]], 0), s) for i, s in zip(idx, st)]
        uv = [_cat([(wu0[i][:, LANES:] + zi[:c2]).astype(BF16), v_ext[i]], 0) for i, zi in zip(idx, z)]
        for i, zi, uvi in zip(idx, z, uv):
            rows, cols = items[i]
            y_ref[rows, cols] = zi[c2:] + _dot(a_rbk[i], uvi)
        st = [s * e_last_t[i] + _dot_tn(bk_end[i], uvi) for i, s, uvi in zip(idx, st, uv)]
    for p in range(n_pairs):
        st_ref[p] = st[p]


def _gdn_scan(q, k, v, g, beta, st_ref, o_ref):
    c2 = 2 * CHUNK
    row = lax.broadcasted_iota(jnp.int32, (c2, c2), 0)
    col = lax.broadcasted_iota(jnp.int32, (c2, c2), 1)
    causal = ((row >= CHUNK) == (col >= CHUNK)) & (row >= col)
    strict = row > col
    tri = (lax.broadcasted_iota(jnp.int32, (CHUNK, CHUNK), 0)
           >= lax.broadcasted_iota(jnp.int32, (CHUNK, CHUNK), 1)).astype(BF16)
    m0, m1 = _pair_masks()
    items = _work_items(q.shape[0])
    n_pairs = WIDTH // LANES
    q, k, v, g, beta = (_tiles(x, items) for x in (q, k, v, g, beta))
    gc = [_cumsum_rows(tri, x) for x in g]
    g_rows = [_cat([jnp.broadcast_to(x[:, 0:1], (CHUNK, LANES)),
                    jnp.broadcast_to(x[:, HEAD_DIM:HEAD_DIM + 1], (CHUNK, LANES))], 0) for x in gc]
    gamma = [jnp.exp(jnp.where(causal, x - x.T, -jnp.inf)) for x in g_rows]
    e_pos = [jnp.exp(x) for x in gc]
    kb = [ki * bi for ki, bi in zip(k, beta)]
    s1 = [_dot_nt(_cat([_ext(kbi, m0, m1), qi.astype(BF16)], 0), _ext(ki, m0, m1)) for kbi, qi, ki in zip(kb, q, k)]
    a_neg = [-jnp.where(strict, s[:c2] * ga, 0.0) for s, ga in zip(s1, gamma)]
    a_in = [(s[c2:] * (ga[:CHUNK] + ga[CHUNK:])).astype(BF16) for s, ga in zip(s1, gamma)]
    t_inv = _neumann_inverse(a_neg)
    uw = [_dot(t, _cat([_ext(vi * bi, m0, m1), _ext(kbi * ei, m0, m1)], 1))
          for t, vi, bi, kbi, ei in zip(t_inv, v, beta, kb, e_pos)]
    q_dec = [(qi * ei).astype(BF16) for qi, ei in zip(q, e_pos)]
    k_dec = [_ext(ki * jnp.exp(x[CHUNK - 1:CHUNK, :] - x), m0, m1, F32) for ki, x in zip(k, gc)]
    st = [st_ref[p] for p in range(n_pairs)]
    for c in range(len(items) // n_pairs):
        idx = range(c * n_pairs, (c + 1) * n_pairs)
        z = [_dot(_cat([uw[i][:, LANES:].astype(BF16), q_dec[i]], 0), s) for i, s in zip(idx, st)]
        v_new = [(uw[i][:, :LANES] - zi[:c2]).astype(BF16) for i, zi in zip(idx, z)]
        for i, zi, vn in zip(idx, z, v_new):
            rows, cols = items[i]
            o_ref[rows, cols] = zi[c2:] + _dot(a_in[i], vn)
        st = [s * e_pos[i][CHUNK - 1:CHUNK, :] + _dot_tn(k_dec[i], vn) for i, s, vn in zip(idx, st, v_new)]
    for p in range(n_pairs):
        st_ref[p] = st[p]


def _rwkv_kernel(*refs, vres):
    if vres:
        z_ref, vf_ref, mu_ref, prm_ref, wup_ref, aup_ref, gup_ref, vup_ref, o_ref, st_ref, tail_ref, y_ref = refs
    else:
        z_ref, mu_ref, prm_ref, wup_ref, aup_ref, gup_ref, o_ref, vf_out_ref, st_ref, tail_ref, y_ref = refs

    @pl.when(pl.program_id(1) == 0)
    def _():
        st_ref[...] = jnp.zeros_like(st_ref)
        tail_ref[...] = jnp.zeros_like(tail_ref)

    z = z_ref[...]
    rows = z.shape[0]
    zs = _shift_rows(z, tail_ref[...], 1)
    tail_ref[...] = z[rows - SUBLANES:]
    zl = z[:, :RWKV_IN] + (zs[:, :RWKV_IN] - z[:, :RWKV_IN]) * mu_ref[...]
    r, k, v = zl[:, :WIDTH], zl[:, WIDTH:2 * WIDTH], zl[:, 2 * WIDTH:3 * WIDTH]
    lo = zl[:, 3 * WIDTH:3 * WIDTH + LANES]
    g_lo = zl[:, 3 * WIDTH + LANES:]
    w0, a0, k_k, k_a, ln_g, ln_b, r_k, v_bias = (prm_ref[i:i + 1, :] for i in range(8))
    log_w = -_softplus(-(w0 + _dot(jnp.tanh(lo), wup_ref[...]))) - 0.5
    ld = -jnp.exp(log_w)
    iclr = jax.nn.sigmoid(a0 + _dot(lo, aup_ref[...]))
    gate = _dot(jax.nn.sigmoid(g_lo), gup_ref[...])
    if vres:
        x = z[:, RWKV_IN:] + pltpu.roll(zs[:, RWKV_IN:], LANES - VRES_LORA, 1)
        v = v + (vf_ref[...] - v) * jax.nn.sigmoid(v_bias + _dot(x, vup_ref[...]))
    else:
        vf_out_ref[...] = v
    kk = k * k_k
    kk = kk * lax.rsqrt(_head_sum(kk * kk) + L2_EPS)
    k = k * (1.0 + (iclr - 1.0) * k_a)
    _rwkv_scan(r, k, v, -kk, kk * iclr, ld, st_ref, y_ref)
    y = y_ref[...]
    d = y - _head_sum(y) * (1.0 / HEAD_DIM)
    y = d * lax.rsqrt(_head_sum(d * d) * (1.0 / HEAD_DIM) + RWKV_GN_EPS) * ln_g + ln_b
    y = y + _head_sum(r * k * r_k) * v
    o_ref[...] = (y * gate).astype(o_ref.dtype)


def _rwkv(z, v_first, b, s, mu, prm, w_up, a_up, g_up, v_up):
    vres = v_first is not None
    rows = min(SCAN_ROWS, s)
    zw = z.shape[1]
    row = lambda width: pl.BlockSpec((rows, width), lambda bb, c: (bb * (s // rows) + c, 0))
    full = lambda a: pl.BlockSpec(a.shape, lambda bb, c: (0, 0))
    zero = jnp.zeros((LANES // 2, WIDTH), F32)
    consts = [mu.reshape(1, -1), prm, _cat([w_up, zero], 0).astype(BF16), _cat([zero, a_up], 0).astype(BF16),
              g_up.astype(BF16)]
    args, in_specs = [z], [row(zw)]
    if vres:
        args.append(v_first)
        in_specs.append(row(WIDTH))
        consts.append(jnp.pad(v_up, ((0, LANES - VRES_LORA), (0, 0))).astype(BF16))
    out_shape = [jax.ShapeDtypeStruct((b * s, WIDTH), BF16)]
    out_specs = [row(WIDTH)]
    if not vres:
        out_shape.append(jax.ShapeDtypeStruct((b * s, WIDTH), F32))
        out_specs.append(row(WIDTH))
    outs = pl.pallas_call(
        functools.partial(_rwkv_kernel, vres=vres),
        grid=(b, s // rows),
        in_specs=in_specs + [full(a) for a in consts],
        out_specs=out_specs,
        out_shape=out_shape,
        scratch_shapes=[pltpu.VMEM((WIDTH // LANES, LANES, LANES), F32), pltpu.VMEM((SUBLANES, zw), F32),
                        pltpu.VMEM((rows, WIDTH), F32)],
        compiler_params=_cparams(("arbitrary", "arbitrary")),
        name="rwkv",
    )(*args, *consts)
    return (outs[0], v_first) if vres else (outs[0], outs[1])


def _gdn_kernel(z_ref, cw_ref, prm_ref, exp_ref, o_ref, st_ref, tail_ref, y_ref):
    @pl.when(pl.program_id(1) == 0)
    def _():
        st_ref[...] = jnp.zeros_like(st_ref)
        tail_ref[...] = jnp.zeros_like(tail_ref)

    rows = z_ref.shape[0]
    x = z_ref[:, :GDN_QKV]
    tail = tail_ref[...]
    conv = x * cw_ref[GDN_CONV - 1:GDN_CONV, :]
    for j in range(1, GDN_CONV):
        conv = conv + _shift_rows(x, tail, j) * cw_ref[GDN_CONV - 1 - j:GDN_CONV - j, :]
    tail_ref[...] = x[rows - SUBLANES:]
    qkv = conv * jax.nn.sigmoid(conv)
    q, k, v = qkv[:, :WIDTH], qkv[:, WIDTH:2 * WIDTH], qkv[:, 2 * WIDTH:]
    q = q * lax.rsqrt(_head_sum(q * q) + L2_EPS) * (HEAD_DIM ** -0.5)
    k = k * lax.rsqrt(_head_sum(k * k) + L2_EPS)
    logits = jnp.dot(_cat(_split3(z_ref[:, GDN_QKV + WIDTH:GDN_QKV + WIDTH + LANES]), 0), exp_ref[...],
                     preferred_element_type=F32)
    logits = logits[:rows] + logits[rows:2 * rows] + logits[2 * rows:]
    neg_a, dt_bias, norm_g = (prm_ref[i:i + 1, :] for i in range(3))
    beta = jax.nn.sigmoid(logits[:, :WIDTH])
    g = neg_a * _softplus(logits[:, WIDTH:] + dt_bias)
    _gdn_scan(q, k, v, g, beta, st_ref, y_ref)
    o = y_ref[...]
    o = o * lax.rsqrt(_head_sum(o * o) * (1.0 / HEAD_DIM) + NORM_EPS) * norm_g
    gate = z_ref[:, GDN_QKV:GDN_QKV + WIDTH]
    o_ref[...] = (o * (gate * jax.nn.sigmoid(gate))).astype(o_ref.dtype)


def _gdn(z, b, s, conv_w, a_log, dt_bias, norm_g):
    rows = min(SCAN_ROWS, s)
    zw = z.shape[1]
    row = lambda width: pl.BlockSpec((rows, width), lambda bb, c: (bb * (s // rows) + c, 0))
    full = lambda a: pl.BlockSpec(a.shape, lambda bb, c: (0, 0))
    per_lane = lambda t: jnp.repeat(t, HEAD_DIM)
    prm = jnp.stack([per_lane(-jnp.exp(a_log)), per_lane(dt_bias), jnp.tile(norm_g, N_HEADS)])
    head_of_lane = np.arange(WIDTH) // HEAD_DIM
    expand = np.zeros((LANES, 2 * WIDTH), np.float32)
    expand[head_of_lane, np.arange(WIDTH)] = 1.0
    expand[N_HEADS + head_of_lane, WIDTH + np.arange(WIDTH)] = 1.0
    consts = [conv_w, prm, jnp.asarray(expand, BF16)]
    return pl.pallas_call(
        _gdn_kernel,
        grid=(b, s // rows),
        in_specs=[row(zw)] + [full(a) for a in consts],
        out_specs=row(WIDTH),
        out_shape=jax.ShapeDtypeStruct((b * s, WIDTH), BF16),
        scratch_shapes=[pltpu.VMEM((WIDTH // LANES, LANES, LANES), F32), pltpu.VMEM((SUBLANES, GDN_QKV), F32),
                        pltpu.VMEM((rows, WIDTH), F32)],
        compiler_params=_cparams(("arbitrary", "arbitrary")),
        name="gdn",
    )(z, *consts)


def _merge_kernel(h_ref, u_ref, oa_ref, ob_ref, oc_ref, wg_ref, wa_ref, wb_ref, wc_ref, wo_ref, out_ref):
    d = h_ref.shape[1]
    u = u_ref[...]
    ob = jnp.concatenate([ob_ref[h] for h in range(N_HEADS)], axis=1)
    merged = jnp.zeros(h_ref.shape, F32)
    for n, (o, w_ref) in enumerate(((oa_ref[...], wa_ref), (ob, wb_ref), (oc_ref[...], wc_ref))):
        proj = jnp.dot(o, w_ref[...], preferred_element_type=F32)
        zg = jnp.dot(u, wg_ref[:, n * d:(n + 1) * d], preferred_element_type=F32)
        merged = merged + jax.nn.sigmoid(zg) * proj
    out_ref[...] = h_ref[...] + jnp.dot(merged.astype(BF16), wo_ref[...], preferred_element_type=F32)


def _merge(h, u, oa, ob, oc, wg, wa, wb, wc, wo, s, *, bm=512):
    m, d = h.shape
    bm = min(bm, s)
    row = lambda width: pl.BlockSpec((bm, width), lambda i: (i, 0))
    full = lambda w: pl.BlockSpec(w.shape, lambda i: (0, 0))
    wb = jnp.pad(wb.reshape(N_HEADS, HEAD_DIM, d), ((0, 0), (0, LANES - HEAD_DIM), (0, 0))).reshape(N_HEADS * LANES, d)
    ws = [w.astype(BF16) for w in (wg, wa, wb, wc, wo)]
    ob_spec = pl.BlockSpec((None, N_HEADS, bm, LANES), lambda i: (i // (s // bm), 0, i % (s // bm), 0))
    return pl.pallas_call(
        _merge_kernel,
        grid=(m // bm,),
        in_specs=[row(d), row(d), row(WIDTH), ob_spec, row(WIDTH)] + [full(w) for w in ws],
        out_specs=row(d),
        out_shape=jax.ShapeDtypeStruct((m, d), F32),
        compiler_params=_cparams(("arbitrary",)),
        name="merge",
    )(h, u, oa, ob, oc, *ws)


def _ffn_kernel(be_ref, nu_ref, x_ref, g_ref, wg_ref, wu_ref, wd_ref, o_ref, xs_ref, acc_ref, *, nf, dense):
    i = pl.program_id(0)
    f = pl.program_id(1)
    used = i < nu_ref[0]

    @pl.when(used & (f == 0))
    def _():
        x = x_ref[...]
        ms = jnp.mean(x * x, axis=-1, keepdims=True)
        xs_ref[...] = (x * lax.rsqrt(ms + NORM_EPS) * g_ref[...]).astype(BF16)

    @pl.when(f == 0)
    def _():
        acc_ref[...] = jnp.zeros_like(acc_ref)

    @pl.when(used)
    def _():
        xs = xs_ref[...]
        gate = jnp.dot(xs, wg_ref[...].astype(BF16), preferred_element_type=F32)
        up = jnp.dot(xs, wu_ref[...].astype(BF16), preferred_element_type=F32)
        act = (jax.nn.silu(gate) * up).astype(BF16)
        acc_ref[...] += jnp.dot(act, wd_ref[...].astype(BF16), preferred_element_type=F32)

    @pl.when(f == nf - 1)
    def _():
        o_ref[...] = (x_ref[...] + acc_ref[...]) if dense else acc_ref[...]


def _ffn(x, g, wg, wu, wd, block_e, n_used, *, bm, tf, dense):
    r, d = x.shape
    ff = wg.shape[2]
    nf = ff // tf

    def ff_idx(i, f, be, nu):
        return jnp.where(i < nu[0], f, nf - 1)

    return pl.pallas_call(
        functools.partial(_ffn_kernel, nf=nf, dense=dense),
        grid_spec=pltpu.PrefetchScalarGridSpec(
            num_scalar_prefetch=2,
            grid=(r // bm, nf),
            in_specs=[
                pl.BlockSpec((bm, d), lambda i, f, be, nu: (i, 0)),
                pl.BlockSpec((1, d), lambda i, f, be, nu: (0, 0)),
                pl.BlockSpec((None, d, tf), lambda i, f, be, nu: (be[i], 0, ff_idx(i, f, be, nu))),
                pl.BlockSpec((None, d, tf), lambda i, f, be, nu: (be[i], 0, ff_idx(i, f, be, nu))),
                pl.BlockSpec((None, tf, d), lambda i, f, be, nu: (be[i], ff_idx(i, f, be, nu), 0)),
            ],
            out_specs=pl.BlockSpec((bm, d), lambda i, f, be, nu: (i, 0)),
            scratch_shapes=[pltpu.VMEM((bm, d), BF16), pltpu.VMEM((bm, d), F32)],
        ),
        out_shape=jax.ShapeDtypeStruct((r, d), F32),
        compiler_params=_cparams(("arbitrary", "arbitrary")),
        name="ffn",
    )(block_e, n_used, x, g.reshape(1, d).astype(F32), wg, wu, wd)


def _ple_kernel(*refs, combine):
    if combine:
        h_ref, y0_ref, y1_ref, route_ref, p_ref, wg_ref, wp_ref, g_ref, o_ref = refs
        h = h_ref[...] + route_ref[:, 2:3] * y0_ref[...] + route_ref[:, 3:4] * y1_ref[...]
    else:
        h_ref, p_ref, wg_ref, wp_ref, g_ref, o_ref = refs
        h = h_ref[...]
    gate = jnp.dot(h.astype(BF16), wg_ref[...], preferred_element_type=F32)
    e = jnp.dot(p_ref[...].astype(BF16), wp_ref[...], preferred_element_type=F32)
    ms = jnp.mean(e * e, axis=-1, keepdims=True)
    e = e * lax.rsqrt(ms + NORM_EPS) * g_ref[...]
    o_ref[...] = h + jax.nn.sigmoid(gate) * e


def _ple(h, p, w_gate, w_proj, g, expert_out=None, *, bm=512):
    m, d = h.shape
    bm = min(bm, m)
    row = lambda a: pl.BlockSpec((bm, a.shape[1]), lambda i: (i, 0))
    full = lambda a: pl.BlockSpec(a.shape, lambda i: (0, 0))
    rows = [h] + list(expert_out or ()) + [p]
    consts = [w_gate.astype(BF16), w_proj.astype(BF16), g.reshape(1, d)]
    return pl.pallas_call(
        functools.partial(_ple_kernel, combine=expert_out is not None),
        grid=(m // bm,),
        in_specs=[row(a) for a in rows] + [full(a) for a in consts],
        out_specs=row(h),
        out_shape=jax.ShapeDtypeStruct((m, d), F32),
        compiler_params=_cparams(("arbitrary",)),
        name="ple",
    )(*rows, *consts)


def _pad_cols(w, n):
    return jnp.pad(w, ((0, 0), (0, n - w.shape[1])))


def _rope_tables(positions):
    t = positions.size
    inv_freq = 1.0 / (ROPE_THETA ** (jnp.arange(0, QK_ROPE, 2, dtype=F32) / QK_ROPE))
    ang = positions.astype(F32).reshape(t, 1) * inv_freq
    cos, sin = jnp.cos(ang), jnp.sin(ang)
    half = QK_ROPE // 2
    ones = jnp.ones((t, QK_NOPE), F32)
    zeros = jnp.zeros((t, QK_NOPE), F32)
    z16 = jnp.zeros((t, half), F32)
    tail1 = jnp.ones((t, LANES - QK_DIM), F32)
    tail0 = jnp.zeros((t, LANES - QK_DIM), F32)
    return (_cat([ones, cos, cos, tail1], 1), _cat([zeros, -sin, z16, tail0], 1), _cat([zeros, z16, sin, tail0], 1))


def _moe(h, g, w_router, wg, wu, wd, *, bm=768, tf=512):
    t, d = h.shape
    n_assign = t * TOP_K
    n_blocks = -(-(n_assign + N_EXPERTS * (bm - 1)) // bm)
    route, counts = _router(h, g, w_router)
    counts = counts[0, :N_EXPERTS].astype(jnp.int32)
    padded = (counts + bm - 1) // bm * bm
    pad_end = jnp.cumsum(padded)
    top_e = route[:, :TOP_K].astype(jnp.int32)
    dest = (pad_end - padded)[top_e] + route[:, 4:4 + TOP_K].astype(jnp.int32)
    src_tok = jnp.zeros((n_blocks * bm,), jnp.int32).at[dest.reshape(-1)].set(
        jnp.arange(n_assign, dtype=jnp.int32) // TOP_K)
    block_e = jnp.minimum(jnp.searchsorted(pad_end, jnp.arange(n_blocks) * bm, side='right'),
                          N_EXPERTS - 1).astype(jnp.int32)
    n_used = (pad_end[-1] // bm).astype(jnp.int32).reshape(1)
    y_rows = _ffn(h[src_tok], g, wg, wu, wd, block_e, n_used, bm=bm, tf=tf, dense=False)
    return y_rows[dest[:, 0]], y_rows[dest[:, 1]], route


def kernel(x, p, positions, norm_mix_g, w_in, rwkv_mu, rwkv_w0, rwkv_w_up, rwkv_a0, rwkv_a_up, rwkv_g_up, rwkv_k_k, rwkv_k_a, rwkv_r_k, rwkv_ln_g, rwkv_ln_b, vres_mu, vres_down, vres_up, vres_b, mla_q_norm_g, mla_kv_norm_g, mla_w_uq, mla_w_ukv, mla_qk_norm_q, mla_qk_norm_k, gdn_conv_w, gdn_a_log, gdn_dt_bias, gdn_norm_g, w_br_rwkv, w_br_mla, w_br_gdn, w_out, norm_ffn_g, ffn_wg, ffn_wu, ffn_wd, moe_router, moe_wg, moe_wu, moe_wd, ple_proj, ple_gate, ple_norm_g):
    b, s, d = x.shape
    t = b * s
    depth = w_in.shape[0]
    rope_tabs = _rope_tables(positions)

    h = x.reshape(t, d)
    v_first = None
    for i in range(depth):
        w = w_in[i]
        g = norm_mix_g[i]
        w_rwkv = w[:, :RWKV_IN]
        if i > 0:
            mu_v = vres_mu[i - 1][:, None]
            vd = vres_down[i - 1]
            w_rwkv = _pad_cols(_cat([w_rwkv, (1.0 - mu_v) * vd, mu_v * vd], 1), RWKV_IN + LANES)
        w_gdn = w[:, RWKV_IN + MLA_IN:RWKV_IN + MLA_IN + GDN_IN]
        w_gdn = _pad_cols(_cat([w_gdn[:, :GDN_QKV], w_gdn[:, GDN_QKV + 2 * N_HEADS:],
                                w_gdn[:, GDN_QKV:GDN_QKV + 2 * N_HEADS]], 1), GDN_QKV + WIDTH + 2 * LANES)
        u = _norm(h, g)
        z_rwkv = _mm(u, w_rwkv)
        z_gdn = _mm(u, w_gdn)
        prm = jnp.stack([rwkv_w0[i], rwkv_a0[i], rwkv_k_k[i], rwkv_k_a[i], rwkv_ln_g[i], rwkv_ln_b[i],
                         rwkv_r_k[i].reshape(-1), vres_b[i - 1] if i > 0 else jnp.zeros((WIDTH,), F32)])
        o_a, v_first = _rwkv(z_rwkv, v_first, b, s, rwkv_mu[i], prm, rwkv_w_up[i], rwkv_a_up[i], rwkv_g_up[i],
                             vres_up[i - 1] if i > 0 else None)
        q, k, v = _mla_prep(u, w[:, RWKV_IN:RWKV_IN + MLA_IN], rope_tabs, b, s, mla_q_norm_g[i], mla_kv_norm_g[i],
                            mla_w_uq[i], mla_w_ukv[i], mla_qk_norm_q[i], mla_qk_norm_k[i])
        o_b = _flash_attention(q, k, v)
        o_c = _gdn(z_gdn, b, s, gdn_conv_w[i], gdn_a_log[i], gdn_dt_bias[i], gdn_norm_g[i])
        h = _merge(h, u, o_a, o_b, o_c, w[:, RWKV_IN + MLA_IN + GDN_IN:], w_br_rwkv[i], w_br_mla[i], w_br_gdn[i],
                   w_out[i], s)
        j = i // 2
        if i % 2 == 0:
            bm = min(1024, t)
            h = _ffn(h, norm_ffn_g[i], ffn_wg[j:j + 1], ffn_wu[j:j + 1], ffn_wd[j:j + 1],
                     jnp.zeros((t // bm,), jnp.int32), jnp.full((1,), t // bm, jnp.int32), bm=bm, tf=256, dense=True)
            expert_out = None
        else:
            expert_out = _moe(h, norm_ffn_g[i], moe_router[j], moe_wg[j], moe_wu[j], moe_wd[j])
        h = _ple(h, p[i].reshape(t, -1), ple_gate[i], ple_proj[i], ple_norm_g[i], expert_out)
    return h.reshape(b, s, d)
```

```python
import functools
import math

import jax
import jax.numpy as jnp
import numpy as np
from jax import lax
from jax.experimental import pallas as pl
from jax.experimental.pallas import tpu as pltpu

F32 = jnp.float32
BF16 = jnp.bfloat16
HI = lax.Precision.HIGHEST

NORM_EPS = 1e-6
RWKV_GN_EPS = 64e-5
L2_EPS = 1e-12
N_HEADS = 8
HEAD_DIM = 64
WIDTH = N_HEADS * HEAD_DIM
RWKV_LORA = 256
RWKV_IN = 3 * WIDTH + RWKV_LORA
VRES_LORA = 32
QK_NOPE = 64
QK_ROPE = 32
QK_DIM = QK_NOPE + QK_ROPE
Q_LORA = 256
KV_LORA = 128
MLA_IN = Q_LORA + KV_LORA + QK_ROPE
ROPE_THETA = 10000.0
GDN_CONV = 4
GDN_QKV = 3 * WIDTH
GDN_IN = GDN_QKV + 2 * N_HEADS + WIDTH
N_EXPERTS = 8
TOP_K = 2
LANES = 128
SUBLANES = 8
CHUNK = 64
SCAN_ROWS = 2 * CHUNK
FFN_SUB_ROWS = 256
VMEM_LIMIT = 48 * 1024 * 1024


def _cparams(sem):
    return pltpu.CompilerParams(dimension_semantics=sem, vmem_limit_bytes=VMEM_LIMIT)


def _pick_bn(n, cap=2304):
    units = n // LANES
    best = 1
    for d in range(1, units + 1):
        if units % d == 0 and d * LANES <= cap:
            best = d
    return best * LANES


def _norm_kernel(x_ref, g_ref, u_ref):
    x = x_ref[...]
    u_ref[...] = (x * lax.rsqrt(jnp.mean(x * x, axis=-1, keepdims=True) + NORM_EPS) * g_ref[...]).astype(u_ref.dtype)


def _norm(x, g, *, bm=1024):
    m, d = x.shape
    bm = min(bm, m)
    return pl.pallas_call(
        _norm_kernel,
        grid=(m // bm,),
        in_specs=[pl.BlockSpec((bm, d), lambda i: (i, 0)), pl.BlockSpec((1, d), lambda i: (0, 0))],
        out_specs=pl.BlockSpec((bm, d), lambda i: (i, 0)),
        out_shape=jax.ShapeDtypeStruct((m, d), BF16),
        compiler_params=_cparams(("arbitrary",)),
        name="norm",
    )(x, g.reshape(1, d))


def _router_kernel(x_ref, g_ref, wr_ref, route_ref, counts_ref, carry_ref):
    @pl.when(pl.program_id(0) == 0)
    def _():
        carry_ref[...] = jnp.zeros_like(carry_ref)

    x = x_ref[...]
    rows = x.shape[0]
    u = x * lax.rsqrt(jnp.mean(x * x, axis=-1, keepdims=True) + NORM_EPS) * g_ref[...]
    lane = lax.broadcasted_iota(jnp.int32, (rows, LANES), 1)
    logits = jnp.where(lane < N_EXPERTS, jnp.dot(u, wr_ref[...], preferred_element_type=F32, precision=HI), -jnp.inf)
    m1 = jnp.max(logits, axis=-1, keepdims=True)
    i1 = jnp.min(jnp.where(logits == m1, lane, LANES), axis=-1, keepdims=True)
    rest = jnp.where(lane == i1, -jnp.inf, logits)
    m2 = jnp.max(rest, axis=-1, keepdims=True)
    i2 = jnp.min(jnp.where(rest == m2, lane, LANES), axis=-1, keepdims=True)
    e2 = jnp.exp(m2 - m1)
    w1 = 1.0 / (1.0 + e2)
    w2 = e2 / (1.0 + e2)
    hit1 = lane == i1
    hit2 = lane == i2
    onehot = jnp.where(hit1 | hit2, 1.0, 0.0)
    before = (lax.broadcasted_iota(jnp.int32, (rows, rows), 0) > lax.broadcasted_iota(jnp.int32, (rows, rows), 1))
    seen = jnp.dot(before.astype(BF16), onehot.astype(BF16), preferred_element_type=F32) + carry_ref[0:1, :]
    r1 = jnp.sum(jnp.where(hit1, seen, 0.0), axis=-1, keepdims=True)
    r2 = jnp.sum(jnp.where(hit2, seen, 0.0), axis=-1, keepdims=True)
    carry_ref[...] = carry_ref[...] + jnp.sum(onehot, axis=0, keepdims=True)
    cols = (i1.astype(F32), i2.astype(F32), w1, w2, r1, r2)
    route = jnp.zeros((rows, LANES), F32)
    for n, c in enumerate(cols):
        route = jnp.where(lane == n, c, route)
    route_ref[...] = route
    counts_ref[...] = carry_ref[...]


def _router(h, g, w_router, *, bm=1024):
    m, d = h.shape
    bm = min(bm, m)
    return pl.pallas_call(
        _router_kernel,
        grid=(m // bm,),
        in_specs=[pl.BlockSpec((bm, d), lambda i: (i, 0)), pl.BlockSpec((1, d), lambda i: (0, 0)),
                  pl.BlockSpec((d, LANES), lambda i: (0, 0))],
        out_specs=[pl.BlockSpec((bm, LANES), lambda i: (i, 0)), pl.BlockSpec((SUBLANES, LANES), lambda i: (0, 0))],
        out_shape=[jax.ShapeDtypeStruct((m, LANES), F32), jax.ShapeDtypeStruct((SUBLANES, LANES), F32)],
        scratch_shapes=[pltpu.VMEM((SUBLANES, LANES), F32)],
        compiler_params=_cparams(("arbitrary",)),
        name="router",
    )(h, g.reshape(1, d), _pad_cols(w_router, LANES))


def _mm_kernel(x_ref, w_ref, o_ref, ws_ref):
    @pl.when(pl.program_id(1) == 0)
    def _():
        ws_ref[...] = w_ref[...].astype(BF16)

    o_ref[...] = jnp.dot(x_ref[...], ws_ref[...], preferred_element_type=F32).astype(o_ref.dtype)


def _mm(x, w, *, bm=1024, out_dtype=F32):
    m, k = x.shape
    n = w.shape[1]
    bm = min(bm, m)
    bn = _pick_bn(n, cap=1024)
    return pl.pallas_call(
        _mm_kernel,
        grid=(n // bn, m // bm),
        in_specs=[pl.BlockSpec((bm, k), lambda j, i: (i, 0)), pl.BlockSpec((k, bn), lambda j, i: (0, j))],
        out_specs=pl.BlockSpec((bm, bn), lambda j, i: (i, j)),
        out_shape=jax.ShapeDtypeStruct((m, n), out_dtype),
        scratch_shapes=[pltpu.VMEM((k, bn), BF16)],
        compiler_params=_cparams(("arbitrary", "arbitrary")),
        name="mm",
    )(x, w)


def _rms(x, n, g):
    return x * lax.rsqrt(jnp.sum(x * x, axis=-1, keepdims=True) * (1.0 / n) + NORM_EPS) * g


def _mla_prep_kernel(u_ref, cos_ref, sin_ref, win_ref, gq_ref, gkv_ref, gqh_ref, gkh_ref, wq_ref, wqr_ref, wk_ref,
                     wv_ref, q_ref, k_ref, v_ref):
    z = jnp.dot(u_ref[...], win_ref[...], preferred_element_type=F32)
    cq = _rms(z[:, :Q_LORA], Q_LORA, gq_ref[...]).astype(BF16)
    ckv = _rms(z[:, Q_LORA:Q_LORA + KV_LORA], KV_LORA, gkv_ref[...]).astype(BF16)
    k_pe = z[:, Q_LORA + KV_LORA:Q_LORA + KV_LORA + LANES]
    k_pe_rot = z[:, Q_LORA + KV_LORA + LANES:]
    q_all = jnp.dot(cq, wq_ref[...], preferred_element_type=F32)
    q_rot = jnp.dot(cq, wqr_ref[...], preferred_element_type=F32)
    k_all = jnp.dot(ckv, wk_ref[...], preferred_element_type=F32)
    v_all = jnp.dot(ckv, wv_ref[...], preferred_element_type=F32)
    sin = sin_ref[...]
    cos_q = cos_ref[...] * gqh_ref[...]
    cos_k = cos_ref[...] * gkh_ref[...]

    def scale(x):
        return lax.rsqrt(jnp.sum(x * x, axis=-1, keepdims=True) * (1.0 / QK_DIM) + NORM_EPS)

    for h in range(N_HEADS):
        sl = slice(h * LANES, (h + 1) * LANES)
        q = q_all[:, sl]
        k = k_all[:, sl] + k_pe
        q_ref[h] = (scale(q) * (QK_DIM ** -0.5) * (q * cos_q + q_rot[:, sl] * sin)).astype(BF16)
        k_ref[h] = (scale(k) * (k * cos_k + k_pe_rot * sin)).astype(BF16)
        v_ref[h] = v_all[:, sl].astype(BF16)


def _pad_heads(w, width):
    k = w.shape[0]
    w = w.reshape(k, N_HEADS, width)
    return jnp.pad(w, ((0, 0), (0, 0), (0, LANES - width))).reshape(k, N_HEADS * LANES)


def _mla_prep(u, w_in, rope_tabs, b, s, q_norm_g, kv_norm_g, w_uq, w_ukv, qk_g_q, qk_g_k, *, bm=512):
    bm = min(bm, s)
    half = QK_ROPE // 2
    swap = lambda w: _cat([w[..., half:], w[..., :half]], -1)
    rope_slab = lambda w: jnp.pad(w, ((0, 0), (QK_NOPE, LANES - QK_DIM)))
    lora = Q_LORA + KV_LORA
    w_in = _cat([w_in[:, :lora], rope_slab(w_in[:, lora:]), rope_slab(swap(w_in[:, lora:] * qk_g_k[QK_NOPE:]))],
                1).astype(BF16)
    w_ukv = w_ukv.reshape(KV_LORA, N_HEADS, QK_NOPE + HEAD_DIM)
    w_uq3 = w_uq.reshape(Q_LORA, N_HEADS, QK_DIM)
    wq = _pad_heads(w_uq, QK_DIM).astype(BF16)
    wq_rot = jnp.pad(swap(w_uq3[..., QK_NOPE:] * qk_g_q[QK_NOPE:]), ((0, 0), (0, 0), (QK_NOPE, LANES - QK_DIM)))
    wq_rot = wq_rot.reshape(Q_LORA, N_HEADS * LANES).astype(BF16)
    wk = _pad_heads(w_ukv[..., :QK_NOPE].reshape(KV_LORA, -1), QK_NOPE).astype(BF16)
    wv = _pad_heads(w_ukv[..., QK_NOPE:].reshape(KV_LORA, -1), HEAD_DIM).astype(BF16)
    pad_g = lambda g: jnp.pad(g, (0, LANES - g.shape[0])).reshape(1, LANES)
    row = lambda width: pl.BlockSpec((bm, width), lambda bb, i: (bb * (s // bm) + i, 0))
    full = lambda a: pl.BlockSpec(a.shape, lambda bb, i: (0, 0))
    consts = [w_in, q_norm_g.reshape(1, -1), kv_norm_g.reshape(1, -1), pad_g(qk_g_q), pad_g(qk_g_k), wq, wq_rot,
              wk, wv]
    out_spec = pl.BlockSpec((None, N_HEADS, bm, LANES), lambda bb, i: (bb, 0, i, 0))
    out_shape = jax.ShapeDtypeStruct((b, N_HEADS, s, LANES), BF16)
    return pl.pallas_call(
        _mla_prep_kernel,
        grid=(b, s // bm),
        in_specs=[row(u.shape[1]), row(LANES), row(LANES)] + [full(a) for a in consts],
        out_specs=[out_spec] * 3,
        out_shape=[out_shape] * 3,
        compiler_params=_cparams(("arbitrary", "arbitrary")),
        name="mla_prep",
    )(u, *rope_tabs, *consts)


def _flash_kernel(qi_ref, kj_ref, q_ref, k_ref, v_ref, o_ref, m_ref, l_ref, acc_ref, *, blk, sub):
    n = pl.program_id(2)
    i = qi_ref[n]
    j = kj_ref[n]
    hb = q_ref.shape[0]

    @pl.when(j == 0)
    def _():
        m_ref[...] = jnp.full_like(m_ref, -jnp.inf)
        l_ref[...] = jnp.zeros_like(l_ref)
        acc_ref[...] = jnp.zeros_like(acc_ref)

    def update(masked):
        chains = [(h, r) for h in range(hb) for r in range(blk // sub)]

        def qk(h, r):
            nk = (r + 1) * sub if masked else blk
            s = lax.dot_general(q_ref[h, r * sub:(r + 1) * sub, :], k_ref[h, :nk, :], (((1,), (1,)), ((), ())),
                                preferred_element_type=F32)
            if masked:
                row = lax.broadcasted_iota(jnp.int32, (sub, nk), 0) + r * sub
                s = jnp.where(row >= lax.broadcasted_iota(jnp.int32, (sub, nk), 1), s, -jnp.inf)
            return s

        def softmax_pv(h, r, s):
            rows = slice(r * sub, (r + 1) * sub)
            m_old = m_ref[h, rows, :]
            m_new = jnp.maximum(m_old, jnp.max(s, axis=-1, keepdims=True))
            alpha = jnp.exp(m_old - m_new)
            p = jnp.exp(s - m_new[:, :1])
            l_ref[h, rows, :] = alpha * l_ref[h, rows, :] + jnp.sum(p, axis=-1, keepdims=True)
            m_ref[h, rows, :] = m_new
            pv = jnp.dot(p.astype(BF16), v_ref[h, :s.shape[1], :], preferred_element_type=F32)
            acc_ref[h, rows, :] = alpha * acc_ref[h, rows, :] + pv

        s_prev = qk(*chains[0])
        for c in range(len(chains)):
            s_next = qk(*chains[c + 1]) if c + 1 < len(chains) else None
            softmax_pv(*chains[c], s_prev)
            s_prev = s_next

    @pl.when(j < i)
    def _():
        update(False)

    @pl.when(j == i)
    def _():
        update(True)
        o_ref[...] = (acc_ref[...] / l_ref[...]).astype(o_ref.dtype)


def _flash_attention(q, k, v, *, blk=1024, sub=256, hb=2):
    b, h, s, d = q.shape
    blk = min(blk, s)
    sub = min(sub, blk)
    nb = s // blk
    qi = np.array([i for i in range(nb) for j in range(i + 1)], np.int32)
    kj = np.array([j for i in range(nb) for j in range(i + 1)], np.int32)
    q_spec = pl.BlockSpec((None, hb, blk, d), lambda bb, hh, n, qi_r, kj_r: (bb, hh, qi_r[n], 0))
    kv_spec = pl.BlockSpec((None, hb, blk, d), lambda bb, hh, n, qi_r, kj_r: (bb, hh, kj_r[n], 0))
    return pl.pallas_call(
        functools.partial(_flash_kernel, blk=blk, sub=sub),
        grid_spec=pltpu.PrefetchScalarGridSpec(
            num_scalar_prefetch=2,
            grid=(b, h // hb, len(qi)),
            in_specs=[q_spec, kv_spec, kv_spec],
            out_specs=q_spec,
            scratch_shapes=[pltpu.VMEM((hb, blk, LANES), F32), pltpu.VMEM((hb, blk, LANES), F32),
                            pltpu.VMEM((hb, blk, d), F32)],
        ),
        out_shape=jax.ShapeDtypeStruct((b, h, s, d), BF16),
        compiler_params=_cparams(("arbitrary", "arbitrary", "arbitrary")),
        name="flash",
    )(jnp.asarray(qi), jnp.asarray(kj), q, k, v)


def _dot(a, b):
    return jnp.dot(a.astype(BF16), b.astype(BF16), preferred_element_type=F32)


def _dot_nt(a, b):
    return lax.dot_general(a.astype(BF16), b.astype(BF16), (((1,), (1,)), ((), ())), preferred_element_type=F32)


def _dot_tn(a, b):
    return jnp.dot(a.T.astype(BF16), b.astype(BF16), preferred_element_type=F32)


def _split3(x):
    hi = x.astype(BF16)
    r1 = x - hi.astype(F32)
    mid = r1.astype(BF16)
    return hi, mid, (r1 - mid.astype(F32)).astype(BF16)


def _cumsum_rows(tri, x):
    s = jnp.dot(tri, jnp.concatenate(_split3(x), axis=1), preferred_element_type=F32)
    n = x.shape[1]
    return s[:, :n] + s[:, n:2 * n] + s[:, 2 * n:]


def _head_sum(x):
    left = lax.broadcasted_iota(jnp.int32, (x.shape[0], LANES), 1) < HEAD_DIM
    outs = []
    for p in range(WIDTH // LANES):
        xs = x[:, p * LANES:(p + 1) * LANES]
        s_left = jnp.sum(jnp.where(left, xs, 0.0), axis=-1, keepdims=True)
        s_right = jnp.sum(jnp.where(left, 0.0, xs), axis=-1, keepdims=True)
        outs.append(jnp.where(left, s_left, s_right))
    return jnp.concatenate(outs, axis=1)


def _shift_rows(x, tail, j):
    rolled = pltpu.roll(x, j, 0)
    top = jnp.where(lax.broadcasted_iota(jnp.int32, (SUBLANES, x.shape[1]), 0) < j,
                    pltpu.roll(tail, j, 0), rolled[:SUBLANES])
    return jnp.concatenate([top, rolled[SUBLANES:]], axis=0)


def _softplus(x):
    return jnp.maximum(x, 0.0) + jnp.log(1.0 + jnp.exp(-jnp.abs(x)))


def _pair_masks():
    lane = lax.broadcasted_iota(jnp.int32, (CHUNK, LANES), 1)
    m0 = (lane < HEAD_DIM).astype(F32)
    return m0, 1.0 - m0


def _ext(x, m0, m1, dtype=BF16):
    return jnp.concatenate([x * m0, x * m1], axis=0).astype(dtype)


def _cat(xs, axis):
    return jnp.concatenate(xs, axis=axis)


def _neumann_inverse(mats):
    n = mats[0].shape[0]
    eye = (lax.broadcasted_iota(jnp.int32, (n, n), 0) == lax.broadcasted_iota(jnp.int32, (n, n), 1)).astype(F32)
    ts = [eye + a for a in mats]
    ps = [_dot(a, a) for a in mats]
    for _ in range(int(math.log2(CHUNK)) - 2):
        pps = [_dot(_cat([p, t], 0), p) for p, t in zip(ps, ts)]
        ps = [pp[:n] for pp in pps]
        ts = [t + pp[n:] for t, pp in zip(ts, pps)]
    return [t + _dot(t, p) for t, p in zip(ts, ps)]


def _work_items(rows):
    return [(slice(c * CHUNK, (c + 1) * CHUNK), slice(p * LANES, (p + 1) * LANES))
            for c in range(rows // CHUNK) for p in range(WIDTH // LANES)]


def _tiles(x, items):
    return [x[rows, cols] for rows, cols in items]


def _rwkv_scan(r, k, v, a, b, ld, st_ref, y_ref):
    c2 = 2 * CHUNK
    strict = lax.broadcasted_iota(jnp.int32, (c2, c2), 0) > lax.broadcasted_iota(jnp.int32, (c2, c2), 1)
    trow = lax.broadcasted_iota(jnp.int32, (CHUNK, c2), 0)
    tcol = lax.broadcasted_iota(jnp.int32, (CHUNK, c2), 1) & (CHUNK - 1)
    incl_w = tcol <= trow
    tri = (lax.broadcasted_iota(jnp.int32, (CHUNK, CHUNK), 0)
           >= lax.broadcasted_iota(jnp.int32, (CHUNK, CHUNK), 1)).astype(BF16)
    m0, m1 = _pair_masks()
    items = _work_items(r.shape[0])
    n_pairs = WIDTH // LANES
    r, k, v, a, b, ld = (_tiles(x, items) for x in (r, k, v, a, b, ld))
    cum = [_cumsum_rows(tri, x) for x in ld]
    e_pos = [jnp.exp(x) for x in cum]
    e_neg = [jnp.exp(-x) for x in cum]
    e_end = [jnp.exp(x[CHUNK - 1:CHUNK, :] - x) for x in cum]
    a_ext = [_ext(ai * jnp.exp(ci - li), m0, m1) for ai, ci, li in zip(a, cum, ld)]
    v_ext = [_ext(x, m0, m1) for x in v]
    r_dec = [(ri * ei).astype(BF16) for ri, ei in zip(r, e_pos)]
    s1 = [_dot_nt(_cat([ae, rd], 0), _cat([_ext(bi * en, m0, m1), _ext(ki * en, m0, m1)], 0))
          for ae, rd, bi, ki, en in zip(a_ext, r_dec, b, k, e_neg)]
    a_ab = [jnp.where(strict, s[:c2, :c2], 0.0) for s in s1]
    a_ak = [jnp.where(strict, s[:c2, c2:], 0.0) for s in s1]
    a_rbk = [_cat([jnp.where(incl_w, s[c2:, :c2], 0.0), jnp.where(incl_w, s[c2:, c2:], 0.0)], 1).astype(BF16)
             for s in s1]
    akv = [_dot(x, ve) for x, ve in zip(a_ak, v_ext)]
    t_inv = _neumann_inverse(a_ab)
    wu0 = [_dot(t, _cat([ae, av.astype(BF16)], 1)) for t, ae, av in zip(t_inv, a_ext, akv)]
    bk_end = [_cat([_ext(bi * ee, m0, m1, F32), _ext(ki * ee, m0, m1, F32)], 0) for bi, ki, ee in zip(b, k, e_end)]
    e_last_t = [jnp.broadcast_to(e[CHUNK - 1:CHUNK, :], (LANES, LANES)).T for e in e_pos]
    st = [st_ref[p] for p in range(n_pairs)]
    for c in range(len(items) // n_pairs):
        idx = range(c * n_pairs, (c + 1) * n_pairs)
        z = [_dot(_cat([wu0[i][:, :LANES].astype(BF16), r_dec[i]], 0), s) for i, s in zip(idx, st)]
        uv = [_cat([(wu0[i][:, LANES:] + zi[:c2]).astype(BF16), v_ext[i]], 0) for i, zi in zip(idx, z)]
        for i, zi, uvi in zip(idx, z, uv):
            rows, cols = items[i]
            y_ref[rows, cols] = zi[c2:] + _dot(a_rbk[i], uvi)
        st = [s * e_last_t[i] + _dot_tn(bk_end[i], uvi) for i, s, uvi in zip(idx, st, uv)]
    for p in range(n_pairs):
        st_ref[p] = st[p]


def _gdn_scan(q, k, v, g, beta, st_ref, o_ref):
    c2 = 2 * CHUNK
    row = lax.broadcasted_iota(jnp.int32, (c2, c2), 0)
    col = lax.broadcasted_iota(jnp.int32, (c2, c2), 1)
    causal = ((row >= CHUNK) == (col >= CHUNK)) & (row >= col)
    strict = row > col
    tri = (lax.broadcasted_iota(jnp.int32, (CHUNK, CHUNK), 0)
           >= lax.broadcasted_iota(jnp.int32, (CHUNK, CHUNK), 1)).astype(BF16)
    m0, m1 = _pair_masks()
    items = _work_items(q.shape[0])
    n_pairs = WIDTH // LANES
    q, k, v, g, beta = (_tiles(x, items) for x in (q, k, v, g, beta))
    gc = [_cumsum_rows(tri, x) for x in g]
    g_rows = [_cat([jnp.broadcast_to(x[:, 0:1], (CHUNK, LANES)),
                    jnp.broadcast_to(x[:, HEAD_DIM:HEAD_DIM + 1], (CHUNK, LANES))], 0) for x in gc]
    gamma = [jnp.exp(jnp.where(causal, x - x.T, -jnp.inf)) for x in g_rows]
    e_pos = [jnp.exp(x) for x in gc]
    kb = [ki * bi for ki, bi in zip(k, beta)]
    s1 = [_dot_nt(_cat([_ext(kbi, m0, m1), qi.astype(BF16)], 0), _ext(ki, m0, m1)) for kbi, qi, ki in zip(kb, q, k)]
    a_neg = [-jnp.where(strict, s[:c2] * ga, 0.0) for s, ga in zip(s1, gamma)]
    a_in = [(s[c2:] * (ga[:CHUNK] + ga[CHUNK:])).astype(BF16) for s, ga in zip(s1, gamma)]
    t_inv = _neumann_inverse(a_neg)
    uw = [_dot(t, _cat([_ext(vi * bi, m0, m1), _ext(kbi * ei, m0, m1)], 1))
          for t, vi, bi, kbi, ei in zip(t_inv, v, beta, kb, e_pos)]
    q_dec = [(qi * ei).astype(BF16) for qi, ei in zip(q, e_pos)]
    k_dec = [_ext(ki * jnp.exp(x[CHUNK - 1:CHUNK, :] - x), m0, m1, F32) for ki, x in zip(k, gc)]
    st = [st_ref[p] for p in range(n_pairs)]
    for c in range(len(items) // n_pairs):
        idx = range(c * n_pairs, (c + 1) * n_pairs)
        z = [_dot(_cat([uw[i][:, LANES:].astype(BF16), q_dec[i]], 0), s) for i, s in zip(idx, st)]
        v_new = [(uw[i][:, :LANES] - zi[:c2]).astype(BF16) for i, zi in zip(idx, z)]
        for i, zi, vn in zip(idx, z, v_new):
            rows, cols = items[i]
            o_ref[rows, cols] = zi[c2:] + _dot(a_in[i], vn)
        st = [s * e_pos[i][CHUNK - 1:CHUNK, :] + _dot_tn(k_dec[i], vn) for i, s, vn in zip(idx, st, v_new)]
    for p in range(n_pairs):
        st_ref[p] = st[p]


def _rwkv_kernel(*refs, vres):
    if vres:
        z_ref, vf_ref, mu_ref, prm_ref, wup_ref, aup_ref, gup_ref, vup_ref, o_ref, st_ref, tail_ref, y_ref = refs
    else:
        z_ref, mu_ref, prm_ref, wup_ref, aup_ref, gup_ref, o_ref, vf_out_ref, st_ref, tail_ref, y_ref = refs

    @pl.when(pl.program_id(1) == 0)
    def _():
        st_ref[...] = jnp.zeros_like(st_ref)
        tail_ref[...] = jnp.zeros_like(tail_ref)

    z = z_ref[...]
    rows = z.shape[0]
    zs = _shift_rows(z, tail_ref[...], 1)
    tail_ref[...] = z[rows - SUBLANES:]
    zl = z[:, :RWKV_IN] + (zs[:, :RWKV_IN] - z[:, :RWKV_IN]) * mu_ref[...]
    r, k, v = zl[:, :WIDTH], zl[:, WIDTH:2 * WIDTH], zl[:, 2 * WIDTH:3 * WIDTH]
    lo = zl[:, 3 * WIDTH:3 * WIDTH + LANES]
    g_lo = zl[:, 3 * WIDTH + LANES:]
    w0, a0, k_k, k_a, ln_g, ln_b, r_k, v_bias = (prm_ref[i:i + 1, :] for i in range(8))
    log_w = -_softplus(-(w0 + _dot(jnp.tanh(lo), wup_ref[...]))) - 0.5
    ld = -jnp.exp(log_w)
    iclr = jax.nn.sigmoid(a0 + _dot(lo, aup_ref[...]))
    gate = _dot(jax.nn.sigmoid(g_lo), gup_ref[...])
    if vres:
        x = z[:, RWKV_IN:] + pltpu.roll(zs[:, RWKV_IN:], LANES - VRES_LORA, 1)
        v = v + (vf_ref[...] - v) * jax.nn.sigmoid(v_bias + _dot(x, vup_ref[...]))
    else:
        vf_out_ref[...] = v
    kk = k * k_k
    kk = kk * lax.rsqrt(_head_sum(kk * kk) + L2_EPS)
    k = k * (1.0 + (iclr - 1.0) * k_a)
    _rwkv_scan(r, k, v, -kk, kk * iclr, ld, st_ref, y_ref)
    y = y_ref[...]
    d = y - _head_sum(y) * (1.0 / HEAD_DIM)
    y = d * lax.rsqrt(_head_sum(d * d) * (1.0 / HEAD_DIM) + RWKV_GN_EPS) * ln_g + ln_b
    y = y + _head_sum(r * k * r_k) * v
    o_ref[...] = (y * gate).astype(o_ref.dtype)


def _rwkv(z, v_first, b, s, mu, prm, w_up, a_up, g_up, v_up):
    vres = v_first is not None
    rows = min(SCAN_ROWS, s)
    zw = z.shape[1]
    row = lambda width: pl.BlockSpec((rows, width), lambda bb, c: (bb * (s // rows) + c, 0))
    full = lambda a: pl.BlockSpec(a.shape, lambda bb, c: (0, 0))
    zero = jnp.zeros((LANES // 2, WIDTH), F32)
    consts = [mu.reshape(1, -1), prm, _cat([w_up, zero], 0).astype(BF16), _cat([zero, a_up], 0).astype(BF16),
              g_up.astype(BF16)]
    args, in_specs = [z], [row(zw)]
    if vres:
        args.append(v_first)
        in_specs.append(row(WIDTH))
        consts.append(jnp.pad(v_up, ((0, LANES - VRES_LORA), (0, 0))).astype(BF16))
    out_shape = [jax.ShapeDtypeStruct((b * s, WIDTH), BF16)]
    out_specs = [row(WIDTH)]
    if not vres:
        out_shape.append(jax.ShapeDtypeStruct((b * s, WIDTH), F32))
        out_specs.append(row(WIDTH))
    outs = pl.pallas_call(
        functools.partial(_rwkv_kernel, vres=vres),
        grid=(b, s // rows),
        in_specs=in_specs + [full(a) for a in consts],
        out_specs=out_specs,
        out_shape=out_shape,
        scratch_shapes=[pltpu.VMEM((WIDTH // LANES, LANES, LANES), F32), pltpu.VMEM((SUBLANES, zw), F32),
                        pltpu.VMEM((rows, WIDTH), F32)],
        compiler_params=_cparams(("arbitrary", "arbitrary")),
        name="rwkv",
    )(*args, *consts)
    return (outs[0], v_first) if vres else (outs[0], outs[1])


def _gdn_kernel(z_ref, cw_ref, prm_ref, exp_ref, o_ref, st_ref, tail_ref, y_ref):
    @pl.when(pl.program_id(1) == 0)
    def _():
        st_ref[...] = jnp.zeros_like(st_ref)
        tail_ref[...] = jnp.zeros_like(tail_ref)

    rows = z_ref.shape[0]
    x = z_ref[:, :GDN_QKV]
    tail = tail_ref[...]
    conv = x * cw_ref[GDN_CONV - 1:GDN_CONV, :]
    for j in range(1, GDN_CONV):
        conv = conv + _shift_rows(x, tail, j) * cw_ref[GDN_CONV - 1 - j:GDN_CONV - j, :]
    tail_ref[...] = x[rows - SUBLANES:]
    qkv = conv * jax.nn.sigmoid(conv)
    q, k, v = qkv[:, :WIDTH], qkv[:, WIDTH:2 * WIDTH], qkv[:, 2 * WIDTH:]
    q = q * lax.rsqrt(_head_sum(q * q) + L2_EPS) * (HEAD_DIM ** -0.5)
    k = k * lax.rsqrt(_head_sum(k * k) + L2_EPS)
    logits = jnp.dot(_cat(_split3(z_ref[:, GDN_QKV + WIDTH:GDN_QKV + WIDTH + LANES]), 0), exp_ref[...],
                     preferred_element_type=F32)
    logits = logits[:rows] + logits[rows:2 * rows] + logits[2 * rows:]
    neg_a, dt_bias, norm_g = (prm_ref[i:i + 1, :] for i in range(3))
    beta = jax.nn.sigmoid(logits[:, :WIDTH])
    g = neg_a * _softplus(logits[:, WIDTH:] + dt_bias)
    _gdn_scan(q, k, v, g, beta, st_ref, y_ref)
    o = y_ref[...]
    o = o * lax.rsqrt(_head_sum(o * o) * (1.0 / HEAD_DIM) + NORM_EPS) * norm_g
    gate = z_ref[:, GDN_QKV:GDN_QKV + WIDTH]
    o_ref[...] = (o * (gate * jax.nn.sigmoid(gate))).astype(o_ref.dtype)


def _gdn(z, b, s, conv_w, a_log, dt_bias, norm_g):
    rows = min(SCAN_ROWS, s)
    zw = z.shape[1]
    row = lambda width: pl.BlockSpec((rows, width), lambda bb, c: (bb * (s // rows) + c, 0))
    full = lambda a: pl.BlockSpec(a.shape, lambda bb, c: (0, 0))
    per_lane = lambda t: jnp.repeat(t, HEAD_DIM)
    prm = jnp.stack([per_lane(-jnp.exp(a_log)), per_lane(dt_bias), jnp.tile(norm_g, N_HEADS)])
    head_of_lane = np.arange(WIDTH) // HEAD_DIM
    expand = np.zeros((LANES, 2 * WIDTH), np.float32)
    expand[head_of_lane, np.arange(WIDTH)] = 1.0
    expand[N_HEADS + head_of_lane, WIDTH + np.arange(WIDTH)] = 1.0
    consts = [conv_w, prm, jnp.asarray(expand, BF16)]
    return pl.pallas_call(
        _gdn_kernel,
        grid=(b, s // rows),
        in_specs=[row(zw)] + [full(a) for a in consts],
        out_specs=row(WIDTH),
        out_shape=jax.ShapeDtypeStruct((b * s, WIDTH), BF16),
        scratch_shapes=[pltpu.VMEM((WIDTH // LANES, LANES, LANES), F32), pltpu.VMEM((SUBLANES, GDN_QKV), F32),
                        pltpu.VMEM((rows, WIDTH), F32)],
        compiler_params=_cparams(("arbitrary", "arbitrary")),
        name="gdn",
    )(z, *consts)


def _merge_kernel(h_ref, u_ref, oa_ref, ob_ref, oc_ref, wg_ref, wa_ref, wb_ref, wc_ref, wo_ref, out_ref):
    d = h_ref.shape[1]
    u = u_ref[...]
    ob = jnp.concatenate([ob_ref[h] for h in range(N_HEADS)], axis=1)
    merged = jnp.zeros(h_ref.shape, F32)
    for n, (o, w_ref) in enumerate(((oa_ref[...], wa_ref), (ob, wb_ref), (oc_ref[...], wc_ref))):
        proj = jnp.dot(o, w_ref[...], preferred_element_type=F32)
        zg = jnp.dot(u, wg_ref[:, n * d:(n + 1) * d], preferred_element_type=F32)
        merged = merged + jax.nn.sigmoid(zg) * proj
    out_ref[...] = h_ref[...] + jnp.dot(merged.astype(BF16), wo_ref[...], preferred_element_type=F32)


def _merge(h, u, oa, ob, oc, wg, wa, wb, wc, wo, s, *, bm=512):
    m, d = h.shape
    bm = min(bm, s)
    row = lambda width: pl.BlockSpec((bm, width), lambda i: (i, 0))
    full = lambda w: pl.BlockSpec(w.shape, lambda i: (0, 0))
    wb = jnp.pad(wb.reshape(N_HEADS, HEAD_DIM, d), ((0, 0), (0, LANES - HEAD_DIM), (0, 0))).reshape(N_HEADS * LANES, d)
    ws = [w.astype(BF16) for w in (wg, wa, wb, wc, wo)]
    ob_spec = pl.BlockSpec((None, N_HEADS, bm, LANES), lambda i: (i // (s // bm), 0, i % (s // bm), 0))
    return pl.pallas_call(
        _merge_kernel,
        grid=(m // bm,),
        in_specs=[row(d), row(d), row(WIDTH), ob_spec, row(WIDTH)] + [full(w) for w in ws],
        out_specs=row(d),
        out_shape=jax.ShapeDtypeStruct((m, d), F32),
        compiler_params=_cparams(("arbitrary",)),
        name="merge",
    )(h, u, oa, ob, oc, *ws)


def _ffn_kernel(be_ref, nu_ref, x_ref, g_ref, wg_ref, wu_ref, wd_ref, o_ref, xs_ref, acc_ref, *, nf, dense):
    i = pl.program_id(0)
    f = pl.program_id(1)
    used = i < nu_ref[0]

    @pl.when(used & (f == 0))
    def _():
        x = x_ref[...]
        ms = jnp.mean(x * x, axis=-1, keepdims=True)
        xs_ref[...] = (x * lax.rsqrt(ms + NORM_EPS) * g_ref[...]).astype(BF16)

    @pl.when(f == 0)
    def _():
        acc_ref[...] = jnp.zeros_like(acc_ref)

    @pl.when(used)
    def _():
        wg, wu, wd = (w_ref[...].astype(BF16) for w_ref in (wg_ref, wu_ref, wd_ref))
        sub = FFN_SUB_ROWS
        chunks = [slice(r, r + sub) for r in range(0, xs_ref.shape[0], sub)]

        def up_proj(rows):
            xs = xs_ref[rows, :]
            return jnp.dot(xs, wg, preferred_element_type=F32), jnp.dot(xs, wu, preferred_element_type=F32)

        pending = up_proj(chunks[0])
        for n, rows in enumerate(chunks):
            gate, up = pending
            if n + 1 < len(chunks):
                pending = up_proj(chunks[n + 1])
            act = (gate * jax.nn.sigmoid(gate) * up).astype(BF16)
            acc_ref[rows, :] += jnp.dot(act, wd, preferred_element_type=F32)

    @pl.when(f == nf - 1)
    def _():
        o_ref[...] = (x_ref[...] + acc_ref[...]) if dense else acc_ref[...]


def _ffn(x, g, wg, wu, wd, block_e, n_used, *, bm, tf, dense):
    r, d = x.shape
    ff = wg.shape[2]
    nf = ff // tf

    def ff_idx(i, f, be, nu):
        return jnp.where(i < nu[0], f, nf - 1)

    return pl.pallas_call(
        functools.partial(_ffn_kernel, nf=nf, dense=dense),
        grid_spec=pltpu.PrefetchScalarGridSpec(
            num_scalar_prefetch=2,
            grid=(r // bm, nf),
            in_specs=[
                pl.BlockSpec((bm, d), lambda i, f, be, nu: (i, 0)),
                pl.BlockSpec((1, d), lambda i, f, be, nu: (0, 0)),
                pl.BlockSpec((None, d, tf), lambda i, f, be, nu: (be[i], 0, ff_idx(i, f, be, nu))),
                pl.BlockSpec((None, d, tf), lambda i, f, be, nu: (be[i], 0, ff_idx(i, f, be, nu))),
                pl.BlockSpec((None, tf, d), lambda i, f, be, nu: (be[i], ff_idx(i, f, be, nu), 0)),
            ],
            out_specs=pl.BlockSpec((bm, d), lambda i, f, be, nu: (i, 0)),
            scratch_shapes=[pltpu.VMEM((bm, d), BF16), pltpu.VMEM((bm, d), F32)],
        ),
        out_shape=jax.ShapeDtypeStruct((r, d), F32),
        compiler_params=_cparams(("arbitrary", "arbitrary")),
        name="ffn",
    )(block_e, n_used, x, g.reshape(1, d).astype(F32), wg, wu, wd)


def _ple_kernel(*refs, combine):
    if combine:
        h_ref, y0_ref, y1_ref, route_ref, p_ref, wg_ref, wp_ref, g_ref, o_ref = refs
        h = h_ref[...] + route_ref[:, 2:3] * y0_ref[...] + route_ref[:, 3:4] * y1_ref[...]
    else:
        h_ref, p_ref, wg_ref, wp_ref, g_ref, o_ref = refs
        h = h_ref[...]
    gate = jnp.dot(h.astype(BF16), wg_ref[...], preferred_element_type=F32)
    e = jnp.dot(p_ref[...].astype(BF16), wp_ref[...], preferred_element_type=F32)
    ms = jnp.mean(e * e, axis=-1, keepdims=True)
    e = e * lax.rsqrt(ms + NORM_EPS) * g_ref[...]
    o_ref[...] = h + jax.nn.sigmoid(gate) * e


def _ple(h, p, w_gate, w_proj, g, expert_out=None, *, bm=512):
    m, d = h.shape
    bm = min(bm, m)
    row = lambda a: pl.BlockSpec((bm, a.shape[1]), lambda i: (i, 0))
    full = lambda a: pl.BlockSpec(a.shape, lambda i: (0, 0))
    rows = [h] + list(expert_out or ()) + [p]
    consts = [w_gate.astype(BF16), w_proj.astype(BF16), g.reshape(1, d)]
    return pl.pallas_call(
        functools.partial(_ple_kernel, combine=expert_out is not None),
        grid=(m // bm,),
        in_specs=[row(a) for a in rows] + [full(a) for a in consts],
        out_specs=row(h),
        out_shape=jax.ShapeDtypeStruct((m, d), F32),
        compiler_params=_cparams(("arbitrary",)),
        name="ple",
    )(*rows, *consts)


def _pad_cols(w, n):
    return jnp.pad(w, ((0, 0), (0, n - w.shape[1])))


def _rope_tables(positions):
    t = positions.size
    inv_freq = 1.0 / (ROPE_THETA ** (jnp.arange(0, QK_ROPE, 2, dtype=F32) / QK_ROPE))
    ang = positions.astype(F32).reshape(t, 1) * inv_freq
    cos, sin = jnp.cos(ang), jnp.sin(ang)
    pad = ((0, 0), (QK_NOPE, LANES - QK_DIM))
    return jnp.pad(_cat([cos, cos], 1), pad, constant_values=1.0), jnp.pad(_cat([-sin, sin], 1), pad)


def _moe(h, g, w_router, wg, wu, wd, *, bm=768, tf=512):
    t, d = h.shape
    n_assign = t * TOP_K
    n_blocks = -(-(n_assign + N_EXPERTS * (bm - 1)) // bm)
    route, counts = _router(h, g, w_router)
    counts = counts[0, :N_EXPERTS].astype(jnp.int32)
    padded = (counts + bm - 1) // bm * bm
    pad_end = jnp.cumsum(padded)
    top_e = route[:, :TOP_K].astype(jnp.int32)
    dest = (pad_end - padded)[top_e] + route[:, 4:4 + TOP_K].astype(jnp.int32)
    src_tok = (jnp.arange(n_blocks * bm, dtype=jnp.int32) % t).at[dest.reshape(-1)].set(
        jnp.arange(n_assign, dtype=jnp.int32) // TOP_K)
    block_e = jnp.minimum(jnp.searchsorted(pad_end, jnp.arange(n_blocks) * bm, side='right'),
                          N_EXPERTS - 1).astype(jnp.int32)
    n_used = (pad_end[-1] // bm).astype(jnp.int32).reshape(1)
    y_rows = _ffn(h[src_tok], g, wg, wu, wd, block_e, n_used, bm=bm, tf=tf, dense=False)
    return y_rows[dest[:, 0]], y_rows[dest[:, 1]], route


def kernel(x, p, positions, norm_mix_g, w_in, rwkv_mu, rwkv_w0, rwkv_w_up, rwkv_a0, rwkv_a_up, rwkv_g_up, rwkv_k_k, rwkv_k_a, rwkv_r_k, rwkv_ln_g, rwkv_ln_b, vres_mu, vres_down, vres_up, vres_b, mla_q_norm_g, mla_kv_norm_g, mla_w_uq, mla_w_ukv, mla_qk_norm_q, mla_qk_norm_k, gdn_conv_w, gdn_a_log, gdn_dt_bias, gdn_norm_g, w_br_rwkv, w_br_mla, w_br_gdn, w_out, norm_ffn_g, ffn_wg, ffn_wu, ffn_wd, moe_router, moe_wg, moe_wu, moe_wd, ple_proj, ple_gate, ple_norm_g):
    b, s, d = x.shape
    t = b * s
    depth = w_in.shape[0]
    rope_tabs = _rope_tables(positions)

    h = x.reshape(t, d)
    v_first = None
    for i in range(depth):
        w = w_in[i]
        g = norm_mix_g[i]
        w_rwkv = w[:, :RWKV_IN]
        if i > 0:
            mu_v = vres_mu[i - 1][:, None]
            vd = vres_down[i - 1]
            w_rwkv = _pad_cols(_cat([w_rwkv, (1.0 - mu_v) * vd, mu_v * vd], 1), RWKV_IN + LANES)
        w_gdn = w[:, RWKV_IN + MLA_IN:RWKV_IN + MLA_IN + GDN_IN]
        w_gdn = _pad_cols(_cat([w_gdn[:, :GDN_QKV], w_gdn[:, GDN_QKV + 2 * N_HEADS:],
                                w_gdn[:, GDN_QKV:GDN_QKV + 2 * N_HEADS]], 1), GDN_QKV + WIDTH + 2 * LANES)
        u = _norm(h, g)
        z_rwkv = _mm(u, w_rwkv)
        z_gdn = _mm(u, w_gdn)
        prm = jnp.stack([rwkv_w0[i], rwkv_a0[i], rwkv_k_k[i], rwkv_k_a[i], rwkv_ln_g[i], rwkv_ln_b[i],
                         rwkv_r_k[i].reshape(-1), vres_b[i - 1] if i > 0 else jnp.zeros((WIDTH,), F32)])
        o_a, v_first = _rwkv(z_rwkv, v_first, b, s, rwkv_mu[i], prm, rwkv_w_up[i], rwkv_a_up[i], rwkv_g_up[i],
                             vres_up[i - 1] if i > 0 else None)
        q, k, v = _mla_prep(u, w[:, RWKV_IN:RWKV_IN + MLA_IN], rope_tabs, b, s, mla_q_norm_g[i], mla_kv_norm_g[i],
                            mla_w_uq[i], mla_w_ukv[i], mla_qk_norm_q[i], mla_qk_norm_k[i])
        o_b = _flash_attention(q, k, v)
        o_c = _gdn(z_gdn, b, s, gdn_conv_w[i], gdn_a_log[i], gdn_dt_bias[i], gdn_norm_g[i])
        h = _merge(h, u, o_a, o_b, o_c, w[:, RWKV_IN + MLA_IN + GDN_IN:], w_br_rwkv[i], w_br_mla[i], w_br_gdn[i],
                   w_out[i], s)
        j = i // 2
        if i % 2 == 0:
            bm = min(1024, t)
            h = _ffn(h, norm_ffn_g[i], ffn_wg[j:j + 1], ffn_wu[j:j + 1], ffn_wd[j:j + 1],
                     jnp.zeros((t // bm,), jnp.int32), jnp.full((1,), t // bm, jnp.int32), bm=bm, tf=256, dense=True)
            expert_out = None
        else:
            expert_out = _moe(h, norm_ffn_g[i], moe_router[j], moe_wg[j], moe_wu[j], moe_wd[j])
        h = _ple(h, p[i].reshape(t, -1), ple_gate[i], ple_proj[i], ple_norm_g[i], expert_out)
    return h.reshape(b, s, d)
```

```python
import functools
import math

import jax
import jax.numpy as jnp
import numpy as np
from jax import lax
from jax.experimental import pallas as pl
from jax.experimental.pallas import tpu as pltpu

F32 = jnp.float32
BF16 = jnp.bfloat16
HI = lax.Precision.HIGHEST

NORM_EPS = 1e-6
RWKV_GN_EPS = 64e-5
L2_EPS = 1e-12
N_HEADS = 8
HEAD_DIM = 64
WIDTH = N_HEADS * HEAD_DIM
RWKV_LORA = 256
RWKV_IN = 3 * WIDTH + RWKV_LORA
VRES_LORA = 32
QK_NOPE = 64
QK_ROPE = 32
QK_DIM = QK_NOPE + QK_ROPE
Q_LORA = 256
KV_LORA = 128
MLA_IN = Q_LORA + KV_LORA + QK_ROPE
ROPE_THETA = 10000.0
GDN_CONV = 4
GDN_QKV = 3 * WIDTH
GDN_IN = GDN_QKV + 2 * N_HEADS + WIDTH
N_EXPERTS = 8
TOP_K = 2
LANES = 128
SUBLANES = 8
CHUNK = 64
SCAN_ROWS = 4 * CHUNK
FFN_SUB_ROWS = 256
VMEM_LIMIT = 48 * 1024 * 1024


def _cparams(sem):
    return pltpu.CompilerParams(dimension_semantics=sem, vmem_limit_bytes=VMEM_LIMIT)


def _norm_kernel(x_ref, g_ref, u_ref):
    x = x_ref[...]
    u_ref[...] = (x * lax.rsqrt(jnp.mean(x * x, axis=-1, keepdims=True) + NORM_EPS) * g_ref[...]).astype(u_ref.dtype)


def _norm(x, g, *, bm=1024):
    m, d = x.shape
    bm = min(bm, m)
    return pl.pallas_call(
        _norm_kernel,
        grid=(m // bm,),
        in_specs=[pl.BlockSpec((bm, d), lambda i: (i, 0)), pl.BlockSpec((1, d), lambda i: (0, 0))],
        out_specs=pl.BlockSpec((bm, d), lambda i: (i, 0)),
        out_shape=jax.ShapeDtypeStruct((m, d), BF16),
        compiler_params=_cparams(("arbitrary",)),
        name="norm",
    )(x, g.reshape(1, d))


def _router_kernel(x_ref, g_ref, wr_ref, route_ref, counts_ref, carry_ref):
    @pl.when(pl.program_id(0) == 0)
    def _():
        carry_ref[...] = jnp.zeros_like(carry_ref)

    x = x_ref[...]
    rows = x.shape[0]
    u = x * lax.rsqrt(jnp.mean(x * x, axis=-1, keepdims=True) + NORM_EPS) * g_ref[...]
    lane = lax.broadcasted_iota(jnp.int32, (rows, LANES), 1)
    logits = jnp.where(lane < N_EXPERTS, jnp.dot(u, wr_ref[...], preferred_element_type=F32, precision=HI), -jnp.inf)
    m1 = jnp.max(logits, axis=-1, keepdims=True)
    i1 = jnp.min(jnp.where(logits == m1, lane, LANES), axis=-1, keepdims=True)
    rest = jnp.where(lane == i1, -jnp.inf, logits)
    m2 = jnp.max(rest, axis=-1, keepdims=True)
    i2 = jnp.min(jnp.where(rest == m2, lane, LANES), axis=-1, keepdims=True)
    e2 = jnp.exp(m2 - m1)
    w1 = 1.0 / (1.0 + e2)
    w2 = e2 / (1.0 + e2)
    hit1 = lane == i1
    hit2 = lane == i2
    onehot = jnp.where(hit1 | hit2, 1.0, 0.0)
    before = (lax.broadcasted_iota(jnp.int32, (rows, rows), 0) > lax.broadcasted_iota(jnp.int32, (rows, rows), 1))
    seen = jnp.dot(before.astype(BF16), onehot.astype(BF16), preferred_element_type=F32) + carry_ref[0:1, :]
    r1 = jnp.sum(jnp.where(hit1, seen, 0.0), axis=-1, keepdims=True)
    r2 = jnp.sum(jnp.where(hit2, seen, 0.0), axis=-1, keepdims=True)
    carry_ref[...] = carry_ref[...] + jnp.sum(onehot, axis=0, keepdims=True)
    cols = (i1.astype(F32), i2.astype(F32), w1, w2, r1, r2)
    route = jnp.zeros((rows, LANES), F32)
    for n, c in enumerate(cols):
        route = jnp.where(lane == n, c, route)
    route_ref[...] = route
    counts_ref[...] = carry_ref[...]


def _router(h, g, w_router, *, bm=1024):
    m, d = h.shape
    bm = min(bm, m)
    return pl.pallas_call(
        _router_kernel,
        grid=(m // bm,),
        in_specs=[pl.BlockSpec((bm, d), lambda i: (i, 0)), pl.BlockSpec((1, d), lambda i: (0, 0)),
                  pl.BlockSpec((d, LANES), lambda i: (0, 0))],
        out_specs=[pl.BlockSpec((bm, LANES), lambda i: (i, 0)), pl.BlockSpec((SUBLANES, LANES), lambda i: (0, 0))],
        out_shape=[jax.ShapeDtypeStruct((m, LANES), F32), jax.ShapeDtypeStruct((SUBLANES, LANES), F32)],
        scratch_shapes=[pltpu.VMEM((SUBLANES, LANES), F32)],
        compiler_params=_cparams(("arbitrary",)),
        name="router",
    )(h, g.reshape(1, d), _pad_cols(w_router, LANES))


def _rms(x, n, g):
    return x * lax.rsqrt(jnp.sum(x * x, axis=-1, keepdims=True) * (1.0 / n) + NORM_EPS) * g


def _mla_prep_kernel(u_ref, cos_ref, sin_ref, win_ref, gq_ref, gkv_ref, gqh_ref, gkh_ref, wq_ref, wqr_ref, wk_ref,
                     wv_ref, q_ref, k_ref, v_ref):
    z = jnp.dot(u_ref[...], win_ref[...], preferred_element_type=F32)
    cq = _rms(z[:, :Q_LORA], Q_LORA, gq_ref[...]).astype(BF16)
    ckv = _rms(z[:, Q_LORA:Q_LORA + KV_LORA], KV_LORA, gkv_ref[...]).astype(BF16)
    k_pe = z[:, Q_LORA + KV_LORA:Q_LORA + KV_LORA + LANES]
    k_pe_rot = z[:, Q_LORA + KV_LORA + LANES:]
    q_all = jnp.dot(cq, wq_ref[...], preferred_element_type=F32)
    q_rot = jnp.dot(cq, wqr_ref[...], preferred_element_type=F32)
    k_all = jnp.dot(ckv, wk_ref[...], preferred_element_type=F32)
    v_all = jnp.dot(ckv, wv_ref[...], preferred_element_type=F32)
    sin = sin_ref[...]
    cos_q = cos_ref[...] * gqh_ref[...]
    cos_k = cos_ref[...] * gkh_ref[...]

    def scale(x):
        return lax.rsqrt(jnp.sum(x * x, axis=-1, keepdims=True) * (1.0 / QK_DIM) + NORM_EPS)

    for h in range(N_HEADS):
        sl = slice(h * LANES, (h + 1) * LANES)
        q = q_all[:, sl]
        k = k_all[:, sl] + k_pe
        q_ref[h] = (scale(q) * (QK_DIM ** -0.5) * (q * cos_q + q_rot[:, sl] * sin)).astype(BF16)
        k_ref[h] = (scale(k) * (k * cos_k + k_pe_rot * sin)).astype(BF16)
        v_ref[h] = v_all[:, sl].astype(BF16)


def _pad_heads(w, width):
    k = w.shape[0]
    w = w.reshape(k, N_HEADS, width)
    return jnp.pad(w, ((0, 0), (0, 0), (0, LANES - width))).reshape(k, N_HEADS * LANES)


def _mla_prep(u, w_in, rope_tabs, b, s, q_norm_g, kv_norm_g, w_uq, w_ukv, qk_g_q, qk_g_k, *, bm=512):
    bm = min(bm, s)
    half = QK_ROPE // 2
    swap = lambda w: _cat([w[..., half:], w[..., :half]], -1)
    rope_slab = lambda w: jnp.pad(w, ((0, 0), (QK_NOPE, LANES - QK_DIM)))
    lora = Q_LORA + KV_LORA
    w_in = _cat([w_in[:, :lora], rope_slab(w_in[:, lora:]), rope_slab(swap(w_in[:, lora:] * qk_g_k[QK_NOPE:]))],
                1).astype(BF16)
    w_ukv = w_ukv.reshape(KV_LORA, N_HEADS, QK_NOPE + HEAD_DIM)
    w_uq3 = w_uq.reshape(Q_LORA, N_HEADS, QK_DIM)
    wq = _pad_heads(w_uq, QK_DIM).astype(BF16)
    wq_rot = jnp.pad(swap(w_uq3[..., QK_NOPE:] * qk_g_q[QK_NOPE:]), ((0, 0), (0, 0), (QK_NOPE, LANES - QK_DIM)))
    wq_rot = wq_rot.reshape(Q_LORA, N_HEADS * LANES).astype(BF16)
    wk = _pad_heads(w_ukv[..., :QK_NOPE].reshape(KV_LORA, -1), QK_NOPE).astype(BF16)
    wv = _pad_heads(w_ukv[..., QK_NOPE:].reshape(KV_LORA, -1), HEAD_DIM).astype(BF16)
    pad_g = lambda g: jnp.pad(g, (0, LANES - g.shape[0])).reshape(1, LANES)
    row = lambda width: pl.BlockSpec((bm, width), lambda bb, i: (bb * (s // bm) + i, 0))
    full = lambda a: pl.BlockSpec(a.shape, lambda bb, i: (0, 0))
    consts = [w_in, q_norm_g.reshape(1, -1), kv_norm_g.reshape(1, -1), pad_g(qk_g_q), pad_g(qk_g_k), wq, wq_rot,
              wk, wv]
    out_spec = pl.BlockSpec((None, N_HEADS, bm, LANES), lambda bb, i: (bb, 0, i, 0))
    out_shape = jax.ShapeDtypeStruct((b, N_HEADS, s, LANES), BF16)
    return pl.pallas_call(
        _mla_prep_kernel,
        grid=(b, s // bm),
        in_specs=[row(u.shape[1]), row(LANES), row(LANES)] + [full(a) for a in consts],
        out_specs=[out_spec] * 3,
        out_shape=[out_shape] * 3,
        compiler_params=_cparams(("arbitrary", "arbitrary")),
        name="mla_prep",
    )(u, *rope_tabs, *consts)


def _flash_kernel(qi_ref, kj_ref, q_ref, k_ref, v_ref, o_ref, m_ref, l_ref, acc_ref, *, blk, sub):
    n = pl.program_id(2)
    i = qi_ref[n]
    j = kj_ref[n]
    hb = q_ref.shape[0]

    @pl.when(j == 0)
    def _():
        m_ref[...] = jnp.full_like(m_ref, -jnp.inf)
        l_ref[...] = jnp.zeros_like(l_ref)
        acc_ref[...] = jnp.zeros_like(acc_ref)

    def update(masked):
        chains = [(h, r) for h in range(hb) for r in range(blk // sub)]

        def qk(h, r):
            nk = (r + 1) * sub if masked else blk
            s = lax.dot_general(q_ref[h, r * sub:(r + 1) * sub, :], k_ref[h, :nk, :], (((1,), (1,)), ((), ())),
                                preferred_element_type=F32)
            if masked:
                row = lax.broadcasted_iota(jnp.int32, (sub, nk), 0) + r * sub
                s = jnp.where(row >= lax.broadcasted_iota(jnp.int32, (sub, nk), 1), s, -jnp.inf)
            return s

        def softmax_pv(h, r, s):
            rows = slice(r * sub, (r + 1) * sub)
            m_old = m_ref[h, rows, :]
            m_new = jnp.maximum(m_old, jnp.max(s, axis=-1, keepdims=True))
            alpha = jnp.exp(m_old - m_new)
            p = jnp.exp(s - m_new[:, :1])
            l_ref[h, rows, :] = alpha * l_ref[h, rows, :] + jnp.sum(p, axis=-1, keepdims=True)
            m_ref[h, rows, :] = m_new
            pv = jnp.dot(p.astype(BF16), v_ref[h, :s.shape[1], :], preferred_element_type=F32)
            acc_ref[h, rows, :] = alpha * acc_ref[h, rows, :] + pv

        s_prev = qk(*chains[0])
        for c in range(len(chains)):
            s_next = qk(*chains[c + 1]) if c + 1 < len(chains) else None
            softmax_pv(*chains[c], s_prev)
            s_prev = s_next

    @pl.when(j < i)
    def _():
        update(False)

    @pl.when(j == i)
    def _():
        update(True)
        o_ref[...] = (acc_ref[...] / l_ref[...]).astype(o_ref.dtype)


def _flash_attention(q, k, v, *, blk=1024, sub=256, hb=2):
    b, h, s, d = q.shape
    blk = min(blk, s)
    sub = min(sub, blk)
    nb = s // blk
    qi = np.array([i for i in range(nb) for j in range(i + 1)], np.int32)
    kj = np.array([j for i in range(nb) for j in range(i + 1)], np.int32)
    q_spec = pl.BlockSpec((None, hb, blk, d), lambda bb, hh, n, qi_r, kj_r: (bb, hh, qi_r[n], 0))
    kv_spec = pl.BlockSpec((None, hb, blk, d), lambda bb, hh, n, qi_r, kj_r: (bb, hh, kj_r[n], 0))
    return pl.pallas_call(
        functools.partial(_flash_kernel, blk=blk, sub=sub),
        grid_spec=pltpu.PrefetchScalarGridSpec(
            num_scalar_prefetch=2,
            grid=(b, h // hb, len(qi)),
            in_specs=[q_spec, kv_spec, kv_spec],
            out_specs=q_spec,
            scratch_shapes=[pltpu.VMEM((hb, blk, LANES), F32), pltpu.VMEM((hb, blk, LANES), F32),
                            pltpu.VMEM((hb, blk, d), F32)],
        ),
        out_shape=jax.ShapeDtypeStruct((b, h, s, d), BF16),
        compiler_params=_cparams(("arbitrary", "arbitrary", "arbitrary")),
        name="flash",
    )(jnp.asarray(qi), jnp.asarray(kj), q, k, v)


def _dot(a, b):
    return jnp.dot(a.astype(BF16), b.astype(BF16), preferred_element_type=F32)


def _dot_nt(a, b):
    return lax.dot_general(a.astype(BF16), b.astype(BF16), (((1,), (1,)), ((), ())), preferred_element_type=F32)


def _dot_tn(a, b):
    return jnp.dot(a.T.astype(BF16), b.astype(BF16), preferred_element_type=F32)


def _split3(x):
    hi = x.astype(BF16)
    r1 = x - hi.astype(F32)
    mid = r1.astype(BF16)
    return hi, mid, (r1 - mid.astype(F32)).astype(BF16)


def _cumsum_rows(tri, x):
    s = jnp.dot(tri, jnp.concatenate(_split3(x), axis=1), preferred_element_type=F32)
    n = x.shape[1]
    return s[:, :n] + s[:, n:2 * n] + s[:, 2 * n:]


def _head_sum(x):
    left = lax.broadcasted_iota(jnp.int32, (x.shape[0], LANES), 1) < HEAD_DIM
    outs = []
    for p in range(WIDTH // LANES):
        xs = x[:, p * LANES:(p + 1) * LANES]
        s_left = jnp.sum(jnp.where(left, xs, 0.0), axis=-1, keepdims=True)
        s_right = jnp.sum(jnp.where(left, 0.0, xs), axis=-1, keepdims=True)
        outs.append(jnp.where(left, s_left, s_right))
    return jnp.concatenate(outs, axis=1)


def _shift_rows(x, tail, j):
    rolled = pltpu.roll(x, j, 0)
    top = jnp.where(lax.broadcasted_iota(jnp.int32, (SUBLANES, x.shape[1]), 0) < j,
                    pltpu.roll(tail, j, 0), rolled[:SUBLANES])
    return jnp.concatenate([top, rolled[SUBLANES:]], axis=0)


def _softplus(x):
    return jnp.maximum(x, 0.0) + jnp.log(1.0 + jnp.exp(-jnp.abs(x)))


def _pair_masks():
    lane = lax.broadcasted_iota(jnp.int32, (CHUNK, LANES), 1)
    m0 = (lane < HEAD_DIM).astype(F32)
    return m0, 1.0 - m0


def _ext(x, m0, m1, dtype=BF16):
    return jnp.concatenate([x * m0, x * m1], axis=0).astype(dtype)


def _cat(xs, axis):
    return jnp.concatenate(xs, axis=axis)


def _neumann_inverse(mats):
    n = mats[0].shape[0]
    eye = (lax.broadcasted_iota(jnp.int32, (n, n), 0) == lax.broadcasted_iota(jnp.int32, (n, n), 1)).astype(F32)
    ts = [eye + a for a in mats]
    ps = [_dot(a, a) for a in mats]
    for _ in range(int(math.log2(CHUNK)) - 2):
        pps = [_dot(_cat([p, t], 0), p) for p, t in zip(ps, ts)]
        ps = [pp[:n] for pp in pps]
        ts = [t + pp[n:] for t, pp in zip(ts, pps)]
    return [t + _dot(t, p) for t, p in zip(ts, ps)]


def _work_items(rows):
    return [(slice(c * CHUNK, (c + 1) * CHUNK), slice(p * LANES, (p + 1) * LANES))
            for c in range(rows // CHUNK) for p in range(WIDTH // LANES)]


def _tiles(x, items):
    return [x[rows, cols] for rows, cols in items]


def _rwkv_scan(r, k, v, a, b, ld, st_ref, y_ref):
    c2 = 2 * CHUNK
    strict = lax.broadcasted_iota(jnp.int32, (c2, c2), 0) > lax.broadcasted_iota(jnp.int32, (c2, c2), 1)
    trow = lax.broadcasted_iota(jnp.int32, (CHUNK, c2), 0)
    tcol = lax.broadcasted_iota(jnp.int32, (CHUNK, c2), 1) & (CHUNK - 1)
    incl_w = tcol <= trow
    tri = (lax.broadcasted_iota(jnp.int32, (CHUNK, CHUNK), 0)
           >= lax.broadcasted_iota(jnp.int32, (CHUNK, CHUNK), 1)).astype(BF16)
    m0, m1 = _pair_masks()
    items = _work_items(r.shape[0])
    n_pairs = WIDTH // LANES
    r, k, v, a, b, ld = (_tiles(x, items) for x in (r, k, v, a, b, ld))
    cum = [_cumsum_rows(tri, x) for x in ld]
    e_pos = [jnp.exp(x) for x in cum]
    e_neg = [jnp.exp(-x) for x in cum]
    e_end = [jnp.exp(x[CHUNK - 1:CHUNK, :] - x) for x in cum]
    a_ext = [_ext(ai * jnp.exp(ci - li), m0, m1) for ai, ci, li in zip(a, cum, ld)]
    v_ext = [_ext(x, m0, m1) for x in v]
    r_dec = [(ri * ei).astype(BF16) for ri, ei in zip(r, e_pos)]
    s1 = [_dot_nt(_cat([ae, rd], 0), _cat([_ext(bi * en, m0, m1), _ext(ki * en, m0, m1)], 0))
          for ae, rd, bi, ki, en in zip(a_ext, r_dec, b, k, e_neg)]
    a_ab = [jnp.where(strict, s[:c2, :c2], 0.0) for s in s1]
    a_ak = [jnp.where(strict, s[:c2, c2:], 0.0) for s in s1]
    a_rbk = [_cat([jnp.where(incl_w, s[c2:, :c2], 0.0), jnp.where(incl_w, s[c2:, c2:], 0.0)], 1).astype(BF16)
             for s in s1]
    akv = [_dot(x, ve) for x, ve in zip(a_ak, v_ext)]
    t_inv = _neumann_inverse(a_ab)
    wu0 = [_dot(t, _cat([ae, av.astype(BF16)], 1)) for t, ae, av in zip(t_inv, a_ext, akv)]
    bk_end = [_cat([_ext(bi * ee, m0, m1, F32), _ext(ki * ee, m0, m1, F32)], 0) for bi, ki, ee in zip(b, k, e_end)]
    e_last_t = [jnp.broadcast_to(e[CHUNK - 1:CHUNK, :], (LANES, LANES)).T for e in e_pos]
    st = [st_ref[p] for p in range(n_pairs)]
    for c in range(len(items) // n_pairs):
        idx = range(c * n_pairs, (c + 1) * n_pairs)
        z = [_dot(_cat([wu0[i][:, :LANES].astype(BF16), r_dec[i]], 0), s) for i, s in zip(idx, st)]
        uv = [_cat([(wu0[i][:, LANES:] + zi[:c2]).astype(BF16), v_ext[i]], 0) for i, zi in zip(idx, z)]
        for i, zi, uvi in zip(idx, z, uv):
            rows, cols = items[i]
            y_ref[rows, cols] = zi[c2:] + _dot(a_rbk[i], uvi)
        st = [s * e_last_t[i] + _dot_tn(bk_end[i], uvi) for i, s, uvi in zip(idx, st, uv)]
    for p in range(n_pairs):
        st_ref[p] = st[p]


def _gdn_scan(q, k, v, g, beta, st_ref, o_ref):
    c2 = 2 * CHUNK
    row = lax.broadcasted_iota(jnp.int32, (c2, c2), 0)
    col = lax.broadcasted_iota(jnp.int32, (c2, c2), 1)
    causal = ((row >= CHUNK) == (col >= CHUNK)) & (row >= col)
    strict = row > col
    tri = (lax.broadcasted_iota(jnp.int32, (CHUNK, CHUNK), 0)
           >= lax.broadcasted_iota(jnp.int32, (CHUNK, CHUNK), 1)).astype(BF16)
    m0, m1 = _pair_masks()
    items = _work_items(q.shape[0])
    n_pairs = WIDTH // LANES
    q, k, v, g, beta = (_tiles(x, items) for x in (q, k, v, g, beta))
    gc = [_cumsum_rows(tri, x) for x in g]
    g_rows = [_cat([jnp.broadcast_to(x[:, 0:1], (CHUNK, LANES)),
                    jnp.broadcast_to(x[:, HEAD_DIM:HEAD_DIM + 1], (CHUNK, LANES))], 0) for x in gc]
    gamma = [jnp.exp(jnp.where(causal, x - x.T, -jnp.inf)) for x in g_rows]
    e_pos = [jnp.exp(x) for x in gc]
    kb = [ki * bi for ki, bi in zip(k, beta)]
    s1 = [_dot_nt(_cat([_ext(kbi, m0, m1), qi.astype(BF16)], 0), _ext(ki, m0, m1)) for kbi, qi, ki in zip(kb, q, k)]
    a_neg = [-jnp.where(strict, s[:c2] * ga, 0.0) for s, ga in zip(s1, gamma)]
    a_in = [(s[c2:] * (ga[:CHUNK] + ga[CHUNK:])).astype(BF16) for s, ga in zip(s1, gamma)]
    t_inv = _neumann_inverse(a_neg)
    uw = [_dot(t, _cat([_ext(vi * bi, m0, m1), _ext(kbi * ei, m0, m1)], 1))
          for t, vi, bi, kbi, ei in zip(t_inv, v, beta, kb, e_pos)]
    q_dec = [(qi * ei).astype(BF16) for qi, ei in zip(q, e_pos)]
    k_dec = [_ext(ki * jnp.exp(x[CHUNK - 1:CHUNK, :] - x), m0, m1, F32) for ki, x in zip(k, gc)]
    st = [st_ref[p] for p in range(n_pairs)]
    for c in range(len(items) // n_pairs):
        idx = range(c * n_pairs, (c + 1) * n_pairs)
        z = [_dot(_cat([uw[i][:, LANES:].astype(BF16), q_dec[i]], 0), s) for i, s in zip(idx, st)]
        v_new = [(uw[i][:, :LANES] - zi[:c2]).astype(BF16) for i, zi in zip(idx, z)]
        for i, zi, vn in zip(idx, z, v_new):
            rows, cols = items[i]
            o_ref[rows, cols] = zi[c2:] + _dot(a_in[i], vn)
        st = [s * e_pos[i][CHUNK - 1:CHUNK, :] + _dot_tn(k_dec[i], vn) for i, s, vn in zip(idx, st, v_new)]
    for p in range(n_pairs):
        st_ref[p] = st[p]


def _rwkv_kernel(*refs, vres):
    if vres:
        (u_ref, vf_ref, win_ref, mu_ref, prm_ref, wup_ref, aup_ref, gup_ref, vup_ref, o_ref, st_ref, tail_ref,
         y_ref) = refs
    else:
        u_ref, win_ref, mu_ref, prm_ref, wup_ref, aup_ref, gup_ref, o_ref, vf_out_ref, st_ref, tail_ref, y_ref = refs

    @pl.when(pl.program_id(1) == 0)
    def _():
        st_ref[...] = jnp.zeros_like(st_ref)
        tail_ref[...] = jnp.zeros_like(tail_ref)

    z = jnp.dot(u_ref[...], win_ref[...], preferred_element_type=F32)
    rows = z.shape[0]
    zs = _shift_rows(z, tail_ref[...], 1)
    tail_ref[...] = z[rows - SUBLANES:]
    zl = z[:, :RWKV_IN] + (zs[:, :RWKV_IN] - z[:, :RWKV_IN]) * mu_ref[...]
    r, k, v = zl[:, :WIDTH], zl[:, WIDTH:2 * WIDTH], zl[:, 2 * WIDTH:3 * WIDTH]
    lo = zl[:, 3 * WIDTH:3 * WIDTH + LANES]
    g_lo = zl[:, 3 * WIDTH + LANES:]
    w0, a0, k_k, k_a, ln_g, ln_b, r_k, v_bias = (prm_ref[i:i + 1, :] for i in range(8))
    log_w = -_softplus(-(w0 + _dot(jnp.tanh(lo), wup_ref[...]))) - 0.5
    ld = -jnp.exp(log_w)
    iclr = jax.nn.sigmoid(a0 + _dot(lo, aup_ref[...]))
    gate = _dot(jax.nn.sigmoid(g_lo), gup_ref[...])
    if vres:
        x = z[:, RWKV_IN:] + pltpu.roll(zs[:, RWKV_IN:], LANES - VRES_LORA, 1)
        v = v + (vf_ref[...] - v) * jax.nn.sigmoid(v_bias + _dot(x, vup_ref[...]))
    else:
        vf_out_ref[...] = v
    kk = k * k_k
    kk = kk * lax.rsqrt(_head_sum(kk * kk) + L2_EPS)
    k = k * (1.0 + (iclr - 1.0) * k_a)
    _rwkv_scan(r, k, v, -kk, kk * iclr, ld, st_ref, y_ref)
    y = y_ref[...]
    d = y - _head_sum(y) * (1.0 / HEAD_DIM)
    y = d * lax.rsqrt(_head_sum(d * d) * (1.0 / HEAD_DIM) + RWKV_GN_EPS) * ln_g + ln_b
    y = y + _head_sum(r * k * r_k) * v
    o_ref[...] = (y * gate).astype(o_ref.dtype)


def _rwkv(u, w_in, v_first, b, s, mu, prm, w_up, a_up, g_up, v_up):
    vres = v_first is not None
    rows = min(SCAN_ROWS, s)
    zw = w_in.shape[1]
    row = lambda width: pl.BlockSpec((rows, width), lambda bb, c: (bb * (s // rows) + c, 0))
    full = lambda a: pl.BlockSpec(a.shape, lambda bb, c: (0, 0))
    zero = jnp.zeros((LANES // 2, WIDTH), F32)
    consts = [w_in.astype(BF16), mu.reshape(1, -1), prm, _cat([w_up, zero], 0).astype(BF16),
              _cat([zero, a_up], 0).astype(BF16), g_up.astype(BF16)]
    args, in_specs = [u], [row(u.shape[1])]
    if vres:
        args.append(v_first)
        in_specs.append(row(WIDTH))
        consts.append(jnp.pad(v_up, ((0, LANES - VRES_LORA), (0, 0))).astype(BF16))
    out_shape = [jax.ShapeDtypeStruct((b * s, WIDTH), BF16)]
    out_specs = [row(WIDTH)]
    if not vres:
        out_shape.append(jax.ShapeDtypeStruct((b * s, WIDTH), F32))
        out_specs.append(row(WIDTH))
    outs = pl.pallas_call(
        functools.partial(_rwkv_kernel, vres=vres),
        grid=(b, s // rows),
        in_specs=in_specs + [full(a) for a in consts],
        out_specs=out_specs,
        out_shape=out_shape,
        scratch_shapes=[pltpu.VMEM((WIDTH // LANES, LANES, LANES), F32), pltpu.VMEM((SUBLANES, zw), F32),
                        pltpu.VMEM((rows, WIDTH), F32)],
        compiler_params=_cparams(("arbitrary", "arbitrary")),
        name="rwkv",
    )(*args, *consts)
    return (outs[0], v_first) if vres else (outs[0], outs[1])


def _gdn_kernel(u_ref, win_ref, cw_ref, prm_ref, exp_ref, o_ref, st_ref, tail_ref, y_ref):
    @pl.when(pl.program_id(1) == 0)
    def _():
        st_ref[...] = jnp.zeros_like(st_ref)
        tail_ref[...] = jnp.zeros_like(tail_ref)

    z = jnp.dot(u_ref[...], win_ref[...], preferred_element_type=F32)
    rows = z.shape[0]
    x = z[:, :GDN_QKV]
    tail = tail_ref[...]
    conv = x * cw_ref[GDN_CONV - 1:GDN_CONV, :]
    for j in range(1, GDN_CONV):
        conv = conv + _shift_rows(x, tail, j) * cw_ref[GDN_CONV - 1 - j:GDN_CONV - j, :]
    tail_ref[...] = x[rows - SUBLANES:]
    qkv = conv * jax.nn.sigmoid(conv)
    q, k, v = qkv[:, :WIDTH], qkv[:, WIDTH:2 * WIDTH], qkv[:, 2 * WIDTH:]
    q = q * lax.rsqrt(_head_sum(q * q) + L2_EPS) * (HEAD_DIM ** -0.5)
    k = k * lax.rsqrt(_head_sum(k * k) + L2_EPS)
    logits = jnp.dot(_cat(_split3(z[:, GDN_QKV + WIDTH:]), 0), exp_ref[...], preferred_element_type=F32)
    logits = logits[:rows] + logits[rows:2 * rows] + logits[2 * rows:]
    neg_a, dt_bias, norm_g = (prm_ref[i:i + 1, :] for i in range(3))
    beta = jax.nn.sigmoid(logits[:, :WIDTH])
    g = neg_a * _softplus(logits[:, WIDTH:] + dt_bias)
    _gdn_scan(q, k, v, g, beta, st_ref, y_ref)
    o = y_ref[...]
    o = o * lax.rsqrt(_head_sum(o * o) * (1.0 / HEAD_DIM) + NORM_EPS) * norm_g
    gate = z[:, GDN_QKV:GDN_QKV + WIDTH]
    o_ref[...] = (o * (gate * jax.nn.sigmoid(gate))).astype(o_ref.dtype)


def _gdn(u, w_in, b, s, conv_w, a_log, dt_bias, norm_g):
    rows = min(SCAN_ROWS, s)
    row = lambda width: pl.BlockSpec((rows, width), lambda bb, c: (bb * (s // rows) + c, 0))
    full = lambda a: pl.BlockSpec(a.shape, lambda bb, c: (0, 0))
    per_lane = lambda t: jnp.repeat(t, HEAD_DIM)
    prm = jnp.stack([per_lane(-jnp.exp(a_log)), per_lane(dt_bias), jnp.tile(norm_g, N_HEADS)])
    head_of_lane = np.arange(WIDTH) // HEAD_DIM
    expand = np.zeros((LANES, 2 * WIDTH), np.float32)
    expand[head_of_lane, np.arange(WIDTH)] = 1.0
    expand[N_HEADS + head_of_lane, WIDTH + np.arange(WIDTH)] = 1.0
    consts = [w_in.astype(BF16), conv_w, prm, jnp.asarray(expand, BF16)]
    return pl.pallas_call(
        _gdn_kernel,
        grid=(b, s // rows),
        in_specs=[row(u.shape[1])] + [full(a) for a in consts],
        out_specs=row(WIDTH),
        out_shape=jax.ShapeDtypeStruct((b * s, WIDTH), BF16),
        scratch_shapes=[pltpu.VMEM((WIDTH // LANES, LANES, LANES), F32), pltpu.VMEM((SUBLANES, GDN_QKV), F32),
                        pltpu.VMEM((rows, WIDTH), F32)],
        compiler_params=_cparams(("arbitrary", "arbitrary")),
        name="gdn",
    )(u, *consts)


def _merge_kernel(h_ref, u_ref, oa_ref, ob_ref, oc_ref, wg_ref, wa_ref, wb_ref, wc_ref, wo_ref, out_ref):
    d = h_ref.shape[1]
    u = u_ref[...]
    ob = jnp.concatenate([ob_ref[h] for h in range(N_HEADS)], axis=1)
    merged = jnp.zeros(h_ref.shape, F32)
    for n, (o, w_ref) in enumerate(((oa_ref[...], wa_ref), (ob, wb_ref), (oc_ref[...], wc_ref))):
        proj = jnp.dot(o, w_ref[...], preferred_element_type=F32)
        zg = jnp.dot(u, wg_ref[:, n * d:(n + 1) * d], preferred_element_type=F32)
        merged = merged + jax.nn.sigmoid(zg) * proj
    out_ref[...] = h_ref[...] + jnp.dot(merged.astype(BF16), wo_ref[...], preferred_element_type=F32)


def _merge(h, u, oa, ob, oc, wg, wa, wb, wc, wo, s, *, bm=512):
    m, d = h.shape
    bm = min(bm, s)
    row = lambda width: pl.BlockSpec((bm, width), lambda i: (i, 0))
    full = lambda w: pl.BlockSpec(w.shape, lambda i: (0, 0))
    wb = jnp.pad(wb.reshape(N_HEADS, HEAD_DIM, d), ((0, 0), (0, LANES - HEAD_DIM), (0, 0))).reshape(N_HEADS * LANES, d)
    ws = [w.astype(BF16) for w in (wg, wa, wb, wc, wo)]
    ob_spec = pl.BlockSpec((None, N_HEADS, bm, LANES), lambda i: (i // (s // bm), 0, i % (s // bm), 0))
    return pl.pallas_call(
        _merge_kernel,
        grid=(m // bm,),
        in_specs=[row(d), row(d), row(WIDTH), ob_spec, row(WIDTH)] + [full(w) for w in ws],
        out_specs=row(d),
        out_shape=jax.ShapeDtypeStruct((m, d), F32),
        compiler_params=_cparams(("arbitrary",)),
        name="merge",
    )(h, u, oa, ob, oc, *ws)


def _ffn_kernel(be_ref, nu_ref, x_ref, g_ref, wg_ref, wu_ref, wd_ref, o_ref, xs_ref, acc_ref, *, nf, dense):
    i = pl.program_id(0)
    f = pl.program_id(1)
    used = i < nu_ref[0]

    @pl.when(used & (f == 0))
    def _():
        x = x_ref[...]
        ms = jnp.mean(x * x, axis=-1, keepdims=True)
        xs_ref[...] = (x * lax.rsqrt(ms + NORM_EPS) * g_ref[...]).astype(BF16)

    @pl.when(f == 0)
    def _():
        acc_ref[...] = jnp.zeros_like(acc_ref)

    @pl.when(used)
    def _():
        wg, wu, wd = (w_ref[...].astype(BF16) for w_ref in (wg_ref, wu_ref, wd_ref))
        sub = FFN_SUB_ROWS
        chunks = [slice(r, r + sub) for r in range(0, xs_ref.shape[0], sub)]

        def up_proj(rows):
            xs = xs_ref[rows, :]
            return jnp.dot(xs, wg, preferred_element_type=F32), jnp.dot(xs, wu, preferred_element_type=F32)

        pending = up_proj(chunks[0])
        for n, rows in enumerate(chunks):
            gate, up = pending
            if n + 1 < len(chunks):
                pending = up_proj(chunks[n + 1])
            act = (gate * jax.nn.sigmoid(gate) * up).astype(BF16)
            acc_ref[rows, :] += jnp.dot(act, wd, preferred_element_type=F32)

    @pl.when(f == nf - 1)
    def _():
        o_ref[...] = (x_ref[...] + acc_ref[...]) if dense else acc_ref[...]


def _ffn(x, g, wg, wu, wd, block_e, n_used, *, bm, tf, dense):
    r, d = x.shape
    ff = wg.shape[2]
    nf = ff // tf

    def ff_idx(i, f, be, nu):
        return jnp.where(i < nu[0], f, nf - 1)

    return pl.pallas_call(
        functools.partial(_ffn_kernel, nf=nf, dense=dense),
        grid_spec=pltpu.PrefetchScalarGridSpec(
            num_scalar_prefetch=2,
            grid=(r // bm, nf),
            in_specs=[
                pl.BlockSpec((bm, d), lambda i, f, be, nu: (i, 0)),
                pl.BlockSpec((1, d), lambda i, f, be, nu: (0, 0)),
                pl.BlockSpec((None, d, tf), lambda i, f, be, nu: (be[i], 0, ff_idx(i, f, be, nu))),
                pl.BlockSpec((None, d, tf), lambda i, f, be, nu: (be[i], 0, ff_idx(i, f, be, nu))),
                pl.BlockSpec((None, tf, d), lambda i, f, be, nu: (be[i], ff_idx(i, f, be, nu), 0)),
            ],
            out_specs=pl.BlockSpec((bm, d), lambda i, f, be, nu: (i, 0)),
            scratch_shapes=[pltpu.VMEM((bm, d), BF16), pltpu.VMEM((bm, d), F32)],
        ),
        out_shape=jax.ShapeDtypeStruct((r, d), F32),
        compiler_params=_cparams(("arbitrary", "arbitrary")),
        name="ffn",
    )(block_e, n_used, x, g.reshape(1, d).astype(F32), wg, wu, wd)


def _ple_kernel(*refs, combine):
    if combine:
        h_ref, y0_ref, y1_ref, route_ref, p_ref, wg_ref, wp_ref, g_ref, o_ref = refs
        h = h_ref[...] + route_ref[:, 2:3] * y0_ref[...] + route_ref[:, 3:4] * y1_ref[...]
    else:
        h_ref, p_ref, wg_ref, wp_ref, g_ref, o_ref = refs
        h = h_ref[...]
    gate = jnp.dot(h.astype(BF16), wg_ref[...], preferred_element_type=F32)
    e = jnp.dot(p_ref[...].astype(BF16), wp_ref[...], preferred_element_type=F32)
    ms = jnp.mean(e * e, axis=-1, keepdims=True)
    e = e * lax.rsqrt(ms + NORM_EPS) * g_ref[...]
    o_ref[...] = h + jax.nn.sigmoid(gate) * e


def _ple(h, p, w_gate, w_proj, g, expert_out=None, *, bm=512):
    m, d = h.shape
    bm = min(bm, m)
    row = lambda a: pl.BlockSpec((bm, a.shape[1]), lambda i: (i, 0))
    full = lambda a: pl.BlockSpec(a.shape, lambda i: (0, 0))
    rows = [h] + list(expert_out or ()) + [p]
    consts = [w_gate.astype(BF16), w_proj.astype(BF16), g.reshape(1, d)]
    return pl.pallas_call(
        functools.partial(_ple_kernel, combine=expert_out is not None),
        grid=(m // bm,),
        in_specs=[row(a) for a in rows] + [full(a) for a in consts],
        out_specs=row(h),
        out_shape=jax.ShapeDtypeStruct((m, d), F32),
        compiler_params=_cparams(("arbitrary",)),
        name="ple",
    )(*rows, *consts)


def _pad_cols(w, n):
    return jnp.pad(w, ((0, 0), (0, n - w.shape[1])))


def _rope_tables(positions):
    t = positions.size
    inv_freq = 1.0 / (ROPE_THETA ** (jnp.arange(0, QK_ROPE, 2, dtype=F32) / QK_ROPE))
    ang = positions.astype(F32).reshape(t, 1) * inv_freq
    cos, sin = jnp.cos(ang), jnp.sin(ang)
    pad = ((0, 0), (QK_NOPE, LANES - QK_DIM))
    return jnp.pad(_cat([cos, cos], 1), pad, constant_values=1.0), jnp.pad(_cat([-sin, sin], 1), pad)


def _moe(h, g, w_router, wg, wu, wd, *, bm=768, tf=896):
    t, d = h.shape
    n_assign = t * TOP_K
    n_blocks = -(-(n_assign + N_EXPERTS * (bm - 1)) // bm)
    route, counts = _router(h, g, w_router)
    counts = counts[0, :N_EXPERTS].astype(jnp.int32)
    padded = (counts + bm - 1) // bm * bm
    pad_end = jnp.cumsum(padded)
    top_e = route[:, :TOP_K].astype(jnp.int32)
    dest = (pad_end - padded)[top_e] + route[:, 4:4 + TOP_K].astype(jnp.int32)
    src_tok = (jnp.arange(n_blocks * bm, dtype=jnp.int32) % t).at[dest.reshape(-1)].set(
        jnp.arange(n_assign, dtype=jnp.int32) // TOP_K)
    block_e = jnp.minimum(jnp.searchsorted(pad_end, jnp.arange(n_blocks) * bm, side='right'),
                          N_EXPERTS - 1).astype(jnp.int32)
    n_used = (pad_end[-1] // bm).astype(jnp.int32).reshape(1)
    y_rows = _ffn(h[src_tok], g, wg, wu, wd, block_e, n_used, bm=bm, tf=tf, dense=False)
    return y_rows[dest[:, 0]], y_rows[dest[:, 1]], route


def kernel(x, p, positions, norm_mix_g, w_in, rwkv_mu, rwkv_w0, rwkv_w_up, rwkv_a0, rwkv_a_up, rwkv_g_up, rwkv_k_k, rwkv_k_a, rwkv_r_k, rwkv_ln_g, rwkv_ln_b, vres_mu, vres_down, vres_up, vres_b, mla_q_norm_g, mla_kv_norm_g, mla_w_uq, mla_w_ukv, mla_qk_norm_q, mla_qk_norm_k, gdn_conv_w, gdn_a_log, gdn_dt_bias, gdn_norm_g, w_br_rwkv, w_br_mla, w_br_gdn, w_out, norm_ffn_g, ffn_wg, ffn_wu, ffn_wd, moe_router, moe_wg, moe_wu, moe_wd, ple_proj, ple_gate, ple_norm_g):
    b, s, d = x.shape
    t = b * s
    depth = w_in.shape[0]
    rope_tabs = _rope_tables(positions)

    h = x.reshape(t, d)
    v_first = None
    for i in range(depth):
        w = w_in[i]
        g = norm_mix_g[i]
        w_rwkv = w[:, :RWKV_IN]
        if i > 0:
            mu_v = vres_mu[i - 1][:, None]
            vd = vres_down[i - 1]
            w_rwkv = _pad_cols(_cat([w_rwkv, (1.0 - mu_v) * vd, mu_v * vd], 1), RWKV_IN + LANES)
        w_gdn = w[:, RWKV_IN + MLA_IN:RWKV_IN + MLA_IN + GDN_IN]
        w_gdn = _pad_cols(_cat([w_gdn[:, :GDN_QKV], w_gdn[:, GDN_QKV + 2 * N_HEADS:],
                                w_gdn[:, GDN_QKV:GDN_QKV + 2 * N_HEADS]], 1), GDN_QKV + WIDTH + LANES)
        u = _norm(h, g)
        prm = jnp.stack([rwkv_w0[i], rwkv_a0[i], rwkv_k_k[i], rwkv_k_a[i], rwkv_ln_g[i], rwkv_ln_b[i],
                         rwkv_r_k[i].reshape(-1), vres_b[i - 1] if i > 0 else jnp.zeros((WIDTH,), F32)])
        o_a, v_first = _rwkv(u, w_rwkv, v_first, b, s, rwkv_mu[i], prm, rwkv_w_up[i], rwkv_a_up[i], rwkv_g_up[i],
                             vres_up[i - 1] if i > 0 else None)
        q, k, v = _mla_prep(u, w[:, RWKV_IN:RWKV_IN + MLA_IN], rope_tabs, b, s, mla_q_norm_g[i], mla_kv_norm_g[i],
                            mla_w_uq[i], mla_w_ukv[i], mla_qk_norm_q[i], mla_qk_norm_k[i])
        o_b = _flash_attention(q, k, v)
        o_c = _gdn(u, w_gdn, b, s, gdn_conv_w[i], gdn_a_log[i], gdn_dt_bias[i], gdn_norm_g[i])
        h = _merge(h, u, o_a, o_b, o_c, w[:, RWKV_IN + MLA_IN + GDN_IN:], w_br_rwkv[i], w_br_mla[i], w_br_gdn[i],
                   w_out[i], s)
        j = i // 2
        if i % 2 == 0:
            bm = min(1024, t)
            h = _ffn(h, norm_ffn_g[i], ffn_wg[j:j + 1], ffn_wu[j:j + 1], ffn_wd[j:j + 1],
                     jnp.zeros((t // bm,), jnp.int32), jnp.full((1,), t // bm, jnp.int32), bm=bm, tf=256, dense=True)
            expert_out = None
        else:
            expert_out = _moe(h, norm_ffn_g[i], moe_router[j], moe_wg[j], moe_wu[j], moe_wd[j])
        h = _ple(h, p[i].reshape(t, -1), ple_gate[i], ple_proj[i], ple_norm_g[i], expert_out)
    return h.reshape(b, s, d)
```

```python
import functools
import math

import jax
import jax.numpy as jnp
import numpy as np
from jax import lax
from jax.experimental import pallas as pl
from jax.experimental.pallas import tpu as pltpu

F32 = jnp.float32
BF16 = jnp.bfloat16
HI = lax.Precision.HIGHEST

NORM_EPS = 1e-6
RWKV_GN_EPS = 64e-5
L2_EPS = 1e-12
N_HEADS = 8
HEAD_DIM = 64
WIDTH = N_HEADS * HEAD_DIM
RWKV_LORA = 256
RWKV_IN = 3 * WIDTH + RWKV_LORA
VRES_LORA = 32
QK_NOPE = 64
QK_ROPE = 32
QK_DIM = QK_NOPE + QK_ROPE
Q_LORA = 256
KV_LORA = 128
MLA_IN = Q_LORA + KV_LORA + QK_ROPE
ROPE_THETA = 10000.0
GDN_CONV = 4
GDN_QKV = 3 * WIDTH
GDN_IN = GDN_QKV + 2 * N_HEADS + WIDTH
N_EXPERTS = 8
TOP_K = 2
LANES = 128
SUBLANES = 8
CHUNK = 64
SCAN_ROWS = 8 * CHUNK
FFN_SUB_ROWS = 256
VMEM_LIMIT = 48 * 1024 * 1024


def _cparams(sem):
    return pltpu.CompilerParams(dimension_semantics=sem, vmem_limit_bytes=VMEM_LIMIT)


def _norm_kernel(x_ref, g_ref, u_ref):
    x = x_ref[...]
    u_ref[...] = (x * lax.rsqrt(jnp.mean(x * x, axis=-1, keepdims=True) + NORM_EPS) * g_ref[...]).astype(u_ref.dtype)


def _norm(x, g, *, bm=1024):
    m, d = x.shape
    bm = min(bm, m)
    return pl.pallas_call(
        _norm_kernel,
        grid=(m // bm,),
        in_specs=[pl.BlockSpec((bm, d), lambda i: (i, 0)), pl.BlockSpec((1, d), lambda i: (0, 0))],
        out_specs=pl.BlockSpec((bm, d), lambda i: (i, 0)),
        out_shape=jax.ShapeDtypeStruct((m, d), BF16),
        compiler_params=_cparams(("arbitrary",)),
        name="norm",
    )(x, g.reshape(1, d))


def _router_kernel(x_ref, g_ref, wr_ref, before_ref, route_ref, counts_ref, carry_ref):
    @pl.when(pl.program_id(0) == 0)
    def _():
        carry_ref[...] = jnp.zeros_like(carry_ref)

    x = x_ref[...]
    rows = x.shape[0]
    u = x * lax.rsqrt(jnp.mean(x * x, axis=-1, keepdims=True) + NORM_EPS) * g_ref[...]
    lane = lax.broadcasted_iota(jnp.int32, (rows, LANES), 1)
    logits = jnp.where(lane < N_EXPERTS, jnp.dot(u, wr_ref[...], preferred_element_type=F32, precision=HI), -jnp.inf)
    m1 = jnp.max(logits, axis=-1, keepdims=True)
    i1 = jnp.min(jnp.where(logits == m1, lane, LANES), axis=-1, keepdims=True)
    rest = jnp.where(lane == i1, -jnp.inf, logits)
    m2 = jnp.max(rest, axis=-1, keepdims=True)
    i2 = jnp.min(jnp.where(rest == m2, lane, LANES), axis=-1, keepdims=True)
    e2 = jnp.exp(m2 - m1)
    w1 = 1.0 / (1.0 + e2)
    w2 = e2 / (1.0 + e2)
    hit1 = lane == i1
    hit2 = lane == i2
    onehot = jnp.where(hit1 | hit2, 1.0, 0.0)
    seen = jnp.dot(before_ref[...], onehot.astype(BF16), preferred_element_type=F32) + carry_ref[0:1, :]
    r1 = jnp.sum(jnp.where(hit1, seen, 0.0), axis=-1, keepdims=True)
    r2 = jnp.sum(jnp.where(hit2, seen, 0.0), axis=-1, keepdims=True)
    carry_ref[...] = carry_ref[...] + jnp.sum(onehot, axis=0, keepdims=True)
    cols = (i1.astype(F32), i2.astype(F32), w1, w2, r1, r2)
    route = jnp.zeros((rows, LANES), F32)
    for n, c in enumerate(cols):
        route = jnp.where(lane == n, c, route)
    route_ref[...] = route
    counts_ref[...] = carry_ref[...]


def _router(h, g, w_router, *, bm=1024):
    m, d = h.shape
    bm = min(bm, m)
    return pl.pallas_call(
        _router_kernel,
        grid=(m // bm,),
        in_specs=[pl.BlockSpec((bm, d), lambda i: (i, 0)), pl.BlockSpec((1, d), lambda i: (0, 0)),
                  pl.BlockSpec((d, LANES), lambda i: (0, 0)), pl.BlockSpec((bm, bm), lambda i: (0, 0))],
        out_specs=[pl.BlockSpec((bm, LANES), lambda i: (i, 0)), pl.BlockSpec((SUBLANES, LANES), lambda i: (0, 0))],
        out_shape=[jax.ShapeDtypeStruct((m, LANES), F32), jax.ShapeDtypeStruct((SUBLANES, LANES), F32)],
        scratch_shapes=[pltpu.VMEM((SUBLANES, LANES), F32)],
        compiler_params=_cparams(("arbitrary",)),
        name="router",
    )(h, g.reshape(1, d), _pad_cols(w_router, LANES), jnp.tril(jnp.ones((bm, bm), BF16), -1))


def _rms(x, n, g):
    return x * lax.rsqrt(jnp.sum(x * x, axis=-1, keepdims=True) * (1.0 / n) + NORM_EPS) * g


def _mla_prep_kernel(u_ref, cos_ref, sin_ref, win_ref, gq_ref, gkv_ref, gqh_ref, gkh_ref, wq_ref, wqr_ref, wk_ref,
                     wv_ref, q_ref, k_ref, v_ref):
    z = jnp.dot(u_ref[...], win_ref[...], preferred_element_type=F32)
    cq = _rms(z[:, :Q_LORA], Q_LORA, gq_ref[...]).astype(BF16)
    ckv = _rms(z[:, Q_LORA:Q_LORA + KV_LORA], KV_LORA, gkv_ref[...]).astype(BF16)
    k_pe = z[:, Q_LORA + KV_LORA:Q_LORA + KV_LORA + LANES]
    k_pe_rot = z[:, Q_LORA + KV_LORA + LANES:]
    q_all = jnp.dot(cq, wq_ref[...], preferred_element_type=F32)
    q_rot = jnp.dot(cq, wqr_ref[...], preferred_element_type=F32)
    k_all = jnp.dot(ckv, wk_ref[...], preferred_element_type=F32)
    v_all = jnp.dot(ckv, wv_ref[...], preferred_element_type=F32)
    sin = sin_ref[...]
    cos_q = cos_ref[...] * gqh_ref[...]
    cos_k = cos_ref[...] * gkh_ref[...]

    def scale(x):
        return lax.rsqrt(jnp.sum(x * x, axis=-1, keepdims=True) * (1.0 / QK_DIM) + NORM_EPS)

    for h in range(N_HEADS):
        sl = slice(h * LANES, (h + 1) * LANES)
        q = q_all[:, sl]
        k = k_all[:, sl] + k_pe
        q_ref[h] = (scale(q) * (QK_DIM ** -0.5) * (q * cos_q + q_rot[:, sl] * sin)).astype(BF16)
        k_ref[h] = (scale(k) * (k * cos_k + k_pe_rot * sin)).astype(BF16)
        v_ref[h] = v_all[:, sl].astype(BF16)


def _pad_heads(w, width):
    k = w.shape[0]
    w = w.reshape(k, N_HEADS, width)
    return jnp.pad(w, ((0, 0), (0, 0), (0, LANES - width))).reshape(k, N_HEADS * LANES)


def _mla_prep(u, w_in, rope_tabs, b, s, q_norm_g, kv_norm_g, w_uq, w_ukv, qk_g_q, qk_g_k, *, bm=512):
    bm = min(bm, s)
    half = QK_ROPE // 2
    swap = lambda w: _cat([w[..., half:], w[..., :half]], -1)
    rope_slab = lambda w: jnp.pad(w, ((0, 0), (QK_NOPE, LANES - QK_DIM)))
    lora = Q_LORA + KV_LORA
    w_in = _cat([w_in[:, :lora], rope_slab(w_in[:, lora:]), rope_slab(swap(w_in[:, lora:] * qk_g_k[QK_NOPE:]))],
                1).astype(BF16)
    w_ukv = w_ukv.reshape(KV_LORA, N_HEADS, QK_NOPE + HEAD_DIM)
    w_uq3 = w_uq.reshape(Q_LORA, N_HEADS, QK_DIM)
    wq = _pad_heads(w_uq, QK_DIM).astype(BF16)
    wq_rot = jnp.pad(swap(w_uq3[..., QK_NOPE:] * qk_g_q[QK_NOPE:]), ((0, 0), (0, 0), (QK_NOPE, LANES - QK_DIM)))
    wq_rot = wq_rot.reshape(Q_LORA, N_HEADS * LANES).astype(BF16)
    wk = _pad_heads(w_ukv[..., :QK_NOPE].reshape(KV_LORA, -1), QK_NOPE).astype(BF16)
    wv = _pad_heads(w_ukv[..., QK_NOPE:].reshape(KV_LORA, -1), HEAD_DIM).astype(BF16)
    pad_g = lambda g: jnp.pad(g, (0, LANES - g.shape[0])).reshape(1, LANES)
    row = lambda width: pl.BlockSpec((bm, width), lambda bb, i: (bb * (s // bm) + i, 0))
    full = lambda a: pl.BlockSpec(a.shape, lambda bb, i: (0, 0))
    consts = [w_in, q_norm_g.reshape(1, -1), kv_norm_g.reshape(1, -1), pad_g(qk_g_q), pad_g(qk_g_k), wq, wq_rot,
              wk, wv]
    out_spec = pl.BlockSpec((None, N_HEADS, bm, LANES), lambda bb, i: (bb, 0, i, 0))
    out_shape = jax.ShapeDtypeStruct((b, N_HEADS, s, LANES), BF16)
    return pl.pallas_call(
        _mla_prep_kernel,
        grid=(b, s // bm),
        in_specs=[row(u.shape[1]), row(LANES), row(LANES)] + [full(a) for a in consts],
        out_specs=[out_spec] * 3,
        out_shape=[out_shape] * 3,
        compiler_params=_cparams(("arbitrary", "arbitrary")),
        name="mla_prep",
    )(u, *rope_tabs, *consts)


def _flash_kernel(qi_ref, kj_ref, q_ref, k_ref, v_ref, o_ref, m_ref, l_ref, acc_ref, *, blk, sub):
    n = pl.program_id(2)
    i = qi_ref[n]
    j = kj_ref[n]
    hb = q_ref.shape[0]

    @pl.when(j == 0)
    def _():
        m_ref[...] = jnp.full_like(m_ref, -jnp.inf)
        l_ref[...] = jnp.zeros_like(l_ref)
        acc_ref[...] = jnp.zeros_like(acc_ref)

    def update(masked):
        chains = [(h, r) for h in range(hb) for r in range(blk // sub)]

        def qk(h, r):
            nk = (r + 1) * sub if masked else blk
            s = lax.dot_general(q_ref[h, r * sub:(r + 1) * sub, :], k_ref[h, :nk, :], (((1,), (1,)), ((), ())),
                                preferred_element_type=F32)
            if masked:
                row = lax.broadcasted_iota(jnp.int32, (sub, nk), 0) + r * sub
                s = jnp.where(row >= lax.broadcasted_iota(jnp.int32, (sub, nk), 1), s, -jnp.inf)
            return s

        def softmax_pv(h, r, s):
            rows = slice(r * sub, (r + 1) * sub)
            m_old = m_ref[h, rows, :]
            m_new = jnp.maximum(m_old, jnp.max(s, axis=-1, keepdims=True))
            alpha = jnp.exp(m_old - m_new)
            p = jnp.exp(s - m_new[:, :1])
            l_ref[h, rows, :] = alpha * l_ref[h, rows, :] + jnp.sum(p, axis=-1, keepdims=True)
            m_ref[h, rows, :] = m_new
            pv = jnp.dot(p.astype(BF16), v_ref[h, :s.shape[1], :], preferred_element_type=F32)
            acc_ref[h, rows, :] = alpha * acc_ref[h, rows, :] + pv

        s_prev = qk(*chains[0])
        for c in range(len(chains)):
            s_next = qk(*chains[c + 1]) if c + 1 < len(chains) else None
            softmax_pv(*chains[c], s_prev)
            s_prev = s_next

    @pl.when(j < i)
    def _():
        update(False)

    @pl.when(j == i)
    def _():
        update(True)
        o_ref[...] = (acc_ref[...] / l_ref[...]).astype(o_ref.dtype)


def _flash_attention(q, k, v, *, blk=1024, sub=256, hb=2):
    b, h, s, d = q.shape
    blk = min(blk, s)
    sub = min(sub, blk)
    nb = s // blk
    qi = np.array([i for i in range(nb) for j in range(i + 1)], np.int32)
    kj = np.array([j for i in range(nb) for j in range(i + 1)], np.int32)
    q_spec = pl.BlockSpec((None, hb, blk, d), lambda bb, hh, n, qi_r, kj_r: (bb, hh, qi_r[n], 0))
    kv_spec = pl.BlockSpec((None, hb, blk, d), lambda bb, hh, n, qi_r, kj_r: (bb, hh, kj_r[n], 0))
    return pl.pallas_call(
        functools.partial(_flash_kernel, blk=blk, sub=sub),
        grid_spec=pltpu.PrefetchScalarGridSpec(
            num_scalar_prefetch=2,
            grid=(b, h // hb, len(qi)),
            in_specs=[q_spec, kv_spec, kv_spec],
            out_specs=q_spec,
            scratch_shapes=[pltpu.VMEM((hb, blk, LANES), F32), pltpu.VMEM((hb, blk, LANES), F32),
                            pltpu.VMEM((hb, blk, d), F32)],
        ),
        out_shape=jax.ShapeDtypeStruct((b, h, s, d), BF16),
        compiler_params=_cparams(("arbitrary", "arbitrary", "arbitrary")),
        name="flash",
    )(jnp.asarray(qi), jnp.asarray(kj), q, k, v)


def _dot(a, b):
    return jnp.dot(a.astype(BF16), b.astype(BF16), preferred_element_type=F32)


def _dot_nt(a, b):
    return lax.dot_general(a.astype(BF16), b.astype(BF16), (((1,), (1,)), ((), ())), preferred_element_type=F32)


def _dot_tn(a, b):
    return jnp.dot(a.T.astype(BF16), b.astype(BF16), preferred_element_type=F32)


def _split3(x):
    hi = x.astype(BF16)
    r1 = x - hi.astype(F32)
    mid = r1.astype(BF16)
    return hi, mid, (r1 - mid.astype(F32)).astype(BF16)


def _cumsum_rows(tri, x):
    s = jnp.dot(tri, jnp.concatenate(_split3(x), axis=1), preferred_element_type=F32)
    n = x.shape[1]
    return s[:, :n] + s[:, n:2 * n] + s[:, 2 * n:]


def _head_sum(x):
    left = lax.broadcasted_iota(jnp.int32, (x.shape[0], LANES), 1) < HEAD_DIM
    outs = []
    for p in range(WIDTH // LANES):
        xs = x[:, p * LANES:(p + 1) * LANES]
        s_left = jnp.sum(jnp.where(left, xs, 0.0), axis=-1, keepdims=True)
        s_right = jnp.sum(jnp.where(left, 0.0, xs), axis=-1, keepdims=True)
        outs.append(jnp.where(left, s_left, s_right))
    return jnp.concatenate(outs, axis=1)


def _shift_rows(x, tail, j):
    rolled = pltpu.roll(x, j, 0)
    top = jnp.where(lax.broadcasted_iota(jnp.int32, (SUBLANES, x.shape[1]), 0) < j,
                    pltpu.roll(tail, j, 0), rolled[:SUBLANES])
    return jnp.concatenate([top, rolled[SUBLANES:]], axis=0)


def _softplus(x):
    return jnp.maximum(x, 0.0) + jnp.log(1.0 + jnp.exp(-jnp.abs(x)))


def _pair_masks():
    lane = lax.broadcasted_iota(jnp.int32, (CHUNK, LANES), 1)
    m0 = (lane < HEAD_DIM).astype(F32)
    return m0, 1.0 - m0


def _ext(x, m0, m1, dtype=BF16):
    return jnp.concatenate([x * m0, x * m1], axis=0).astype(dtype)


def _cat(xs, axis):
    return jnp.concatenate(xs, axis=axis)


def _neumann_inverse(mats):
    n = mats[0].shape[0]
    eye = (lax.broadcasted_iota(jnp.int32, (n, n), 0) == lax.broadcasted_iota(jnp.int32, (n, n), 1)).astype(F32)
    ts = [eye + a for a in mats]
    ps = [_dot(a, a) for a in mats]
    for _ in range(int(math.log2(CHUNK)) - 2):
        pps = [_dot(_cat([p, t], 0), p) for p, t in zip(ps, ts)]
        ps = [pp[:n] for pp in pps]
        ts = [t + pp[n:] for t, pp in zip(ts, pps)]
    return [t + _dot(t, p) for t, p in zip(ts, ps)]


def _work_items(rows):
    return [(slice(c * CHUNK, (c + 1) * CHUNK), slice(p * LANES, (p + 1) * LANES))
            for c in range(rows // CHUNK) for p in range(WIDTH // LANES)]


def _tiles(x, items):
    return [x[rows, cols] for rows, cols in items]


def _rwkv_scan(r, k, v, a, b, ld, st_ref, y_ref):
    c2 = 2 * CHUNK
    strict = lax.broadcasted_iota(jnp.int32, (c2, c2), 0) > lax.broadcasted_iota(jnp.int32, (c2, c2), 1)
    trow = lax.broadcasted_iota(jnp.int32, (CHUNK, c2), 0)
    tcol = lax.broadcasted_iota(jnp.int32, (CHUNK, c2), 1) & (CHUNK - 1)
    incl_w = tcol <= trow
    tri = (lax.broadcasted_iota(jnp.int32, (CHUNK, CHUNK), 0)
           >= lax.broadcasted_iota(jnp.int32, (CHUNK, CHUNK), 1)).astype(BF16)
    m0, m1 = _pair_masks()
    items = _work_items(r.shape[0])
    n_pairs = WIDTH // LANES
    r, k, v, a, b, ld = (_tiles(x, items) for x in (r, k, v, a, b, ld))
    cum = [_cumsum_rows(tri, x) for x in ld]
    e_pos = [jnp.exp(x) for x in cum]
    e_neg = [jnp.exp(-x) for x in cum]
    e_end = [jnp.exp(x[CHUNK - 1:CHUNK, :] - x) for x in cum]
    a_ext = [_ext(ai * jnp.exp(ci - li), m0, m1) for ai, ci, li in zip(a, cum, ld)]
    v_ext = [_ext(x, m0, m1) for x in v]
    r_dec = [(ri * ei).astype(BF16) for ri, ei in zip(r, e_pos)]
    s1 = [_dot_nt(_cat([ae, rd], 0), _cat([_ext(bi * en, m0, m1), _ext(ki * en, m0, m1)], 0))
          for ae, rd, bi, ki, en in zip(a_ext, r_dec, b, k, e_neg)]
    a_ab = [jnp.where(strict, s[:c2, :c2], 0.0) for s in s1]
    a_ak = [jnp.where(strict, s[:c2, c2:], 0.0) for s in s1]
    a_rbk = [_cat([jnp.where(incl_w, s[c2:, :c2], 0.0), jnp.where(incl_w, s[c2:, c2:], 0.0)], 1).astype(BF16)
             for s in s1]
    akv = [_dot(x, ve) for x, ve in zip(a_ak, v_ext)]
    t_inv = _neumann_inverse(a_ab)
    wu0 = [_dot(t, _cat([ae, av.astype(BF16)], 1)) for t, ae, av in zip(t_inv, a_ext, akv)]
    bk_end = [_cat([_ext(bi * ee, m0, m1, F32), _ext(ki * ee, m0, m1, F32)], 0) for bi, ki, ee in zip(b, k, e_end)]
    e_last_t = [jnp.broadcast_to(e[CHUNK - 1:CHUNK, :], (LANES, LANES)).T for e in e_pos]
    st = [st_ref[p] for p in range(n_pairs)]
    for c in range(len(items) // n_pairs):
        idx = range(c * n_pairs, (c + 1) * n_pairs)
        z = [_dot(_cat([wu0[i][:, :LANES].astype(BF16), r_dec[i]], 0), s) for i, s in zip(idx, st)]
        uv = [_cat([(wu0[i][:, LANES:] + zi[:c2]).astype(BF16), v_ext[i]], 0) for i, zi in zip(idx, z)]
        for i, zi, uvi in zip(idx, z, uv):
            rows, cols = items[i]
            y_ref[rows, cols] = zi[c2:] + _dot(a_rbk[i], uvi)
        st = [s * e_last_t[i] + _dot_tn(bk_end[i], uvi) for i, s, uvi in zip(idx, st, uv)]
    for p in range(n_pairs):
        st_ref[p] = st[p]


def _gdn_scan(q, k, v, g, beta, st_ref, o_ref):
    c2 = 2 * CHUNK
    row = lax.broadcasted_iota(jnp.int32, (c2, c2), 0)
    col = lax.broadcasted_iota(jnp.int32, (c2, c2), 1)
    causal = ((row >= CHUNK) == (col >= CHUNK)) & (row >= col)
    strict = row > col
    tri = (lax.broadcasted_iota(jnp.int32, (CHUNK, CHUNK), 0)
           >= lax.broadcasted_iota(jnp.int32, (CHUNK, CHUNK), 1)).astype(BF16)
    m0, m1 = _pair_masks()
    items = _work_items(q.shape[0])
    n_pairs = WIDTH // LANES
    q, k, v, g, beta = (_tiles(x, items) for x in (q, k, v, g, beta))
    gc = [_cumsum_rows(tri, x) for x in g]
    g_rows = [_cat([jnp.broadcast_to(x[:, 0:1], (CHUNK, LANES)),
                    jnp.broadcast_to(x[:, HEAD_DIM:HEAD_DIM + 1], (CHUNK, LANES))], 0) for x in gc]
    gamma = [jnp.exp(jnp.where(causal, x - x.T, -jnp.inf)) for x in g_rows]
    e_pos = [jnp.exp(x) for x in gc]
    kb = [ki * bi for ki, bi in zip(k, beta)]
    s1 = [_dot_nt(_cat([_ext(kbi, m0, m1), qi.astype(BF16)], 0), _ext(ki, m0, m1)) for kbi, qi, ki in zip(kb, q, k)]
    a_neg = [-jnp.where(strict, s[:c2] * ga, 0.0) for s, ga in zip(s1, gamma)]
    a_in = [(s[c2:] * (ga[:CHUNK] + ga[CHUNK:])).astype(BF16) for s, ga in zip(s1, gamma)]
    t_inv = _neumann_inverse(a_neg)
    uw = [_dot(t, _cat([_ext(vi * bi, m0, m1), _ext(kbi * ei, m0, m1)], 1))
          for t, vi, bi, kbi, ei in zip(t_inv, v, beta, kb, e_pos)]
    q_dec = [(qi * ei).astype(BF16) for qi, ei in zip(q, e_pos)]
    k_dec = [_ext(ki * jnp.exp(x[CHUNK - 1:CHUNK, :] - x), m0, m1, F32) for ki, x in zip(k, gc)]
    st = [st_ref[p] for p in range(n_pairs)]
    for c in range(len(items) // n_pairs):
        idx = range(c * n_pairs, (c + 1) * n_pairs)
        z = [_dot(_cat([uw[i][:, LANES:].astype(BF16), q_dec[i]], 0), s) for i, s in zip(idx, st)]
        v_new = [(uw[i][:, :LANES] - zi[:c2]).astype(BF16) for i, zi in zip(idx, z)]
        for i, zi, vn in zip(idx, z, v_new):
            rows, cols = items[i]
            o_ref[rows, cols] = zi[c2:] + _dot(a_in[i], vn)
        st = [s * e_pos[i][CHUNK - 1:CHUNK, :] + _dot_tn(k_dec[i], vn) for i, s, vn in zip(idx, st, v_new)]
    for p in range(n_pairs):
        st_ref[p] = st[p]


def _rwkv_kernel(*refs, vres):
    if vres:
        (u_ref, vf_ref, win_ref, mu_ref, prm_ref, wup_ref, aup_ref, gup_ref, vup_ref, o_ref, st_ref, tail_ref,
         y_ref) = refs
    else:
        u_ref, win_ref, mu_ref, prm_ref, wup_ref, aup_ref, gup_ref, o_ref, vf_out_ref, st_ref, tail_ref, y_ref = refs

    @pl.when(pl.program_id(1) == 0)
    def _():
        st_ref[...] = jnp.zeros_like(st_ref)
        tail_ref[...] = jnp.zeros_like(tail_ref)

    z = jnp.dot(u_ref[...], win_ref[...], preferred_element_type=F32)
    rows = z.shape[0]
    zs = _shift_rows(z, tail_ref[...], 1)
    tail_ref[...] = z[rows - SUBLANES:]
    zl = z[:, :RWKV_IN] + (zs[:, :RWKV_IN] - z[:, :RWKV_IN]) * mu_ref[...]
    r, k, v = zl[:, :WIDTH], zl[:, WIDTH:2 * WIDTH], zl[:, 2 * WIDTH:3 * WIDTH]
    lo = zl[:, 3 * WIDTH:3 * WIDTH + LANES]
    g_lo = zl[:, 3 * WIDTH + LANES:]
    w0, a0, k_k, k_a, ln_g, ln_b, r_k, v_bias = (prm_ref[i:i + 1, :] for i in range(8))
    log_w = -_softplus(-(w0 + _dot(jnp.tanh(lo), wup_ref[...]))) - 0.5
    ld = -jnp.exp(log_w)
    iclr = jax.nn.sigmoid(a0 + _dot(lo, aup_ref[...]))
    gate = _dot(jax.nn.sigmoid(g_lo), gup_ref[...])
    if vres:
        x = z[:, RWKV_IN:] + pltpu.roll(zs[:, RWKV_IN:], LANES - VRES_LORA, 1)
        v = v + (vf_ref[...] - v) * jax.nn.sigmoid(v_bias + _dot(x, vup_ref[...]))
    else:
        vf_out_ref[...] = v
    kk = k * k_k
    kk = kk * lax.rsqrt(_head_sum(kk * kk) + L2_EPS)
    k = k * (1.0 + (iclr - 1.0) * k_a)
    _rwkv_scan(r, k, v, -kk, kk * iclr, ld, st_ref, y_ref)
    y = y_ref[...]
    d = y - _head_sum(y) * (1.0 / HEAD_DIM)
    y = d * lax.rsqrt(_head_sum(d * d) * (1.0 / HEAD_DIM) + RWKV_GN_EPS) * ln_g + ln_b
    y = y + _head_sum(r * k * r_k) * v
    o_ref[...] = (y * gate).astype(o_ref.dtype)


def _rwkv(u, w_in, v_first, b, s, mu, prm, w_up, a_up, g_up, v_up):
    vres = v_first is not None
    rows = min(SCAN_ROWS, s)
    zw = w_in.shape[1]
    row = lambda width: pl.BlockSpec((rows, width), lambda bb, c: (bb * (s // rows) + c, 0))
    full = lambda a: pl.BlockSpec(a.shape, lambda bb, c: (0, 0))
    zero = jnp.zeros((LANES // 2, WIDTH), F32)
    consts = [w_in.astype(BF16), mu.reshape(1, -1), prm, _cat([w_up, zero], 0).astype(BF16),
              _cat([zero, a_up], 0).astype(BF16), g_up.astype(BF16)]
    args, in_specs = [u], [row(u.shape[1])]
    if vres:
        args.append(v_first)
        in_specs.append(row(WIDTH))
        consts.append(jnp.pad(v_up, ((0, LANES - VRES_LORA), (0, 0))).astype(BF16))
    out_shape = [jax.ShapeDtypeStruct((b * s, WIDTH), BF16)]
    out_specs = [row(WIDTH)]
    if not vres:
        out_shape.append(jax.ShapeDtypeStruct((b * s, WIDTH), F32))
        out_specs.append(row(WIDTH))
    outs = pl.pallas_call(
        functools.partial(_rwkv_kernel, vres=vres),
        grid=(b, s // rows),
        in_specs=in_specs + [full(a) for a in consts],
        out_specs=out_specs,
        out_shape=out_shape,
        scratch_shapes=[pltpu.VMEM((WIDTH // LANES, LANES, LANES), F32), pltpu.VMEM((SUBLANES, zw), F32),
                        pltpu.VMEM((rows, WIDTH), F32)],
        compiler_params=_cparams(("arbitrary", "arbitrary")),
        name="rwkv",
    )(*args, *consts)
    return (outs[0], v_first) if vres else (outs[0], outs[1])


def _gdn_kernel(u_ref, win_ref, cw_ref, prm_ref, exp_ref, o_ref, st_ref, tail_ref, y_ref):
    @pl.when(pl.program_id(1) == 0)
    def _():
        st_ref[...] = jnp.zeros_like(st_ref)
        tail_ref[...] = jnp.zeros_like(tail_ref)

    z = jnp.dot(u_ref[...], win_ref[...], preferred_element_type=F32)
    rows = z.shape[0]
    x = z[:, :GDN_QKV]
    tail = tail_ref[...]
    conv = x * cw_ref[GDN_CONV - 1:GDN_CONV, :]
    for j in range(1, GDN_CONV):
        conv = conv + _shift_rows(x, tail, j) * cw_ref[GDN_CONV - 1 - j:GDN_CONV - j, :]
    tail_ref[...] = x[rows - SUBLANES:]
    qkv = conv * jax.nn.sigmoid(conv)
    q, k, v = qkv[:, :WIDTH], qkv[:, WIDTH:2 * WIDTH], qkv[:, 2 * WIDTH:]
    q = q * lax.rsqrt(_head_sum(q * q) + L2_EPS) * (HEAD_DIM ** -0.5)
    k = k * lax.rsqrt(_head_sum(k * k) + L2_EPS)
    logits = jnp.dot(_cat(_split3(z[:, GDN_QKV + WIDTH:]), 0), exp_ref[...], preferred_element_type=F32)
    logits = logits[:rows] + logits[rows:2 * rows] + logits[2 * rows:]
    neg_a, dt_bias, norm_g = (prm_ref[i:i + 1, :] for i in range(3))
    beta = jax.nn.sigmoid(logits[:, :WIDTH])
    g = neg_a * _softplus(logits[:, WIDTH:] + dt_bias)
    _gdn_scan(q, k, v, g, beta, st_ref, y_ref)
    o = y_ref[...]
    o = o * lax.rsqrt(_head_sum(o * o) * (1.0 / HEAD_DIM) + NORM_EPS) * norm_g
    gate = z[:, GDN_QKV:GDN_QKV + WIDTH]
    o_ref[...] = (o * (gate * jax.nn.sigmoid(gate))).astype(o_ref.dtype)


def _gdn(u, w_in, b, s, conv_w, a_log, dt_bias, norm_g):
    rows = min(SCAN_ROWS, s)
    row = lambda width: pl.BlockSpec((rows, width), lambda bb, c: (bb * (s // rows) + c, 0))
    full = lambda a: pl.BlockSpec(a.shape, lambda bb, c: (0, 0))
    per_lane = lambda t: jnp.repeat(t, HEAD_DIM)
    prm = jnp.stack([per_lane(-jnp.exp(a_log)), per_lane(dt_bias), jnp.tile(norm_g, N_HEADS)])
    head_of_lane = np.arange(WIDTH) // HEAD_DIM
    expand = np.zeros((LANES, 2 * WIDTH), np.float32)
    expand[head_of_lane, np.arange(WIDTH)] = 1.0
    expand[N_HEADS + head_of_lane, WIDTH + np.arange(WIDTH)] = 1.0
    consts = [w_in.astype(BF16), conv_w, prm, jnp.asarray(expand, BF16)]
    return pl.pallas_call(
        _gdn_kernel,
        grid=(b, s // rows),
        in_specs=[row(u.shape[1])] + [full(a) for a in consts],
        out_specs=row(WIDTH),
        out_shape=jax.ShapeDtypeStruct((b * s, WIDTH), BF16),
        scratch_shapes=[pltpu.VMEM((WIDTH // LANES, LANES, LANES), F32), pltpu.VMEM((SUBLANES, GDN_QKV), F32),
                        pltpu.VMEM((rows, WIDTH), F32)],
        compiler_params=_cparams(("arbitrary", "arbitrary")),
        name="gdn",
    )(u, *consts)


def _merge_kernel(h_ref, u_ref, oa_ref, ob_ref, oc_ref, wg_ref, wa_ref, wb_ref, wc_ref, wo_ref, out_ref):
    d = h_ref.shape[1]
    u = u_ref[...]
    ob = jnp.concatenate([ob_ref[h] for h in range(N_HEADS)], axis=1)
    merged = jnp.zeros(h_ref.shape, F32)
    for n, (o, w_ref) in enumerate(((oa_ref[...], wa_ref), (ob, wb_ref), (oc_ref[...], wc_ref))):
        proj = jnp.dot(o, w_ref[...], preferred_element_type=F32)
        zg = jnp.dot(u, wg_ref[:, n * d:(n + 1) * d], preferred_element_type=F32)
        merged = merged + jax.nn.sigmoid(zg) * proj
    out_ref[...] = h_ref[...] + jnp.dot(merged.astype(BF16), wo_ref[...], preferred_element_type=F32)


def _merge(h, u, oa, ob, oc, wg, wa, wb, wc, wo, s, *, bm=512):
    m, d = h.shape
    bm = min(bm, s)
    row = lambda width: pl.BlockSpec((bm, width), lambda i: (i, 0))
    full = lambda w: pl.BlockSpec(w.shape, lambda i: (0, 0))
    wb = jnp.pad(wb.reshape(N_HEADS, HEAD_DIM, d), ((0, 0), (0, LANES - HEAD_DIM), (0, 0))).reshape(N_HEADS * LANES, d)
    ws = [w.astype(BF16) for w in (wg, wa, wb, wc, wo)]
    ob_spec = pl.BlockSpec((None, N_HEADS, bm, LANES), lambda i: (i // (s // bm), 0, i % (s // bm), 0))
    return pl.pallas_call(
        _merge_kernel,
        grid=(m // bm,),
        in_specs=[row(d), row(d), row(WIDTH), ob_spec, row(WIDTH)] + [full(w) for w in ws],
        out_specs=row(d),
        out_shape=jax.ShapeDtypeStruct((m, d), F32),
        compiler_params=_cparams(("arbitrary",)),
        name="merge",
    )(h, u, oa, ob, oc, *ws)


def _ffn_kernel(be_ref, nv_ref, x_ref, g_ref, wg_ref, wu_ref, wd_ref, o_ref, xs_ref, acc_ref, wgs_ref, wus_ref,
                wds_ref, *, nf, dense):
    i = pl.program_id(0)
    f = pl.program_id(1)
    n_valid = nv_ref[i]

    @pl.when((n_valid > 0) & (f == 0))
    def _():
        x = x_ref[...]
        ms = jnp.mean(x * x, axis=-1, keepdims=True)
        xs_ref[...] = (x * lax.rsqrt(ms + NORM_EPS) * g_ref[...]).astype(BF16)

    @pl.when(f == 0)
    def _():
        acc_ref[...] = jnp.zeros_like(acc_ref)

    @pl.when(n_valid > 0)
    def _():
        wgs_ref[...] = wg_ref[...].astype(BF16)
        wus_ref[...] = wu_ref[...].astype(BF16)
        wds_ref[...] = wd_ref[...].astype(BF16)

    for r in range(0, xs_ref.shape[0], FFN_SUB_ROWS):
        @pl.when(n_valid > r)
        def _():
            rows = slice(r, r + FFN_SUB_ROWS)
            xs = xs_ref[rows, :]
            gate = jnp.dot(xs, wgs_ref[...], preferred_element_type=F32)
            up = jnp.dot(xs, wus_ref[...], preferred_element_type=F32)
            act = (gate * jax.nn.sigmoid(gate) * up).astype(BF16)
            acc_ref[rows, :] += jnp.dot(act, wds_ref[...], preferred_element_type=F32)

    @pl.when(f == nf - 1)
    def _():
        o_ref[...] = (x_ref[...] + acc_ref[...]) if dense else acc_ref[...]


def _ffn(x, g, wg, wu, wd, block_e, n_valid, *, bm, tf, dense):
    r, d = x.shape
    ff = wg.shape[2]
    nf = ff // tf

    def ff_idx(i, f, be, nv):
        return jnp.where(nv[i] > 0, f, nf - 1)

    return pl.pallas_call(
        functools.partial(_ffn_kernel, nf=nf, dense=dense),
        grid_spec=pltpu.PrefetchScalarGridSpec(
            num_scalar_prefetch=2,
            grid=(r // bm, nf),
            in_specs=[
                pl.BlockSpec((bm, d), lambda i, f, be, nu: (i, 0)),
                pl.BlockSpec((1, d), lambda i, f, be, nu: (0, 0)),
                pl.BlockSpec((None, d, tf), lambda i, f, be, nu: (be[i], 0, ff_idx(i, f, be, nu))),
                pl.BlockSpec((None, d, tf), lambda i, f, be, nu: (be[i], 0, ff_idx(i, f, be, nu))),
                pl.BlockSpec((None, tf, d), lambda i, f, be, nu: (be[i], ff_idx(i, f, be, nu), 0)),
            ],
            out_specs=pl.BlockSpec((bm, d), lambda i, f, be, nu: (i, 0)),
            scratch_shapes=[pltpu.VMEM((bm, d), BF16), pltpu.VMEM((bm, d), F32), pltpu.VMEM((d, tf), BF16),
                            pltpu.VMEM((d, tf), BF16), pltpu.VMEM((tf, d), BF16)],
        ),
        out_shape=jax.ShapeDtypeStruct((r, d), F32),
        compiler_params=_cparams(("arbitrary", "arbitrary")),
        name="ffn",
    )(block_e, n_valid, x, g.reshape(1, d).astype(F32), wg, wu, wd)


def _ple_kernel(*refs, combine, emit_norm):
    refs = list(refs)
    h = refs.pop(0)[...]
    if combine:
        y0_ref, y1_ref, route_ref = refs[:3]
        del refs[:3]
        h = h + route_ref[:, 2:3] * y0_ref[...] + route_ref[:, 3:4] * y1_ref[...]
    p_ref, wg_ref, wp_ref, g_ref = refs[:4]
    gate = jnp.dot(h.astype(BF16), wg_ref[...], preferred_element_type=F32)
    e = jnp.dot(p_ref[...].astype(BF16), wp_ref[...], preferred_element_type=F32)
    ms = jnp.mean(e * e, axis=-1, keepdims=True)
    e = e * lax.rsqrt(ms + NORM_EPS) * g_ref[...]
    out = h + jax.nn.sigmoid(gate) * e
    if emit_norm:
        gn_ref, o_ref, u_ref = refs[4:]
        u_ref[...] = (out * lax.rsqrt(jnp.mean(out * out, axis=-1, keepdims=True) + NORM_EPS) * gn_ref[...]).astype(BF16)
    else:
        o_ref, = refs[4:]
    o_ref[...] = out


def _ple(h, p, w_gate, w_proj, g, expert_out=None, g_next=None, *, bm=512):
    m, d = h.shape
    bm = min(bm, m)
    row = lambda a: pl.BlockSpec((bm, a.shape[1]), lambda i: (i, 0))
    full = lambda a: pl.BlockSpec(a.shape, lambda i: (0, 0))
    rows = [h] + list(expert_out or ()) + [p]
    consts = [w_gate.astype(BF16), w_proj.astype(BF16), g.reshape(1, d)]
    out_shape = [jax.ShapeDtypeStruct((m, d), F32)]
    if g_next is not None:
        consts.append(g_next.reshape(1, d))
        out_shape.append(jax.ShapeDtypeStruct((m, d), BF16))
    outs = pl.pallas_call(
        functools.partial(_ple_kernel, combine=expert_out is not None, emit_norm=g_next is not None),
        grid=(m // bm,),
        in_specs=[row(a) for a in rows] + [full(a) for a in consts],
        out_specs=[row(h)] * len(out_shape),
        out_shape=out_shape,
        compiler_params=_cparams(("arbitrary",)),
        name="ple",
    )(*rows, *consts)
    return outs if g_next is not None else (outs[0], None)


def _pad_cols(w, n):
    return jnp.pad(w, ((0, 0), (0, n - w.shape[1])))


def _rope_tables(positions):
    t = positions.size
    inv_freq = 1.0 / (ROPE_THETA ** (jnp.arange(0, QK_ROPE, 2, dtype=F32) / QK_ROPE))
    ang = positions.astype(F32).reshape(t, 1) * inv_freq
    cos, sin = jnp.cos(ang), jnp.sin(ang)
    pad = ((0, 0), (QK_NOPE, LANES - QK_DIM))
    return jnp.pad(_cat([cos, cos], 1), pad, constant_values=1.0), jnp.pad(_cat([-sin, sin], 1), pad)


def _moe(h, g, w_router, wg, wu, wd, *, bm=768, tf=512):
    t, d = h.shape
    n_assign = t * TOP_K
    n_blocks = -(-(n_assign + N_EXPERTS * (bm - 1)) // bm)
    route, counts = _router(h, g, w_router)
    counts = counts[0, :N_EXPERTS].astype(jnp.int32)
    padded = (counts + bm - 1) // bm * bm
    pad_end = jnp.cumsum(padded)
    top_e = route[:, :TOP_K].astype(jnp.int32)
    dest = (pad_end - padded)[top_e] + route[:, 4:4 + TOP_K].astype(jnp.int32)
    src_tok = (jnp.arange(n_blocks * bm, dtype=jnp.int32) % t).at[dest.reshape(-1)].set(
        jnp.arange(n_assign, dtype=jnp.int32) // TOP_K)
    block_e = jnp.minimum(jnp.searchsorted(pad_end, jnp.arange(n_blocks) * bm, side='right'),
                          N_EXPERTS - 1).astype(jnp.int32)
    seg_end = (pad_end - padded + counts)[block_e]
    n_valid = jnp.clip(seg_end - jnp.arange(n_blocks) * bm, 0, bm).astype(jnp.int32)
    y_rows = _ffn(h[src_tok], g, wg, wu, wd, block_e, n_valid, bm=bm, tf=tf, dense=False)
    return y_rows[dest[:, 0]], y_rows[dest[:, 1]], route


def kernel(x, p, positions, norm_mix_g, w_in, rwkv_mu, rwkv_w0, rwkv_w_up, rwkv_a0, rwkv_a_up, rwkv_g_up, rwkv_k_k, rwkv_k_a, rwkv_r_k, rwkv_ln_g, rwkv_ln_b, vres_mu, vres_down, vres_up, vres_b, mla_q_norm_g, mla_kv_norm_g, mla_w_uq, mla_w_ukv, mla_qk_norm_q, mla_qk_norm_k, gdn_conv_w, gdn_a_log, gdn_dt_bias, gdn_norm_g, w_br_rwkv, w_br_mla, w_br_gdn, w_out, norm_ffn_g, ffn_wg, ffn_wu, ffn_wd, moe_router, moe_wg, moe_wu, moe_wd, ple_proj, ple_gate, ple_norm_g):
    b, s, d = x.shape
    t = b * s
    depth = w_in.shape[0]
    rope_tabs = _rope_tables(positions)

    h = x.reshape(t, d)
    v_first = None
    for i in range(depth):
        w = w_in[i]
        g = norm_mix_g[i]
        w_rwkv = w[:, :RWKV_IN]
        if i > 0:
            mu_v = vres_mu[i - 1][:, None]
            vd = vres_down[i - 1]
            w_rwkv = _pad_cols(_cat([w_rwkv, (1.0 - mu_v) * vd, mu_v * vd], 1), RWKV_IN + LANES)
        w_gdn = w[:, RWKV_IN + MLA_IN:RWKV_IN + MLA_IN + GDN_IN]
        w_gdn = _pad_cols(_cat([w_gdn[:, :GDN_QKV], w_gdn[:, GDN_QKV + 2 * N_HEADS:],
                                w_gdn[:, GDN_QKV:GDN_QKV + 2 * N_HEADS]], 1), GDN_QKV + WIDTH + LANES)
        if i == 0:
            u = _norm(h, g)
        prm = jnp.stack([rwkv_w0[i], rwkv_a0[i], rwkv_k_k[i], rwkv_k_a[i], rwkv_ln_g[i], rwkv_ln_b[i],
                         rwkv_r_k[i].reshape(-1), vres_b[i - 1] if i > 0 else jnp.zeros((WIDTH,), F32)])
        o_a, v_first = _rwkv(u, w_rwkv, v_first, b, s, rwkv_mu[i], prm, rwkv_w_up[i], rwkv_a_up[i], rwkv_g_up[i],
                             vres_up[i - 1] if i > 0 else None)
        q, k, v = _mla_prep(u, w[:, RWKV_IN:RWKV_IN + MLA_IN], rope_tabs, b, s, mla_q_norm_g[i], mla_kv_norm_g[i],
                            mla_w_uq[i], mla_w_ukv[i], mla_qk_norm_q[i], mla_qk_norm_k[i])
        o_b = _flash_attention(q, k, v)
        o_c = _gdn(u, w_gdn, b, s, gdn_conv_w[i], gdn_a_log[i], gdn_dt_bias[i], gdn_norm_g[i])
        h = _merge(h, u, o_a, o_b, o_c, w[:, RWKV_IN + MLA_IN + GDN_IN:], w_br_rwkv[i], w_br_mla[i], w_br_gdn[i],
                   w_out[i], s)
        j = i // 2
        if i % 2 == 0:
            bm = min(1024, t)
            h = _ffn(h, norm_ffn_g[i], ffn_wg[j:j + 1], ffn_wu[j:j + 1], ffn_wd[j:j + 1],
                     jnp.zeros((t // bm,), jnp.int32), jnp.full((t // bm,), bm, jnp.int32), bm=bm, tf=256, dense=True)
            expert_out = None
        else:
            expert_out = _moe(h, norm_ffn_g[i], moe_router[j], moe_wg[j], moe_wu[j], moe_wd[j])
        h, u = _ple(h, p[i].reshape(t, -1), ple_gate[i], ple_proj[i], ple_norm_g[i], expert_out,
                    norm_mix_g[i + 1] if i + 1 < depth else None)
    return h.reshape(b, s, d)
```

```python
import functools
import math

import jax
import jax.numpy as jnp
import numpy as np
from jax import lax
from jax.experimental import pallas as pl
from jax.experimental.pallas import tpu as pltpu

F32 = jnp.float32
BF16 = jnp.bfloat16

NORM_EPS = 1e-6
RWKV_GN_EPS = 64e-5
L2_EPS = 1e-12
N_HEADS = 8
HEAD_DIM = 64
WIDTH = N_HEADS * HEAD_DIM
RWKV_LORA = 256
RWKV_IN = 3 * WIDTH + RWKV_LORA
VRES_LORA = 32
QK_NOPE = 64
QK_ROPE = 32
QK_DIM = QK_NOPE + QK_ROPE
Q_LORA = 256
KV_LORA = 128
MLA_IN = Q_LORA + KV_LORA + QK_ROPE
ROPE_THETA = 10000.0
GDN_CONV = 4
GDN_QKV = 3 * WIDTH
GDN_IN = GDN_QKV + 2 * N_HEADS + WIDTH
N_EXPERTS = 8
TOP_K = 2
LANES = 128
SUBLANES = 8
CHUNK = 64
SCAN_ROWS = 8 * CHUNK
VMEM_LIMIT = 48 * 1024 * 1024


def _cparams(sem):
    return pltpu.CompilerParams(dimension_semantics=sem, vmem_limit_bytes=VMEM_LIMIT)


def _norm_kernel(x_ref, g_ref, u_ref):
    x = x_ref[...]
    u_ref[...] = (x * lax.rsqrt(jnp.mean(x * x, axis=-1, keepdims=True) + NORM_EPS) * g_ref[...]).astype(u_ref.dtype)


def _norm(x, g, *, bm=1024):
    m, d = x.shape
    bm = min(bm, m)
    return pl.pallas_call(
        _norm_kernel,
        grid=(m // bm,),
        in_specs=[pl.BlockSpec((bm, d), lambda i: (i, 0)), pl.BlockSpec((1, d), lambda i: (0, 0))],
        out_specs=pl.BlockSpec((bm, d), lambda i: (i, 0)),
        out_shape=jax.ShapeDtypeStruct((m, d), BF16),
        compiler_params=_cparams(("arbitrary",)),
        name="norm",
    )(x, g.reshape(1, d))


def _router_kernel(x_ref, g_ref, wr_ref, before_ref, route_ref, counts_ref, carry_ref):
    @pl.when(pl.program_id(0) == 0)
    def _():
        carry_ref[...] = jnp.zeros_like(carry_ref)

    x = x_ref[...]
    rows = x.shape[0]
    u = x * lax.rsqrt(jnp.mean(x * x, axis=-1, keepdims=True) + NORM_EPS) * g_ref[...]
    lane = lax.broadcasted_iota(jnp.int32, (rows, LANES), 1)
    u_hi = u.astype(BF16)
    u_lo = (u - u_hi.astype(F32)).astype(BF16)
    hi_part = jnp.dot(u_hi, wr_ref[...], preferred_element_type=F32)
    logits = hi_part[:, :LANES] + hi_part[:, LANES:] + jnp.dot(u_lo, wr_ref[:, :LANES], preferred_element_type=F32)
    logits = jnp.where(lane < N_EXPERTS, logits, -jnp.inf)
    m1 = jnp.max(logits, axis=-1, keepdims=True)
    i1 = jnp.min(jnp.where(logits == m1, lane, LANES), axis=-1, keepdims=True)
    rest = jnp.where(lane == i1, -jnp.inf, logits)
    m2 = jnp.max(rest, axis=-1, keepdims=True)
    i2 = jnp.min(jnp.where(rest == m2, lane, LANES), axis=-1, keepdims=True)
    e2 = jnp.exp(m2 - m1)
    w1 = 1.0 / (1.0 + e2)
    w2 = e2 / (1.0 + e2)
    hit1 = lane == i1
    hit2 = lane == i2
    onehot = jnp.where(hit1 | hit2, 1.0, 0.0)
    seen = jnp.dot(before_ref[...], onehot.astype(BF16), preferred_element_type=F32) + carry_ref[0:1, :]
    r1 = jnp.sum(jnp.where(hit1, seen, 0.0), axis=-1, keepdims=True)
    r2 = jnp.sum(jnp.where(hit2, seen, 0.0), axis=-1, keepdims=True)
    carry_ref[...] = carry_ref[...] + jnp.sum(onehot, axis=0, keepdims=True)
    cols = (i1.astype(F32), i2.astype(F32), w1, w2, r1, r2)
    route = jnp.zeros((rows, LANES), F32)
    for n, c in enumerate(cols):
        route = jnp.where(lane == n, c, route)
    route_ref[...] = route
    counts_ref[...] = carry_ref[...]


def _router(h, g, w_router, *, bm=1024):
    m, d = h.shape
    bm = min(bm, m)
    w = _pad_cols(w_router, LANES)
    w_hi = w.astype(BF16)
    w_split = _cat([w_hi, (w - w_hi.astype(F32)).astype(BF16)], 1)
    return pl.pallas_call(
        _router_kernel,
        grid=(m // bm,),
        in_specs=[pl.BlockSpec((bm, d), lambda i: (i, 0)), pl.BlockSpec((1, d), lambda i: (0, 0)),
                  pl.BlockSpec((d, 2 * LANES), lambda i: (0, 0)), pl.BlockSpec((bm, bm), lambda i: (0, 0))],
        out_specs=[pl.BlockSpec((bm, LANES), lambda i: (i, 0)), pl.BlockSpec((SUBLANES, LANES), lambda i: (0, 0))],
        out_shape=[jax.ShapeDtypeStruct((m, LANES), F32), jax.ShapeDtypeStruct((SUBLANES, LANES), F32)],
        scratch_shapes=[pltpu.VMEM((SUBLANES, LANES), F32)],
        compiler_params=_cparams(("arbitrary",)),
        name="router",
    )(h, g.reshape(1, d), w_split, jnp.tril(jnp.ones((bm, bm), BF16), -1))


def _rms(x, n, g):
    return x * lax.rsqrt(jnp.sum(x * x, axis=-1, keepdims=True) * (1.0 / n) + NORM_EPS) * g


def _mla_prep_kernel(u_ref, cos_ref, sin_ref, win_ref, gq_ref, gkv_ref, gqh_ref, gkh_ref, wq_ref, wqr_ref, wk_ref,
                     wv_ref, q_ref, k_ref, v_ref):
    z = jnp.dot(u_ref[...], win_ref[...], preferred_element_type=F32)
    cq = _rms(z[:, :Q_LORA], Q_LORA, gq_ref[...]).astype(BF16)
    ckv = _rms(z[:, Q_LORA:Q_LORA + KV_LORA], KV_LORA, gkv_ref[...]).astype(BF16)
    k_pe = z[:, Q_LORA + KV_LORA:Q_LORA + KV_LORA + LANES]
    k_pe_rot = z[:, Q_LORA + KV_LORA + LANES:]
    q_all = jnp.dot(cq, wq_ref[...], preferred_element_type=F32)
    q_rot = jnp.dot(cq, wqr_ref[...], preferred_element_type=F32)
    k_all = jnp.dot(ckv, wk_ref[...], preferred_element_type=F32)
    v_all = jnp.dot(ckv, wv_ref[...], preferred_element_type=F32)
    sin = sin_ref[...]
    cos_q = cos_ref[...] * gqh_ref[...]
    cos_k = cos_ref[...] * gkh_ref[...]

    def scale(x):
        return lax.rsqrt(jnp.sum(x * x, axis=-1, keepdims=True) * (1.0 / QK_DIM) + NORM_EPS)

    for h in range(N_HEADS):
        sl = slice(h * LANES, (h + 1) * LANES)
        q = q_all[:, sl]
        k = k_all[:, sl] + k_pe
        q_ref[h] = (scale(q) * (QK_DIM ** -0.5) * (q * cos_q + q_rot[:, sl] * sin)).astype(BF16)
        k_ref[h] = (scale(k) * (k * cos_k + k_pe_rot * sin)).astype(BF16)
        v_ref[h] = v_all[:, sl].astype(BF16)


def _pad_heads(w, width):
    k = w.shape[0]
    w = w.reshape(k, N_HEADS, width)
    return jnp.pad(w, ((0, 0), (0, 0), (0, LANES - width))).reshape(k, N_HEADS * LANES)


def _mla_prep(u, w_in, rope_tabs, b, s, q_norm_g, kv_norm_g, w_uq, w_ukv, qk_g_q, qk_g_k, *, bm=512):
    bm = min(bm, s)
    half = QK_ROPE // 2
    swap = lambda w: _cat([w[..., half:], w[..., :half]], -1)
    rope_slab = lambda w: jnp.pad(w, ((0, 0), (QK_NOPE, LANES - QK_DIM)))
    lora = Q_LORA + KV_LORA
    w_in = _cat([w_in[:, :lora], rope_slab(w_in[:, lora:]), rope_slab(swap(w_in[:, lora:] * qk_g_k[QK_NOPE:]))],
                1).astype(BF16)
    w_ukv = w_ukv.reshape(KV_LORA, N_HEADS, QK_NOPE + HEAD_DIM)
    w_uq3 = w_uq.reshape(Q_LORA, N_HEADS, QK_DIM)
    wq = _pad_heads(w_uq, QK_DIM).astype(BF16)
    wq_rot = jnp.pad(swap(w_uq3[..., QK_NOPE:] * qk_g_q[QK_NOPE:]), ((0, 0), (0, 0), (QK_NOPE, LANES - QK_DIM)))
    wq_rot = wq_rot.reshape(Q_LORA, N_HEADS * LANES).astype(BF16)
    wk = _pad_heads(w_ukv[..., :QK_NOPE].reshape(KV_LORA, -1), QK_NOPE).astype(BF16)
    wv = _pad_heads(w_ukv[..., QK_NOPE:].reshape(KV_LORA, -1), HEAD_DIM).astype(BF16)
    pad_g = lambda g: jnp.pad(g, (0, LANES - g.shape[0])).reshape(1, LANES)
    row = lambda width: pl.BlockSpec((bm, width), lambda bb, i: (bb * (s // bm) + i, 0))
    full = lambda a: pl.BlockSpec(a.shape, lambda bb, i: (0, 0))
    consts = [w_in, q_norm_g.reshape(1, -1), kv_norm_g.reshape(1, -1), pad_g(qk_g_q), pad_g(qk_g_k), wq, wq_rot,
              wk, wv]
    out_spec = pl.BlockSpec((None, N_HEADS, bm, LANES), lambda bb, i: (bb, 0, i, 0))
    out_shape = jax.ShapeDtypeStruct((b, N_HEADS, s, LANES), BF16)
    return pl.pallas_call(
        _mla_prep_kernel,
        grid=(b, s // bm),
        in_specs=[row(u.shape[1]), row(LANES), row(LANES)] + [full(a) for a in consts],
        out_specs=[out_spec] * 3,
        out_shape=[out_shape] * 3,
        compiler_params=_cparams(("arbitrary", "arbitrary")),
        name="mla_prep",
    )(u, *rope_tabs, *consts)


def _flash_kernel(qi_ref, kj_ref, q_ref, k_ref, v_ref, o_ref, m_ref, l_ref, acc_ref, *, blk, sub):
    n = pl.program_id(2)
    i = qi_ref[n]
    j = kj_ref[n]
    hb = q_ref.shape[0]

    @pl.when(j == 0)
    def _():
        m_ref[...] = jnp.full_like(m_ref, -jnp.inf)
        l_ref[...] = jnp.zeros_like(l_ref)
        acc_ref[...] = jnp.zeros_like(acc_ref)

    def update(masked):
        chains = [(h, r) for h in range(hb) for r in range(blk // sub)]

        def qk(h, r):
            nk = (r + 1) * sub if masked else blk
            s = lax.dot_general(q_ref[h, r * sub:(r + 1) * sub, :], k_ref[h, :nk, :], (((1,), (1,)), ((), ())),
                                preferred_element_type=F32)
            if masked:
                row = lax.broadcasted_iota(jnp.int32, (sub, nk), 0) + r * sub
                s = jnp.where(row >= lax.broadcasted_iota(jnp.int32, (sub, nk), 1), s, -jnp.inf)
            return s

        def softmax_pv(h, r, s):
            rows = slice(r * sub, (r + 1) * sub)
            m_old = m_ref[h, rows, :]
            m_new = jnp.maximum(m_old, jnp.max(s, axis=-1, keepdims=True))
            alpha = jnp.exp(m_old - m_new)
            p = jnp.exp(s - m_new[:, :1])
            l_ref[h, rows, :] = alpha * l_ref[h, rows, :] + jnp.sum(p, axis=-1, keepdims=True)
            m_ref[h, rows, :] = m_new
            pv = jnp.dot(p.astype(BF16), v_ref[h, :s.shape[1], :], preferred_element_type=F32)
            acc_ref[h, rows, :] = alpha * acc_ref[h, rows, :] + pv

        s_prev = qk(*chains[0])
        for c in range(len(chains)):
            s_next = qk(*chains[c + 1]) if c + 1 < len(chains) else None
            softmax_pv(*chains[c], s_prev)
            s_prev = s_next

    @pl.when(j < i)
    def _():
        update(False)

    @pl.when(j == i)
    def _():
        update(True)
        o_ref[...] = (acc_ref[...] / l_ref[...]).astype(o_ref.dtype)


def _flash_attention(q, k, v, *, blk=1024, sub=256, hb=4):
    b, h, s, d = q.shape
    blk = min(blk, s)
    sub = min(sub, blk)
    nb = s // blk
    qi = np.array([i for i in range(nb) for j in range(i + 1)], np.int32)
    kj = np.array([j for i in range(nb) for j in range(i + 1)], np.int32)
    q_spec = pl.BlockSpec((None, hb, blk, d), lambda bb, hh, n, qi_r, kj_r: (bb, hh, qi_r[n], 0))
    kv_spec = pl.BlockSpec((None, hb, blk, d), lambda bb, hh, n, qi_r, kj_r: (bb, hh, kj_r[n], 0))
    return pl.pallas_call(
        functools.partial(_flash_kernel, blk=blk, sub=sub),
        grid_spec=pltpu.PrefetchScalarGridSpec(
            num_scalar_prefetch=2,
            grid=(b, h // hb, len(qi)),
            in_specs=[q_spec, kv_spec, kv_spec],
            out_specs=q_spec,
            scratch_shapes=[pltpu.VMEM((hb, blk, LANES), F32), pltpu.VMEM((hb, blk, LANES), F32),
                            pltpu.VMEM((hb, blk, d), F32)],
        ),
        out_shape=jax.ShapeDtypeStruct((b, h, s, d), BF16),
        compiler_params=_cparams(("arbitrary", "arbitrary", "arbitrary")),
        name="flash",
    )(jnp.asarray(qi), jnp.asarray(kj), q, k, v)


def _dot(a, b):
    return jnp.dot(a.astype(BF16), b.astype(BF16), preferred_element_type=F32)


def _dot_nt(a, b):
    return lax.dot_general(a.astype(BF16), b.astype(BF16), (((1,), (1,)), ((), ())), preferred_element_type=F32)


def _dot_tn(a, b):
    return jnp.dot(a.T.astype(BF16), b.astype(BF16), preferred_element_type=F32)


def _split3(x):
    hi = x.astype(BF16)
    r1 = x - hi.astype(F32)
    mid = r1.astype(BF16)
    return hi, mid, (r1 - mid.astype(F32)).astype(BF16)


def _cumsum_rows(tri, x):
    s = jnp.dot(tri, jnp.concatenate(_split3(x), axis=1), preferred_element_type=F32)
    n = x.shape[1]
    return s[:, :n] + s[:, n:2 * n] + s[:, 2 * n:]


def _head_sum(x):
    left = lax.broadcasted_iota(jnp.int32, (x.shape[0], LANES), 1) < HEAD_DIM
    outs = []
    for p in range(WIDTH // LANES):
        xs = x[:, p * LANES:(p + 1) * LANES]
        s_left = jnp.sum(jnp.where(left, xs, 0.0), axis=-1, keepdims=True)
        s_right = jnp.sum(jnp.where(left, 0.0, xs), axis=-1, keepdims=True)
        outs.append(jnp.where(left, s_left, s_right))
    return jnp.concatenate(outs, axis=1)


def _shift_rows(x, tail, j):
    rolled = pltpu.roll(x, j, 0)
    top = jnp.where(lax.broadcasted_iota(jnp.int32, (SUBLANES, x.shape[1]), 0) < j,
                    pltpu.roll(tail, j, 0), rolled[:SUBLANES])
    return jnp.concatenate([top, rolled[SUBLANES:]], axis=0)


def _softplus(x):
    return jnp.maximum(x, 0.0) + jnp.log(1.0 + jnp.exp(-jnp.abs(x)))


def _pair_masks():
    lane = lax.broadcasted_iota(jnp.int32, (CHUNK, LANES), 1)
    m0 = (lane < HEAD_DIM).astype(F32)
    return m0, 1.0 - m0


def _ext(x, m0, m1, dtype=BF16):
    return jnp.concatenate([x * m0, x * m1], axis=0).astype(dtype)


def _cat(xs, axis):
    return jnp.concatenate(xs, axis=axis)


def _neumann_inverse(mats):
    n = mats[0].shape[0]
    eye = (lax.broadcasted_iota(jnp.int32, (n, n), 0) == lax.broadcasted_iota(jnp.int32, (n, n), 1)).astype(F32)
    ts = [eye + a for a in mats]
    ps = [_dot(a, a) for a in mats]
    for _ in range(int(math.log2(CHUNK)) - 2):
        pps = [_dot(_cat([p, t], 0), p) for p, t in zip(ps, ts)]
        ps = [pp[:n] for pp in pps]
        ts = [t + pp[n:] for t, pp in zip(ts, pps)]
    return [t + _dot(t, p) for t, p in zip(ts, ps)]


def _work_items(rows):
    return [(slice(c * CHUNK, (c + 1) * CHUNK), slice(p * LANES, (p + 1) * LANES))
            for c in range(rows // CHUNK) for p in range(WIDTH // LANES)]


def _tiles(x, items):
    return [x[rows, cols] for rows, cols in items]


def _rwkv_scan(r, k, v, a, b, ld, st_ref, y_ref):
    c2 = 2 * CHUNK
    strict = lax.broadcasted_iota(jnp.int32, (c2, c2), 0) > lax.broadcasted_iota(jnp.int32, (c2, c2), 1)
    trow = lax.broadcasted_iota(jnp.int32, (CHUNK, c2), 0)
    tcol = lax.broadcasted_iota(jnp.int32, (CHUNK, c2), 1) & (CHUNK - 1)
    incl_w = tcol <= trow
    tri = (lax.broadcasted_iota(jnp.int32, (CHUNK, CHUNK), 0)
           >= lax.broadcasted_iota(jnp.int32, (CHUNK, CHUNK), 1)).astype(BF16)
    m0, m1 = _pair_masks()
    items = _work_items(r.shape[0])
    n_pairs = WIDTH // LANES
    r, k, v, a, b, ld = (_tiles(x, items) for x in (r, k, v, a, b, ld))
    cum = [_cumsum_rows(tri, x) for x in ld]
    e_pos = [jnp.exp(x) for x in cum]
    e_neg = [jnp.exp(-x) for x in cum]
    e_end = [jnp.exp(x[CHUNK - 1:CHUNK, :] - x) for x in cum]
    a_ext = [_ext(ai * jnp.exp(ci - li), m0, m1) for ai, ci, li in zip(a, cum, ld)]
    v_ext = [_ext(x, m0, m1) for x in v]
    r_dec = [(ri * ei).astype(BF16) for ri, ei in zip(r, e_pos)]
    s1 = [_dot_nt(_cat([ae, rd], 0), _cat([_ext(bi * en, m0, m1), _ext(ki * en, m0, m1)], 0))
          for ae, rd, bi, ki, en in zip(a_ext, r_dec, b, k, e_neg)]
    a_ab = [jnp.where(strict, s[:c2, :c2], 0.0) for s in s1]
    a_ak = [jnp.where(strict, s[:c2, c2:], 0.0) for s in s1]
    a_rbk = [_cat([jnp.where(incl_w, s[c2:, :c2], 0.0), jnp.where(incl_w, s[c2:, c2:], 0.0)], 1).astype(BF16)
             for s in s1]
    akv = [_dot(x, ve) for x, ve in zip(a_ak, v_ext)]
    t_inv = _neumann_inverse(a_ab)
    wu0 = [_dot(t, _cat([ae, av.astype(BF16)], 1)) for t, ae, av in zip(t_inv, a_ext, akv)]
    bk_end = [_cat([_ext(bi * ee, m0, m1, F32), _ext(ki * ee, m0, m1, F32)], 0) for bi, ki, ee in zip(b, k, e_end)]
    e_last_t = [jnp.broadcast_to(e[CHUNK - 1:CHUNK, :], (LANES, LANES)).T for e in e_pos]
    st = [st_ref[p] for p in range(n_pairs)]
    for c in range(len(items) // n_pairs):
        idx = range(c * n_pairs, (c + 1) * n_pairs)
        z = [_dot(_cat([wu0[i][:, :LANES].astype(BF16), r_dec[i]], 0), s) for i, s in zip(idx, st)]
        uv = [_cat([(wu0[i][:, LANES:] + zi[:c2]).astype(BF16), v_ext[i]], 0) for i, zi in zip(idx, z)]
        for i, zi, uvi in zip(idx, z, uv):
            rows, cols = items[i]
            y_ref[rows, cols] = zi[c2:] + _dot(a_rbk[i], uvi)
        st = [s * e_last_t[i] + _dot_tn(bk_end[i], uvi) for i, s, uvi in zip(idx, st, uv)]
    for p in range(n_pairs):
        st_ref[p] = st[p]


def _gdn_scan(q, k, v, g, beta, st_ref, o_ref):
    c2 = 2 * CHUNK
    row = lax.broadcasted_iota(jnp.int32, (c2, c2), 0)
    col = lax.broadcasted_iota(jnp.int32, (c2, c2), 1)
    causal = ((row >= CHUNK) == (col >= CHUNK)) & (row >= col)
    strict = row > col
    tri = (lax.broadcasted_iota(jnp.int32, (CHUNK, CHUNK), 0)
           >= lax.broadcasted_iota(jnp.int32, (CHUNK, CHUNK), 1)).astype(BF16)
    m0, m1 = _pair_masks()
    items = _work_items(q.shape[0])
    n_pairs = WIDTH // LANES
    q, k, v, g, beta = (_tiles(x, items) for x in (q, k, v, g, beta))
    gc = [_cumsum_rows(tri, x) for x in g]
    g_rows = [_cat([jnp.broadcast_to(x[:, 0:1], (CHUNK, LANES)),
                    jnp.broadcast_to(x[:, HEAD_DIM:HEAD_DIM + 1], (CHUNK, LANES))], 0) for x in gc]
    gamma = [jnp.exp(jnp.where(causal, x - x.T, -jnp.inf)) for x in g_rows]
    e_pos = [jnp.exp(x) for x in gc]
    kb = [ki * bi for ki, bi in zip(k, beta)]
    s1 = [_dot_nt(_cat([_ext(kbi, m0, m1), qi.astype(BF16)], 0), _ext(ki, m0, m1)) for kbi, qi, ki in zip(kb, q, k)]
    a_neg = [-jnp.where(strict, s[:c2] * ga, 0.0) for s, ga in zip(s1, gamma)]
    a_in = [(s[c2:] * (ga[:CHUNK] + ga[CHUNK:])).astype(BF16) for s, ga in zip(s1, gamma)]
    t_inv = _neumann_inverse(a_neg)
    uw = [_dot(t, _cat([_ext(vi * bi, m0, m1), _ext(kbi * ei, m0, m1)], 1))
          for t, vi, bi, kbi, ei in zip(t_inv, v, beta, kb, e_pos)]
    q_dec = [(qi * ei).astype(BF16) for qi, ei in zip(q, e_pos)]
    k_dec = [_ext(ki * jnp.exp(x[CHUNK - 1:CHUNK, :] - x), m0, m1, F32) for ki, x in zip(k, gc)]
    st = [st_ref[p] for p in range(n_pairs)]
    for c in range(len(items) // n_pairs):
        idx = range(c * n_pairs, (c + 1) * n_pairs)
        z = [_dot(_cat([uw[i][:, LANES:].astype(BF16), q_dec[i]], 0), s) for i, s in zip(idx, st)]
        v_new = [(uw[i][:, :LANES] - zi[:c2]).astype(BF16) for i, zi in zip(idx, z)]
        for i, zi, vn in zip(idx, z, v_new):
            rows, cols = items[i]
            o_ref[rows, cols] = zi[c2:] + _dot(a_in[i], vn)
        st = [s * e_pos[i][CHUNK - 1:CHUNK, :] + _dot_tn(k_dec[i], vn) for i, s, vn in zip(idx, st, v_new)]
    for p in range(n_pairs):
        st_ref[p] = st[p]


def _rwkv_kernel(*refs, vres):
    if vres:
        (u_ref, vf_ref, win_ref, mu_ref, prm_ref, wup_ref, aup_ref, gup_ref, vup_ref, o_ref, st_ref, tail_ref,
         y_ref) = refs
    else:
        u_ref, win_ref, mu_ref, prm_ref, wup_ref, aup_ref, gup_ref, o_ref, vf_out_ref, st_ref, tail_ref, y_ref = refs

    @pl.when(pl.program_id(1) == 0)
    def _():
        st_ref[...] = jnp.zeros_like(st_ref)
        tail_ref[...] = jnp.zeros_like(tail_ref)

    z = jnp.dot(u_ref[...], win_ref[...], preferred_element_type=F32)
    rows = z.shape[0]
    zs = _shift_rows(z, tail_ref[...], 1)
    tail_ref[...] = z[rows - SUBLANES:]
    zl = z[:, :RWKV_IN] + (zs[:, :RWKV_IN] - z[:, :RWKV_IN]) * mu_ref[...]
    r, k, v = zl[:, :WIDTH], zl[:, WIDTH:2 * WIDTH], zl[:, 2 * WIDTH:3 * WIDTH]
    lo = zl[:, 3 * WIDTH:3 * WIDTH + LANES]
    g_lo = zl[:, 3 * WIDTH + LANES:]
    w0, a0, k_k, k_a, ln_g, ln_b, r_k, v_bias = (prm_ref[i:i + 1, :] for i in range(8))
    log_w = -_softplus(-(w0 + _dot(jnp.tanh(lo), wup_ref[...]))) - 0.5
    ld = -jnp.exp(log_w)
    iclr = jax.nn.sigmoid(a0 + _dot(lo, aup_ref[...]))
    gate = _dot(jax.nn.sigmoid(g_lo), gup_ref[...])
    if vres:
        x = z[:, RWKV_IN:] + pltpu.roll(zs[:, RWKV_IN:], LANES - VRES_LORA, 1)
        v = v + (vf_ref[...] - v) * jax.nn.sigmoid(v_bias + _dot(x, vup_ref[...]))
    else:
        vf_out_ref[...] = v
    kk = k * k_k
    kk = kk * lax.rsqrt(_head_sum(kk * kk) + L2_EPS)
    k = k * (1.0 + (iclr - 1.0) * k_a)
    _rwkv_scan(r, k, v, -kk, kk * iclr, ld, st_ref, y_ref)
    y = y_ref[...]
    d = y - _head_sum(y) * (1.0 / HEAD_DIM)
    y = d * lax.rsqrt(_head_sum(d * d) * (1.0 / HEAD_DIM) + RWKV_GN_EPS) * ln_g + ln_b
    y = y + _head_sum(r * k * r_k) * v
    o_ref[...] = (y * gate).astype(o_ref.dtype)


def _rwkv(u, w_in, v_first, b, s, mu, prm, w_up, a_up, g_up, v_up):
    vres = v_first is not None
    rows = min(SCAN_ROWS, s)
    zw = w_in.shape[1]
    row = lambda width: pl.BlockSpec((rows, width), lambda bb, c: (bb * (s // rows) + c, 0))
    full = lambda a: pl.BlockSpec(a.shape, lambda bb, c: (0, 0))
    zero = jnp.zeros((LANES // 2, WIDTH), F32)
    consts = [w_in.astype(BF16), mu.reshape(1, -1), prm, _cat([w_up, zero], 0).astype(BF16),
              _cat([zero, a_up], 0).astype(BF16), g_up.astype(BF16)]
    args, in_specs = [u], [row(u.shape[1])]
    if vres:
        args.append(v_first)
        in_specs.append(row(WIDTH))
        consts.append(jnp.pad(v_up, ((0, LANES - VRES_LORA), (0, 0))).astype(BF16))
    out_shape = [jax.ShapeDtypeStruct((b * s, WIDTH), BF16)]
    out_specs = [row(WIDTH)]
    if not vres:
        out_shape.append(jax.ShapeDtypeStruct((b * s, WIDTH), F32))
        out_specs.append(row(WIDTH))
    outs = pl.pallas_call(
        functools.partial(_rwkv_kernel, vres=vres),
        grid=(b, s // rows),
        in_specs=in_specs + [full(a) for a in consts],
        out_specs=out_specs,
        out_shape=out_shape,
        scratch_shapes=[pltpu.VMEM((WIDTH // LANES, LANES, LANES), F32), pltpu.VMEM((SUBLANES, zw), F32),
                        pltpu.VMEM((rows, WIDTH), F32)],
        compiler_params=_cparams(("arbitrary", "arbitrary")),
        name="rwkv",
    )(*args, *consts)
    return (outs[0], v_first) if vres else (outs[0], outs[1])


def _gdn_kernel(u_ref, win_ref, cw_ref, prm_ref, exp_ref, o_ref, st_ref, tail_ref, y_ref):
    @pl.when(pl.program_id(1) == 0)
    def _():
        st_ref[...] = jnp.zeros_like(st_ref)
        tail_ref[...] = jnp.zeros_like(tail_ref)

    z = jnp.dot(u_ref[...], win_ref[...], preferred_element_type=F32)
    rows = z.shape[0]
    x = z[:, :GDN_QKV]
    tail = tail_ref[...]
    conv = x * cw_ref[GDN_CONV - 1:GDN_CONV, :]
    for j in range(1, GDN_CONV):
        conv = conv + _shift_rows(x, tail, j) * cw_ref[GDN_CONV - 1 - j:GDN_CONV - j, :]
    tail_ref[...] = x[rows - SUBLANES:]
    qkv = conv * jax.nn.sigmoid(conv)
    q, k, v = qkv[:, :WIDTH], qkv[:, WIDTH:2 * WIDTH], qkv[:, 2 * WIDTH:]
    q = q * lax.rsqrt(_head_sum(q * q) + L2_EPS) * (HEAD_DIM ** -0.5)
    k = k * lax.rsqrt(_head_sum(k * k) + L2_EPS)
    logits = jnp.dot(_cat(_split3(z[:, GDN_QKV + WIDTH:]), 0), exp_ref[...], preferred_element_type=F32)
    logits = logits[:rows] + logits[rows:2 * rows] + logits[2 * rows:]
    neg_a, dt_bias, norm_g = (prm_ref[i:i + 1, :] for i in range(3))
    beta = jax.nn.sigmoid(logits[:, :WIDTH])
    g = neg_a * _softplus(logits[:, WIDTH:] + dt_bias)
    _gdn_scan(q, k, v, g, beta, st_ref, y_ref)
    o = y_ref[...]
    o = o * lax.rsqrt(_head_sum(o * o) * (1.0 / HEAD_DIM) + NORM_EPS) * norm_g
    gate = z[:, GDN_QKV:GDN_QKV + WIDTH]
    o_ref[...] = (o * (gate * jax.nn.sigmoid(gate))).astype(o_ref.dtype)


def _gdn(u, w_in, b, s, conv_w, a_log, dt_bias, norm_g):
    rows = min(SCAN_ROWS, s)
    row = lambda width: pl.BlockSpec((rows, width), lambda bb, c: (bb * (s // rows) + c, 0))
    full = lambda a: pl.BlockSpec(a.shape, lambda bb, c: (0, 0))
    per_lane = lambda t: jnp.repeat(t, HEAD_DIM)
    prm = jnp.stack([per_lane(-jnp.exp(a_log)), per_lane(dt_bias), jnp.tile(norm_g, N_HEADS)])
    head_of_lane = np.arange(WIDTH) // HEAD_DIM
    expand = np.zeros((LANES, 2 * WIDTH), np.float32)
    expand[head_of_lane, np.arange(WIDTH)] = 1.0
    expand[N_HEADS + head_of_lane, WIDTH + np.arange(WIDTH)] = 1.0
    consts = [w_in.astype(BF16), conv_w, prm, jnp.asarray(expand, BF16)]
    return pl.pallas_call(
        _gdn_kernel,
        grid=(b, s // rows),
        in_specs=[row(u.shape[1])] + [full(a) for a in consts],
        out_specs=row(WIDTH),
        out_shape=jax.ShapeDtypeStruct((b * s, WIDTH), BF16),
        scratch_shapes=[pltpu.VMEM((WIDTH // LANES, LANES, LANES), F32), pltpu.VMEM((SUBLANES, GDN_QKV), F32),
                        pltpu.VMEM((rows, WIDTH), F32)],
        compiler_params=_cparams(("arbitrary", "arbitrary")),
        name="gdn",
    )(u, *consts)


def _merge_kernel(h_ref, u_ref, oa_ref, ob_ref, oc_ref, wg_ref, wa_ref, wb_ref, wc_ref, wo_ref, out_ref):
    d = h_ref.shape[1]
    u = u_ref[...]
    ob = jnp.concatenate([ob_ref[h] for h in range(N_HEADS)], axis=1)
    merged = jnp.zeros(h_ref.shape, F32)
    for n, (o, w_ref) in enumerate(((oa_ref[...], wa_ref), (ob, wb_ref), (oc_ref[...], wc_ref))):
        proj = jnp.dot(o, w_ref[...], preferred_element_type=F32)
        zg = jnp.dot(u, wg_ref[:, n * d:(n + 1) * d], preferred_element_type=F32)
        merged = merged + jax.nn.sigmoid(zg) * proj
    out_ref[...] = h_ref[...] + jnp.dot(merged.astype(BF16), wo_ref[...], preferred_element_type=F32)


def _merge(h, u, oa, ob, oc, wg, wa, wb, wc, wo, s, *, bm=512):
    m, d = h.shape
    bm = min(bm, s)
    row = lambda width: pl.BlockSpec((bm, width), lambda i: (i, 0))
    full = lambda w: pl.BlockSpec(w.shape, lambda i: (0, 0))
    wb = jnp.pad(wb.reshape(N_HEADS, HEAD_DIM, d), ((0, 0), (0, LANES - HEAD_DIM), (0, 0))).reshape(N_HEADS * LANES, d)
    ws = [w.astype(BF16) for w in (wg, wa, wb, wc, wo)]
    ob_spec = pl.BlockSpec((None, N_HEADS, bm, LANES), lambda i: (i // (s // bm), 0, i % (s // bm), 0))
    return pl.pallas_call(
        _merge_kernel,
        grid=(m // bm,),
        in_specs=[row(d), row(d), row(WIDTH), ob_spec, row(WIDTH)] + [full(w) for w in ws],
        out_specs=row(d),
        out_shape=jax.ShapeDtypeStruct((m, d), F32),
        compiler_params=_cparams(("arbitrary",)),
        name="merge",
    )(h, u, oa, ob, oc, *ws)


def _ffn_kernel(be_ref, nu_ref, x_ref, g_ref, wg_ref, wu_ref, wd_ref, o_ref, xs_ref, acc_ref, *, nf, dense):
    i = pl.program_id(0)
    f = pl.program_id(1)
    used = i < nu_ref[0]

    @pl.when(used & (f == 0))
    def _():
        x = x_ref[...]
        ms = jnp.mean(x * x, axis=-1, keepdims=True)
        xs_ref[...] = (x * lax.rsqrt(ms + NORM_EPS) * g_ref[...]).astype(BF16)

    @pl.when(f == 0)
    def _():
        acc_ref[...] = jnp.zeros_like(acc_ref)

    @pl.when(used)
    def _():
        xs = xs_ref[...]
        gate = jnp.dot(xs, wg_ref[...].astype(BF16), preferred_element_type=F32)
        up = jnp.dot(xs, wu_ref[...].astype(BF16), preferred_element_type=F32)
        act = (gate * jax.nn.sigmoid(gate) * up).astype(BF16)
        acc_ref[...] += jnp.dot(act, wd_ref[...].astype(BF16), preferred_element_type=F32)

    @pl.when(f == nf - 1)
    def _():
        o_ref[...] = (x_ref[...] + acc_ref[...]) if dense else acc_ref[...]


def _ffn(x, g, wg, wu, wd, block_e, n_used, *, bm, tf, dense):
    r, d = x.shape
    ff = wg.shape[2]
    nf = ff // tf

    def ff_idx(i, f, be, nu):
        return jnp.where(i < nu[0], f, nf - 1)

    return pl.pallas_call(
        functools.partial(_ffn_kernel, nf=nf, dense=dense),
        grid_spec=pltpu.PrefetchScalarGridSpec(
            num_scalar_prefetch=2,
            grid=(r // bm, nf),
            in_specs=[
                pl.BlockSpec((bm, d), lambda i, f, be, nu: (i, 0)),
                pl.BlockSpec((1, d), lambda i, f, be, nu: (0, 0)),
                pl.BlockSpec((None, d, tf), lambda i, f, be, nu: (be[i], 0, ff_idx(i, f, be, nu))),
                pl.BlockSpec((None, d, tf), lambda i, f, be, nu: (be[i], 0, ff_idx(i, f, be, nu))),
                pl.BlockSpec((None, tf, d), lambda i, f, be, nu: (be[i], ff_idx(i, f, be, nu), 0)),
            ],
            out_specs=pl.BlockSpec((bm, d), lambda i, f, be, nu: (i, 0)),
            scratch_shapes=[pltpu.VMEM((bm, d), BF16), pltpu.VMEM((bm, d), F32)],
        ),
        out_shape=jax.ShapeDtypeStruct((r, d), F32),
        compiler_params=_cparams(("arbitrary", "arbitrary")),
        name="ffn",
    )(block_e, n_used, x, g.reshape(1, d).astype(F32), wg, wu, wd)


def _ple_kernel(*refs, combine, emit_norm):
    refs = list(refs)
    h = refs.pop(0)[...]
    if combine:
        y0_ref, y1_ref, route_ref = refs[:3]
        del refs[:3]
        h = h + route_ref[:, 2:3] * y0_ref[...] + route_ref[:, 3:4] * y1_ref[...]
    p_ref, wg_ref, wp_ref, g_ref = refs[:4]
    gate = jnp.dot(h.astype(BF16), wg_ref[...], preferred_element_type=F32)
    e = jnp.dot(p_ref[...].astype(BF16), wp_ref[...], preferred_element_type=F32)
    ms = jnp.mean(e * e, axis=-1, keepdims=True)
    e = e * lax.rsqrt(ms + NORM_EPS) * g_ref[...]
    out = h + jax.nn.sigmoid(gate) * e
    if emit_norm:
        gn_ref, o_ref, u_ref = refs[4:]
        u_ref[...] = (out * lax.rsqrt(jnp.mean(out * out, axis=-1, keepdims=True) + NORM_EPS) * gn_ref[...]).astype(BF16)
    else:
        o_ref, = refs[4:]
    o_ref[...] = out


def _ple(h, p, w_gate, w_proj, g, expert_out=None, g_next=None, *, bm=512):
    m, d = h.shape
    bm = min(bm, m)
    row = lambda a: pl.BlockSpec((bm, a.shape[1]), lambda i: (i, 0))
    full = lambda a: pl.BlockSpec(a.shape, lambda i: (0, 0))
    rows = [h] + list(expert_out or ()) + [p]
    consts = [w_gate.astype(BF16), w_proj.astype(BF16), g.reshape(1, d)]
    out_shape = [jax.ShapeDtypeStruct((m, d), F32)]
    if g_next is not None:
        consts.append(g_next.reshape(1, d))
        out_shape.append(jax.ShapeDtypeStruct((m, d), BF16))
    outs = pl.pallas_call(
        functools.partial(_ple_kernel, combine=expert_out is not None, emit_norm=g_next is not None),
        grid=(m // bm,),
        in_specs=[row(a) for a in rows] + [full(a) for a in consts],
        out_specs=[row(h)] * len(out_shape),
        out_shape=out_shape,
        compiler_params=_cparams(("arbitrary",)),
        name="ple",
    )(*rows, *consts)
    return outs if g_next is not None else (outs[0], None)


def _pad_cols(w, n):
    return jnp.pad(w, ((0, 0), (0, n - w.shape[1])))


def _rope_tables(positions):
    t = positions.size
    inv_freq = 1.0 / (ROPE_THETA ** (jnp.arange(0, QK_ROPE, 2, dtype=F32) / QK_ROPE))
    ang = positions.astype(F32).reshape(t, 1) * inv_freq
    cos, sin = jnp.cos(ang), jnp.sin(ang)
    pad = ((0, 0), (QK_NOPE, LANES - QK_DIM))
    return jnp.pad(_cat([cos, cos], 1), pad, constant_values=1.0), jnp.pad(_cat([-sin, sin], 1), pad)


def _moe(h, g, w_router, wg, wu, wd, *, bm=768, tf=512):
    t, d = h.shape
    n_assign = t * TOP_K
    n_blocks = -(-(n_assign + N_EXPERTS * (bm - 1)) // bm)
    route, counts = _router(h, g, w_router)
    counts = counts[0, :N_EXPERTS].astype(jnp.int32)
    padded = (counts + bm - 1) // bm * bm
    pad_end = jnp.cumsum(padded)
    top_e = route[:, :TOP_K].astype(jnp.int32)
    dest = (pad_end - padded)[top_e] + route[:, 4:4 + TOP_K].astype(jnp.int32)
    src_tok = (jnp.arange(n_blocks * bm, dtype=jnp.int32) % t).at[dest.reshape(-1)].set(
        jnp.arange(n_assign, dtype=jnp.int32) // TOP_K)
    block_start = jnp.arange(n_blocks, dtype=jnp.int32) * bm
    block_e = jnp.minimum(jnp.sum(pad_end[None, :] <= block_start[:, None], axis=1), N_EXPERTS - 1).astype(jnp.int32)
    n_used = (pad_end[-1] // bm).astype(jnp.int32).reshape(1)
    y_rows = _ffn(h[src_tok], g, wg, wu, wd, block_e, n_used, bm=bm, tf=tf, dense=False)
    return y_rows[dest[:, 0]], y_rows[dest[:, 1]], route


def kernel(x, p, positions, norm_mix_g, w_in, rwkv_mu, rwkv_w0, rwkv_w_up, rwkv_a0, rwkv_a_up, rwkv_g_up, rwkv_k_k, rwkv_k_a, rwkv_r_k, rwkv_ln_g, rwkv_ln_b, vres_mu, vres_down, vres_up, vres_b, mla_q_norm_g, mla_kv_norm_g, mla_w_uq, mla_w_ukv, mla_qk_norm_q, mla_qk_norm_k, gdn_conv_w, gdn_a_log, gdn_dt_bias, gdn_norm_g, w_br_rwkv, w_br_mla, w_br_gdn, w_out, norm_ffn_g, ffn_wg, ffn_wu, ffn_wd, moe_router, moe_wg, moe_wu, moe_wd, ple_proj, ple_gate, ple_norm_g):
    b, s, d = x.shape
    t = b * s
    depth = w_in.shape[0]
    rope_tabs = _rope_tables(positions)

    h = x.reshape(t, d)
    v_first = None
    for i in range(depth):
        w = w_in[i]
        g = norm_mix_g[i]
        w_rwkv = w[:, :RWKV_IN]
        if i > 0:
            mu_v = vres_mu[i - 1][:, None]
            vd = vres_down[i - 1]
            w_rwkv = _pad_cols(_cat([w_rwkv, (1.0 - mu_v) * vd, mu_v * vd], 1), RWKV_IN + LANES)
        w_gdn = w[:, RWKV_IN + MLA_IN:RWKV_IN + MLA_IN + GDN_IN]
        w_gdn = _pad_cols(_cat([w_gdn[:, :GDN_QKV], w_gdn[:, GDN_QKV + 2 * N_HEADS:],
                                w_gdn[:, GDN_QKV:GDN_QKV + 2 * N_HEADS]], 1), GDN_QKV + WIDTH + LANES)
        if i == 0:
            u = _norm(h, g)
        prm = jnp.stack([rwkv_w0[i], rwkv_a0[i], rwkv_k_k[i], rwkv_k_a[i], rwkv_ln_g[i], rwkv_ln_b[i],
                         rwkv_r_k[i].reshape(-1), vres_b[i - 1] if i > 0 else jnp.zeros((WIDTH,), F32)])
        o_a, v_first = _rwkv(u, w_rwkv, v_first, b, s, rwkv_mu[i], prm, rwkv_w_up[i], rwkv_a_up[i], rwkv_g_up[i],
                             vres_up[i - 1] if i > 0 else None)
        q, k, v = _mla_prep(u, w[:, RWKV_IN:RWKV_IN + MLA_IN], rope_tabs, b, s, mla_q_norm_g[i], mla_kv_norm_g[i],
                            mla_w_uq[i], mla_w_ukv[i], mla_qk_norm_q[i], mla_qk_norm_k[i])
        o_b = _flash_attention(q, k, v)
        o_c = _gdn(u, w_gdn, b, s, gdn_conv_w[i], gdn_a_log[i], gdn_dt_bias[i], gdn_norm_g[i])
        h = _merge(h, u, o_a, o_b, o_c, w[:, RWKV_IN + MLA_IN + GDN_IN:], w_br_rwkv[i], w_br_mla[i], w_br_gdn[i],
                   w_out[i], s)
        j = i // 2
        if i % 2 == 0:
            bm = min(1024, t)
            h = _ffn(h, norm_ffn_g[i], ffn_wg[j:j + 1], ffn_wu[j:j + 1], ffn_wd[j:j + 1],
                     jnp.zeros((t // bm,), jnp.int32), jnp.full((1,), t // bm, jnp.int32), bm=bm, tf=256, dense=True)
            expert_out = None
        else:
            expert_out = _moe(h, norm_ffn_g[i], moe_router[j], moe_wg[j], moe_wu[j], moe_wd[j])
        h, u = _ple(h, p[i].reshape(t, -1), ple_gate[i], ple_proj[i], ple_norm_g[i], expert_out,
                    norm_mix_g[i + 1] if i + 1 < depth else None)
    return h.reshape(b, s, d)
```

```python
import functools
import math

import jax
import jax.numpy as jnp
import numpy as np
from jax import lax
from jax.experimental import pallas as pl
from jax.experimental.pallas import tpu as pltpu

F32 = jnp.float32
BF16 = jnp.bfloat16

NORM_EPS = 1e-6
RWKV_GN_EPS = 64e-5
L2_EPS = 1e-12
N_HEADS = 8
HEAD_DIM = 64
WIDTH = N_HEADS * HEAD_DIM
RWKV_LORA = 256
RWKV_IN = 3 * WIDTH + RWKV_LORA
VRES_LORA = 32
QK_NOPE = 64
QK_ROPE = 32
QK_DIM = QK_NOPE + QK_ROPE
Q_LORA = 256
KV_LORA = 128
MLA_IN = Q_LORA + KV_LORA + QK_ROPE
ROPE_THETA = 10000.0
GDN_CONV = 4
GDN_QKV = 3 * WIDTH
GDN_IN = GDN_QKV + 2 * N_HEADS + WIDTH
N_EXPERTS = 8
TOP_K = 2
LANES = 128
SUBLANES = 8
CHUNK = 64
SCAN_ROWS = 8 * CHUNK
VMEM_LIMIT = 48 * 1024 * 1024


def _cparams(sem):
    return pltpu.CompilerParams(dimension_semantics=sem, vmem_limit_bytes=VMEM_LIMIT)


def _norm_kernel(x_ref, g_ref, u_ref):
    x = x_ref[...]
    u_ref[...] = (x * lax.rsqrt(jnp.mean(x * x, axis=-1, keepdims=True) + NORM_EPS) * g_ref[...]).astype(u_ref.dtype)


def _norm(x, g, *, bm=1024):
    m, d = x.shape
    bm = min(bm, m)
    return pl.pallas_call(
        _norm_kernel,
        grid=(m // bm,),
        in_specs=[pl.BlockSpec((bm, d), lambda i: (i, 0)), pl.BlockSpec((1, d), lambda i: (0, 0))],
        out_specs=pl.BlockSpec((bm, d), lambda i: (i, 0)),
        out_shape=jax.ShapeDtypeStruct((m, d), BF16),
        compiler_params=_cparams(("arbitrary",)),
        name="norm",
    )(x, g.reshape(1, d))


def _router_kernel(x_ref, g_ref, wr_ref, before_ref, route_ref, counts_ref, carry_ref):
    @pl.when(pl.program_id(0) == 0)
    def _():
        carry_ref[...] = jnp.zeros_like(carry_ref)

    x = x_ref[...]
    rows = x.shape[0]
    u = x * lax.rsqrt(jnp.mean(x * x, axis=-1, keepdims=True) + NORM_EPS) * g_ref[...]
    lane = lax.broadcasted_iota(jnp.int32, (rows, LANES), 1)
    u_hi = u.astype(BF16)
    u_lo = (u - u_hi.astype(F32)).astype(BF16)
    hi_part = jnp.dot(u_hi, wr_ref[...], preferred_element_type=F32)
    logits = hi_part[:, :LANES] + hi_part[:, LANES:] + jnp.dot(u_lo, wr_ref[:, :LANES], preferred_element_type=F32)
    logits = jnp.where(lane < N_EXPERTS, logits, -jnp.inf)
    m1 = jnp.max(logits, axis=-1, keepdims=True)
    i1 = jnp.min(jnp.where(logits == m1, lane, LANES), axis=-1, keepdims=True)
    rest = jnp.where(lane == i1, -jnp.inf, logits)
    m2 = jnp.max(rest, axis=-1, keepdims=True)
    i2 = jnp.min(jnp.where(rest == m2, lane, LANES), axis=-1, keepdims=True)
    e2 = jnp.exp(m2 - m1)
    w1 = 1.0 / (1.0 + e2)
    w2 = e2 / (1.0 + e2)
    hit1 = lane == i1
    hit2 = lane == i2
    onehot = jnp.where(hit1 | hit2, 1.0, 0.0)
    seen = jnp.dot(before_ref[...], onehot.astype(BF16), preferred_element_type=F32) + carry_ref[0:1, :]
    r1 = jnp.sum(jnp.where(hit1, seen, 0.0), axis=-1, keepdims=True)
    r2 = jnp.sum(jnp.where(hit2, seen, 0.0), axis=-1, keepdims=True)
    carry_ref[...] = carry_ref[...] + jnp.sum(onehot, axis=0, keepdims=True)
    cols = (i1.astype(F32), i2.astype(F32), w1, w2, r1, r2)
    route = jnp.zeros((rows, LANES), F32)
    for n, c in enumerate(cols):
        route = jnp.where(lane == n, c, route)
    route_ref[...] = route
    counts_ref[...] = carry_ref[...]


def _router(h, g, w_router, *, bm=1024):
    m, d = h.shape
    bm = min(bm, m)
    w = _pad_cols(w_router, LANES)
    w_hi = w.astype(BF16)
    w_split = _cat([w_hi, (w - w_hi.astype(F32)).astype(BF16)], 1)
    return pl.pallas_call(
        _router_kernel,
        grid=(m // bm,),
        in_specs=[pl.BlockSpec((bm, d), lambda i: (i, 0)), pl.BlockSpec((1, d), lambda i: (0, 0)),
                  pl.BlockSpec((d, 2 * LANES), lambda i: (0, 0)), pl.BlockSpec((bm, bm), lambda i: (0, 0))],
        out_specs=[pl.BlockSpec((bm, LANES), lambda i: (i, 0)), pl.BlockSpec((SUBLANES, LANES), lambda i: (0, 0))],
        out_shape=[jax.ShapeDtypeStruct((m, LANES), F32), jax.ShapeDtypeStruct((SUBLANES, LANES), F32)],
        scratch_shapes=[pltpu.VMEM((SUBLANES, LANES), F32)],
        compiler_params=_cparams(("arbitrary",)),
        name="router",
    )(h, g.reshape(1, d), w_split, jnp.tril(jnp.ones((bm, bm), BF16), -1))


def _rms(x, n, g):
    return x * lax.rsqrt(jnp.sum(x * x, axis=-1, keepdims=True) * (1.0 / n) + NORM_EPS) * g


def _mla_prep_kernel(u_ref, cos_ref, sin_ref, win_ref, gq_ref, gkv_ref, gqh_ref, gkh_ref, wq_ref, wqr_ref, wk_ref,
                     wv_ref, q_ref, k_ref, v_ref):
    z = jnp.dot(u_ref[...], win_ref[...], preferred_element_type=F32)
    cq = _rms(z[:, :Q_LORA], Q_LORA, gq_ref[...]).astype(BF16)
    ckv = _rms(z[:, Q_LORA:Q_LORA + KV_LORA], KV_LORA, gkv_ref[...]).astype(BF16)
    k_pe = z[:, Q_LORA + KV_LORA:Q_LORA + KV_LORA + LANES]
    k_pe_rot = z[:, Q_LORA + KV_LORA + LANES:]
    q_all = jnp.dot(cq, wq_ref[...], preferred_element_type=F32)
    q_rot = jnp.dot(cq, wqr_ref[...], preferred_element_type=F32)
    k_all = jnp.dot(ckv, wk_ref[...], preferred_element_type=F32)
    v_all = jnp.dot(ckv, wv_ref[...], preferred_element_type=F32)
    sin = sin_ref[...]
    cos_q = cos_ref[...] * gqh_ref[...]
    cos_k = cos_ref[...] * gkh_ref[...]

    def scale(x):
        return lax.rsqrt(jnp.sum(x * x, axis=-1, keepdims=True) * (1.0 / QK_DIM) + NORM_EPS)

    for h in range(N_HEADS):
        sl = slice(h * LANES, (h + 1) * LANES)
        q = q_all[:, sl]
        k = k_all[:, sl] + k_pe
        q_ref[h] = (scale(q) * (QK_DIM ** -0.5) * (q * cos_q + q_rot[:, sl] * sin)).astype(BF16)
        k_ref[h] = (scale(k) * (k * cos_k + k_pe_rot * sin)).astype(BF16)
        v_ref[h] = v_all[:, sl].astype(BF16)


def _pad_heads(w, width):
    k = w.shape[0]
    w = w.reshape(k, N_HEADS, width)
    return jnp.pad(w, ((0, 0), (0, 0), (0, LANES - width))).reshape(k, N_HEADS * LANES)


def _mla_prep(u, w_in, rope_tabs, b, s, q_norm_g, kv_norm_g, w_uq, w_ukv, qk_g_q, qk_g_k, *, bm=512):
    bm = min(bm, s)
    half = QK_ROPE // 2
    swap = lambda w: _cat([w[..., half:], w[..., :half]], -1)
    rope_slab = lambda w: jnp.pad(w, ((0, 0), (QK_NOPE, LANES - QK_DIM)))
    lora = Q_LORA + KV_LORA
    w_in = _cat([w_in[:, :lora], rope_slab(w_in[:, lora:]), rope_slab(swap(w_in[:, lora:] * qk_g_k[QK_NOPE:]))],
                1).astype(BF16)
    w_ukv = w_ukv.reshape(KV_LORA, N_HEADS, QK_NOPE + HEAD_DIM)
    w_uq3 = w_uq.reshape(Q_LORA, N_HEADS, QK_DIM)
    wq = _pad_heads(w_uq, QK_DIM).astype(BF16)
    wq_rot = jnp.pad(swap(w_uq3[..., QK_NOPE:] * qk_g_q[QK_NOPE:]), ((0, 0), (0, 0), (QK_NOPE, LANES - QK_DIM)))
    wq_rot = wq_rot.reshape(Q_LORA, N_HEADS * LANES).astype(BF16)
    wk = _pad_heads(w_ukv[..., :QK_NOPE].reshape(KV_LORA, -1), QK_NOPE).astype(BF16)
    wv = _pad_heads(w_ukv[..., QK_NOPE:].reshape(KV_LORA, -1), HEAD_DIM).astype(BF16)
    pad_g = lambda g: jnp.pad(g, (0, LANES - g.shape[0])).reshape(1, LANES)
    row = lambda width: pl.BlockSpec((bm, width), lambda bb, i: (bb * (s // bm) + i, 0))
    full = lambda a: pl.BlockSpec(a.shape, lambda bb, i: (0, 0))
    consts = [w_in, q_norm_g.reshape(1, -1), kv_norm_g.reshape(1, -1), pad_g(qk_g_q), pad_g(qk_g_k), wq, wq_rot,
              wk, wv]
    out_spec = pl.BlockSpec((None, N_HEADS, bm, LANES), lambda bb, i: (bb, 0, i, 0))
    out_shape = jax.ShapeDtypeStruct((b, N_HEADS, s, LANES), BF16)
    return pl.pallas_call(
        _mla_prep_kernel,
        grid=(b, s // bm),
        in_specs=[row(u.shape[1]), row(LANES), row(LANES)] + [full(a) for a in consts],
        out_specs=[out_spec] * 3,
        out_shape=[out_shape] * 3,
        compiler_params=_cparams(("arbitrary", "arbitrary")),
        name="mla_prep",
    )(u, *rope_tabs, *consts)


def _flash_kernel(qi_ref, kj_ref, q_ref, k_ref, v_ref, o_ref, m_ref, l_ref, acc_ref, *, blk, sub):
    n = pl.program_id(2)
    i = qi_ref[n]
    j = kj_ref[n]
    hb = q_ref.shape[0]

    @pl.when(j == 0)
    def _():
        m_ref[...] = jnp.full_like(m_ref, -jnp.inf)
        l_ref[...] = jnp.zeros_like(l_ref)
        acc_ref[...] = jnp.zeros_like(acc_ref)

    def update(masked):
        chains = [(h, r) for h in range(hb) for r in range(blk // sub)]

        def qk(h, r):
            nk = (r + 1) * sub if masked else blk
            s = lax.dot_general(q_ref[h, r * sub:(r + 1) * sub, :], k_ref[h, :nk, :], (((1,), (1,)), ((), ())),
                                preferred_element_type=F32)
            if masked:
                row = lax.broadcasted_iota(jnp.int32, (sub, nk), 0) + r * sub
                s = jnp.where(row >= lax.broadcasted_iota(jnp.int32, (sub, nk), 1), s, -jnp.inf)
            return s

        def softmax_pv(h, r, s):
            rows = slice(r * sub, (r + 1) * sub)
            m_old = m_ref[h, rows, :]
            m_new = jnp.maximum(m_old, jnp.max(s, axis=-1, keepdims=True))
            alpha = jnp.exp(m_old - m_new)
            p = jnp.exp(s - m_new[:, :1])
            l_ref[h, rows, :] = alpha * l_ref[h, rows, :] + jnp.sum(p, axis=-1, keepdims=True)
            m_ref[h, rows, :] = m_new
            pv = jnp.dot(p.astype(BF16), v_ref[h, :s.shape[1], :], preferred_element_type=F32)
            acc_ref[h, rows, :] = alpha * acc_ref[h, rows, :] + pv

        s_prev = qk(*chains[0])
        for c in range(len(chains)):
            s_next = qk(*chains[c + 1]) if c + 1 < len(chains) else None
            softmax_pv(*chains[c], s_prev)
            s_prev = s_next

    @pl.when(j < i)
    def _():
        update(False)

    @pl.when(j == i)
    def _():
        update(True)
        o_ref[...] = (acc_ref[...] / l_ref[...]).astype(o_ref.dtype)


def _flash_attention(q, k, v, *, blk=1024, sub=256, hb=4):
    b, h, s, d = q.shape
    blk = min(blk, s)
    sub = min(sub, blk)
    nb = s // blk
    qi = np.array([i for i in range(nb) for j in range(i + 1)], np.int32)
    kj = np.array([j for i in range(nb) for j in range(i + 1)], np.int32)
    q_spec = pl.BlockSpec((None, hb, blk, d), lambda bb, hh, n, qi_r, kj_r: (bb, hh, qi_r[n], 0))
    kv_spec = pl.BlockSpec((None, hb, blk, d), lambda bb, hh, n, qi_r, kj_r: (bb, hh, kj_r[n], 0))
    return pl.pallas_call(
        functools.partial(_flash_kernel, blk=blk, sub=sub),
        grid_spec=pltpu.PrefetchScalarGridSpec(
            num_scalar_prefetch=2,
            grid=(b, h // hb, len(qi)),
            in_specs=[q_spec, kv_spec, kv_spec],
            out_specs=q_spec,
            scratch_shapes=[pltpu.VMEM((hb, blk, LANES), F32), pltpu.VMEM((hb, blk, LANES), F32),
                            pltpu.VMEM((hb, blk, d), F32)],
        ),
        out_shape=jax.ShapeDtypeStruct((b, h, s, d), BF16),
        compiler_params=_cparams(("arbitrary", "arbitrary", "arbitrary")),
        name="flash",
    )(jnp.asarray(qi), jnp.asarray(kj), q, k, v)


def _dot(a, b):
    return jnp.dot(a.astype(BF16), b.astype(BF16), preferred_element_type=F32)


def _dot_nt(a, b):
    return lax.dot_general(a.astype(BF16), b.astype(BF16), (((1,), (1,)), ((), ())), preferred_element_type=F32)


def _dot_tn(a, b):
    return jnp.dot(a.T.astype(BF16), b.astype(BF16), preferred_element_type=F32)


def _split3(x):
    hi = x.astype(BF16)
    r1 = x - hi.astype(F32)
    mid = r1.astype(BF16)
    return hi, mid, (r1 - mid.astype(F32)).astype(BF16)


def _cumsum_rows(tri, x):
    s = jnp.dot(tri, jnp.concatenate(_split3(x), axis=1), preferred_element_type=F32)
    n = x.shape[1]
    return s[:, :n] + s[:, n:2 * n] + s[:, 2 * n:]


def _head_sum(x):
    left = lax.broadcasted_iota(jnp.int32, (x.shape[0], LANES), 1) < HEAD_DIM
    outs = []
    for p in range(WIDTH // LANES):
        xs = x[:, p * LANES:(p + 1) * LANES]
        s_left = jnp.sum(jnp.where(left, xs, 0.0), axis=-1, keepdims=True)
        s_right = jnp.sum(jnp.where(left, 0.0, xs), axis=-1, keepdims=True)
        outs.append(jnp.where(left, s_left, s_right))
    return jnp.concatenate(outs, axis=1)


def _shift_rows(x, tail, j):
    rolled = pltpu.roll(x, j, 0)
    top = jnp.where(lax.broadcasted_iota(jnp.int32, (SUBLANES, x.shape[1]), 0) < j,
                    pltpu.roll(tail, j, 0), rolled[:SUBLANES])
    return jnp.concatenate([top, rolled[SUBLANES:]], axis=0)


def _softplus(x):
    return jnp.maximum(x, 0.0) + jnp.log(1.0 + jnp.exp(-jnp.abs(x)))


def _pair_masks():
    lane = lax.broadcasted_iota(jnp.int32, (CHUNK, LANES), 1)
    m0 = (lane < HEAD_DIM).astype(F32)
    return m0, 1.0 - m0


def _ext(x, m0, m1, dtype=BF16):
    return jnp.concatenate([x * m0, x * m1], axis=0).astype(dtype)


def _cat(xs, axis):
    return jnp.concatenate(xs, axis=axis)


SCAN_GROUP = 2


def _neumann_inverse(mats):
    n = mats[0].shape[0]
    eye = (lax.broadcasted_iota(jnp.int32, (n, n), 0) == lax.broadcasted_iota(jnp.int32, (n, n), 1)).astype(F32)
    ts = [eye + a for a in mats]
    ps = [_dot(a, a) for a in mats]
    yield
    for _ in range(int(math.log2(CHUNK)) - 2):
        pps = [_dot(_cat([p, t], 0), p) for p, t in zip(ps, ts)]
        ps = [pp[:n] for pp in pps]
        ts = [t + pp[n:] for t, pp in zip(ts, pps)]
        yield
    return [t + _dot(t, p) for t, p in zip(ts, ps)]


def _interleave(prepare, serial, n_groups):
    for _ in prepare(0):
        pass
    for g in range(n_groups):
        gens = [serial(g)] + ([prepare(g + 1)] if g + 1 < n_groups else [])
        while gens:
            for gen in list(gens):
                if next(gen, StopIteration) is StopIteration:
                    gens.remove(gen)


def _work_items(rows):
    return [(slice(c * CHUNK, (c + 1) * CHUNK), slice(p * LANES, (p + 1) * LANES))
            for c in range(rows // CHUNK) for p in range(WIDTH // LANES)]


def _tiles(x, items):
    return [x[rows, cols] for rows, cols in items]


def _rwkv_scan(r, k, v, a, b, ld, st_ref, y_ref):
    c2 = 2 * CHUNK
    strict = lax.broadcasted_iota(jnp.int32, (c2, c2), 0) > lax.broadcasted_iota(jnp.int32, (c2, c2), 1)
    trow = lax.broadcasted_iota(jnp.int32, (CHUNK, c2), 0)
    tcol = lax.broadcasted_iota(jnp.int32, (CHUNK, c2), 1) & (CHUNK - 1)
    incl_w = tcol <= trow
    tri = (lax.broadcasted_iota(jnp.int32, (CHUNK, CHUNK), 0)
           >= lax.broadcasted_iota(jnp.int32, (CHUNK, CHUNK), 1)).astype(BF16)
    m0, m1 = _pair_masks()
    items = _work_items(r.shape[0])
    n_pairs = WIDTH // LANES
    per_group = min(SCAN_GROUP, len(items) // n_pairs) * n_pairs
    full = (r, k, v, a, b, ld)
    ready = {}
    state = [st_ref[p] for p in range(n_pairs)]

    def prepare(g):
        sub = items[g * per_group:(g + 1) * per_group]
        r, k, v, a, b, ld = (_tiles(x, sub) for x in full)
        cum = [_cumsum_rows(tri, x) for x in ld]
        yield
        e_pos = [jnp.exp(x) for x in cum]
        e_neg = [jnp.exp(-x) for x in cum]
        e_end = [jnp.exp(x[CHUNK - 1:CHUNK, :] - x) for x in cum]
        a_ext = [_ext(ai * jnp.exp(ci - li), m0, m1) for ai, ci, li in zip(a, cum, ld)]
        v_ext = [_ext(x, m0, m1) for x in v]
        r_dec = [(ri * ei).astype(BF16) for ri, ei in zip(r, e_pos)]
        yield
        s1 = [_dot_nt(_cat([ae, rd], 0), _cat([_ext(bi * en, m0, m1), _ext(ki * en, m0, m1)], 0))
              for ae, rd, bi, ki, en in zip(a_ext, r_dec, b, k, e_neg)]
        yield
        a_ab = [jnp.where(strict, s[:c2, :c2], 0.0) for s in s1]
        a_ak = [jnp.where(strict, s[:c2, c2:], 0.0) for s in s1]
        a_rbk = [_cat([jnp.where(incl_w, s[c2:, :c2], 0.0), jnp.where(incl_w, s[c2:, c2:], 0.0)], 1).astype(BF16)
                 for s in s1]
        akv = [_dot(x, ve) for x, ve in zip(a_ak, v_ext)]
        t_inv = yield from _neumann_inverse(a_ab)
        yield
        wu0 = [_dot(t, _cat([ae, av.astype(BF16)], 1)) for t, ae, av in zip(t_inv, a_ext, akv)]
        bk_end = [_cat([_ext(bi * ee, m0, m1, F32), _ext(ki * ee, m0, m1, F32)], 0)
                  for bi, ki, ee in zip(b, k, e_end)]
        e_last_t = [jnp.broadcast_to(e[CHUNK - 1:CHUNK, :], (LANES, LANES)).T for e in e_pos]
        ready[g] = (sub, wu0, r_dec, v_ext, a_rbk, bk_end, e_last_t)

    def serial(g):
        sub, wu0, r_dec, v_ext, a_rbk, bk_end, e_last_t = ready.pop(g)
        for c in range(len(sub) // n_pairs):
            idx = range(c * n_pairs, (c + 1) * n_pairs)
            z = [_dot(_cat([wu0[i][:, :LANES].astype(BF16), r_dec[i]], 0), s) for i, s in zip(idx, state)]
            yield
            uv = [_cat([(wu0[i][:, LANES:] + zi[:c2]).astype(BF16), v_ext[i]], 0) for i, zi in zip(idx, z)]
            for i, zi, uvi in zip(idx, z, uv):
                rows, cols = sub[i]
                y_ref[rows, cols] = zi[c2:] + _dot(a_rbk[i], uvi)
            state[:] = [s * e_last_t[i] + _dot_tn(bk_end[i], uvi) for i, s, uvi in zip(idx, state, uv)]
            yield

    _interleave(prepare, serial, len(items) // per_group)
    for p in range(n_pairs):
        st_ref[p] = state[p]


def _gdn_scan(q, k, v, g, beta, st_ref, o_ref):
    c2 = 2 * CHUNK
    row = lax.broadcasted_iota(jnp.int32, (c2, c2), 0)
    col = lax.broadcasted_iota(jnp.int32, (c2, c2), 1)
    causal = ((row >= CHUNK) == (col >= CHUNK)) & (row >= col)
    strict = row > col
    tri = (lax.broadcasted_iota(jnp.int32, (CHUNK, CHUNK), 0)
           >= lax.broadcasted_iota(jnp.int32, (CHUNK, CHUNK), 1)).astype(BF16)
    m0, m1 = _pair_masks()
    items = _work_items(q.shape[0])
    n_pairs = WIDTH // LANES
    per_group = min(SCAN_GROUP, len(items) // n_pairs) * n_pairs
    full = (q, k, v, g, beta)
    ready = {}
    state = [st_ref[p] for p in range(n_pairs)]

    def prepare(grp):
        sub = items[grp * per_group:(grp + 1) * per_group]
        q, k, v, g, beta = (_tiles(x, sub) for x in full)
        gc = [_cumsum_rows(tri, x) for x in g]
        yield
        g_rows = [_cat([jnp.broadcast_to(x[:, 0:1], (CHUNK, LANES)),
                        jnp.broadcast_to(x[:, HEAD_DIM:HEAD_DIM + 1], (CHUNK, LANES))], 0) for x in gc]
        gamma = [jnp.exp(jnp.where(causal, x - x.T, -jnp.inf)) for x in g_rows]
        e_pos = [jnp.exp(x) for x in gc]
        kb = [ki * bi for ki, bi in zip(k, beta)]
        yield
        s1 = [_dot_nt(_cat([_ext(kbi, m0, m1), qi.astype(BF16)], 0), _ext(ki, m0, m1))
              for kbi, qi, ki in zip(kb, q, k)]
        yield
        a_neg = [-jnp.where(strict, s[:c2] * ga, 0.0) for s, ga in zip(s1, gamma)]
        a_in = [(s[c2:] * (ga[:CHUNK] + ga[CHUNK:])).astype(BF16) for s, ga in zip(s1, gamma)]
        t_inv = yield from _neumann_inverse(a_neg)
        yield
        uw = [_dot(t, _cat([_ext(vi * bi, m0, m1), _ext(kbi * ei, m0, m1)], 1))
              for t, vi, bi, kbi, ei in zip(t_inv, v, beta, kb, e_pos)]
        q_dec = [(qi * ei).astype(BF16) for qi, ei in zip(q, e_pos)]
        k_dec = [_ext(ki * jnp.exp(x[CHUNK - 1:CHUNK, :] - x), m0, m1, F32) for ki, x in zip(k, gc)]
        ready[grp] = (sub, uw, q_dec, k_dec, a_in, e_pos)

    def serial(grp):
        sub, uw, q_dec, k_dec, a_in, e_pos = ready.pop(grp)
        for c in range(len(sub) // n_pairs):
            idx = range(c * n_pairs, (c + 1) * n_pairs)
            z = [_dot(_cat([uw[i][:, LANES:].astype(BF16), q_dec[i]], 0), s) for i, s in zip(idx, state)]
            yield
            v_new = [(uw[i][:, :LANES] - zi[:c2]).astype(BF16) for i, zi in zip(idx, z)]
            for i, zi, vn in zip(idx, z, v_new):
                rows, cols = sub[i]
                o_ref[rows, cols] = zi[c2:] + _dot(a_in[i], vn)
            state[:] = [s * e_pos[i][CHUNK - 1:CHUNK, :] + _dot_tn(k_dec[i], vn)
                        for i, s, vn in zip(idx, state, v_new)]
            yield

    _interleave(prepare, serial, len(items) // per_group)
    for p in range(n_pairs):
        st_ref[p] = state[p]


def _rwkv_kernel(*refs, vres):
    if vres:
        (u_ref, vf_ref, win_ref, mu_ref, prm_ref, wup_ref, aup_ref, gup_ref, vup_ref, o_ref, st_ref, tail_ref,
         y_ref) = refs
    else:
        u_ref, win_ref, mu_ref, prm_ref, wup_ref, aup_ref, gup_ref, o_ref, vf_out_ref, st_ref, tail_ref, y_ref = refs

    @pl.when(pl.program_id(1) == 0)
    def _():
        st_ref[...] = jnp.zeros_like(st_ref)
        tail_ref[...] = jnp.zeros_like(tail_ref)

    z = jnp.dot(u_ref[...], win_ref[...], preferred_element_type=F32)
    rows = z.shape[0]
    zs = _shift_rows(z, tail_ref[...], 1)
    tail_ref[...] = z[rows - SUBLANES:]
    zl = z[:, :RWKV_IN] + (zs[:, :RWKV_IN] - z[:, :RWKV_IN]) * mu_ref[...]
    r, k, v = zl[:, :WIDTH], zl[:, WIDTH:2 * WIDTH], zl[:, 2 * WIDTH:3 * WIDTH]
    lo = zl[:, 3 * WIDTH:3 * WIDTH + LANES]
    g_lo = zl[:, 3 * WIDTH + LANES:]
    w0, a0, k_k, k_a, ln_g, ln_b, r_k, v_bias = (prm_ref[i:i + 1, :] for i in range(8))
    log_w = -_softplus(-(w0 + _dot(jnp.tanh(lo), wup_ref[...]))) - 0.5
    ld = -jnp.exp(log_w)
    iclr = jax.nn.sigmoid(a0 + _dot(lo, aup_ref[...]))
    gate = _dot(jax.nn.sigmoid(g_lo), gup_ref[...])
    if vres:
        x = z[:, RWKV_IN:] + pltpu.roll(zs[:, RWKV_IN:], LANES - VRES_LORA, 1)
        v = v + (vf_ref[...] - v) * jax.nn.sigmoid(v_bias + _dot(x, vup_ref[...]))
    else:
        vf_out_ref[...] = v
    kk = k * k_k
    kk = kk * lax.rsqrt(_head_sum(kk * kk) + L2_EPS)
    k = k * (1.0 + (iclr - 1.0) * k_a)
    _rwkv_scan(r, k, v, -kk, kk * iclr, ld, st_ref, y_ref)
    y = y_ref[...]
    d = y - _head_sum(y) * (1.0 / HEAD_DIM)
    y = d * lax.rsqrt(_head_sum(d * d) * (1.0 / HEAD_DIM) + RWKV_GN_EPS) * ln_g + ln_b
    y = y + _head_sum(r * k * r_k) * v
    o_ref[...] = (y * gate).astype(o_ref.dtype)


def _rwkv(u, w_in, v_first, b, s, mu, prm, w_up, a_up, g_up, v_up):
    vres = v_first is not None
    rows = min(SCAN_ROWS, s)
    zw = w_in.shape[1]
    row = lambda width: pl.BlockSpec((rows, width), lambda bb, c: (bb * (s // rows) + c, 0))
    full = lambda a: pl.BlockSpec(a.shape, lambda bb, c: (0, 0))
    zero = jnp.zeros((LANES // 2, WIDTH), F32)
    consts = [w_in.astype(BF16), mu.reshape(1, -1), prm, _cat([w_up, zero], 0).astype(BF16),
              _cat([zero, a_up], 0).astype(BF16), g_up.astype(BF16)]
    args, in_specs = [u], [row(u.shape[1])]
    if vres:
        args.append(v_first)
        in_specs.append(row(WIDTH))
        consts.append(jnp.pad(v_up, ((0, LANES - VRES_LORA), (0, 0))).astype(BF16))
    out_shape = [jax.ShapeDtypeStruct((b * s, WIDTH), BF16)]
    out_specs = [row(WIDTH)]
    if not vres:
        out_shape.append(jax.ShapeDtypeStruct((b * s, WIDTH), F32))
        out_specs.append(row(WIDTH))
    outs = pl.pallas_call(
        functools.partial(_rwkv_kernel, vres=vres),
        grid=(b, s // rows),
        in_specs=in_specs + [full(a) for a in consts],
        out_specs=out_specs,
        out_shape=out_shape,
        scratch_shapes=[pltpu.VMEM((WIDTH // LANES, LANES, LANES), F32), pltpu.VMEM((SUBLANES, zw), F32),
                        pltpu.VMEM((rows, WIDTH), F32)],
        compiler_params=_cparams(("arbitrary", "arbitrary")),
        name="rwkv",
    )(*args, *consts)
    return (outs[0], v_first) if vres else (outs[0], outs[1])


def _gdn_kernel(u_ref, win_ref, cw_ref, prm_ref, exp_ref, o_ref, st_ref, tail_ref, y_ref):
    @pl.when(pl.program_id(1) == 0)
    def _():
        st_ref[...] = jnp.zeros_like(st_ref)
        tail_ref[...] = jnp.zeros_like(tail_ref)

    z = jnp.dot(u_ref[...], win_ref[...], preferred_element_type=F32)
    rows = z.shape[0]
    x = z[:, :GDN_QKV]
    tail = tail_ref[...]
    conv = x * cw_ref[GDN_CONV - 1:GDN_CONV, :]
    for j in range(1, GDN_CONV):
        conv = conv + _shift_rows(x, tail, j) * cw_ref[GDN_CONV - 1 - j:GDN_CONV - j, :]
    tail_ref[...] = x[rows - SUBLANES:]
    qkv = conv * jax.nn.sigmoid(conv)
    q, k, v = qkv[:, :WIDTH], qkv[:, WIDTH:2 * WIDTH], qkv[:, 2 * WIDTH:]
    q = q * lax.rsqrt(_head_sum(q * q) + L2_EPS) * (HEAD_DIM ** -0.5)
    k = k * lax.rsqrt(_head_sum(k * k) + L2_EPS)
    logits = jnp.dot(_cat(_split3(z[:, GDN_QKV + WIDTH:]), 0), exp_ref[...], preferred_element_type=F32)
    logits = logits[:rows] + logits[rows:2 * rows] + logits[2 * rows:]
    neg_a, dt_bias, norm_g = (prm_ref[i:i + 1, :] for i in range(3))
    beta = jax.nn.sigmoid(logits[:, :WIDTH])
    g = neg_a * _softplus(logits[:, WIDTH:] + dt_bias)
    _gdn_scan(q, k, v, g, beta, st_ref, y_ref)
    o = y_ref[...]
    o = o * lax.rsqrt(_head_sum(o * o) * (1.0 / HEAD_DIM) + NORM_EPS) * norm_g
    gate = z[:, GDN_QKV:GDN_QKV + WIDTH]
    o_ref[...] = (o * (gate * jax.nn.sigmoid(gate))).astype(o_ref.dtype)


def _gdn(u, w_in, b, s, conv_w, a_log, dt_bias, norm_g):
    rows = min(SCAN_ROWS, s)
    row = lambda width: pl.BlockSpec((rows, width), lambda bb, c: (bb * (s // rows) + c, 0))
    full = lambda a: pl.BlockSpec(a.shape, lambda bb, c: (0, 0))
    per_lane = lambda t: jnp.repeat(t, HEAD_DIM)
    prm = jnp.stack([per_lane(-jnp.exp(a_log)), per_lane(dt_bias), jnp.tile(norm_g, N_HEADS)])
    head_of_lane = np.arange(WIDTH) // HEAD_DIM
    expand = np.zeros((LANES, 2 * WIDTH), np.float32)
    expand[head_of_lane, np.arange(WIDTH)] = 1.0
    expand[N_HEADS + head_of_lane, WIDTH + np.arange(WIDTH)] = 1.0
    consts = [w_in.astype(BF16), conv_w, prm, jnp.asarray(expand, BF16)]
    return pl.pallas_call(
        _gdn_kernel,
        grid=(b, s // rows),
        in_specs=[row(u.shape[1])] + [full(a) for a in consts],
        out_specs=row(WIDTH),
        out_shape=jax.ShapeDtypeStruct((b * s, WIDTH), BF16),
        scratch_shapes=[pltpu.VMEM((WIDTH // LANES, LANES, LANES), F32), pltpu.VMEM((SUBLANES, GDN_QKV), F32),
                        pltpu.VMEM((rows, WIDTH), F32)],
        compiler_params=_cparams(("arbitrary", "arbitrary")),
        name="gdn",
    )(u, *consts)


def _merge_kernel(h_ref, u_ref, oa_ref, ob_ref, oc_ref, wg_ref, wa_ref, wb_ref, wc_ref, wo_ref, out_ref):
    d = h_ref.shape[1]
    u = u_ref[...]
    ob = jnp.concatenate([ob_ref[h] for h in range(N_HEADS)], axis=1)
    merged = jnp.zeros(h_ref.shape, F32)
    for n, (o, w_ref) in enumerate(((oa_ref[...], wa_ref), (ob, wb_ref), (oc_ref[...], wc_ref))):
        proj = jnp.dot(o, w_ref[...], preferred_element_type=F32)
        zg = jnp.dot(u, wg_ref[:, n * d:(n + 1) * d], preferred_element_type=F32)
        merged = merged + jax.nn.sigmoid(zg) * proj
    out_ref[...] = h_ref[...] + jnp.dot(merged.astype(BF16), wo_ref[...], preferred_element_type=F32)


def _merge(h, u, oa, ob, oc, wg, wa, wb, wc, wo, s, *, bm=512):
    m, d = h.shape
    bm = min(bm, s)
    row = lambda width: pl.BlockSpec((bm, width), lambda i: (i, 0))
    full = lambda w: pl.BlockSpec(w.shape, lambda i: (0, 0))
    wb = jnp.pad(wb.reshape(N_HEADS, HEAD_DIM, d), ((0, 0), (0, LANES - HEAD_DIM), (0, 0))).reshape(N_HEADS * LANES, d)
    ws = [w.astype(BF16) for w in (wg, wa, wb, wc, wo)]
    ob_spec = pl.BlockSpec((None, N_HEADS, bm, LANES), lambda i: (i // (s // bm), 0, i % (s // bm), 0))
    return pl.pallas_call(
        _merge_kernel,
        grid=(m // bm,),
        in_specs=[row(d), row(d), row(WIDTH), ob_spec, row(WIDTH)] + [full(w) for w in ws],
        out_specs=row(d),
        out_shape=jax.ShapeDtypeStruct((m, d), F32),
        compiler_params=_cparams(("arbitrary",)),
        name="merge",
    )(h, u, oa, ob, oc, *ws)


def _ffn_kernel(be_ref, nu_ref, x_ref, g_ref, wg_ref, wu_ref, wd_ref, o_ref, xs_ref, acc_ref, *, nf, dense):
    i = pl.program_id(0)
    f = pl.program_id(1)
    used = i < nu_ref[0]

    @pl.when(used & (f == 0))
    def _():
        x = x_ref[...]
        ms = jnp.mean(x * x, axis=-1, keepdims=True)
        xs_ref[...] = (x * lax.rsqrt(ms + NORM_EPS) * g_ref[...]).astype(BF16)

    @pl.when(f == 0)
    def _():
        acc_ref[...] = jnp.zeros_like(acc_ref)

    @pl.when(used)
    def _():
        xs = xs_ref[...]
        gate = jnp.dot(xs, wg_ref[...].astype(BF16), preferred_element_type=F32)
        up = jnp.dot(xs, wu_ref[...].astype(BF16), preferred_element_type=F32)
        act = (gate * jax.nn.sigmoid(gate) * up).astype(BF16)
        acc_ref[...] += jnp.dot(act, wd_ref[...].astype(BF16), preferred_element_type=F32)

    @pl.when(f == nf - 1)
    def _():
        o_ref[...] = (x_ref[...] + acc_ref[...]) if dense else acc_ref[...]


def _ffn(x, g, wg, wu, wd, block_e, n_used, *, bm, tf, dense):
    r, d = x.shape
    ff = wg.shape[2]
    nf = ff // tf

    def ff_idx(i, f, be, nu):
        return jnp.where(i < nu[0], f, nf - 1)

    return pl.pallas_call(
        functools.partial(_ffn_kernel, nf=nf, dense=dense),
        grid_spec=pltpu.PrefetchScalarGridSpec(
            num_scalar_prefetch=2,
            grid=(r // bm, nf),
            in_specs=[
                pl.BlockSpec((bm, d), lambda i, f, be, nu: (i, 0)),
                pl.BlockSpec((1, d), lambda i, f, be, nu: (0, 0)),
                pl.BlockSpec((None, d, tf), lambda i, f, be, nu: (be[i], 0, ff_idx(i, f, be, nu))),
                pl.BlockSpec((None, d, tf), lambda i, f, be, nu: (be[i], 0, ff_idx(i, f, be, nu))),
                pl.BlockSpec((None, tf, d), lambda i, f, be, nu: (be[i], ff_idx(i, f, be, nu), 0)),
            ],
            out_specs=pl.BlockSpec((bm, d), lambda i, f, be, nu: (i, 0)),
            scratch_shapes=[pltpu.VMEM((bm, d), BF16), pltpu.VMEM((bm, d), F32)],
        ),
        out_shape=jax.ShapeDtypeStruct((r, d), F32),
        compiler_params=_cparams(("arbitrary", "arbitrary")),
        name="ffn",
    )(block_e, n_used, x, g.reshape(1, d).astype(F32), wg, wu, wd)


def _ple_kernel(*refs, combine, emit_norm):
    refs = list(refs)
    h = refs.pop(0)[...]
    if combine:
        y0_ref, y1_ref, route_ref = refs[:3]
        del refs[:3]
        h = h + route_ref[:, 2:3] * y0_ref[...] + route_ref[:, 3:4] * y1_ref[...]
    p_ref, wg_ref, wp_ref, g_ref = refs[:4]
    gate = jnp.dot(h.astype(BF16), wg_ref[...], preferred_element_type=F32)
    e = jnp.dot(p_ref[...].astype(BF16), wp_ref[...], preferred_element_type=F32)
    ms = jnp.mean(e * e, axis=-1, keepdims=True)
    e = e * lax.rsqrt(ms + NORM_EPS) * g_ref[...]
    out = h + jax.nn.sigmoid(gate) * e
    if emit_norm:
        gn_ref, o_ref, u_ref = refs[4:]
        u_ref[...] = (out * lax.rsqrt(jnp.mean(out * out, axis=-1, keepdims=True) + NORM_EPS) * gn_ref[...]).astype(BF16)
    else:
        o_ref, = refs[4:]
    o_ref[...] = out


def _ple(h, p, w_gate, w_proj, g, expert_out=None, g_next=None, *, bm=512):
    m, d = h.shape
    bm = min(bm, m)
    row = lambda a: pl.BlockSpec((bm, a.shape[1]), lambda i: (i, 0))
    full = lambda a: pl.BlockSpec(a.shape, lambda i: (0, 0))
    rows = [h] + list(expert_out or ()) + [p]
    consts = [w_gate.astype(BF16), w_proj.astype(BF16), g.reshape(1, d)]
    out_shape = [jax.ShapeDtypeStruct((m, d), F32)]
    if g_next is not None:
        consts.append(g_next.reshape(1, d))
        out_shape.append(jax.ShapeDtypeStruct((m, d), BF16))
    outs = pl.pallas_call(
        functools.partial(_ple_kernel, combine=expert_out is not None, emit_norm=g_next is not None),
        grid=(m // bm,),
        in_specs=[row(a) for a in rows] + [full(a) for a in consts],
        out_specs=[row(h)] * len(out_shape),
        out_shape=out_shape,
        compiler_params=_cparams(("arbitrary",)),
        name="ple",
    )(*rows, *consts)
    return outs if g_next is not None else (outs[0], None)


def _pad_cols(w, n):
    return jnp.pad(w, ((0, 0), (0, n - w.shape[1])))


def _rope_tables(positions):
    t = positions.size
    inv_freq = 1.0 / (ROPE_THETA ** (jnp.arange(0, QK_ROPE, 2, dtype=F32) / QK_ROPE))
    ang = positions.astype(F32).reshape(t, 1) * inv_freq
    cos, sin = jnp.cos(ang), jnp.sin(ang)
    pad = ((0, 0), (QK_NOPE, LANES - QK_DIM))
    return jnp.pad(_cat([cos, cos], 1), pad, constant_values=1.0), jnp.pad(_cat([-sin, sin], 1), pad)


def _moe(h, g, w_router, wg, wu, wd, *, bm=768, tf=512):
    t, d = h.shape
    n_assign = t * TOP_K
    n_blocks = -(-(n_assign + N_EXPERTS * (bm - 1)) // bm)
    route, counts = _router(h, g, w_router)
    counts = counts[0, :N_EXPERTS].astype(jnp.int32)
    padded = (counts + bm - 1) // bm * bm
    pad_end = jnp.cumsum(padded)
    top_e = route[:, :TOP_K].astype(jnp.int32)
    dest = (pad_end - padded)[top_e] + route[:, 4:4 + TOP_K].astype(jnp.int32)
    src_tok = (jnp.arange(n_blocks * bm, dtype=jnp.int32) % t).at[dest.reshape(-1)].set(
        jnp.arange(n_assign, dtype=jnp.int32) // TOP_K)
    block_start = jnp.arange(n_blocks, dtype=jnp.int32) * bm
    block_e = jnp.minimum(jnp.sum(pad_end[None, :] <= block_start[:, None], axis=1), N_EXPERTS - 1).astype(jnp.int32)
    n_used = (pad_end[-1] // bm).astype(jnp.int32).reshape(1)
    y_rows = _ffn(h[src_tok], g, wg, wu, wd, block_e, n_used, bm=bm, tf=tf, dense=False)
    return y_rows[dest[:, 0]], y_rows[dest[:, 1]], route


def kernel(x, p, positions, norm_mix_g, w_in, rwkv_mu, rwkv_w0, rwkv_w_up, rwkv_a0, rwkv_a_up, rwkv_g_up, rwkv_k_k, rwkv_k_a, rwkv_r_k, rwkv_ln_g, rwkv_ln_b, vres_mu, vres_down, vres_up, vres_b, mla_q_norm_g, mla_kv_norm_g, mla_w_uq, mla_w_ukv, mla_qk_norm_q, mla_qk_norm_k, gdn_conv_w, gdn_a_log, gdn_dt_bias, gdn_norm_g, w_br_rwkv, w_br_mla, w_br_gdn, w_out, norm_ffn_g, ffn_wg, ffn_wu, ffn_wd, moe_router, moe_wg, moe_wu, moe_wd, ple_proj, ple_gate, ple_norm_g):
    b, s, d = x.shape
    t = b * s
    depth = w_in.shape[0]
    rope_tabs = _rope_tables(positions)

    h = x.reshape(t, d)
    v_first = None
    for i in range(depth):
        w = w_in[i]
        g = norm_mix_g[i]
        w_rwkv = w[:, :RWKV_IN]
        if i > 0:
            mu_v = vres_mu[i - 1][:, None]
            vd = vres_down[i - 1]
            w_rwkv = _pad_cols(_cat([w_rwkv, (1.0 - mu_v) * vd, mu_v * vd], 1), RWKV_IN + LANES)
        w_gdn = w[:, RWKV_IN + MLA_IN:RWKV_IN + MLA_IN + GDN_IN]
        w_gdn = _pad_cols(_cat([w_gdn[:, :GDN_QKV], w_gdn[:, GDN_QKV + 2 * N_HEADS:],
                                w_gdn[:, GDN_QKV:GDN_QKV + 2 * N_HEADS]], 1), GDN_QKV + WIDTH + LANES)
        if i == 0:
            u = _norm(h, g)
        prm = jnp.stack([rwkv_w0[i], rwkv_a0[i], rwkv_k_k[i], rwkv_k_a[i], rwkv_ln_g[i], rwkv_ln_b[i],
                         rwkv_r_k[i].reshape(-1), vres_b[i - 1] if i > 0 else jnp.zeros((WIDTH,), F32)])
        o_a, v_first = _rwkv(u, w_rwkv, v_first, b, s, rwkv_mu[i], prm, rwkv_w_up[i], rwkv_a_up[i], rwkv_g_up[i],
                             vres_up[i - 1] if i > 0 else None)
        q, k, v = _mla_prep(u, w[:, RWKV_IN:RWKV_IN + MLA_IN], rope_tabs, b, s, mla_q_norm_g[i], mla_kv_norm_g[i],
                            mla_w_uq[i], mla_w_ukv[i], mla_qk_norm_q[i], mla_qk_norm_k[i])
        o_b = _flash_attention(q, k, v)
        o_c = _gdn(u, w_gdn, b, s, gdn_conv_w[i], gdn_a_log[i], gdn_dt_bias[i], gdn_norm_g[i])
        h = _merge(h, u, o_a, o_b, o_c, w[:, RWKV_IN + MLA_IN + GDN_IN:], w_br_rwkv[i], w_br_mla[i], w_br_gdn[i],
                   w_out[i], s)
        j = i // 2
        if i % 2 == 0:
            bm = min(1024, t)
            h = _ffn(h, norm_ffn_g[i], ffn_wg[j:j + 1], ffn_wu[j:j + 1], ffn_wd[j:j + 1],
                     jnp.zeros((t // bm,), jnp.int32), jnp.full((1,), t // bm, jnp.int32), bm=bm, tf=256, dense=True)
            expert_out = None
        else:
            expert_out = _moe(h, norm_ffn_g[i], moe_router[j], moe_wg[j], moe_wu[j], moe_wd[j])
        h, u = _ple(h, p[i].reshape(t, -1), ple_gate[i], ple_proj[i], ple_norm_g[i], expert_out,
                    norm_mix_g[i + 1] if i + 1 < depth else None)
    return h.reshape(b, s, d)
```

```python
import functools
import math

import jax
import jax.numpy as jnp
import numpy as np
from jax import lax
from jax.experimental import pallas as pl
from jax.experimental.pallas import tpu as pltpu

F32 = jnp.float32
BF16 = jnp.bfloat16

NORM_EPS = 1e-6
RWKV_GN_EPS = 64e-5
L2_EPS = 1e-12
N_HEADS = 8
HEAD_DIM = 64
WIDTH = N_HEADS * HEAD_DIM
RWKV_LORA = 256
RWKV_IN = 3 * WIDTH + RWKV_LORA
VRES_LORA = 32
QK_NOPE = 64
QK_ROPE = 32
QK_DIM = QK_NOPE + QK_ROPE
Q_LORA = 256
KV_LORA = 128
MLA_IN = Q_LORA + KV_LORA + QK_ROPE
ROPE_THETA = 10000.0
GDN_CONV = 4
GDN_QKV = 3 * WIDTH
GDN_IN = GDN_QKV + 2 * N_HEADS + WIDTH
N_EXPERTS = 8
TOP_K = 2
LANES = 128
SUBLANES = 8
CHUNK = 64
SCAN_ROWS = 8 * CHUNK
VMEM_LIMIT = 48 * 1024 * 1024


def _cparams(sem):
    return pltpu.CompilerParams(dimension_semantics=sem, vmem_limit_bytes=VMEM_LIMIT)


def _norm_kernel(x_ref, g_ref, u_ref):
    x = x_ref[...]
    u_ref[...] = (x * lax.rsqrt(jnp.mean(x * x, axis=-1, keepdims=True) + NORM_EPS) * g_ref[...]).astype(u_ref.dtype)


def _norm(x, g, *, bm=1024):
    m, d = x.shape
    bm = min(bm, m)
    return pl.pallas_call(
        _norm_kernel,
        grid=(m // bm,),
        in_specs=[pl.BlockSpec((bm, d), lambda i: (i, 0)), pl.BlockSpec((1, d), lambda i: (0, 0))],
        out_specs=pl.BlockSpec((bm, d), lambda i: (i, 0)),
        out_shape=jax.ShapeDtypeStruct((m, d), BF16),
        compiler_params=_cparams(("arbitrary",)),
        name="norm",
    )(x, g.reshape(1, d))


def _router_kernel(x_ref, g_ref, wr_ref, before_ref, route_ref, counts_ref, carry_ref):
    @pl.when(pl.program_id(0) == 0)
    def _():
        carry_ref[...] = jnp.zeros_like(carry_ref)

    x = x_ref[...]
    rows = x.shape[0]
    u = x * lax.rsqrt(jnp.mean(x * x, axis=-1, keepdims=True) + NORM_EPS) * g_ref[...]
    lane = lax.broadcasted_iota(jnp.int32, (rows, LANES), 1)
    u_hi = u.astype(BF16)
    u_lo = (u - u_hi.astype(F32)).astype(BF16)
    hi_part = jnp.dot(u_hi, wr_ref[...], preferred_element_type=F32)
    logits = hi_part[:, :LANES] + hi_part[:, LANES:] + jnp.dot(u_lo, wr_ref[:, :LANES], preferred_element_type=F32)
    logits = jnp.where(lane < N_EXPERTS, logits, -jnp.inf)
    m1 = jnp.max(logits, axis=-1, keepdims=True)
    i1 = jnp.min(jnp.where(logits == m1, lane, LANES), axis=-1, keepdims=True)
    rest = jnp.where(lane == i1, -jnp.inf, logits)
    m2 = jnp.max(rest, axis=-1, keepdims=True)
    i2 = jnp.min(jnp.where(rest == m2, lane, LANES), axis=-1, keepdims=True)
    e2 = jnp.exp(m2 - m1)
    w1 = 1.0 / (1.0 + e2)
    w2 = e2 / (1.0 + e2)
    hit1 = lane == i1
    hit2 = lane == i2
    onehot = jnp.where(hit1 | hit2, 1.0, 0.0)
    seen = jnp.dot(before_ref[...], onehot.astype(BF16), preferred_element_type=F32) + carry_ref[0:1, :]
    r1 = jnp.sum(jnp.where(hit1, seen, 0.0), axis=-1, keepdims=True)
    r2 = jnp.sum(jnp.where(hit2, seen, 0.0), axis=-1, keepdims=True)
    carry_ref[...] = carry_ref[...] + jnp.sum(onehot, axis=0, keepdims=True)
    cols = (i1.astype(F32), i2.astype(F32), w1, w2, r1, r2)
    route = jnp.zeros((rows, LANES), F32)
    for n, c in enumerate(cols):
        route = jnp.where(lane == n, c, route)
    route_ref[...] = route
    counts_ref[...] = carry_ref[...]


def _router(h, g, w_router, *, bm=1024):
    m, d = h.shape
    bm = min(bm, m)
    w = _pad_cols(w_router, LANES)
    w_hi = w.astype(BF16)
    w_split = _cat([w_hi, (w - w_hi.astype(F32)).astype(BF16)], 1)
    return pl.pallas_call(
        _router_kernel,
        grid=(m // bm,),
        in_specs=[pl.BlockSpec((bm, d), lambda i: (i, 0)), pl.BlockSpec((1, d), lambda i: (0, 0)),
                  pl.BlockSpec((d, 2 * LANES), lambda i: (0, 0)), pl.BlockSpec((bm, bm), lambda i: (0, 0))],
        out_specs=[pl.BlockSpec((bm, LANES), lambda i: (i, 0)), pl.BlockSpec((SUBLANES, LANES), lambda i: (0, 0))],
        out_shape=[jax.ShapeDtypeStruct((m, LANES), F32), jax.ShapeDtypeStruct((SUBLANES, LANES), F32)],
        scratch_shapes=[pltpu.VMEM((SUBLANES, LANES), F32)],
        compiler_params=_cparams(("arbitrary",)),
        name="router",
    )(h, g.reshape(1, d), w_split, jnp.tril(jnp.ones((bm, bm), BF16), -1))


def _rms(x, n, g):
    return x * lax.rsqrt(jnp.sum(x * x, axis=-1, keepdims=True) * (1.0 / n) + NORM_EPS) * g


def _mla_prep_kernel(u_ref, cos_ref, sin_ref, win_ref, gq_ref, gkv_ref, gqh_ref, gkh_ref, wq_ref, wqr_ref, wk_ref,
                     wv_ref, q_ref, k_ref, v_ref):
    z = jnp.dot(u_ref[...], win_ref[...], preferred_element_type=F32)
    cq = _rms(z[:, :Q_LORA], Q_LORA, gq_ref[...]).astype(BF16)
    ckv = _rms(z[:, Q_LORA:Q_LORA + KV_LORA], KV_LORA, gkv_ref[...]).astype(BF16)
    k_pe = z[:, Q_LORA + KV_LORA:Q_LORA + KV_LORA + LANES]
    k_pe_rot = z[:, Q_LORA + KV_LORA + LANES:]
    q_all = jnp.dot(cq, wq_ref[...], preferred_element_type=F32)
    q_rot = jnp.dot(cq, wqr_ref[...], preferred_element_type=F32)
    k_all = jnp.dot(ckv, wk_ref[...], preferred_element_type=F32)
    v_all = jnp.dot(ckv, wv_ref[...], preferred_element_type=F32)
    sin = sin_ref[...]
    cos_q = cos_ref[...] * gqh_ref[...]
    cos_k = cos_ref[...] * gkh_ref[...]

    def scale(x):
        return lax.rsqrt(jnp.sum(x * x, axis=-1, keepdims=True) * (1.0 / QK_DIM) + NORM_EPS)

    for h in range(N_HEADS):
        sl = slice(h * LANES, (h + 1) * LANES)
        q = q_all[:, sl]
        k = k_all[:, sl] + k_pe
        q_ref[h] = (scale(q) * (QK_DIM ** -0.5) * (q * cos_q + q_rot[:, sl] * sin)).astype(BF16)
        k_ref[h] = (scale(k) * (k * cos_k + k_pe_rot * sin)).astype(BF16)
        v_ref[h] = v_all[:, sl].astype(BF16)


def _pad_heads(w, width):
    k = w.shape[0]
    w = w.reshape(k, N_HEADS, width)
    return jnp.pad(w, ((0, 0), (0, 0), (0, LANES - width))).reshape(k, N_HEADS * LANES)


def _mla_prep(u, w_in, rope_tabs, b, s, q_norm_g, kv_norm_g, w_uq, w_ukv, qk_g_q, qk_g_k, *, bm=1024):
    bm = min(bm, s)
    half = QK_ROPE // 2
    swap = lambda w: _cat([w[..., half:], w[..., :half]], -1)
    rope_slab = lambda w: jnp.pad(w, ((0, 0), (QK_NOPE, LANES - QK_DIM)))
    lora = Q_LORA + KV_LORA
    w_in = _cat([w_in[:, :lora], rope_slab(w_in[:, lora:]), rope_slab(swap(w_in[:, lora:] * qk_g_k[QK_NOPE:]))],
                1).astype(BF16)
    w_ukv = w_ukv.reshape(KV_LORA, N_HEADS, QK_NOPE + HEAD_DIM)
    w_uq3 = w_uq.reshape(Q_LORA, N_HEADS, QK_DIM)
    wq = _pad_heads(w_uq, QK_DIM).astype(BF16)
    wq_rot = jnp.pad(swap(w_uq3[..., QK_NOPE:] * qk_g_q[QK_NOPE:]), ((0, 0), (0, 0), (QK_NOPE, LANES - QK_DIM)))
    wq_rot = wq_rot.reshape(Q_LORA, N_HEADS * LANES).astype(BF16)
    wk = _pad_heads(w_ukv[..., :QK_NOPE].reshape(KV_LORA, -1), QK_NOPE).astype(BF16)
    wv = _pad_heads(w_ukv[..., QK_NOPE:].reshape(KV_LORA, -1), HEAD_DIM).astype(BF16)
    pad_g = lambda g: jnp.pad(g, (0, LANES - g.shape[0])).reshape(1, LANES)
    row = lambda width: pl.BlockSpec((bm, width), lambda bb, i: (bb * (s // bm) + i, 0))
    full = lambda a: pl.BlockSpec(a.shape, lambda bb, i: (0, 0))
    consts = [w_in, q_norm_g.reshape(1, -1), kv_norm_g.reshape(1, -1), pad_g(qk_g_q), pad_g(qk_g_k), wq, wq_rot,
              wk, wv]
    out_spec = pl.BlockSpec((None, N_HEADS, bm, LANES), lambda bb, i: (bb, 0, i, 0))
    out_shape = jax.ShapeDtypeStruct((b, N_HEADS, s, LANES), BF16)
    return pl.pallas_call(
        _mla_prep_kernel,
        grid=(b, s // bm),
        in_specs=[row(u.shape[1]), row(LANES), row(LANES)] + [full(a) for a in consts],
        out_specs=[out_spec] * 3,
        out_shape=[out_shape] * 3,
        compiler_params=_cparams(("arbitrary", "arbitrary")),
        name="mla_prep",
    )(u, *rope_tabs, *consts)


def _flash_kernel(qi_ref, kj_ref, q_ref, k_ref, v_ref, o_ref, m_ref, l_ref, acc_ref, *, blk, sub):
    n = pl.program_id(2)
    i = qi_ref[n]
    j = kj_ref[n]
    hb = q_ref.shape[0]

    @pl.when(j == 0)
    def _():
        m_ref[...] = jnp.full_like(m_ref, -jnp.inf)
        l_ref[...] = jnp.zeros_like(l_ref)
        acc_ref[...] = jnp.zeros_like(acc_ref)

    def update(masked):
        chains = [(h, r) for h in range(hb) for r in range(blk // sub)]

        def qk(h, r):
            nk = (r + 1) * sub if masked else blk
            s = lax.dot_general(q_ref[h, r * sub:(r + 1) * sub, :], k_ref[h, :nk, :], (((1,), (1,)), ((), ())),
                                preferred_element_type=F32)
            if masked:
                row = lax.broadcasted_iota(jnp.int32, (sub, nk), 0) + r * sub
                s = jnp.where(row >= lax.broadcasted_iota(jnp.int32, (sub, nk), 1), s, -jnp.inf)
            return s

        def softmax_pv(h, r, s):
            rows = slice(r * sub, (r + 1) * sub)
            m_old = m_ref[h, rows, :]
            m_new = jnp.maximum(m_old, jnp.max(s, axis=-1, keepdims=True))
            alpha = jnp.exp(m_old - m_new)
            p = jnp.exp(s - m_new[:, :1])
            l_ref[h, rows, :] = alpha * l_ref[h, rows, :] + jnp.sum(p, axis=-1, keepdims=True)
            m_ref[h, rows, :] = m_new
            pv = jnp.dot(p.astype(BF16), v_ref[h, :s.shape[1], :], preferred_element_type=F32)
            acc_ref[h, rows, :] = alpha * acc_ref[h, rows, :] + pv

        s_prev = qk(*chains[0])
        for c in range(len(chains)):
            s_next = qk(*chains[c + 1]) if c + 1 < len(chains) else None
            softmax_pv(*chains[c], s_prev)
            s_prev = s_next

    @pl.when(j < i)
    def _():
        update(False)

    @pl.when(j == i)
    def _():
        update(True)
        o_ref[...] = (acc_ref[...] / l_ref[...]).astype(o_ref.dtype)


def _flash_attention(q, k, v, *, blk=1024, sub=256, hb=4):
    b, h, s, d = q.shape
    blk = min(blk, s)
    sub = min(sub, blk)
    nb = s // blk
    qi = np.array([i for i in range(nb) for j in range(i + 1)], np.int32)
    kj = np.array([j for i in range(nb) for j in range(i + 1)], np.int32)
    q_spec = pl.BlockSpec((None, hb, blk, d), lambda bb, hh, n, qi_r, kj_r: (bb, hh, qi_r[n], 0))
    kv_spec = pl.BlockSpec((None, hb, blk, d), lambda bb, hh, n, qi_r, kj_r: (bb, hh, kj_r[n], 0))
    return pl.pallas_call(
        functools.partial(_flash_kernel, blk=blk, sub=sub),
        grid_spec=pltpu.PrefetchScalarGridSpec(
            num_scalar_prefetch=2,
            grid=(b, h // hb, len(qi)),
            in_specs=[q_spec, kv_spec, kv_spec],
            out_specs=q_spec,
            scratch_shapes=[pltpu.VMEM((hb, blk, LANES), F32), pltpu.VMEM((hb, blk, LANES), F32),
                            pltpu.VMEM((hb, blk, d), F32)],
        ),
        out_shape=jax.ShapeDtypeStruct((b, h, s, d), BF16),
        compiler_params=_cparams(("arbitrary", "arbitrary", "arbitrary")),
        name="flash",
    )(jnp.asarray(qi), jnp.asarray(kj), q, k, v)


def _dot(a, b):
    return jnp.dot(a.astype(BF16), b.astype(BF16), preferred_element_type=F32)


def _dot_nt(a, b):
    return lax.dot_general(a.astype(BF16), b.astype(BF16), (((1,), (1,)), ((), ())), preferred_element_type=F32)


def _dot_tn(a, b):
    return jnp.dot(a.T.astype(BF16), b.astype(BF16), preferred_element_type=F32)


def _split3(x):
    hi = x.astype(BF16)
    r1 = x - hi.astype(F32)
    mid = r1.astype(BF16)
    return hi, mid, (r1 - mid.astype(F32)).astype(BF16)


def _cumsum_rows(tri, x):
    s = jnp.dot(tri, jnp.concatenate(_split3(x), axis=1), preferred_element_type=F32)
    n = x.shape[1]
    return s[:, :n] + s[:, n:2 * n] + s[:, 2 * n:]


def _head_sum(x):
    left = lax.broadcasted_iota(jnp.int32, (x.shape[0], LANES), 1) < HEAD_DIM
    outs = []
    for p in range(WIDTH // LANES):
        xs = x[:, p * LANES:(p + 1) * LANES]
        s_left = jnp.sum(jnp.where(left, xs, 0.0), axis=-1, keepdims=True)
        s_right = jnp.sum(jnp.where(left, 0.0, xs), axis=-1, keepdims=True)
        outs.append(jnp.where(left, s_left, s_right))
    return jnp.concatenate(outs, axis=1)


def _shift_rows(x, tail, j):
    rolled = pltpu.roll(x, j, 0)
    top = jnp.where(lax.broadcasted_iota(jnp.int32, (SUBLANES, x.shape[1]), 0) < j,
                    pltpu.roll(tail, j, 0), rolled[:SUBLANES])
    return jnp.concatenate([top, rolled[SUBLANES:]], axis=0)


def _softplus(x):
    return jnp.maximum(x, 0.0) + jnp.log(1.0 + jnp.exp(-jnp.abs(x)))


def _pair_masks():
    lane = lax.broadcasted_iota(jnp.int32, (CHUNK, LANES), 1)
    m0 = (lane < HEAD_DIM).astype(F32)
    return m0, 1.0 - m0


def _ext(x, m0, m1, dtype=BF16):
    return jnp.concatenate([x * m0, x * m1], axis=0).astype(dtype)


def _cat(xs, axis):
    return jnp.concatenate(xs, axis=axis)


SCAN_GROUP = 2


def _neumann_inverse(mats):
    n = mats[0].shape[0]
    eye = (lax.broadcasted_iota(jnp.int32, (n, n), 0) == lax.broadcasted_iota(jnp.int32, (n, n), 1)).astype(F32)
    ts = [eye + a for a in mats]
    ps = [_dot(a, a) for a in mats]
    yield
    for _ in range(int(math.log2(CHUNK)) - 2):
        pps = [_dot(_cat([p, t], 0), p) for p, t in zip(ps, ts)]
        ps = [pp[:n] for pp in pps]
        ts = [t + pp[n:] for t, pp in zip(ts, pps)]
        yield
    return [t + _dot(t, p) for t, p in zip(ts, ps)]


def _interleave(prepare, serial, n_groups):
    for _ in prepare(0):
        pass
    for g in range(n_groups):
        gens = [serial(g)] + ([prepare(g + 1)] if g + 1 < n_groups else [])
        while gens:
            for gen in list(gens):
                if next(gen, StopIteration) is StopIteration:
                    gens.remove(gen)


def _work_items(rows):
    return [(slice(c * CHUNK, (c + 1) * CHUNK), slice(p * LANES, (p + 1) * LANES))
            for c in range(rows // CHUNK) for p in range(WIDTH // LANES)]


def _tiles(x, items):
    return [x[rows, cols] for rows, cols in items]


def _rwkv_scan(r, k, v, a, b, ld, st_ref, y_ref):
    c2 = 2 * CHUNK
    strict = lax.broadcasted_iota(jnp.int32, (c2, c2), 0) > lax.broadcasted_iota(jnp.int32, (c2, c2), 1)
    trow = lax.broadcasted_iota(jnp.int32, (CHUNK, c2), 0)
    tcol = lax.broadcasted_iota(jnp.int32, (CHUNK, c2), 1) & (CHUNK - 1)
    incl_w = tcol <= trow
    tri = (lax.broadcasted_iota(jnp.int32, (CHUNK, CHUNK), 0)
           >= lax.broadcasted_iota(jnp.int32, (CHUNK, CHUNK), 1)).astype(BF16)
    m0, m1 = _pair_masks()
    items = _work_items(r.shape[0])
    n_pairs = WIDTH // LANES
    per_group = min(SCAN_GROUP, len(items) // n_pairs) * n_pairs
    full = (r, k, v, a, b, ld)
    ready = {}
    state = [st_ref[p] for p in range(n_pairs)]

    def prepare(g):
        sub = items[g * per_group:(g + 1) * per_group]
        r, k, v, a, b, ld = (_tiles(x, sub) for x in full)
        cum = [_cumsum_rows(tri, x) for x in ld]
        yield
        e_pos = [jnp.exp(x) for x in cum]
        e_neg = [jnp.exp(-x) for x in cum]
        e_end = [jnp.exp(x[CHUNK - 1:CHUNK, :] - x) for x in cum]
        a_ext = [_ext(ai * jnp.exp(ci - li), m0, m1) for ai, ci, li in zip(a, cum, ld)]
        v_ext = [_ext(x, m0, m1) for x in v]
        r_dec = [(ri * ei).astype(BF16) for ri, ei in zip(r, e_pos)]
        yield
        s1 = [_dot_nt(_cat([ae, rd], 0), _cat([_ext(bi * en, m0, m1), _ext(ki * en, m0, m1)], 0))
              for ae, rd, bi, ki, en in zip(a_ext, r_dec, b, k, e_neg)]
        yield
        a_ab = [jnp.where(strict, s[:c2, :c2], 0.0) for s in s1]
        a_ak = [jnp.where(strict, s[:c2, c2:], 0.0) for s in s1]
        a_rbk = [_cat([jnp.where(incl_w, s[c2:, :c2], 0.0), jnp.where(incl_w, s[c2:, c2:], 0.0)], 1).astype(BF16)
                 for s in s1]
        akv = [_dot(x, ve) for x, ve in zip(a_ak, v_ext)]
        t_inv = yield from _neumann_inverse(a_ab)
        yield
        wu0 = [_dot(t, _cat([ae, av.astype(BF16)], 1)) for t, ae, av in zip(t_inv, a_ext, akv)]
        bk_end = [_cat([_ext(bi * ee, m0, m1, F32), _ext(ki * ee, m0, m1, F32)], 0)
                  for bi, ki, ee in zip(b, k, e_end)]
        e_last_t = [jnp.broadcast_to(e[CHUNK - 1:CHUNK, :], (LANES, LANES)).T for e in e_pos]
        ready[g] = (sub, wu0, r_dec, v_ext, a_rbk, bk_end, e_last_t)

    def serial(g):
        sub, wu0, r_dec, v_ext, a_rbk, bk_end, e_last_t = ready.pop(g)
        for c in range(len(sub) // n_pairs):
            idx = range(c * n_pairs, (c + 1) * n_pairs)
            z = [_dot(_cat([wu0[i][:, :LANES].astype(BF16), r_dec[i]], 0), s) for i, s in zip(idx, state)]
            yield
            uv = [_cat([(wu0[i][:, LANES:] + zi[:c2]).astype(BF16), v_ext[i]], 0) for i, zi in zip(idx, z)]
            for i, zi, uvi in zip(idx, z, uv):
                rows, cols = sub[i]
                y_ref[rows, cols] = zi[c2:] + _dot(a_rbk[i], uvi)
            state[:] = [s * e_last_t[i] + _dot_tn(bk_end[i], uvi) for i, s, uvi in zip(idx, state, uv)]
            yield

    _interleave(prepare, serial, len(items) // per_group)
    for p in range(n_pairs):
        st_ref[p] = state[p]


def _gdn_scan(q, k, v, g, beta, st_ref, o_ref):
    c2 = 2 * CHUNK
    row = lax.broadcasted_iota(jnp.int32, (c2, c2), 0)
    col = lax.broadcasted_iota(jnp.int32, (c2, c2), 1)
    causal = ((row >= CHUNK) == (col >= CHUNK)) & (row >= col)
    strict = row > col
    tri = (lax.broadcasted_iota(jnp.int32, (CHUNK, CHUNK), 0)
           >= lax.broadcasted_iota(jnp.int32, (CHUNK, CHUNK), 1)).astype(BF16)
    m0, m1 = _pair_masks()
    items = _work_items(q.shape[0])
    n_pairs = WIDTH // LANES
    per_group = min(SCAN_GROUP, len(items) // n_pairs) * n_pairs
    full = (q, k, v, g, beta)
    ready = {}
    state = [st_ref[p] for p in range(n_pairs)]

    def prepare(grp):
        sub = items[grp * per_group:(grp + 1) * per_group]
        q, k, v, g, beta = (_tiles(x, sub) for x in full)
        gc = [_cumsum_rows(tri, x) for x in g]
        yield
        g_rows = [_cat([jnp.broadcast_to(x[:, 0:1], (CHUNK, LANES)),
                        jnp.broadcast_to(x[:, HEAD_DIM:HEAD_DIM + 1], (CHUNK, LANES))], 0) for x in gc]
        gamma = [jnp.exp(jnp.where(causal, x - x.T, -jnp.inf)) for x in g_rows]
        e_pos = [jnp.exp(x) for x in gc]
        kb = [ki * bi for ki, bi in zip(k, beta)]
        yield
        s1 = [_dot_nt(_cat([_ext(kbi, m0, m1), qi.astype(BF16)], 0), _ext(ki, m0, m1))
              for kbi, qi, ki in zip(kb, q, k)]
        yield
        a_neg = [-jnp.where(strict, s[:c2] * ga, 0.0) for s, ga in zip(s1, gamma)]
        a_in = [(s[c2:] * (ga[:CHUNK] + ga[CHUNK:])).astype(BF16) for s, ga in zip(s1, gamma)]
        t_inv = yield from _neumann_inverse(a_neg)
        yield
        uw = [_dot(t, _cat([_ext(vi * bi, m0, m1), _ext(kbi * ei, m0, m1)], 1))
              for t, vi, bi, kbi, ei in zip(t_inv, v, beta, kb, e_pos)]
        q_dec = [(qi * ei).astype(BF16) for qi, ei in zip(q, e_pos)]
        k_dec = [_ext(ki * jnp.exp(x[CHUNK - 1:CHUNK, :] - x), m0, m1, F32) for ki, x in zip(k, gc)]
        ready[grp] = (sub, uw, q_dec, k_dec, a_in, e_pos)

    def serial(grp):
        sub, uw, q_dec, k_dec, a_in, e_pos = ready.pop(grp)
        for c in range(len(sub) // n_pairs):
            idx = range(c * n_pairs, (c + 1) * n_pairs)
            z = [_dot(_cat([uw[i][:, LANES:].astype(BF16), q_dec[i]], 0), s) for i, s in zip(idx, state)]
            yield
            v_new = [(uw[i][:, :LANES] - zi[:c2]).astype(BF16) for i, zi in zip(idx, z)]
            for i, zi, vn in zip(idx, z, v_new):
                rows, cols = sub[i]
                o_ref[rows, cols] = zi[c2:] + _dot(a_in[i], vn)
            state[:] = [s * e_pos[i][CHUNK - 1:CHUNK, :] + _dot_tn(k_dec[i], vn)
                        for i, s, vn in zip(idx, state, v_new)]
            yield

    _interleave(prepare, serial, len(items) // per_group)
    for p in range(n_pairs):
        st_ref[p] = state[p]


def _rwkv_kernel(*refs, vres):
    if vres:
        (u_ref, vf_ref, win_ref, mu_ref, prm_ref, wup_ref, aup_ref, gup_ref, vup_ref, o_ref, st_ref, tail_ref,
         y_ref) = refs
    else:
        u_ref, win_ref, mu_ref, prm_ref, wup_ref, aup_ref, gup_ref, o_ref, vf_out_ref, st_ref, tail_ref, y_ref = refs

    @pl.when(pl.program_id(1) == 0)
    def _():
        st_ref[...] = jnp.zeros_like(st_ref)
        tail_ref[...] = jnp.zeros_like(tail_ref)

    z = jnp.dot(u_ref[...], win_ref[...], preferred_element_type=F32)
    rows = z.shape[0]
    zs = _shift_rows(z, tail_ref[...], 1)
    tail_ref[...] = z[rows - SUBLANES:]
    zl = z[:, :RWKV_IN] + (zs[:, :RWKV_IN] - z[:, :RWKV_IN]) * mu_ref[...]
    r, k, v = zl[:, :WIDTH], zl[:, WIDTH:2 * WIDTH], zl[:, 2 * WIDTH:3 * WIDTH]
    lo = zl[:, 3 * WIDTH:3 * WIDTH + LANES]
    g_lo = zl[:, 3 * WIDTH + LANES:]
    w0, a0, k_k, k_a, ln_g, ln_b, r_k, v_bias = (prm_ref[i:i + 1, :] for i in range(8))
    log_w = -_softplus(-(w0 + _dot(jnp.tanh(lo), wup_ref[...]))) - 0.5
    ld = -jnp.exp(log_w)
    iclr = jax.nn.sigmoid(a0 + _dot(lo, aup_ref[...]))
    gate = _dot(jax.nn.sigmoid(g_lo), gup_ref[...])
    if vres:
        x = z[:, RWKV_IN:] + pltpu.roll(zs[:, RWKV_IN:], LANES - VRES_LORA, 1)
        v = v + (vf_ref[...] - v) * jax.nn.sigmoid(v_bias + _dot(x, vup_ref[...]))
    else:
        vf_out_ref[...] = v
    kk = k * k_k
    kk = kk * lax.rsqrt(_head_sum(kk * kk) + L2_EPS)
    k = k * (1.0 + (iclr - 1.0) * k_a)
    _rwkv_scan(r, k, v, -kk, kk * iclr, ld, st_ref, y_ref)
    y = y_ref[...]
    d = y - _head_sum(y) * (1.0 / HEAD_DIM)
    y = d * lax.rsqrt(_head_sum(d * d) * (1.0 / HEAD_DIM) + RWKV_GN_EPS) * ln_g + ln_b
    y = y + _head_sum(r * k * r_k) * v
    o_ref[...] = (y * gate).astype(o_ref.dtype)


def _rwkv(u, w_in, v_first, b, s, mu, prm, w_up, a_up, g_up, v_up):
    vres = v_first is not None
    rows = min(SCAN_ROWS, s)
    zw = w_in.shape[1]
    row = lambda width: pl.BlockSpec((rows, width), lambda bb, c: (bb * (s // rows) + c, 0))
    full = lambda a: pl.BlockSpec(a.shape, lambda bb, c: (0, 0))
    zero = jnp.zeros((LANES // 2, WIDTH), F32)
    consts = [w_in.astype(BF16), mu.reshape(1, -1), prm, _cat([w_up, zero], 0).astype(BF16),
              _cat([zero, a_up], 0).astype(BF16), g_up.astype(BF16)]
    args, in_specs = [u], [row(u.shape[1])]
    if vres:
        args.append(v_first)
        in_specs.append(row(WIDTH))
        consts.append(jnp.pad(v_up, ((0, LANES - VRES_LORA), (0, 0))).astype(BF16))
    out_shape = [jax.ShapeDtypeStruct((b * s, WIDTH), BF16)]
    out_specs = [row(WIDTH)]
    if not vres:
        out_shape.append(jax.ShapeDtypeStruct((b * s, WIDTH), F32))
        out_specs.append(row(WIDTH))
    outs = pl.pallas_call(
        functools.partial(_rwkv_kernel, vres=vres),
        grid=(b, s // rows),
        in_specs=in_specs + [full(a) for a in consts],
        out_specs=out_specs,
        out_shape=out_shape,
        scratch_shapes=[pltpu.VMEM((WIDTH // LANES, LANES, LANES), F32), pltpu.VMEM((SUBLANES, zw), F32),
                        pltpu.VMEM((rows, WIDTH), F32)],
        compiler_params=_cparams(("arbitrary", "arbitrary")),
        name="rwkv",
    )(*args, *consts)
    return (outs[0], v_first) if vres else (outs[0], outs[1])


def _gdn_kernel(u_ref, win_ref, cw_ref, prm_ref, exp_ref, o_ref, st_ref, tail_ref, y_ref):
    @pl.when(pl.program_id(1) == 0)
    def _():
        st_ref[...] = jnp.zeros_like(st_ref)
        tail_ref[...] = jnp.zeros_like(tail_ref)

    z = jnp.dot(u_ref[...], win_ref[...], preferred_element_type=F32)
    rows = z.shape[0]
    x = z[:, :GDN_QKV]
    tail = tail_ref[...]
    conv = x * cw_ref[GDN_CONV - 1:GDN_CONV, :]
    for j in range(1, GDN_CONV):
        conv = conv + _shift_rows(x, tail, j) * cw_ref[GDN_CONV - 1 - j:GDN_CONV - j, :]
    tail_ref[...] = x[rows - SUBLANES:]
    qkv = conv * jax.nn.sigmoid(conv)
    q, k, v = qkv[:, :WIDTH], qkv[:, WIDTH:2 * WIDTH], qkv[:, 2 * WIDTH:]
    q = q * lax.rsqrt(_head_sum(q * q) + L2_EPS) * (HEAD_DIM ** -0.5)
    k = k * lax.rsqrt(_head_sum(k * k) + L2_EPS)
    logits = jnp.dot(_cat(_split3(z[:, GDN_QKV + WIDTH:]), 0), exp_ref[...], preferred_element_type=F32)
    logits = logits[:rows] + logits[rows:2 * rows] + logits[2 * rows:]
    neg_a, dt_bias, norm_g = (prm_ref[i:i + 1, :] for i in range(3))
    beta = jax.nn.sigmoid(logits[:, :WIDTH])
    g = neg_a * _softplus(logits[:, WIDTH:] + dt_bias)
    _gdn_scan(q, k, v, g, beta, st_ref, y_ref)
    o = y_ref[...]
    o = o * lax.rsqrt(_head_sum(o * o) * (1.0 / HEAD_DIM) + NORM_EPS) * norm_g
    gate = z[:, GDN_QKV:GDN_QKV + WIDTH]
    o_ref[...] = (o * (gate * jax.nn.sigmoid(gate))).astype(o_ref.dtype)


def _gdn(u, w_in, b, s, conv_w, a_log, dt_bias, norm_g):
    rows = min(SCAN_ROWS, s)
    row = lambda width: pl.BlockSpec((rows, width), lambda bb, c: (bb * (s // rows) + c, 0))
    full = lambda a: pl.BlockSpec(a.shape, lambda bb, c: (0, 0))
    per_lane = lambda t: jnp.repeat(t, HEAD_DIM)
    prm = jnp.stack([per_lane(-jnp.exp(a_log)), per_lane(dt_bias), jnp.tile(norm_g, N_HEADS)])
    head_of_lane = np.arange(WIDTH) // HEAD_DIM
    expand = np.zeros((LANES, 2 * WIDTH), np.float32)
    expand[head_of_lane, np.arange(WIDTH)] = 1.0
    expand[N_HEADS + head_of_lane, WIDTH + np.arange(WIDTH)] = 1.0
    consts = [w_in.astype(BF16), conv_w, prm, jnp.asarray(expand, BF16)]
    return pl.pallas_call(
        _gdn_kernel,
        grid=(b, s // rows),
        in_specs=[row(u.shape[1])] + [full(a) for a in consts],
        out_specs=row(WIDTH),
        out_shape=jax.ShapeDtypeStruct((b * s, WIDTH), BF16),
        scratch_shapes=[pltpu.VMEM((WIDTH // LANES, LANES, LANES), F32), pltpu.VMEM((SUBLANES, GDN_QKV), F32),
                        pltpu.VMEM((rows, WIDTH), F32)],
        compiler_params=_cparams(("arbitrary", "arbitrary")),
        name="gdn",
    )(u, *consts)


def _merge_kernel(h_ref, u_ref, oa_ref, ob_ref, oc_ref, wg_ref, wa_ref, wb_ref, wc_ref, wo_ref, out_ref):
    d = h_ref.shape[1]
    u = u_ref[...]
    ob = jnp.concatenate([ob_ref[h] for h in range(N_HEADS)], axis=1)
    merged = jnp.zeros(h_ref.shape, F32)
    for n, (o, w_ref) in enumerate(((oa_ref[...], wa_ref), (ob, wb_ref), (oc_ref[...], wc_ref))):
        proj = jnp.dot(o, w_ref[...], preferred_element_type=F32)
        zg = jnp.dot(u, wg_ref[:, n * d:(n + 1) * d], preferred_element_type=F32)
        merged = merged + jax.nn.sigmoid(zg) * proj
    out_ref[...] = h_ref[...] + jnp.dot(merged.astype(BF16), wo_ref[...], preferred_element_type=F32)


def _merge(h, u, oa, ob, oc, wg, wa, wb, wc, wo, s, *, bm=512):
    m, d = h.shape
    bm = min(bm, s)
    row = lambda width: pl.BlockSpec((bm, width), lambda i: (i, 0))
    full = lambda w: pl.BlockSpec(w.shape, lambda i: (0, 0))
    wb = jnp.pad(wb.reshape(N_HEADS, HEAD_DIM, d), ((0, 0), (0, LANES - HEAD_DIM), (0, 0))).reshape(N_HEADS * LANES, d)
    ws = [w.astype(BF16) for w in (wg, wa, wb, wc, wo)]
    ob_spec = pl.BlockSpec((None, N_HEADS, bm, LANES), lambda i: (i // (s // bm), 0, i % (s // bm), 0))
    return pl.pallas_call(
        _merge_kernel,
        grid=(m // bm,),
        in_specs=[row(d), row(d), row(WIDTH), ob_spec, row(WIDTH)] + [full(w) for w in ws],
        out_specs=row(d),
        out_shape=jax.ShapeDtypeStruct((m, d), F32),
        compiler_params=_cparams(("arbitrary",)),
        name="merge",
    )(h, u, oa, ob, oc, *ws)


def _ffn_kernel(be_ref, nu_ref, x_ref, g_ref, wg_ref, wu_ref, wd_ref, o_ref, xs_ref, acc_ref, *, nf, dense):
    i = pl.program_id(0)
    f = pl.program_id(1)
    used = i < nu_ref[0]

    @pl.when(used & (f == 0))
    def _():
        x = x_ref[...]
        ms = jnp.mean(x * x, axis=-1, keepdims=True)
        xs_ref[...] = (x * lax.rsqrt(ms + NORM_EPS) * g_ref[...]).astype(BF16)

    @pl.when(f == 0)
    def _():
        acc_ref[...] = jnp.zeros_like(acc_ref)

    @pl.when(used)
    def _():
        xs = xs_ref[...]
        gate = jnp.dot(xs, wg_ref[...].astype(BF16), preferred_element_type=F32)
        up = jnp.dot(xs, wu_ref[...].astype(BF16), preferred_element_type=F32)
        act = (gate * jax.nn.sigmoid(gate) * up).astype(BF16)
        acc_ref[...] += jnp.dot(act, wd_ref[...].astype(BF16), preferred_element_type=F32)

    @pl.when(f == nf - 1)
    def _():
        o_ref[...] = (x_ref[...] + acc_ref[...]) if dense else acc_ref[...]


def _ffn(x, g, wg, wu, wd, block_e, n_used, *, bm, tf, dense):
    r, d = x.shape
    ff = wg.shape[2]
    nf = ff // tf

    def ff_idx(i, f, be, nu):
        return jnp.where(i < nu[0], f, nf - 1)

    return pl.pallas_call(
        functools.partial(_ffn_kernel, nf=nf, dense=dense),
        grid_spec=pltpu.PrefetchScalarGridSpec(
            num_scalar_prefetch=2,
            grid=(r // bm, nf),
            in_specs=[
                pl.BlockSpec((bm, d), lambda i, f, be, nu: (i, 0)),
                pl.BlockSpec((1, d), lambda i, f, be, nu: (0, 0)),
                pl.BlockSpec((None, d, tf), lambda i, f, be, nu: (be[i], 0, ff_idx(i, f, be, nu))),
                pl.BlockSpec((None, d, tf), lambda i, f, be, nu: (be[i], 0, ff_idx(i, f, be, nu))),
                pl.BlockSpec((None, tf, d), lambda i, f, be, nu: (be[i], ff_idx(i, f, be, nu), 0)),
            ],
            out_specs=pl.BlockSpec((bm, d), lambda i, f, be, nu: (i, 0)),
            scratch_shapes=[pltpu.VMEM((bm, d), BF16), pltpu.VMEM((bm, d), F32)],
        ),
        out_shape=jax.ShapeDtypeStruct((r, d), F32),
        compiler_params=_cparams(("arbitrary", "arbitrary")),
        name="ffn",
    )(block_e, n_used, x, g.reshape(1, d).astype(F32), wg, wu, wd)


def _ple_kernel(*refs, combine, emit_norm):
    refs = list(refs)
    h = refs.pop(0)[...]
    if combine:
        y0_ref, y1_ref, route_ref = refs[:3]
        del refs[:3]
        h = h + route_ref[:, 2:3] * y0_ref[...] + route_ref[:, 3:4] * y1_ref[...]
    p_ref, wg_ref, wp_ref, g_ref = refs[:4]
    gate = jnp.dot(h.astype(BF16), wg_ref[...], preferred_element_type=F32)
    e = jnp.dot(p_ref[...].astype(BF16), wp_ref[...], preferred_element_type=F32)
    ms = jnp.mean(e * e, axis=-1, keepdims=True)
    e = e * lax.rsqrt(ms + NORM_EPS) * g_ref[...]
    out = h + jax.nn.sigmoid(gate) * e
    if emit_norm:
        gn_ref, o_ref, u_ref = refs[4:]
        u_ref[...] = (out * lax.rsqrt(jnp.mean(out * out, axis=-1, keepdims=True) + NORM_EPS) * gn_ref[...]).astype(BF16)
    else:
        o_ref, = refs[4:]
    o_ref[...] = out


def _ple(h, p, w_gate, w_proj, g, expert_out=None, g_next=None, *, bm=512):
    m, d = h.shape
    bm = min(bm, m)
    row = lambda a: pl.BlockSpec((bm, a.shape[1]), lambda i: (i, 0))
    full = lambda a: pl.BlockSpec(a.shape, lambda i: (0, 0))
    rows = [h] + list(expert_out or ()) + [p]
    consts = [w_gate.astype(BF16), w_proj.astype(BF16), g.reshape(1, d)]
    out_shape = [jax.ShapeDtypeStruct((m, d), F32)]
    if g_next is not None:
        consts.append(g_next.reshape(1, d))
        out_shape.append(jax.ShapeDtypeStruct((m, d), BF16))
    outs = pl.pallas_call(
        functools.partial(_ple_kernel, combine=expert_out is not None, emit_norm=g_next is not None),
        grid=(m // bm,),
        in_specs=[row(a) for a in rows] + [full(a) for a in consts],
        out_specs=[row(h)] * len(out_shape),
        out_shape=out_shape,
        compiler_params=_cparams(("arbitrary",)),
        name="ple",
    )(*rows, *consts)
    return outs if g_next is not None else (outs[0], None)


def _pad_cols(w, n):
    return jnp.pad(w, ((0, 0), (0, n - w.shape[1])))


def _rope_tables(positions):
    t = positions.size
    inv_freq = 1.0 / (ROPE_THETA ** (jnp.arange(0, QK_ROPE, 2, dtype=F32) / QK_ROPE))
    ang = positions.astype(F32).reshape(t, 1) * inv_freq
    cos, sin = jnp.cos(ang), jnp.sin(ang)
    pad = ((0, 0), (QK_NOPE, LANES - QK_DIM))
    return jnp.pad(_cat([cos, cos], 1), pad, constant_values=1.0), jnp.pad(_cat([-sin, sin], 1), pad)


def _moe(h, g, w_router, wg, wu, wd, *, bm=1024, tf=512):
    t, d = h.shape
    n_assign = t * TOP_K
    n_blocks = -(-(n_assign + N_EXPERTS * (bm - 1)) // bm)
    route, counts = _router(h, g, w_router)
    counts = counts[0, :N_EXPERTS].astype(jnp.int32)
    padded = (counts + bm - 1) // bm * bm
    pad_end = jnp.cumsum(padded)
    top_e = route[:, :TOP_K].astype(jnp.int32)
    dest = (pad_end - padded)[top_e] + route[:, 4:4 + TOP_K].astype(jnp.int32)
    src_tok = (jnp.arange(n_blocks * bm, dtype=jnp.int32) % t).at[dest.reshape(-1)].set(
        jnp.arange(n_assign, dtype=jnp.int32) // TOP_K)
    block_start = jnp.arange(n_blocks, dtype=jnp.int32) * bm
    block_e = jnp.minimum(jnp.sum(pad_end[None, :] <= block_start[:, None], axis=1), N_EXPERTS - 1).astype(jnp.int32)
    n_used = (pad_end[-1] // bm).astype(jnp.int32).reshape(1)
    y_rows = _ffn(h[src_tok], g, wg, wu, wd, block_e, n_used, bm=bm, tf=tf, dense=False)
    return y_rows[dest[:, 0]], y_rows[dest[:, 1]], route


def kernel(x, p, positions, norm_mix_g, w_in, rwkv_mu, rwkv_w0, rwkv_w_up, rwkv_a0, rwkv_a_up, rwkv_g_up, rwkv_k_k, rwkv_k_a, rwkv_r_k, rwkv_ln_g, rwkv_ln_b, vres_mu, vres_down, vres_up, vres_b, mla_q_norm_g, mla_kv_norm_g, mla_w_uq, mla_w_ukv, mla_qk_norm_q, mla_qk_norm_k, gdn_conv_w, gdn_a_log, gdn_dt_bias, gdn_norm_g, w_br_rwkv, w_br_mla, w_br_gdn, w_out, norm_ffn_g, ffn_wg, ffn_wu, ffn_wd, moe_router, moe_wg, moe_wu, moe_wd, ple_proj, ple_gate, ple_norm_g):
    b, s, d = x.shape
    t = b * s
    depth = w_in.shape[0]
    rope_tabs = _rope_tables(positions)

    h = x.reshape(t, d)
    v_first = None
    for i in range(depth):
        w = w_in[i]
        g = norm_mix_g[i]
        w_rwkv = w[:, :RWKV_IN]
        if i > 0:
            mu_v = vres_mu[i - 1][:, None]
            vd = vres_down[i - 1]
            w_rwkv = _pad_cols(_cat([w_rwkv, (1.0 - mu_v) * vd, mu_v * vd], 1), RWKV_IN + LANES)
        w_gdn = w[:, RWKV_IN + MLA_IN:RWKV_IN + MLA_IN + GDN_IN]
        w_gdn = _pad_cols(_cat([w_gdn[:, :GDN_QKV], w_gdn[:, GDN_QKV + 2 * N_HEADS:],
                                w_gdn[:, GDN_QKV:GDN_QKV + 2 * N_HEADS]], 1), GDN_QKV + WIDTH + LANES)
        if i == 0:
            u = _norm(h, g)
        prm = jnp.stack([rwkv_w0[i], rwkv_a0[i], rwkv_k_k[i], rwkv_k_a[i], rwkv_ln_g[i], rwkv_ln_b[i],
                         rwkv_r_k[i].reshape(-1), vres_b[i - 1] if i > 0 else jnp.zeros((WIDTH,), F32)])
        o_a, v_first = _rwkv(u, w_rwkv, v_first, b, s, rwkv_mu[i], prm, rwkv_w_up[i], rwkv_a_up[i], rwkv_g_up[i],
                             vres_up[i - 1] if i > 0 else None)
        q, k, v = _mla_prep(u, w[:, RWKV_IN:RWKV_IN + MLA_IN], rope_tabs, b, s, mla_q_norm_g[i], mla_kv_norm_g[i],
                            mla_w_uq[i], mla_w_ukv[i], mla_qk_norm_q[i], mla_qk_norm_k[i])
        o_b = _flash_attention(q, k, v)
        o_c = _gdn(u, w_gdn, b, s, gdn_conv_w[i], gdn_a_log[i], gdn_dt_bias[i], gdn_norm_g[i])
        h = _merge(h, u, o_a, o_b, o_c, w[:, RWKV_IN + MLA_IN + GDN_IN:], w_br_rwkv[i], w_br_mla[i], w_br_gdn[i],
                   w_out[i], s)
        j = i // 2
        if i % 2 == 0:
            bm = min(1024, t)
            h = _ffn(h, norm_ffn_g[i], ffn_wg[j:j + 1], ffn_wu[j:j + 1], ffn_wd[j:j + 1],
                     jnp.zeros((t // bm,), jnp.int32), jnp.full((1,), t // bm, jnp.int32), bm=bm, tf=256, dense=True)
            expert_out = None
        else:
            expert_out = _moe(h, norm_ffn_g[i], moe_router[j], moe_wg[j], moe_wu[j], moe_wd[j])
        h, u = _ple(h, p[i].reshape(t, -1), ple_gate[i], ple_proj[i], ple_norm_g[i], expert_out,
                    norm_mix_g[i + 1] if i + 1 < depth else None)
    return h.reshape(b, s, d)
```

```python
import functools
import math

import jax
import jax.numpy as jnp
import numpy as np
from jax import lax
from jax.experimental import pallas as pl
from jax.experimental.pallas import tpu as pltpu

F32 = jnp.float32
BF16 = jnp.bfloat16

NORM_EPS = 1e-6
RWKV_GN_EPS = 64e-5
L2_EPS = 1e-12
N_HEADS = 8
HEAD_DIM = 64
WIDTH = N_HEADS * HEAD_DIM
RWKV_LORA = 256
RWKV_IN = 3 * WIDTH + RWKV_LORA
VRES_LORA = 32
QK_NOPE = 64
QK_ROPE = 32
QK_DIM = QK_NOPE + QK_ROPE
Q_LORA = 256
KV_LORA = 128
MLA_IN = Q_LORA + KV_LORA + QK_ROPE
ROPE_THETA = 10000.0
GDN_CONV = 4
GDN_QKV = 3 * WIDTH
GDN_IN = GDN_QKV + 2 * N_HEADS + WIDTH
N_EXPERTS = 8
TOP_K = 2
LANES = 128
SUBLANES = 8
CHUNK = 64
SCAN_ROWS = 8 * CHUNK
VMEM_LIMIT = 48 * 1024 * 1024


def _cparams(sem):
    return pltpu.CompilerParams(dimension_semantics=sem, vmem_limit_bytes=VMEM_LIMIT)


def _norm_kernel(x_ref, g_ref, u_ref):
    x = x_ref[...]
    u_ref[...] = (x * lax.rsqrt(jnp.mean(x * x, axis=-1, keepdims=True) + NORM_EPS) * g_ref[...]).astype(u_ref.dtype)


def _norm(x, g, *, bm=1024):
    m, d = x.shape
    bm = min(bm, m)
    return pl.pallas_call(
        _norm_kernel,
        grid=(m // bm,),
        in_specs=[pl.BlockSpec((bm, d), lambda i: (i, 0)), pl.BlockSpec((1, d), lambda i: (0, 0))],
        out_specs=pl.BlockSpec((bm, d), lambda i: (i, 0)),
        out_shape=jax.ShapeDtypeStruct((m, d), BF16),
        compiler_params=_cparams(("arbitrary",)),
        name="norm",
    )(x, g.reshape(1, d))


def _router_kernel(x_ref, g_ref, wr_ref, before_ref, route_ref, counts_ref, carry_ref):
    @pl.when(pl.program_id(0) == 0)
    def _():
        carry_ref[...] = jnp.zeros_like(carry_ref)

    x = x_ref[...]
    rows = x.shape[0]
    u = x * lax.rsqrt(jnp.mean(x * x, axis=-1, keepdims=True) + NORM_EPS) * g_ref[...]
    lane = lax.broadcasted_iota(jnp.int32, (rows, LANES), 1)
    u_hi = u.astype(BF16)
    u_lo = (u - u_hi.astype(F32)).astype(BF16)
    hi_part = jnp.dot(u_hi, wr_ref[...], preferred_element_type=F32)
    logits = hi_part[:, :LANES] + hi_part[:, LANES:] + jnp.dot(u_lo, wr_ref[:, :LANES], preferred_element_type=F32)
    logits = jnp.where(lane < N_EXPERTS, logits, -jnp.inf)
    m1 = jnp.max(logits, axis=-1, keepdims=True)
    i1 = jnp.min(jnp.where(logits == m1, lane, LANES), axis=-1, keepdims=True)
    rest = jnp.where(lane == i1, -jnp.inf, logits)
    m2 = jnp.max(rest, axis=-1, keepdims=True)
    i2 = jnp.min(jnp.where(rest == m2, lane, LANES), axis=-1, keepdims=True)
    e2 = jnp.exp(m2 - m1)
    w1 = 1.0 / (1.0 + e2)
    w2 = e2 / (1.0 + e2)
    hit1 = lane == i1
    hit2 = lane == i2
    onehot = jnp.where(hit1 | hit2, 1.0, 0.0)
    seen = jnp.dot(before_ref[...], onehot.astype(BF16), preferred_element_type=F32) + carry_ref[0:1, :]
    r1 = jnp.sum(jnp.where(hit1, seen, 0.0), axis=-1, keepdims=True)
    r2 = jnp.sum(jnp.where(hit2, seen, 0.0), axis=-1, keepdims=True)
    carry_ref[...] = carry_ref[...] + jnp.sum(onehot, axis=0, keepdims=True)
    cols = (i1.astype(F32), i2.astype(F32), w1, w2, r1, r2)
    route = jnp.zeros((rows, LANES), F32)
    for n, c in enumerate(cols):
        route = jnp.where(lane == n, c, route)
    route_ref[...] = route
    counts_ref[...] = carry_ref[...]


def _router(h, g, w_router, *, bm=1024):
    m, d = h.shape
    bm = min(bm, m)
    w = _pad_cols(w_router, LANES)
    w_hi = w.astype(BF16)
    w_split = _cat([w_hi, (w - w_hi.astype(F32)).astype(BF16)], 1)
    return pl.pallas_call(
        _router_kernel,
        grid=(m // bm,),
        in_specs=[pl.BlockSpec((bm, d), lambda i: (i, 0)), pl.BlockSpec((1, d), lambda i: (0, 0)),
                  pl.BlockSpec((d, 2 * LANES), lambda i: (0, 0)), pl.BlockSpec((bm, bm), lambda i: (0, 0))],
        out_specs=[pl.BlockSpec((bm, LANES), lambda i: (i, 0)), pl.BlockSpec((SUBLANES, LANES), lambda i: (0, 0))],
        out_shape=[jax.ShapeDtypeStruct((m, LANES), F32), jax.ShapeDtypeStruct((SUBLANES, LANES), F32)],
        scratch_shapes=[pltpu.VMEM((SUBLANES, LANES), F32)],
        compiler_params=_cparams(("arbitrary",)),
        name="router",
    )(h, g.reshape(1, d), w_split, jnp.tril(jnp.ones((bm, bm), BF16), -1))


def _rms(x, n, g):
    return x * lax.rsqrt(jnp.sum(x * x, axis=-1, keepdims=True) * (1.0 / n) + NORM_EPS) * g


def _mla_prep_kernel(u_ref, cos_ref, sin_ref, win_ref, gq_ref, gkv_ref, gqh_ref, gkh_ref, wq_ref, wqr_ref, wk_ref,
                     wv_ref, q_ref, k_ref, v_ref):
    z = jnp.dot(u_ref[...], win_ref[...], preferred_element_type=F32)
    cq = _rms(z[:, :Q_LORA], Q_LORA, gq_ref[...]).astype(BF16)
    ckv = _rms(z[:, Q_LORA:Q_LORA + KV_LORA], KV_LORA, gkv_ref[...]).astype(BF16)
    k_pe = z[:, Q_LORA + KV_LORA:Q_LORA + KV_LORA + LANES]
    k_pe_rot = z[:, Q_LORA + KV_LORA + LANES:]
    q_all = jnp.dot(cq, wq_ref[...], preferred_element_type=F32)
    q_rot = jnp.dot(cq, wqr_ref[...], preferred_element_type=F32)
    k_all = jnp.dot(ckv, wk_ref[...], preferred_element_type=F32)
    v_all = jnp.dot(ckv, wv_ref[...], preferred_element_type=F32)
    sin = sin_ref[...]
    cos_q = cos_ref[...] * gqh_ref[...]
    cos_k = cos_ref[...] * gkh_ref[...]

    def scale(x):
        return lax.rsqrt(jnp.sum(x * x, axis=-1, keepdims=True) * (1.0 / QK_DIM) + NORM_EPS)

    ones_lane = lax.broadcasted_iota(jnp.int32, sin.shape, 1) == HEAD_DIM
    for h in range(N_HEADS):
        sl = slice(h * LANES, (h + 1) * LANES)
        q = q_all[:, sl]
        k = k_all[:, sl] + k_pe
        q_ref[h] = (scale(q) * (QK_DIM ** -0.5) * (q * cos_q + q_rot[:, sl] * sin)).astype(BF16)
        k_ref[h] = (scale(k) * (k * cos_k + k_pe_rot * sin)).astype(BF16)
        v_ref[h] = jnp.where(ones_lane, 1.0, v_all[:, sl]).astype(BF16)


def _pad_heads(w, width):
    k = w.shape[0]
    w = w.reshape(k, N_HEADS, width)
    return jnp.pad(w, ((0, 0), (0, 0), (0, LANES - width))).reshape(k, N_HEADS * LANES)


def _mla_prep(u, w_in, rope_tabs, b, s, q_norm_g, kv_norm_g, w_uq, w_ukv, qk_g_q, qk_g_k, *, bm=1024):
    bm = min(bm, s)
    half = QK_ROPE // 2
    swap = lambda w: _cat([w[..., half:], w[..., :half]], -1)
    rope_slab = lambda w: jnp.pad(w, ((0, 0), (QK_NOPE, LANES - QK_DIM)))
    lora = Q_LORA + KV_LORA
    w_in = _cat([w_in[:, :lora], rope_slab(w_in[:, lora:]), rope_slab(swap(w_in[:, lora:] * qk_g_k[QK_NOPE:]))],
                1).astype(BF16)
    w_ukv = w_ukv.reshape(KV_LORA, N_HEADS, QK_NOPE + HEAD_DIM)
    w_uq3 = w_uq.reshape(Q_LORA, N_HEADS, QK_DIM)
    wq = _pad_heads(w_uq, QK_DIM).astype(BF16)
    wq_rot = jnp.pad(swap(w_uq3[..., QK_NOPE:] * qk_g_q[QK_NOPE:]), ((0, 0), (0, 0), (QK_NOPE, LANES - QK_DIM)))
    wq_rot = wq_rot.reshape(Q_LORA, N_HEADS * LANES).astype(BF16)
    wk = _pad_heads(w_ukv[..., :QK_NOPE].reshape(KV_LORA, -1), QK_NOPE).astype(BF16)
    wv = _pad_heads(w_ukv[..., QK_NOPE:].reshape(KV_LORA, -1), HEAD_DIM).astype(BF16)
    pad_g = lambda g: jnp.pad(g, (0, LANES - g.shape[0])).reshape(1, LANES)
    row = lambda width: pl.BlockSpec((bm, width), lambda bb, i: (bb * (s // bm) + i, 0))
    full = lambda a: pl.BlockSpec(a.shape, lambda bb, i: (0, 0))
    consts = [w_in, q_norm_g.reshape(1, -1), kv_norm_g.reshape(1, -1), pad_g(qk_g_q), pad_g(qk_g_k), wq, wq_rot,
              wk, wv]
    out_spec = pl.BlockSpec((None, N_HEADS, bm, LANES), lambda bb, i: (bb, 0, i, 0))
    out_shape = jax.ShapeDtypeStruct((b, N_HEADS, s, LANES), BF16)
    return pl.pallas_call(
        _mla_prep_kernel,
        grid=(b, s // bm),
        in_specs=[row(u.shape[1]), row(LANES), row(LANES)] + [full(a) for a in consts],
        out_specs=[out_spec] * 3,
        out_shape=[out_shape] * 3,
        compiler_params=_cparams(("arbitrary", "arbitrary")),
        name="mla_prep",
    )(u, *rope_tabs, *consts)


def _flash_kernel(qi_ref, kj_ref, q_ref, k_ref, v_ref, o_ref, m_ref, acc_ref, *, blk, sub):
    n = pl.program_id(2)
    i = qi_ref[n]
    j = kj_ref[n]
    hb = q_ref.shape[0]

    @pl.when(j == 0)
    def _():
        m_ref[...] = jnp.full_like(m_ref, -jnp.inf)
        acc_ref[...] = jnp.zeros_like(acc_ref)

    def update(masked):
        chains = [(h, r) for h in range(hb) for r in range(blk // sub)]

        def qk(h, r):
            nk = (r + 1) * sub if masked else blk
            s = lax.dot_general(q_ref[h, r * sub:(r + 1) * sub, :], k_ref[h, :nk, :], (((1,), (1,)), ((), ())),
                                preferred_element_type=F32)
            if masked:
                row = lax.broadcasted_iota(jnp.int32, (sub, nk), 0) + r * sub
                s = jnp.where(row >= lax.broadcasted_iota(jnp.int32, (sub, nk), 1), s, -jnp.inf)
            return s

        def softmax_pv(h, r, s):
            rows = slice(r * sub, (r + 1) * sub)
            m_old = m_ref[h, rows, :]
            m_new = jnp.maximum(m_old, jnp.max(s, axis=-1, keepdims=True))
            alpha = jnp.exp(m_old - m_new)
            p = jnp.exp((s - m_new[:, :1]).astype(BF16))
            m_ref[h, rows, :] = m_new
            pv = jnp.dot(p, v_ref[h, :s.shape[1], :], preferred_element_type=F32)
            acc_ref[h, rows, :] = alpha * acc_ref[h, rows, :] + pv

        s_prev = qk(*chains[0])
        for c in range(len(chains)):
            s_next = qk(*chains[c + 1]) if c + 1 < len(chains) else None
            softmax_pv(*chains[c], s_prev)
            s_prev = s_next

    @pl.when(j < i)
    def _():
        update(False)

    @pl.when(j == i)
    def _():
        update(True)
        acc = acc_ref[...]
        o_ref[...] = (acc / acc[:, :, HEAD_DIM:HEAD_DIM + 1]).astype(o_ref.dtype)


def _flash_attention(q, k, v, *, blk=1024, sub=256, hb=4):
    b, h, s, d = q.shape
    blk = min(blk, s)
    sub = min(sub, blk)
    nb = s // blk
    qi = np.array([i for i in range(nb) for j in range(i + 1)], np.int32)
    kj = np.array([j for i in range(nb) for j in range(i + 1)], np.int32)
    q_spec = pl.BlockSpec((None, hb, blk, d), lambda bb, hh, n, qi_r, kj_r: (bb, hh, qi_r[n], 0))
    kv_spec = pl.BlockSpec((None, hb, blk, d), lambda bb, hh, n, qi_r, kj_r: (bb, hh, kj_r[n], 0))
    return pl.pallas_call(
        functools.partial(_flash_kernel, blk=blk, sub=sub),
        grid_spec=pltpu.PrefetchScalarGridSpec(
            num_scalar_prefetch=2,
            grid=(b, h // hb, len(qi)),
            in_specs=[q_spec, kv_spec, kv_spec],
            out_specs=q_spec,
            scratch_shapes=[pltpu.VMEM((hb, blk, LANES), F32), pltpu.VMEM((hb, blk, d), F32)],
        ),
        out_shape=jax.ShapeDtypeStruct((b, h, s, d), BF16),
        compiler_params=_cparams(("arbitrary", "arbitrary", "arbitrary")),
        name="flash",
    )(jnp.asarray(qi), jnp.asarray(kj), q, k, v)


def _dot(a, b):
    return jnp.dot(a.astype(BF16), b.astype(BF16), preferred_element_type=F32)


def _dot_nt(a, b):
    return lax.dot_general(a.astype(BF16), b.astype(BF16), (((1,), (1,)), ((), ())), preferred_element_type=F32)


def _dot_tn(a, b):
    return jnp.dot(a.T.astype(BF16), b.astype(BF16), preferred_element_type=F32)


def _split3(x):
    hi = x.astype(BF16)
    r1 = x - hi.astype(F32)
    mid = r1.astype(BF16)
    return hi, mid, (r1 - mid.astype(F32)).astype(BF16)


def _cumsum_rows(tri, x):
    s = jnp.dot(tri, jnp.concatenate(_split3(x), axis=1), preferred_element_type=F32)
    n = x.shape[1]
    return s[:, :n] + s[:, n:2 * n] + s[:, 2 * n:]


def _head_sum(x):
    left = lax.broadcasted_iota(jnp.int32, (x.shape[0], LANES), 1) < HEAD_DIM
    outs = []
    for p in range(WIDTH // LANES):
        xs = x[:, p * LANES:(p + 1) * LANES]
        s_left = jnp.sum(jnp.where(left, xs, 0.0), axis=-1, keepdims=True)
        s_right = jnp.sum(jnp.where(left, 0.0, xs), axis=-1, keepdims=True)
        outs.append(jnp.where(left, s_left, s_right))
    return jnp.concatenate(outs, axis=1)


def _shift_rows(x, tail, j):
    rolled = pltpu.roll(x, j, 0)
    top = jnp.where(lax.broadcasted_iota(jnp.int32, (SUBLANES, x.shape[1]), 0) < j,
                    pltpu.roll(tail, j, 0), rolled[:SUBLANES])
    return jnp.concatenate([top, rolled[SUBLANES:]], axis=0)


def _softplus(x):
    return jnp.maximum(x, 0.0) + jnp.log(1.0 + jnp.exp(-jnp.abs(x)))


def _pair_masks():
    lane = lax.broadcasted_iota(jnp.int32, (CHUNK, LANES), 1)
    m0 = (lane < HEAD_DIM).astype(F32)
    return m0, 1.0 - m0


def _ext(x, m0, m1, dtype=BF16):
    return jnp.concatenate([x * m0, x * m1], axis=0).astype(dtype)


def _cat(xs, axis):
    return jnp.concatenate(xs, axis=axis)


SCAN_GROUP = 2


def _neumann_inverse(mats):
    n = mats[0].shape[0]
    eye = (lax.broadcasted_iota(jnp.int32, (n, n), 0) == lax.broadcasted_iota(jnp.int32, (n, n), 1)).astype(F32)
    ts = [eye + a for a in mats]
    ps = [_dot(a, a) for a in mats]
    yield
    for _ in range(int(math.log2(CHUNK)) - 2):
        pps = [_dot(_cat([p, t], 0), p) for p, t in zip(ps, ts)]
        ps = [pp[:n] for pp in pps]
        ts = [t + pp[n:] for t, pp in zip(ts, pps)]
        yield
    return [t + _dot(t, p) for t, p in zip(ts, ps)]


def _interleave(prepare, serial, n_groups):
    for _ in prepare(0):
        pass
    for g in range(n_groups):
        gens = [serial(g)] + ([prepare(g + 1)] if g + 1 < n_groups else [])
        while gens:
            for gen in list(gens):
                if next(gen, StopIteration) is StopIteration:
                    gens.remove(gen)


def _work_items(rows):
    return [(slice(c * CHUNK, (c + 1) * CHUNK), slice(p * LANES, (p + 1) * LANES))
            for c in range(rows // CHUNK) for p in range(WIDTH // LANES)]


def _tiles(x, items):
    return [x[rows, cols] for rows, cols in items]


def _rwkv_scan(r, k, v, a, b, ld, st_ref, y_ref):
    c2 = 2 * CHUNK
    strict = lax.broadcasted_iota(jnp.int32, (c2, c2), 0) > lax.broadcasted_iota(jnp.int32, (c2, c2), 1)
    trow = lax.broadcasted_iota(jnp.int32, (CHUNK, c2), 0)
    tcol = lax.broadcasted_iota(jnp.int32, (CHUNK, c2), 1) & (CHUNK - 1)
    incl_w = tcol <= trow
    tri = (lax.broadcasted_iota(jnp.int32, (CHUNK, CHUNK), 0)
           >= lax.broadcasted_iota(jnp.int32, (CHUNK, CHUNK), 1)).astype(BF16)
    m0, m1 = _pair_masks()
    items = _work_items(r.shape[0])
    n_pairs = WIDTH // LANES
    per_group = min(SCAN_GROUP, len(items) // n_pairs) * n_pairs
    full = (r, k, v, a, b, ld)
    ready = {}
    state = [st_ref[p] for p in range(n_pairs)]

    def prepare(g):
        sub = items[g * per_group:(g + 1) * per_group]
        r, k, v, a, b, ld = (_tiles(x, sub) for x in full)
        cum = [_cumsum_rows(tri, x) for x in ld]
        yield
        e_pos = [jnp.exp(x) for x in cum]
        e_neg = [jnp.exp(-x) for x in cum]
        e_end = [jnp.exp(x[CHUNK - 1:CHUNK, :] - x) for x in cum]
        a_ext = [_ext(ai * jnp.exp(ci - li), m0, m1) for ai, ci, li in zip(a, cum, ld)]
        v_ext = [_ext(x, m0, m1) for x in v]
        r_dec = [(ri * ei).astype(BF16) for ri, ei in zip(r, e_pos)]
        yield
        s1 = [_dot_nt(_cat([ae, rd], 0), _cat([_ext(bi * en, m0, m1), _ext(ki * en, m0, m1)], 0))
              for ae, rd, bi, ki, en in zip(a_ext, r_dec, b, k, e_neg)]
        yield
        a_ab = [jnp.where(strict, s[:c2, :c2], 0.0) for s in s1]
        a_ak = [jnp.where(strict, s[:c2, c2:], 0.0) for s in s1]
        a_rbk = [_cat([jnp.where(incl_w, s[c2:, :c2], 0.0), jnp.where(incl_w, s[c2:, c2:], 0.0)], 1).astype(BF16)
                 for s in s1]
        akv = [_dot(x, ve) for x, ve in zip(a_ak, v_ext)]
        t_inv = yield from _neumann_inverse(a_ab)
        yield
        wu0 = [_dot(t, _cat([ae, av.astype(BF16)], 1)) for t, ae, av in zip(t_inv, a_ext, akv)]
        bk_end = [_cat([_ext(bi * ee, m0, m1, F32), _ext(ki * ee, m0, m1, F32)], 0)
                  for bi, ki, ee in zip(b, k, e_end)]
        e_last_t = [jnp.broadcast_to(e[CHUNK - 1:CHUNK, :], (LANES, LANES)).T for e in e_pos]
        ready[g] = (sub, wu0, r_dec, v_ext, a_rbk, bk_end, e_last_t)

    def serial(g):
        sub, wu0, r_dec, v_ext, a_rbk, bk_end, e_last_t = ready.pop(g)
        for c in range(len(sub) // n_pairs):
            idx = range(c * n_pairs, (c + 1) * n_pairs)
            z = [_dot(_cat([wu0[i][:, :LANES].astype(BF16), r_dec[i]], 0), s) for i, s in zip(idx, state)]
            yield
            uv = [_cat([(wu0[i][:, LANES:] + zi[:c2]).astype(BF16), v_ext[i]], 0) for i, zi in zip(idx, z)]
            for i, zi, uvi in zip(idx, z, uv):
                rows, cols = sub[i]
                y_ref[rows, cols] = zi[c2:] + _dot(a_rbk[i], uvi)
            state[:] = [s * e_last_t[i] + _dot_tn(bk_end[i], uvi) for i, s, uvi in zip(idx, state, uv)]
            yield

    _interleave(prepare, serial, len(items) // per_group)
    for p in range(n_pairs):
        st_ref[p] = state[p]


def _gdn_scan(q, k, v, g, beta, st_ref, o_ref):
    c2 = 2 * CHUNK
    row = lax.broadcasted_iota(jnp.int32, (c2, c2), 0)
    col = lax.broadcasted_iota(jnp.int32, (c2, c2), 1)
    causal = ((row >= CHUNK) == (col >= CHUNK)) & (row >= col)
    strict = row > col
    tri = (lax.broadcasted_iota(jnp.int32, (CHUNK, CHUNK), 0)
           >= lax.broadcasted_iota(jnp.int32, (CHUNK, CHUNK), 1)).astype(BF16)
    m0, m1 = _pair_masks()
    items = _work_items(q.shape[0])
    n_pairs = WIDTH // LANES
    per_group = min(SCAN_GROUP, len(items) // n_pairs) * n_pairs
    full = (q, k, v, g, beta)
    ready = {}
    state = [st_ref[p] for p in range(n_pairs)]

    def prepare(grp):
        sub = items[grp * per_group:(grp + 1) * per_group]
        q, k, v, g, beta = (_tiles(x, sub) for x in full)
        gc = [_cumsum_rows(tri, x) for x in g]
        yield
        g_rows = [_cat([jnp.broadcast_to(x[:, 0:1], (CHUNK, LANES)),
                        jnp.broadcast_to(x[:, HEAD_DIM:HEAD_DIM + 1], (CHUNK, LANES))], 0) for x in gc]
        gamma = [jnp.exp(jnp.where(causal, x - x.T, -jnp.inf)) for x in g_rows]
        e_pos = [jnp.exp(x) for x in gc]
        kb = [ki * bi for ki, bi in zip(k, beta)]
        yield
        s1 = [_dot_nt(_cat([_ext(kbi, m0, m1), qi.astype(BF16)], 0), _ext(ki, m0, m1))
              for kbi, qi, ki in zip(kb, q, k)]
        yield
        a_neg = [-jnp.where(strict, s[:c2] * ga, 0.0) for s, ga in zip(s1, gamma)]
        a_in = [(s[c2:] * (ga[:CHUNK] + ga[CHUNK:])).astype(BF16) for s, ga in zip(s1, gamma)]
        t_inv = yield from _neumann_inverse(a_neg)
        yield
        uw = [_dot(t, _cat([_ext(vi * bi, m0, m1), _ext(kbi * ei, m0, m1)], 1))
              for t, vi, bi, kbi, ei in zip(t_inv, v, beta, kb, e_pos)]
        q_dec = [(qi * ei).astype(BF16) for qi, ei in zip(q, e_pos)]
        k_dec = [_ext(ki * jnp.exp(x[CHUNK - 1:CHUNK, :] - x), m0, m1, F32) for ki, x in zip(k, gc)]
        ready[grp] = (sub, uw, q_dec, k_dec, a_in, e_pos)

    def serial(grp):
        sub, uw, q_dec, k_dec, a_in, e_pos = ready.pop(grp)
        for c in range(len(sub) // n_pairs):
            idx = range(c * n_pairs, (c + 1) * n_pairs)
            z = [_dot(_cat([uw[i][:, LANES:].astype(BF16), q_dec[i]], 0), s) for i, s in zip(idx, state)]
            yield
            v_new = [(uw[i][:, :LANES] - zi[:c2]).astype(BF16) for i, zi in zip(idx, z)]
            for i, zi, vn in zip(idx, z, v_new):
                rows, cols = sub[i]
                o_ref[rows, cols] = zi[c2:] + _dot(a_in[i], vn)
            state[:] = [s * e_pos[i][CHUNK - 1:CHUNK, :] + _dot_tn(k_dec[i], vn)
                        for i, s, vn in zip(idx, state, v_new)]
            yield

    _interleave(prepare, serial, len(items) // per_group)
    for p in range(n_pairs):
        st_ref[p] = state[p]


def _rwkv_kernel(*refs, vres):
    if vres:
        (u_ref, vf_ref, win_ref, mu_ref, prm_ref, wup_ref, aup_ref, gup_ref, vup_ref, o_ref, st_ref, tail_ref,
         y_ref) = refs
    else:
        u_ref, win_ref, mu_ref, prm_ref, wup_ref, aup_ref, gup_ref, o_ref, vf_out_ref, st_ref, tail_ref, y_ref = refs

    @pl.when(pl.program_id(1) == 0)
    def _():
        st_ref[...] = jnp.zeros_like(st_ref)
        tail_ref[...] = jnp.zeros_like(tail_ref)

    z = jnp.dot(u_ref[...], win_ref[...], preferred_element_type=F32)
    rows = z.shape[0]
    zs = _shift_rows(z, tail_ref[...], 1)
    tail_ref[...] = z[rows - SUBLANES:]
    zl = z[:, :RWKV_IN] + (zs[:, :RWKV_IN] - z[:, :RWKV_IN]) * mu_ref[...]
    r, k, v = zl[:, :WIDTH], zl[:, WIDTH:2 * WIDTH], zl[:, 2 * WIDTH:3 * WIDTH]
    lo = zl[:, 3 * WIDTH:3 * WIDTH + LANES]
    g_lo = zl[:, 3 * WIDTH + LANES:]
    w0, a0, k_k, k_a, ln_g, ln_b, r_k, v_bias = (prm_ref[i:i + 1, :] for i in range(8))
    log_w = -_softplus(-(w0 + _dot(jnp.tanh(lo), wup_ref[...]))) - 0.5
    ld = -jnp.exp(log_w)
    iclr = jax.nn.sigmoid(a0 + _dot(lo, aup_ref[...]))
    gate = _dot(jax.nn.sigmoid(g_lo), gup_ref[...])
    if vres:
        x = z[:, RWKV_IN:] + pltpu.roll(zs[:, RWKV_IN:], LANES - VRES_LORA, 1)
        v = v + (vf_ref[...] - v) * jax.nn.sigmoid(v_bias + _dot(x, vup_ref[...]))
    else:
        vf_out_ref[...] = v
    kk = k * k_k
    kk = kk * lax.rsqrt(_head_sum(kk * kk) + L2_EPS)
    k = k * (1.0 + (iclr - 1.0) * k_a)
    _rwkv_scan(r, k, v, -kk, kk * iclr, ld, st_ref, y_ref)
    y = y_ref[...]
    d = y - _head_sum(y) * (1.0 / HEAD_DIM)
    y = d * lax.rsqrt(_head_sum(d * d) * (1.0 / HEAD_DIM) + RWKV_GN_EPS) * ln_g + ln_b
    y = y + _head_sum(r * k * r_k) * v
    o_ref[...] = (y * gate).astype(o_ref.dtype)


def _rwkv(u, w_in, v_first, b, s, mu, prm, w_up, a_up, g_up, v_up):
    vres = v_first is not None
    rows = min(SCAN_ROWS, s)
    zw = w_in.shape[1]
    row = lambda width: pl.BlockSpec((rows, width), lambda bb, c: (bb * (s // rows) + c, 0))
    full = lambda a: pl.BlockSpec(a.shape, lambda bb, c: (0, 0))
    zero = jnp.zeros((LANES // 2, WIDTH), F32)
    consts = [w_in.astype(BF16), mu.reshape(1, -1), prm, _cat([w_up, zero], 0).astype(BF16),
              _cat([zero, a_up], 0).astype(BF16), g_up.astype(BF16)]
    args, in_specs = [u], [row(u.shape[1])]
    if vres:
        args.append(v_first)
        in_specs.append(row(WIDTH))
        consts.append(jnp.pad(v_up, ((0, LANES - VRES_LORA), (0, 0))).astype(BF16))
    out_shape = [jax.ShapeDtypeStruct((b * s, WIDTH), BF16)]
    out_specs = [row(WIDTH)]
    if not vres:
        out_shape.append(jax.ShapeDtypeStruct((b * s, WIDTH), F32))
        out_specs.append(row(WIDTH))
    outs = pl.pallas_call(
        functools.partial(_rwkv_kernel, vres=vres),
        grid=(b, s // rows),
        in_specs=in_specs + [full(a) for a in consts],
        out_specs=out_specs,
        out_shape=out_shape,
        scratch_shapes=[pltpu.VMEM((WIDTH // LANES, LANES, LANES), F32), pltpu.VMEM((SUBLANES, zw), F32),
                        pltpu.VMEM((rows, WIDTH), F32)],
        compiler_params=_cparams(("arbitrary", "arbitrary")),
        name="rwkv",
    )(*args, *consts)
    return (outs[0], v_first) if vres else (outs[0], outs[1])


def _gdn_kernel(u_ref, win_ref, cw_ref, prm_ref, exp_ref, o_ref, st_ref, tail_ref, y_ref):
    @pl.when(pl.program_id(1) == 0)
    def _():
        st_ref[...] = jnp.zeros_like(st_ref)
        tail_ref[...] = jnp.zeros_like(tail_ref)

    z = jnp.dot(u_ref[...], win_ref[...], preferred_element_type=F32)
    rows = z.shape[0]
    x = z[:, :GDN_QKV]
    tail = tail_ref[...]
    conv = x * cw_ref[GDN_CONV - 1:GDN_CONV, :]
    for j in range(1, GDN_CONV):
        conv = conv + _shift_rows(x, tail, j) * cw_ref[GDN_CONV - 1 - j:GDN_CONV - j, :]
    tail_ref[...] = x[rows - SUBLANES:]
    qkv = conv * jax.nn.sigmoid(conv)
    q, k, v = qkv[:, :WIDTH], qkv[:, WIDTH:2 * WIDTH], qkv[:, 2 * WIDTH:]
    q = q * lax.rsqrt(_head_sum(q * q) + L2_EPS) * (HEAD_DIM ** -0.5)
    k = k * lax.rsqrt(_head_sum(k * k) + L2_EPS)
    logits = jnp.dot(_cat(_split3(z[:, GDN_QKV + WIDTH:]), 0), exp_ref[...], preferred_element_type=F32)
    logits = logits[:rows] + logits[rows:2 * rows] + logits[2 * rows:]
    neg_a, dt_bias, norm_g = (prm_ref[i:i + 1, :] for i in range(3))
    beta = jax.nn.sigmoid(logits[:, :WIDTH])
    g = neg_a * _softplus(logits[:, WIDTH:] + dt_bias)
    _gdn_scan(q, k, v, g, beta, st_ref, y_ref)
    o = y_ref[...]
    o = o * lax.rsqrt(_head_sum(o * o) * (1.0 / HEAD_DIM) + NORM_EPS) * norm_g
    gate = z[:, GDN_QKV:GDN_QKV + WIDTH]
    o_ref[...] = (o * (gate * jax.nn.sigmoid(gate))).astype(o_ref.dtype)


def _gdn(u, w_in, b, s, conv_w, a_log, dt_bias, norm_g):
    rows = min(SCAN_ROWS, s)
    row = lambda width: pl.BlockSpec((rows, width), lambda bb, c: (bb * (s // rows) + c, 0))
    full = lambda a: pl.BlockSpec(a.shape, lambda bb, c: (0, 0))
    per_lane = lambda t: jnp.repeat(t, HEAD_DIM)
    prm = jnp.stack([per_lane(-jnp.exp(a_log)), per_lane(dt_bias), jnp.tile(norm_g, N_HEADS)])
    head_of_lane = np.arange(WIDTH) // HEAD_DIM
    expand = np.zeros((LANES, 2 * WIDTH), np.float32)
    expand[head_of_lane, np.arange(WIDTH)] = 1.0
    expand[N_HEADS + head_of_lane, WIDTH + np.arange(WIDTH)] = 1.0
    consts = [w_in.astype(BF16), conv_w, prm, jnp.asarray(expand, BF16)]
    return pl.pallas_call(
        _gdn_kernel,
        grid=(b, s // rows),
        in_specs=[row(u.shape[1])] + [full(a) for a in consts],
        out_specs=row(WIDTH),
        out_shape=jax.ShapeDtypeStruct((b * s, WIDTH), BF16),
        scratch_shapes=[pltpu.VMEM((WIDTH // LANES, LANES, LANES), F32), pltpu.VMEM((SUBLANES, GDN_QKV), F32),
                        pltpu.VMEM((rows, WIDTH), F32)],
        compiler_params=_cparams(("arbitrary", "arbitrary")),
        name="gdn",
    )(u, *consts)


def _merge_kernel(h_ref, u_ref, oa_ref, ob_ref, oc_ref, wg_ref, wa_ref, wb_ref, wc_ref, wo_ref, out_ref):
    d = h_ref.shape[1]
    u = u_ref[...]
    ob = jnp.concatenate([ob_ref[h] for h in range(N_HEADS)], axis=1)
    merged = jnp.zeros(h_ref.shape, F32)
    for n, (o, w_ref) in enumerate(((oa_ref[...], wa_ref), (ob, wb_ref), (oc_ref[...], wc_ref))):
        proj = jnp.dot(o, w_ref[...], preferred_element_type=F32)
        zg = jnp.dot(u, wg_ref[:, n * d:(n + 1) * d], preferred_element_type=F32)
        merged = merged + jax.nn.sigmoid(zg) * proj
    out_ref[...] = h_ref[...] + jnp.dot(merged.astype(BF16), wo_ref[...], preferred_element_type=F32)


def _merge(h, u, oa, ob, oc, wg, wa, wb, wc, wo, s, *, bm=512):
    m, d = h.shape
    bm = min(bm, s)
    row = lambda width: pl.BlockSpec((bm, width), lambda i: (i, 0))
    full = lambda w: pl.BlockSpec(w.shape, lambda i: (0, 0))
    wb = jnp.pad(wb.reshape(N_HEADS, HEAD_DIM, d), ((0, 0), (0, LANES - HEAD_DIM), (0, 0))).reshape(N_HEADS * LANES, d)
    ws = [w.astype(BF16) for w in (wg, wa, wb, wc, wo)]
    ob_spec = pl.BlockSpec((None, N_HEADS, bm, LANES), lambda i: (i // (s // bm), 0, i % (s // bm), 0))
    return pl.pallas_call(
        _merge_kernel,
        grid=(m // bm,),
        in_specs=[row(d), row(d), row(WIDTH), ob_spec, row(WIDTH)] + [full(w) for w in ws],
        out_specs=row(d),
        out_shape=jax.ShapeDtypeStruct((m, d), F32),
        compiler_params=_cparams(("arbitrary",)),
        name="merge",
    )(h, u, oa, ob, oc, *ws)


def _ffn_kernel(be_ref, nu_ref, x_ref, g_ref, wg_ref, wu_ref, wd_ref, o_ref, xs_ref, acc_ref, *, nf, dense):
    i = pl.program_id(0)
    f = pl.program_id(1)
    used = i < nu_ref[0]

    @pl.when(used & (f == 0))
    def _():
        x = x_ref[...]
        ms = jnp.mean(x * x, axis=-1, keepdims=True)
        xs_ref[...] = (x * lax.rsqrt(ms + NORM_EPS) * g_ref[...]).astype(BF16)

    @pl.when(f == 0)
    def _():
        acc_ref[...] = jnp.zeros_like(acc_ref)

    @pl.when(used)
    def _():
        xs = xs_ref[...]
        gate = jnp.dot(xs, wg_ref[...].astype(BF16), preferred_element_type=F32)
        up = jnp.dot(xs, wu_ref[...].astype(BF16), preferred_element_type=F32)
        act = (gate * jax.nn.sigmoid(gate) * up).astype(BF16)
        acc_ref[...] += jnp.dot(act, wd_ref[...].astype(BF16), preferred_element_type=F32)

    @pl.when(f == nf - 1)
    def _():
        o_ref[...] = (x_ref[...] + acc_ref[...]) if dense else acc_ref[...]


def _ffn(x, g, wg, wu, wd, block_e, n_used, *, bm, tf, dense):
    r, d = x.shape
    ff = wg.shape[2]
    nf = ff // tf

    def ff_idx(i, f, be, nu):
        return jnp.where(i < nu[0], f, nf - 1)

    return pl.pallas_call(
        functools.partial(_ffn_kernel, nf=nf, dense=dense),
        grid_spec=pltpu.PrefetchScalarGridSpec(
            num_scalar_prefetch=2,
            grid=(r // bm, nf),
            in_specs=[
                pl.BlockSpec((bm, d), lambda i, f, be, nu: (i, 0)),
                pl.BlockSpec((1, d), lambda i, f, be, nu: (0, 0)),
                pl.BlockSpec((None, d, tf), lambda i, f, be, nu: (be[i], 0, ff_idx(i, f, be, nu))),
                pl.BlockSpec((None, d, tf), lambda i, f, be, nu: (be[i], 0, ff_idx(i, f, be, nu))),
                pl.BlockSpec((None, tf, d), lambda i, f, be, nu: (be[i], ff_idx(i, f, be, nu), 0)),
            ],
            out_specs=pl.BlockSpec((bm, d), lambda i, f, be, nu: (i, 0)),
            scratch_shapes=[pltpu.VMEM((bm, d), BF16), pltpu.VMEM((bm, d), F32)],
        ),
        out_shape=jax.ShapeDtypeStruct((r, d), F32),
        compiler_params=_cparams(("arbitrary", "arbitrary")),
        name="ffn",
    )(block_e, n_used, x, g.reshape(1, d).astype(F32), wg, wu, wd)


def _ple_kernel(*refs, combine, emit_norm):
    refs = list(refs)
    h = refs.pop(0)[...]
    if combine:
        y0_ref, y1_ref, route_ref = refs[:3]
        del refs[:3]
        h = h + route_ref[:, 2:3] * y0_ref[...] + route_ref[:, 3:4] * y1_ref[...]
    p_ref, wg_ref, wp_ref, g_ref = refs[:4]
    gate = jnp.dot(h.astype(BF16), wg_ref[...], preferred_element_type=F32)
    e = jnp.dot(p_ref[...].astype(BF16), wp_ref[...], preferred_element_type=F32)
    ms = jnp.mean(e * e, axis=-1, keepdims=True)
    e = e * lax.rsqrt(ms + NORM_EPS) * g_ref[...]
    out = h + jax.nn.sigmoid(gate) * e
    if emit_norm:
        gn_ref, o_ref, u_ref = refs[4:]
        u_ref[...] = (out * lax.rsqrt(jnp.mean(out * out, axis=-1, keepdims=True) + NORM_EPS) * gn_ref[...]).astype(BF16)
    else:
        o_ref, = refs[4:]
    o_ref[...] = out


def _ple(h, p, w_gate, w_proj, g, expert_out=None, g_next=None, *, bm=512):
    m, d = h.shape
    bm = min(bm, m)
    row = lambda a: pl.BlockSpec((bm, a.shape[1]), lambda i: (i, 0))
    full = lambda a: pl.BlockSpec(a.shape, lambda i: (0, 0))
    rows = [h] + list(expert_out or ()) + [p]
    consts = [w_gate.astype(BF16), w_proj.astype(BF16), g.reshape(1, d)]
    out_shape = [jax.ShapeDtypeStruct((m, d), F32)]
    if g_next is not None:
        consts.append(g_next.reshape(1, d))
        out_shape.append(jax.ShapeDtypeStruct((m, d), BF16))
    outs = pl.pallas_call(
        functools.partial(_ple_kernel, combine=expert_out is not None, emit_norm=g_next is not None),
        grid=(m // bm,),
        in_specs=[row(a) for a in rows] + [full(a) for a in consts],
        out_specs=[row(h)] * len(out_shape),
        out_shape=out_shape,
        compiler_params=_cparams(("arbitrary",)),
        name="ple",
    )(*rows, *consts)
    return outs if g_next is not None else (outs[0], None)


def _pad_cols(w, n):
    return jnp.pad(w, ((0, 0), (0, n - w.shape[1])))


def _rope_tables(positions):
    t = positions.size
    inv_freq = 1.0 / (ROPE_THETA ** (jnp.arange(0, QK_ROPE, 2, dtype=F32) / QK_ROPE))
    ang = positions.astype(F32).reshape(t, 1) * inv_freq
    cos, sin = jnp.cos(ang), jnp.sin(ang)
    pad = ((0, 0), (QK_NOPE, LANES - QK_DIM))
    return jnp.pad(_cat([cos, cos], 1), pad, constant_values=1.0), jnp.pad(_cat([-sin, sin], 1), pad)


def _moe(h, g, w_router, wg, wu, wd, *, bm=1024, tf=512):
    t, d = h.shape
    n_assign = t * TOP_K
    n_blocks = -(-(n_assign + N_EXPERTS * (bm - 1)) // bm)
    route, counts = _router(h, g, w_router)
    counts = counts[0, :N_EXPERTS].astype(jnp.int32)
    padded = (counts + bm - 1) // bm * bm
    pad_end = jnp.cumsum(padded)
    top_e = route[:, :TOP_K].astype(jnp.int32)
    dest = (pad_end - padded)[top_e] + route[:, 4:4 + TOP_K].astype(jnp.int32)
    src_tok = (jnp.arange(n_blocks * bm, dtype=jnp.int32) % t).at[dest.reshape(-1)].set(
        jnp.arange(n_assign, dtype=jnp.int32) // TOP_K)
    block_start = jnp.arange(n_blocks, dtype=jnp.int32) * bm
    block_e = jnp.minimum(jnp.sum(pad_end[None, :] <= block_start[:, None], axis=1), N_EXPERTS - 1).astype(jnp.int32)
    n_used = (pad_end[-1] // bm).astype(jnp.int32).reshape(1)
    y_rows = _ffn(h[src_tok], g, wg, wu, wd, block_e, n_used, bm=bm, tf=tf, dense=False)
    return y_rows[dest[:, 0]], y_rows[dest[:, 1]], route


def kernel(x, p, positions, norm_mix_g, w_in, rwkv_mu, rwkv_w0, rwkv_w_up, rwkv_a0, rwkv_a_up, rwkv_g_up, rwkv_k_k, rwkv_k_a, rwkv_r_k, rwkv_ln_g, rwkv_ln_b, vres_mu, vres_down, vres_up, vres_b, mla_q_norm_g, mla_kv_norm_g, mla_w_uq, mla_w_ukv, mla_qk_norm_q, mla_qk_norm_k, gdn_conv_w, gdn_a_log, gdn_dt_bias, gdn_norm_g, w_br_rwkv, w_br_mla, w_br_gdn, w_out, norm_ffn_g, ffn_wg, ffn_wu, ffn_wd, moe_router, moe_wg, moe_wu, moe_wd, ple_proj, ple_gate, ple_norm_g):
    b, s, d = x.shape
    t = b * s
    depth = w_in.shape[0]
    rope_tabs = _rope_tables(positions)

    h = x.reshape(t, d)
    v_first = None
    for i in range(depth):
        w = w_in[i]
        g = norm_mix_g[i]
        w_rwkv = w[:, :RWKV_IN]
        if i > 0:
            mu_v = vres_mu[i - 1][:, None]
            vd = vres_down[i - 1]
            w_rwkv = _pad_cols(_cat([w_rwkv, (1.0 - mu_v) * vd, mu_v * vd], 1), RWKV_IN + LANES)
        w_gdn = w[:, RWKV_IN + MLA_IN:RWKV_IN + MLA_IN + GDN_IN]
        w_gdn = _pad_cols(_cat([w_gdn[:, :GDN_QKV], w_gdn[:, GDN_QKV + 2 * N_HEADS:],
                                w_gdn[:, GDN_QKV:GDN_QKV + 2 * N_HEADS]], 1), GDN_QKV + WIDTH + LANES)
        if i == 0:
            u = _norm(h, g)
        prm = jnp.stack([rwkv_w0[i], rwkv_a0[i], rwkv_k_k[i], rwkv_k_a[i], rwkv_ln_g[i], rwkv_ln_b[i],
                         rwkv_r_k[i].reshape(-1), vres_b[i - 1] if i > 0 else jnp.zeros((WIDTH,), F32)])
        o_a, v_first = _rwkv(u, w_rwkv, v_first, b, s, rwkv_mu[i], prm, rwkv_w_up[i], rwkv_a_up[i], rwkv_g_up[i],
                             vres_up[i - 1] if i > 0 else None)
        q, k, v = _mla_prep(u, w[:, RWKV_IN:RWKV_IN + MLA_IN], rope_tabs, b, s, mla_q_norm_g[i], mla_kv_norm_g[i],
                            mla_w_uq[i], mla_w_ukv[i], mla_qk_norm_q[i], mla_qk_norm_k[i])
        o_b = _flash_attention(q, k, v)
        o_c = _gdn(u, w_gdn, b, s, gdn_conv_w[i], gdn_a_log[i], gdn_dt_bias[i], gdn_norm_g[i])
        h = _merge(h, u, o_a, o_b, o_c, w[:, RWKV_IN + MLA_IN + GDN_IN:], w_br_rwkv[i], w_br_mla[i], w_br_gdn[i],
                   w_out[i], s)
        j = i // 2
        if i % 2 == 0:
            bm = min(1024, t)
            h = _ffn(h, norm_ffn_g[i], ffn_wg[j:j + 1], ffn_wu[j:j + 1], ffn_wd[j:j + 1],
                     jnp.zeros((t // bm,), jnp.int32), jnp.full((1,), t // bm, jnp.int32), bm=bm, tf=256, dense=True)
            expert_out = None
        else:
            expert_out = _moe(h, norm_ffn_g[i], moe_router[j], moe_wg[j], moe_wu[j], moe_wd[j])
        h, u = _ple(h, p[i].reshape(t, -1), ple_gate[i], ple_proj[i], ple_norm_g[i], expert_out,
                    norm_mix_g[i + 1] if i + 1 < depth else None)
    return h.reshape(b, s, d)
```

```python
import functools
import math

import jax
import jax.numpy as jnp
import numpy as np
from jax import lax
from jax.experimental import pallas as pl
from jax.experimental.pallas import tpu as pltpu

F32 = jnp.float32
BF16 = jnp.bfloat16

NORM_EPS = 1e-6
RWKV_GN_EPS = 64e-5
L2_EPS = 1e-12
N_HEADS = 8
HEAD_DIM = 64
WIDTH = N_HEADS * HEAD_DIM
RWKV_LORA = 256
RWKV_IN = 3 * WIDTH + RWKV_LORA
VRES_LORA = 32
QK_NOPE = 64
QK_ROPE = 32
QK_DIM = QK_NOPE + QK_ROPE
Q_LORA = 256
KV_LORA = 128
MLA_IN = Q_LORA + KV_LORA + QK_ROPE
ROPE_THETA = 10000.0
GDN_CONV = 4
GDN_QKV = 3 * WIDTH
GDN_IN = GDN_QKV + 2 * N_HEADS + WIDTH
N_EXPERTS = 8
TOP_K = 2
LANES = 128
SUBLANES = 8
CHUNK = 64
SCAN_ROWS = 8 * CHUNK
VMEM_LIMIT = 48 * 1024 * 1024


def _cparams(sem):
    return pltpu.CompilerParams(dimension_semantics=sem, vmem_limit_bytes=VMEM_LIMIT)


def _norm_kernel(x_ref, g_ref, u_ref):
    x = x_ref[...]
    u_ref[...] = (x * lax.rsqrt(jnp.mean(x * x, axis=-1, keepdims=True) + NORM_EPS) * g_ref[...]).astype(u_ref.dtype)


def _norm(x, g, *, bm=1024):
    m, d = x.shape
    bm = min(bm, m)
    return pl.pallas_call(
        _norm_kernel,
        grid=(m // bm,),
        in_specs=[pl.BlockSpec((bm, d), lambda i: (i, 0)), pl.BlockSpec((1, d), lambda i: (0, 0))],
        out_specs=pl.BlockSpec((bm, d), lambda i: (i, 0)),
        out_shape=jax.ShapeDtypeStruct((m, d), BF16),
        compiler_params=_cparams(("arbitrary",)),
        name="norm",
    )(x, g.reshape(1, d))


def _router_kernel(x_ref, g_ref, wr_ref, before_ref, route_ref, counts_ref, carry_ref):
    @pl.when(pl.program_id(0) == 0)
    def _():
        carry_ref[...] = jnp.zeros_like(carry_ref)

    x = x_ref[...]
    rows = x.shape[0]
    u = x * lax.rsqrt(jnp.mean(x * x, axis=-1, keepdims=True) + NORM_EPS) * g_ref[...]
    lane = lax.broadcasted_iota(jnp.int32, (rows, LANES), 1)
    u_hi = u.astype(BF16)
    u_lo = (u - u_hi.astype(F32)).astype(BF16)
    hi_part = jnp.dot(u_hi, wr_ref[...], preferred_element_type=F32)
    logits = hi_part[:, :LANES] + hi_part[:, LANES:] + jnp.dot(u_lo, wr_ref[:, :LANES], preferred_element_type=F32)
    logits = jnp.where(lane < N_EXPERTS, logits, -jnp.inf)
    m1 = jnp.max(logits, axis=-1, keepdims=True)
    i1 = jnp.min(jnp.where(logits == m1, lane, LANES), axis=-1, keepdims=True)
    rest = jnp.where(lane == i1, -jnp.inf, logits)
    m2 = jnp.max(rest, axis=-1, keepdims=True)
    i2 = jnp.min(jnp.where(rest == m2, lane, LANES), axis=-1, keepdims=True)
    e2 = jnp.exp(m2 - m1)
    w1 = 1.0 / (1.0 + e2)
    w2 = e2 / (1.0 + e2)
    hit1 = lane == i1
    hit2 = lane == i2
    onehot = jnp.where(hit1 | hit2, 1.0, 0.0)
    seen = jnp.dot(before_ref[...], onehot.astype(BF16), preferred_element_type=F32) + carry_ref[0:1, :]
    r1 = jnp.sum(jnp.where(hit1, seen, 0.0), axis=-1, keepdims=True)
    r2 = jnp.sum(jnp.where(hit2, seen, 0.0), axis=-1, keepdims=True)
    carry_ref[...] = carry_ref[...] + jnp.sum(onehot, axis=0, keepdims=True)
    cols = (i1.astype(F32), i2.astype(F32), w1, w2, r1, r2)
    route = jnp.zeros((rows, LANES), F32)
    for n, c in enumerate(cols):
        route = jnp.where(lane == n, c, route)
    route_ref[...] = route
    counts_ref[...] = carry_ref[...]


def _router(h, g, w_router, *, bm=1024):
    m, d = h.shape
    bm = min(bm, m)
    w = _pad_cols(w_router, LANES)
    w_hi = w.astype(BF16)
    w_split = _cat([w_hi, (w - w_hi.astype(F32)).astype(BF16)], 1)
    return pl.pallas_call(
        _router_kernel,
        grid=(m // bm,),
        in_specs=[pl.BlockSpec((bm, d), lambda i: (i, 0)), pl.BlockSpec((1, d), lambda i: (0, 0)),
                  pl.BlockSpec((d, 2 * LANES), lambda i: (0, 0)), pl.BlockSpec((bm, bm), lambda i: (0, 0))],
        out_specs=[pl.BlockSpec((bm, LANES), lambda i: (i, 0)), pl.BlockSpec((SUBLANES, LANES), lambda i: (0, 0))],
        out_shape=[jax.ShapeDtypeStruct((m, LANES), F32), jax.ShapeDtypeStruct((SUBLANES, LANES), F32)],
        scratch_shapes=[pltpu.VMEM((SUBLANES, LANES), F32)],
        compiler_params=_cparams(("arbitrary",)),
        name="router",
    )(h, g.reshape(1, d), w_split, jnp.tril(jnp.ones((bm, bm), BF16), -1))


def _rms(x, n, g):
    return x * lax.rsqrt(jnp.sum(x * x, axis=-1, keepdims=True) * (1.0 / n) + NORM_EPS) * g


def _mla_prep_kernel(u_ref, cos_ref, sin_ref, win_ref, gq_ref, gkv_ref, gqh_ref, gkh_ref, wq_ref, wqr_ref, wk_ref,
                     wv_ref, q_ref, k_ref, v_ref):
    z = jnp.dot(u_ref[...], win_ref[...], preferred_element_type=F32)
    cq = _rms(z[:, :Q_LORA], Q_LORA, gq_ref[...]).astype(BF16)
    ckv = _rms(z[:, Q_LORA:Q_LORA + KV_LORA], KV_LORA, gkv_ref[...]).astype(BF16)
    k_pe = z[:, Q_LORA + KV_LORA:Q_LORA + KV_LORA + LANES]
    k_pe_rot = z[:, Q_LORA + KV_LORA + LANES:]
    q_all = jnp.dot(cq, wq_ref[...], preferred_element_type=F32)
    q_rot = jnp.dot(cq, wqr_ref[...], preferred_element_type=F32)
    k_all = jnp.dot(ckv, wk_ref[...], preferred_element_type=F32)
    v_all = jnp.dot(ckv, wv_ref[...], preferred_element_type=F32)
    sin = sin_ref[...]
    cos_q = cos_ref[...] * gqh_ref[...]
    cos_k = cos_ref[...] * gkh_ref[...]

    def scale(x):
        return lax.rsqrt(jnp.sum(x * x, axis=-1, keepdims=True) * (1.0 / QK_DIM) + NORM_EPS)

    ones_lane = lax.broadcasted_iota(jnp.int32, sin.shape, 1) == HEAD_DIM
    for h in range(N_HEADS):
        sl = slice(h * LANES, (h + 1) * LANES)
        q = q_all[:, sl]
        k = k_all[:, sl] + k_pe
        q_ref[h] = (scale(q) * (QK_DIM ** -0.5) * (q * cos_q + q_rot[:, sl] * sin)).astype(BF16)
        k_ref[h] = (scale(k) * (k * cos_k + k_pe_rot * sin)).astype(BF16)
        v_ref[h] = jnp.where(ones_lane, 1.0, v_all[:, sl]).astype(BF16)


def _pad_heads(w, width):
    k = w.shape[0]
    w = w.reshape(k, N_HEADS, width)
    return jnp.pad(w, ((0, 0), (0, 0), (0, LANES - width))).reshape(k, N_HEADS * LANES)


def _mla_prep(u, w_in, rope_tabs, b, s, q_norm_g, kv_norm_g, w_uq, w_ukv, qk_g_q, qk_g_k, *, bm=1024):
    bm = min(bm, s)
    half = QK_ROPE // 2
    swap = lambda w: _cat([w[..., half:], w[..., :half]], -1)
    rope_slab = lambda w: jnp.pad(w, ((0, 0), (QK_NOPE, LANES - QK_DIM)))
    lora = Q_LORA + KV_LORA
    w_in = _cat([w_in[:, :lora], rope_slab(w_in[:, lora:]), rope_slab(swap(w_in[:, lora:] * qk_g_k[QK_NOPE:]))],
                1).astype(BF16)
    w_ukv = w_ukv.reshape(KV_LORA, N_HEADS, QK_NOPE + HEAD_DIM)
    w_uq3 = w_uq.reshape(Q_LORA, N_HEADS, QK_DIM)
    wq = _pad_heads(w_uq, QK_DIM).astype(BF16)
    wq_rot = jnp.pad(swap(w_uq3[..., QK_NOPE:] * qk_g_q[QK_NOPE:]), ((0, 0), (0, 0), (QK_NOPE, LANES - QK_DIM)))
    wq_rot = wq_rot.reshape(Q_LORA, N_HEADS * LANES).astype(BF16)
    wk = _pad_heads(w_ukv[..., :QK_NOPE].reshape(KV_LORA, -1), QK_NOPE).astype(BF16)
    wv = _pad_heads(w_ukv[..., QK_NOPE:].reshape(KV_LORA, -1), HEAD_DIM).astype(BF16)
    pad_g = lambda g: jnp.pad(g, (0, LANES - g.shape[0])).reshape(1, LANES)
    row = lambda width: pl.BlockSpec((bm, width), lambda bb, i: (bb * (s // bm) + i, 0))
    full = lambda a: pl.BlockSpec(a.shape, lambda bb, i: (0, 0))
    consts = [w_in, q_norm_g.reshape(1, -1), kv_norm_g.reshape(1, -1), pad_g(qk_g_q), pad_g(qk_g_k), wq, wq_rot,
              wk, wv]
    out_spec = pl.BlockSpec((None, N_HEADS, bm, LANES), lambda bb, i: (bb, 0, i, 0))
    out_shape = jax.ShapeDtypeStruct((b, N_HEADS, s, LANES), BF16)
    return pl.pallas_call(
        _mla_prep_kernel,
        grid=(b, s // bm),
        in_specs=[row(u.shape[1]), row(LANES), row(LANES)] + [full(a) for a in consts],
        out_specs=[out_spec] * 3,
        out_shape=[out_shape] * 3,
        compiler_params=_cparams(("arbitrary", "arbitrary")),
        name="mla_prep",
    )(u, *rope_tabs, *consts)


def _flash_kernel(qi_ref, kj_ref, q_ref, k_ref, v_ref, o_ref, m_ref, acc_ref, *, blk, sub):
    n = pl.program_id(2)
    i = qi_ref[n]
    j = kj_ref[n]
    hb = q_ref.shape[0]

    @pl.when(j == 0)
    def _():
        m_ref[...] = jnp.full_like(m_ref, -jnp.inf)
        acc_ref[...] = jnp.zeros_like(acc_ref)

    def update(masked):
        chains = [(h, r) for h in range(hb) for r in range(blk // sub)]

        def qk(h, r):
            nk = (r + 1) * sub if masked else blk
            s = lax.dot_general(q_ref[h, r * sub:(r + 1) * sub, :], k_ref[h, :nk, :], (((1,), (1,)), ((), ())),
                                preferred_element_type=F32)
            if masked:
                row = lax.broadcasted_iota(jnp.int32, (sub, nk), 0) + r * sub
                s = jnp.where(row >= lax.broadcasted_iota(jnp.int32, (sub, nk), 1), s, -jnp.inf)
            return s

        def softmax_pv(h, r, s):
            rows = slice(r * sub, (r + 1) * sub)
            m_old = m_ref[h, rows, :]
            m_new = jnp.maximum(m_old, jnp.max(s, axis=-1, keepdims=True))
            alpha = jnp.exp(m_old - m_new)
            p = jnp.exp((s - m_new[:, :1]).astype(BF16))
            m_ref[h, rows, :] = m_new
            pv = jnp.dot(p, v_ref[h, :s.shape[1], :], preferred_element_type=F32)
            acc_ref[h, rows, :] = alpha * acc_ref[h, rows, :] + pv

        s_prev = qk(*chains[0])
        for c in range(len(chains)):
            s_next = qk(*chains[c + 1]) if c + 1 < len(chains) else None
            softmax_pv(*chains[c], s_prev)
            s_prev = s_next

    @pl.when(j < i)
    def _():
        update(False)

    @pl.when(j == i)
    def _():
        update(True)
        acc = acc_ref[...]
        o_ref[...] = (acc / acc[:, :, HEAD_DIM:HEAD_DIM + 1]).astype(o_ref.dtype)


def _flash_attention(q, k, v, *, blk=2048, sub=256, hb=4):
    b, h, s, d = q.shape
    blk = min(blk, s)
    sub = min(sub, blk)
    nb = s // blk
    qi = np.array([i for i in range(nb) for j in range(i + 1)], np.int32)
    kj = np.array([j for i in range(nb) for j in range(i + 1)], np.int32)
    q_spec = pl.BlockSpec((None, hb, blk, d), lambda bb, hh, n, qi_r, kj_r: (bb, hh, qi_r[n], 0))
    kv_spec = pl.BlockSpec((None, hb, blk, d), lambda bb, hh, n, qi_r, kj_r: (bb, hh, kj_r[n], 0))
    return pl.pallas_call(
        functools.partial(_flash_kernel, blk=blk, sub=sub),
        grid_spec=pltpu.PrefetchScalarGridSpec(
            num_scalar_prefetch=2,
            grid=(b, h // hb, len(qi)),
            in_specs=[q_spec, kv_spec, kv_spec],
            out_specs=q_spec,
            scratch_shapes=[pltpu.VMEM((hb, blk, LANES), F32), pltpu.VMEM((hb, blk, d), F32)],
        ),
        out_shape=jax.ShapeDtypeStruct((b, h, s, d), BF16),
        compiler_params=_cparams(("arbitrary", "arbitrary", "arbitrary")),
        name="flash",
    )(jnp.asarray(qi), jnp.asarray(kj), q, k, v)


def _dot(a, b):
    return jnp.dot(a.astype(BF16), b.astype(BF16), preferred_element_type=F32)


def _dot_nt(a, b):
    return lax.dot_general(a.astype(BF16), b.astype(BF16), (((1,), (1,)), ((), ())), preferred_element_type=F32)


def _dot_tn(a, b):
    return jnp.dot(a.T.astype(BF16), b.astype(BF16), preferred_element_type=F32)


def _split3(x):
    hi = x.astype(BF16)
    r1 = x - hi.astype(F32)
    mid = r1.astype(BF16)
    return hi, mid, (r1 - mid.astype(F32)).astype(BF16)


def _cumsum_rows(tri, x):
    s = jnp.dot(tri, jnp.concatenate(_split3(x), axis=1), preferred_element_type=F32)
    n = x.shape[1]
    return s[:, :n] + s[:, n:2 * n] + s[:, 2 * n:]


def _head_sum(x):
    left = lax.broadcasted_iota(jnp.int32, (x.shape[0], LANES), 1) < HEAD_DIM
    outs = []
    for p in range(WIDTH // LANES):
        xs = x[:, p * LANES:(p + 1) * LANES]
        s_left = jnp.sum(jnp.where(left, xs, 0.0), axis=-1, keepdims=True)
        s_right = jnp.sum(jnp.where(left, 0.0, xs), axis=-1, keepdims=True)
        outs.append(jnp.where(left, s_left, s_right))
    return jnp.concatenate(outs, axis=1)


def _shift_rows(x, tail, j):
    rolled = pltpu.roll(x, j, 0)
    top = jnp.where(lax.broadcasted_iota(jnp.int32, (SUBLANES, x.shape[1]), 0) < j,
                    pltpu.roll(tail, j, 0), rolled[:SUBLANES])
    return jnp.concatenate([top, rolled[SUBLANES:]], axis=0)


def _softplus(x):
    return jnp.maximum(x, 0.0) + jnp.log(1.0 + jnp.exp(-jnp.abs(x)))


def _pair_masks():
    lane = lax.broadcasted_iota(jnp.int32, (CHUNK, LANES), 1)
    m0 = (lane < HEAD_DIM).astype(F32)
    return m0, 1.0 - m0


def _ext(x, m0, m1, dtype=BF16):
    return jnp.concatenate([x * m0, x * m1], axis=0).astype(dtype)


def _cat(xs, axis):
    return jnp.concatenate(xs, axis=axis)


SCAN_GROUP = 2


def _neumann_inverse(mats):
    n = mats[0].shape[0]
    eye = (lax.broadcasted_iota(jnp.int32, (n, n), 0) == lax.broadcasted_iota(jnp.int32, (n, n), 1)).astype(F32)
    ts = [eye + a for a in mats]
    ps = [_dot(a, a) for a in mats]
    yield
    for _ in range(int(math.log2(CHUNK)) - 2):
        pps = [_dot(_cat([p, t], 0), p) for p, t in zip(ps, ts)]
        ps = [pp[:n] for pp in pps]
        ts = [t + pp[n:] for t, pp in zip(ts, pps)]
        yield
    return [t + _dot(t, p) for t, p in zip(ts, ps)]


def _interleave(prepare, serial, n_groups):
    for _ in prepare(0):
        pass
    for g in range(n_groups):
        gens = [serial(g)] + ([prepare(g + 1)] if g + 1 < n_groups else [])
        while gens:
            for gen in list(gens):
                if next(gen, StopIteration) is StopIteration:
                    gens.remove(gen)


def _work_items(rows):
    return [(slice(c * CHUNK, (c + 1) * CHUNK), slice(p * LANES, (p + 1) * LANES))
            for c in range(rows // CHUNK) for p in range(WIDTH // LANES)]


def _tiles(x, items):
    return [x[rows, cols] for rows, cols in items]


def _rwkv_scan(r, k, v, a, b, ld, st_ref, y_ref):
    c2 = 2 * CHUNK
    strict = lax.broadcasted_iota(jnp.int32, (c2, c2), 0) > lax.broadcasted_iota(jnp.int32, (c2, c2), 1)
    trow = lax.broadcasted_iota(jnp.int32, (CHUNK, c2), 0)
    tcol = lax.broadcasted_iota(jnp.int32, (CHUNK, c2), 1) & (CHUNK - 1)
    incl_w = tcol <= trow
    tri = (lax.broadcasted_iota(jnp.int32, (CHUNK, CHUNK), 0)
           >= lax.broadcasted_iota(jnp.int32, (CHUNK, CHUNK), 1)).astype(BF16)
    m0, m1 = _pair_masks()
    items = _work_items(r.shape[0])
    n_pairs = WIDTH // LANES
    per_group = min(SCAN_GROUP, len(items) // n_pairs) * n_pairs
    full = (r, k, v, a, b, ld)
    ready = {}
    state = [st_ref[p] for p in range(n_pairs)]

    def prepare(g):
        sub = items[g * per_group:(g + 1) * per_group]
        r, k, v, a, b, ld = (_tiles(x, sub) for x in full)
        cum = [_cumsum_rows(tri, x) for x in ld]
        yield
        e_pos = [jnp.exp(x) for x in cum]
        e_neg = [jnp.exp(-x) for x in cum]
        e_end = [jnp.exp(x[CHUNK - 1:CHUNK, :] - x) for x in cum]
        a_ext = [_ext(ai * jnp.exp(ci - li), m0, m1) for ai, ci, li in zip(a, cum, ld)]
        v_ext = [_ext(x, m0, m1) for x in v]
        r_dec = [(ri * ei).astype(BF16) for ri, ei in zip(r, e_pos)]
        yield
        s1 = [_dot_nt(_cat([ae, rd], 0), _cat([_ext(bi * en, m0, m1), _ext(ki * en, m0, m1)], 0))
              for ae, rd, bi, ki, en in zip(a_ext, r_dec, b, k, e_neg)]
        yield
        a_ab = [jnp.where(strict, s[:c2, :c2], 0.0) for s in s1]
        a_ak = [jnp.where(strict, s[:c2, c2:], 0.0) for s in s1]
        a_rbk = [_cat([jnp.where(incl_w, s[c2:, :c2], 0.0), jnp.where(incl_w, s[c2:, c2:], 0.0)], 1).astype(BF16)
                 for s in s1]
        akv = [_dot(x, ve) for x, ve in zip(a_ak, v_ext)]
        t_inv = yield from _neumann_inverse(a_ab)
        yield
        wu0 = [_dot(t, _cat([ae, av.astype(BF16)], 1)) for t, ae, av in zip(t_inv, a_ext, akv)]
        bk_end = [_cat([_ext(bi * ee, m0, m1, F32), _ext(ki * ee, m0, m1, F32)], 0)
                  for bi, ki, ee in zip(b, k, e_end)]
        e_last_t = [jnp.broadcast_to(e[CHUNK - 1:CHUNK, :], (LANES, LANES)).T for e in e_pos]
        ready[g] = (sub, wu0, r_dec, v_ext, a_rbk, bk_end, e_last_t)

    def serial(g):
        sub, wu0, r_dec, v_ext, a_rbk, bk_end, e_last_t = ready.pop(g)
        for c in range(len(sub) // n_pairs):
            idx = range(c * n_pairs, (c + 1) * n_pairs)
            z = [_dot(_cat([wu0[i][:, :LANES].astype(BF16), r_dec[i]], 0), s) for i, s in zip(idx, state)]
            yield
            uv = [_cat([(wu0[i][:, LANES:] + zi[:c2]).astype(BF16), v_ext[i]], 0) for i, zi in zip(idx, z)]
            for i, zi, uvi in zip(idx, z, uv):
                rows, cols = sub[i]
                y_ref[rows, cols] = zi[c2:] + _dot(a_rbk[i], uvi)
            state[:] = [s * e_last_t[i] + _dot_tn(bk_end[i], uvi) for i, s, uvi in zip(idx, state, uv)]
            yield

    _interleave(prepare, serial, len(items) // per_group)
    for p in range(n_pairs):
        st_ref[p] = state[p]


def _gdn_scan(q, k, v, g, beta, st_ref, o_ref):
    c2 = 2 * CHUNK
    row = lax.broadcasted_iota(jnp.int32, (c2, c2), 0)
    col = lax.broadcasted_iota(jnp.int32, (c2, c2), 1)
    causal = ((row >= CHUNK) == (col >= CHUNK)) & (row >= col)
    strict = row > col
    tri = (lax.broadcasted_iota(jnp.int32, (CHUNK, CHUNK), 0)
           >= lax.broadcasted_iota(jnp.int32, (CHUNK, CHUNK), 1)).astype(BF16)
    m0, m1 = _pair_masks()
    items = _work_items(q.shape[0])
    n_pairs = WIDTH // LANES
    per_group = min(SCAN_GROUP, len(items) // n_pairs) * n_pairs
    full = (q, k, v, g, beta)
    ready = {}
    state = [st_ref[p] for p in range(n_pairs)]

    def prepare(grp):
        sub = items[grp * per_group:(grp + 1) * per_group]
        q, k, v, g, beta = (_tiles(x, sub) for x in full)
        gc = [_cumsum_rows(tri, x) for x in g]
        yield
        g_rows = [_cat([jnp.broadcast_to(x[:, 0:1], (CHUNK, LANES)),
                        jnp.broadcast_to(x[:, HEAD_DIM:HEAD_DIM + 1], (CHUNK, LANES))], 0) for x in gc]
        gamma = [jnp.exp(jnp.where(causal, x - x.T, -jnp.inf)) for x in g_rows]
        e_pos = [jnp.exp(x) for x in gc]
        kb = [ki * bi for ki, bi in zip(k, beta)]
        yield
        s1 = [_dot_nt(_cat([_ext(kbi, m0, m1), qi.astype(BF16)], 0), _ext(ki, m0, m1))
              for kbi, qi, ki in zip(kb, q, k)]
        yield
        a_neg = [-jnp.where(strict, s[:c2] * ga, 0.0) for s, ga in zip(s1, gamma)]
        a_in = [(s[c2:] * (ga[:CHUNK] + ga[CHUNK:])).astype(BF16) for s, ga in zip(s1, gamma)]
        t_inv = yield from _neumann_inverse(a_neg)
        yield
        uw = [_dot(t, _cat([_ext(vi * bi, m0, m1), _ext(kbi * ei, m0, m1)], 1))
              for t, vi, bi, kbi, ei in zip(t_inv, v, beta, kb, e_pos)]
        q_dec = [(qi * ei).astype(BF16) for qi, ei in zip(q, e_pos)]
        k_dec = [_ext(ki * jnp.exp(x[CHUNK - 1:CHUNK, :] - x), m0, m1, F32) for ki, x in zip(k, gc)]
        ready[grp] = (sub, uw, q_dec, k_dec, a_in, e_pos)

    def serial(grp):
        sub, uw, q_dec, k_dec, a_in, e_pos = ready.pop(grp)
        for c in range(len(sub) // n_pairs):
            idx = range(c * n_pairs, (c + 1) * n_pairs)
            z = [_dot(_cat([uw[i][:, LANES:].astype(BF16), q_dec[i]], 0), s) for i, s in zip(idx, state)]
            yield
            v_new = [(uw[i][:, :LANES] - zi[:c2]).astype(BF16) for i, zi in zip(idx, z)]
            for i, zi, vn in zip(idx, z, v_new):
                rows, cols = sub[i]
                o_ref[rows, cols] = zi[c2:] + _dot(a_in[i], vn)
            state[:] = [s * e_pos[i][CHUNK - 1:CHUNK, :] + _dot_tn(k_dec[i], vn)
                        for i, s, vn in zip(idx, state, v_new)]
            yield

    _interleave(prepare, serial, len(items) // per_group)
    for p in range(n_pairs):
        st_ref[p] = state[p]


def _rwkv_kernel(*refs, vres):
    if vres:
        (u_ref, vf_ref, win_ref, mu_ref, prm_ref, wup_ref, aup_ref, gup_ref, vup_ref, o_ref, st_ref, tail_ref,
         y_ref) = refs
    else:
        u_ref, win_ref, mu_ref, prm_ref, wup_ref, aup_ref, gup_ref, o_ref, vf_out_ref, st_ref, tail_ref, y_ref = refs

    @pl.when(pl.program_id(1) == 0)
    def _():
        st_ref[...] = jnp.zeros_like(st_ref)
        tail_ref[...] = jnp.zeros_like(tail_ref)

    z = jnp.dot(u_ref[...], win_ref[...], preferred_element_type=F32)
    rows = z.shape[0]
    zs = _shift_rows(z, tail_ref[...], 1)
    tail_ref[...] = z[rows - SUBLANES:]
    zl = z[:, :RWKV_IN] + (zs[:, :RWKV_IN] - z[:, :RWKV_IN]) * mu_ref[...]
    r, k, v = zl[:, :WIDTH], zl[:, WIDTH:2 * WIDTH], zl[:, 2 * WIDTH:3 * WIDTH]
    lo = zl[:, 3 * WIDTH:3 * WIDTH + LANES]
    g_lo = zl[:, 3 * WIDTH + LANES:]
    w0, a0, k_k, k_a, ln_g, ln_b, r_k, v_bias = (prm_ref[i:i + 1, :] for i in range(8))
    log_w = -_softplus(-(w0 + _dot(jnp.tanh(lo), wup_ref[...]))) - 0.5
    ld = -jnp.exp(log_w)
    iclr = jax.nn.sigmoid(a0 + _dot(lo, aup_ref[...]))
    gate = _dot(jax.nn.sigmoid(g_lo), gup_ref[...])
    if vres:
        x = z[:, RWKV_IN:] + pltpu.roll(zs[:, RWKV_IN:], LANES - VRES_LORA, 1)
        v = v + (vf_ref[...] - v) * jax.nn.sigmoid(v_bias + _dot(x, vup_ref[...]))
    else:
        vf_out_ref[...] = v
    kk = k * k_k
    kk = kk * lax.rsqrt(_head_sum(kk * kk) + L2_EPS)
    k = k * (1.0 + (iclr - 1.0) * k_a)
    _rwkv_scan(r, k, v, -kk, kk * iclr, ld, st_ref, y_ref)
    y = y_ref[...]
    d = y - _head_sum(y) * (1.0 / HEAD_DIM)
    y = d * lax.rsqrt(_head_sum(d * d) * (1.0 / HEAD_DIM) + RWKV_GN_EPS) * ln_g + ln_b
    y = y + _head_sum(r * k * r_k) * v
    o_ref[...] = (y * gate).astype(o_ref.dtype)


def _rwkv(u, w_in, v_first, b, s, mu, prm, w_up, a_up, g_up, v_up):
    vres = v_first is not None
    rows = min(SCAN_ROWS, s)
    zw = w_in.shape[1]
    row = lambda width: pl.BlockSpec((rows, width), lambda bb, c: (bb * (s // rows) + c, 0))
    full = lambda a: pl.BlockSpec(a.shape, lambda bb, c: (0, 0))
    zero = jnp.zeros((LANES // 2, WIDTH), F32)
    consts = [w_in.astype(BF16), mu.reshape(1, -1), prm, _cat([w_up, zero], 0).astype(BF16),
              _cat([zero, a_up], 0).astype(BF16), g_up.astype(BF16)]
    args, in_specs = [u], [row(u.shape[1])]
    if vres:
        args.append(v_first)
        in_specs.append(row(WIDTH))
        consts.append(jnp.pad(v_up, ((0, LANES - VRES_LORA), (0, 0))).astype(BF16))
    out_shape = [jax.ShapeDtypeStruct((b * s, WIDTH), BF16)]
    out_specs = [row(WIDTH)]
    if not vres:
        out_shape.append(jax.ShapeDtypeStruct((b * s, WIDTH), F32))
        out_specs.append(row(WIDTH))
    outs = pl.pallas_call(
        functools.partial(_rwkv_kernel, vres=vres),
        grid=(b, s // rows),
        in_specs=in_specs + [full(a) for a in consts],
        out_specs=out_specs,
        out_shape=out_shape,
        scratch_shapes=[pltpu.VMEM((WIDTH // LANES, LANES, LANES), F32), pltpu.VMEM((SUBLANES, zw), F32),
                        pltpu.VMEM((rows, WIDTH), F32)],
        compiler_params=_cparams(("arbitrary", "arbitrary")),
        name="rwkv",
    )(*args, *consts)
    return (outs[0], v_first) if vres else (outs[0], outs[1])


def _gdn_kernel(u_ref, win_ref, cw_ref, prm_ref, exp_ref, o_ref, st_ref, tail_ref, y_ref):
    @pl.when(pl.program_id(1) == 0)
    def _():
        st_ref[...] = jnp.zeros_like(st_ref)
        tail_ref[...] = jnp.zeros_like(tail_ref)

    z = jnp.dot(u_ref[...], win_ref[...], preferred_element_type=F32)
    rows = z.shape[0]
    x = z[:, :GDN_QKV]
    tail = tail_ref[...]
    conv = x * cw_ref[GDN_CONV - 1:GDN_CONV, :]
    for j in range(1, GDN_CONV):
        conv = conv + _shift_rows(x, tail, j) * cw_ref[GDN_CONV - 1 - j:GDN_CONV - j, :]
    tail_ref[...] = x[rows - SUBLANES:]
    qkv = conv * jax.nn.sigmoid(conv)
    q, k, v = qkv[:, :WIDTH], qkv[:, WIDTH:2 * WIDTH], qkv[:, 2 * WIDTH:]
    q = q * lax.rsqrt(_head_sum(q * q) + L2_EPS) * (HEAD_DIM ** -0.5)
    k = k * lax.rsqrt(_head_sum(k * k) + L2_EPS)
    logits = jnp.dot(_cat(_split3(z[:, GDN_QKV + WIDTH:]), 0), exp_ref[...], preferred_element_type=F32)
    logits = logits[:rows] + logits[rows:2 * rows] + logits[2 * rows:]
    neg_a, dt_bias, norm_g = (prm_ref[i:i + 1, :] for i in range(3))
    beta = jax.nn.sigmoid(logits[:, :WIDTH])
    g = neg_a * _softplus(logits[:, WIDTH:] + dt_bias)
    _gdn_scan(q, k, v, g, beta, st_ref, y_ref)
    o = y_ref[...]
    o = o * lax.rsqrt(_head_sum(o * o) * (1.0 / HEAD_DIM) + NORM_EPS) * norm_g
    gate = z[:, GDN_QKV:GDN_QKV + WIDTH]
    o_ref[...] = (o * (gate * jax.nn.sigmoid(gate))).astype(o_ref.dtype)


def _gdn(u, w_in, b, s, conv_w, a_log, dt_bias, norm_g):
    rows = min(SCAN_ROWS, s)
    row = lambda width: pl.BlockSpec((rows, width), lambda bb, c: (bb * (s // rows) + c, 0))
    full = lambda a: pl.BlockSpec(a.shape, lambda bb, c: (0, 0))
    per_lane = lambda t: jnp.repeat(t, HEAD_DIM)
    prm = jnp.stack([per_lane(-jnp.exp(a_log)), per_lane(dt_bias), jnp.tile(norm_g, N_HEADS)])
    head_of_lane = np.arange(WIDTH) // HEAD_DIM
    expand = np.zeros((LANES, 2 * WIDTH), np.float32)
    expand[head_of_lane, np.arange(WIDTH)] = 1.0
    expand[N_HEADS + head_of_lane, WIDTH + np.arange(WIDTH)] = 1.0
    consts = [w_in.astype(BF16), conv_w, prm, jnp.asarray(expand, BF16)]
    return pl.pallas_call(
        _gdn_kernel,
        grid=(b, s // rows),
        in_specs=[row(u.shape[1])] + [full(a) for a in consts],
        out_specs=row(WIDTH),
        out_shape=jax.ShapeDtypeStruct((b * s, WIDTH), BF16),
        scratch_shapes=[pltpu.VMEM((WIDTH // LANES, LANES, LANES), F32), pltpu.VMEM((SUBLANES, GDN_QKV), F32),
                        pltpu.VMEM((rows, WIDTH), F32)],
        compiler_params=_cparams(("arbitrary", "arbitrary")),
        name="gdn",
    )(u, *consts)


def _merge_kernel(h_ref, u_ref, oa_ref, ob_ref, oc_ref, wg_ref, wa_ref, wb_ref, wc_ref, wo_ref, out_ref):
    d = h_ref.shape[1]
    u = u_ref[...]
    ob = jnp.concatenate([ob_ref[h] for h in range(N_HEADS)], axis=1)
    merged = jnp.zeros(h_ref.shape, F32)
    for n, (o, w_ref) in enumerate(((oa_ref[...], wa_ref), (ob, wb_ref), (oc_ref[...], wc_ref))):
        proj = jnp.dot(o, w_ref[...], preferred_element_type=F32)
        zg = jnp.dot(u, wg_ref[:, n * d:(n + 1) * d], preferred_element_type=F32)
        merged = merged + jax.nn.sigmoid(zg) * proj
    out_ref[...] = h_ref[...] + jnp.dot(merged.astype(BF16), wo_ref[...], preferred_element_type=F32)


def _merge(h, u, oa, ob, oc, wg, wa, wb, wc, wo, s, *, bm=512):
    m, d = h.shape
    bm = min(bm, s)
    row = lambda width: pl.BlockSpec((bm, width), lambda i: (i, 0))
    full = lambda w: pl.BlockSpec(w.shape, lambda i: (0, 0))
    wb = jnp.pad(wb.reshape(N_HEADS, HEAD_DIM, d), ((0, 0), (0, LANES - HEAD_DIM), (0, 0))).reshape(N_HEADS * LANES, d)
    ws = [w.astype(BF16) for w in (wg, wa, wb, wc, wo)]
    ob_spec = pl.BlockSpec((None, N_HEADS, bm, LANES), lambda i: (i // (s // bm), 0, i % (s // bm), 0))
    return pl.pallas_call(
        _merge_kernel,
        grid=(m // bm,),
        in_specs=[row(d), row(d), row(WIDTH), ob_spec, row(WIDTH)] + [full(w) for w in ws],
        out_specs=row(d),
        out_shape=jax.ShapeDtypeStruct((m, d), F32),
        compiler_params=_cparams(("arbitrary",)),
        name="merge",
    )(h, u, oa, ob, oc, *ws)


def _ffn_kernel(be_ref, nu_ref, x_ref, g_ref, wg_ref, wu_ref, wd_ref, o_ref, xs_ref, acc_ref, *, nf, dense):
    i = pl.program_id(0)
    f = pl.program_id(1)
    used = i < nu_ref[0]

    @pl.when(used & (f == 0))
    def _():
        x = x_ref[...]
        ms = jnp.mean(x * x, axis=-1, keepdims=True)
        xs_ref[...] = (x * lax.rsqrt(ms + NORM_EPS) * g_ref[...]).astype(BF16)

    @pl.when(f == 0)
    def _():
        acc_ref[...] = jnp.zeros_like(acc_ref)

    @pl.when(used)
    def _():
        xs = xs_ref[...]
        gate = jnp.dot(xs, wg_ref[...].astype(BF16), preferred_element_type=F32)
        up = jnp.dot(xs, wu_ref[...].astype(BF16), preferred_element_type=F32)
        act = (gate * jax.nn.sigmoid(gate) * up).astype(BF16)
        acc_ref[...] += jnp.dot(act, wd_ref[...].astype(BF16), preferred_element_type=F32)

    @pl.when(f == nf - 1)
    def _():
        o_ref[...] = (x_ref[...] + acc_ref[...]) if dense else acc_ref[...]


def _ffn(x, g, wg, wu, wd, block_e, n_used, *, bm, tf, dense):
    r, d = x.shape
    ff = wg.shape[2]
    nf = ff // tf

    def ff_idx(i, f, be, nu):
        return jnp.where(i < nu[0], f, nf - 1)

    return pl.pallas_call(
        functools.partial(_ffn_kernel, nf=nf, dense=dense),
        grid_spec=pltpu.PrefetchScalarGridSpec(
            num_scalar_prefetch=2,
            grid=(r // bm, nf),
            in_specs=[
                pl.BlockSpec((bm, d), lambda i, f, be, nu: (i, 0)),
                pl.BlockSpec((1, d), lambda i, f, be, nu: (0, 0)),
                pl.BlockSpec((None, d, tf), lambda i, f, be, nu: (be[i], 0, ff_idx(i, f, be, nu))),
                pl.BlockSpec((None, d, tf), lambda i, f, be, nu: (be[i], 0, ff_idx(i, f, be, nu))),
                pl.BlockSpec((None, tf, d), lambda i, f, be, nu: (be[i], ff_idx(i, f, be, nu), 0)),
            ],
            out_specs=pl.BlockSpec((bm, d), lambda i, f, be, nu: (i, 0)),
            scratch_shapes=[pltpu.VMEM((bm, d), BF16), pltpu.VMEM((bm, d), F32)],
        ),
        out_shape=jax.ShapeDtypeStruct((r, d), F32),
        compiler_params=_cparams(("arbitrary", "arbitrary")),
        name="ffn",
    )(block_e, n_used, x, g.reshape(1, d).astype(F32), wg, wu, wd)


def _ple_kernel(*refs, combine, emit_norm):
    refs = list(refs)
    h = refs.pop(0)[...]
    if combine:
        y0_ref, y1_ref, route_ref = refs[:3]
        del refs[:3]
        h = h + route_ref[:, 2:3] * y0_ref[...] + route_ref[:, 3:4] * y1_ref[...]
    p_ref, wg_ref, wp_ref, g_ref = refs[:4]
    gate = jnp.dot(h.astype(BF16), wg_ref[...], preferred_element_type=F32)
    e = jnp.dot(p_ref[...].astype(BF16), wp_ref[...], preferred_element_type=F32)
    ms = jnp.mean(e * e, axis=-1, keepdims=True)
    e = e * lax.rsqrt(ms + NORM_EPS) * g_ref[...]
    out = h + jax.nn.sigmoid(gate) * e
    if emit_norm:
        gn_ref, o_ref, u_ref = refs[4:]
        u_ref[...] = (out * lax.rsqrt(jnp.mean(out * out, axis=-1, keepdims=True) + NORM_EPS) * gn_ref[...]).astype(BF16)
    else:
        o_ref, = refs[4:]
    o_ref[...] = out


def _ple(h, p, w_gate, w_proj, g, expert_out=None, g_next=None, *, bm=512):
    m, d = h.shape
    bm = min(bm, m)
    row = lambda a: pl.BlockSpec((bm, a.shape[1]), lambda i: (i, 0))
    full = lambda a: pl.BlockSpec(a.shape, lambda i: (0, 0))
    rows = [h] + list(expert_out or ()) + [p]
    consts = [w_gate.astype(BF16), w_proj.astype(BF16), g.reshape(1, d)]
    out_shape = [jax.ShapeDtypeStruct((m, d), F32)]
    if g_next is not None:
        consts.append(g_next.reshape(1, d))
        out_shape.append(jax.ShapeDtypeStruct((m, d), BF16))
    outs = pl.pallas_call(
        functools.partial(_ple_kernel, combine=expert_out is not None, emit_norm=g_next is not None),
        grid=(m // bm,),
        in_specs=[row(a) for a in rows] + [full(a) for a in consts],
        out_specs=[row(h)] * len(out_shape),
        out_shape=out_shape,
        compiler_params=_cparams(("arbitrary",)),
        name="ple",
    )(*rows, *consts)
    return outs if g_next is not None else (outs[0], None)


def _pad_cols(w, n):
    return jnp.pad(w, ((0, 0), (0, n - w.shape[1])))


def _rope_tables(positions):
    t = positions.size
    inv_freq = 1.0 / (ROPE_THETA ** (jnp.arange(0, QK_ROPE, 2, dtype=F32) / QK_ROPE))
    ang = positions.astype(F32).reshape(t, 1) * inv_freq
    cos, sin = jnp.cos(ang), jnp.sin(ang)
    pad = ((0, 0), (QK_NOPE, LANES - QK_DIM))
    return jnp.pad(_cat([cos, cos], 1), pad, constant_values=1.0), jnp.pad(_cat([-sin, sin], 1), pad)


def _moe(h, g, w_router, wg, wu, wd, *, bm=1024, tf=512):
    t, d = h.shape
    n_assign = t * TOP_K
    n_blocks = -(-(n_assign + N_EXPERTS * (bm - 1)) // bm)
    route, counts = _router(h, g, w_router)
    counts = counts[0, :N_EXPERTS].astype(jnp.int32)
    padded = (counts + bm - 1) // bm * bm
    pad_end = jnp.cumsum(padded)
    top_e = route[:, :TOP_K].astype(jnp.int32)
    dest = (pad_end - padded)[top_e] + route[:, 4:4 + TOP_K].astype(jnp.int32)
    src_tok = (jnp.arange(n_blocks * bm, dtype=jnp.int32) % t).at[dest.reshape(-1)].set(
        jnp.arange(n_assign, dtype=jnp.int32) // TOP_K)
    block_start = jnp.arange(n_blocks, dtype=jnp.int32) * bm
    block_e = jnp.minimum(jnp.sum(pad_end[None, :] <= block_start[:, None], axis=1), N_EXPERTS - 1).astype(jnp.int32)
    n_used = (pad_end[-1] // bm).astype(jnp.int32).reshape(1)
    y_rows = _ffn(h[src_tok], g, wg, wu, wd, block_e, n_used, bm=bm, tf=tf, dense=False)
    return y_rows[dest[:, 0]], y_rows[dest[:, 1]], route


def kernel(x, p, positions, norm_mix_g, w_in, rwkv_mu, rwkv_w0, rwkv_w_up, rwkv_a0, rwkv_a_up, rwkv_g_up, rwkv_k_k, rwkv_k_a, rwkv_r_k, rwkv_ln_g, rwkv_ln_b, vres_mu, vres_down, vres_up, vres_b, mla_q_norm_g, mla_kv_norm_g, mla_w_uq, mla_w_ukv, mla_qk_norm_q, mla_qk_norm_k, gdn_conv_w, gdn_a_log, gdn_dt_bias, gdn_norm_g, w_br_rwkv, w_br_mla, w_br_gdn, w_out, norm_ffn_g, ffn_wg, ffn_wu, ffn_wd, moe_router, moe_wg, moe_wu, moe_wd, ple_proj, ple_gate, ple_norm_g):
    b, s, d = x.shape
    t = b * s
    depth = w_in.shape[0]
    rope_tabs = _rope_tables(positions)

    h = x.reshape(t, d)
    v_first = None
    for i in range(depth):
        w = w_in[i]
        g = norm_mix_g[i]
        w_rwkv = w[:, :RWKV_IN]
        if i > 0:
            mu_v = vres_mu[i - 1][:, None]
            vd = vres_down[i - 1]
            w_rwkv = _pad_cols(_cat([w_rwkv, (1.0 - mu_v) * vd, mu_v * vd], 1), RWKV_IN + LANES)
        w_gdn = w[:, RWKV_IN + MLA_IN:RWKV_IN + MLA_IN + GDN_IN]
        w_gdn = _pad_cols(_cat([w_gdn[:, :GDN_QKV], w_gdn[:, GDN_QKV + 2 * N_HEADS:],
                                w_gdn[:, GDN_QKV:GDN_QKV + 2 * N_HEADS]], 1), GDN_QKV + WIDTH + LANES)
        if i == 0:
            u = _norm(h, g)
        prm = jnp.stack([rwkv_w0[i], rwkv_a0[i], rwkv_k_k[i], rwkv_k_a[i], rwkv_ln_g[i], rwkv_ln_b[i],
                         rwkv_r_k[i].reshape(-1), vres_b[i - 1] if i > 0 else jnp.zeros((WIDTH,), F32)])
        o_a, v_first = _rwkv(u, w_rwkv, v_first, b, s, rwkv_mu[i], prm, rwkv_w_up[i], rwkv_a_up[i], rwkv_g_up[i],
                             vres_up[i - 1] if i > 0 else None)
        q, k, v = _mla_prep(u, w[:, RWKV_IN:RWKV_IN + MLA_IN], rope_tabs, b, s, mla_q_norm_g[i], mla_kv_norm_g[i],
                            mla_w_uq[i], mla_w_ukv[i], mla_qk_norm_q[i], mla_qk_norm_k[i])
        o_b = _flash_attention(q, k, v)
        o_c = _gdn(u, w_gdn, b, s, gdn_conv_w[i], gdn_a_log[i], gdn_dt_bias[i], gdn_norm_g[i])
        h = _merge(h, u, o_a, o_b, o_c, w[:, RWKV_IN + MLA_IN + GDN_IN:], w_br_rwkv[i], w_br_mla[i], w_br_gdn[i],
                   w_out[i], s)
        j = i // 2
        if i % 2 == 0:
            bm = min(1024, t)
            h = _ffn(h, norm_ffn_g[i], ffn_wg[j:j + 1], ffn_wu[j:j + 1], ffn_wd[j:j + 1],
                     jnp.zeros((t // bm,), jnp.int32), jnp.full((1,), t // bm, jnp.int32), bm=bm, tf=256, dense=True)
            expert_out = None
        else:
            expert_out = _moe(h, norm_ffn_g[i], moe_router[j], moe_wg[j], moe_wu[j], moe_wd[j])
        h, u = _ple(h, p[i].reshape(t, -1), ple_gate[i], ple_proj[i], ple_norm_g[i], expert_out,
                    norm_mix_g[i + 1] if i + 1 < depth else None)
    return h.reshape(b, s, d)
```

```python
import functools
import math

import jax
import jax.numpy as jnp
import numpy as np
from jax import lax
from jax.experimental import pallas as pl
from jax.experimental.pallas import tpu as pltpu

F32 = jnp.float32
BF16 = jnp.bfloat16

NORM_EPS = 1e-6
RWKV_GN_EPS = 64e-5
L2_EPS = 1e-12
N_HEADS = 8
HEAD_DIM = 64
WIDTH = N_HEADS * HEAD_DIM
RWKV_LORA = 256
RWKV_IN = 3 * WIDTH + RWKV_LORA
VRES_LORA = 32
QK_NOPE = 64
QK_ROPE = 32
QK_DIM = QK_NOPE + QK_ROPE
Q_LORA = 256
KV_LORA = 128
MLA_IN = Q_LORA + KV_LORA + QK_ROPE
ROPE_THETA = 10000.0
GDN_CONV = 4
GDN_QKV = 3 * WIDTH
GDN_IN = GDN_QKV + 2 * N_HEADS + WIDTH
N_EXPERTS = 8
TOP_K = 2
LANES = 128
SUBLANES = 8
CHUNK = 64
SCAN_ROWS = 8 * CHUNK
VMEM_LIMIT = 48 * 1024 * 1024


def _cparams(sem):
    return pltpu.CompilerParams(dimension_semantics=sem, vmem_limit_bytes=VMEM_LIMIT)


def _norm_kernel(x_ref, g_ref, u_ref):
    x = x_ref[...]
    u_ref[...] = (x * lax.rsqrt(jnp.mean(x * x, axis=-1, keepdims=True) + NORM_EPS) * g_ref[...]).astype(u_ref.dtype)


def _norm(x, g, *, bm=1024):
    m, d = x.shape
    bm = min(bm, m)
    return pl.pallas_call(
        _norm_kernel,
        grid=(m // bm,),
        in_specs=[pl.BlockSpec((bm, d), lambda i: (i, 0)), pl.BlockSpec((1, d), lambda i: (0, 0))],
        out_specs=pl.BlockSpec((bm, d), lambda i: (i, 0)),
        out_shape=jax.ShapeDtypeStruct((m, d), BF16),
        compiler_params=_cparams(("arbitrary",)),
        name="norm",
    )(x, g.reshape(1, d))


def _router_kernel(x_ref, g_ref, wr_ref, before_ref, route_ref, counts_ref, carry_ref):
    @pl.when(pl.program_id(0) == 0)
    def _():
        carry_ref[...] = jnp.zeros_like(carry_ref)

    x = x_ref[...]
    rows = x.shape[0]
    u = x * lax.rsqrt(jnp.mean(x * x, axis=-1, keepdims=True) + NORM_EPS) * g_ref[...]
    lane = lax.broadcasted_iota(jnp.int32, (rows, LANES), 1)
    u_hi = u.astype(BF16)
    u_lo = (u - u_hi.astype(F32)).astype(BF16)
    hi_part = jnp.dot(u_hi, wr_ref[...], preferred_element_type=F32)
    logits = hi_part[:, :LANES] + hi_part[:, LANES:] + jnp.dot(u_lo, wr_ref[:, :LANES], preferred_element_type=F32)
    logits = jnp.where(lane < N_EXPERTS, logits, -jnp.inf)
    m1 = jnp.max(logits, axis=-1, keepdims=True)
    i1 = jnp.min(jnp.where(logits == m1, lane, LANES), axis=-1, keepdims=True)
    rest = jnp.where(lane == i1, -jnp.inf, logits)
    m2 = jnp.max(rest, axis=-1, keepdims=True)
    i2 = jnp.min(jnp.where(rest == m2, lane, LANES), axis=-1, keepdims=True)
    e2 = jnp.exp(m2 - m1)
    w1 = 1.0 / (1.0 + e2)
    w2 = e2 / (1.0 + e2)
    hit1 = lane == i1
    hit2 = lane == i2
    onehot = jnp.where(hit1 | hit2, 1.0, 0.0)
    seen = jnp.dot(before_ref[...], onehot.astype(BF16), preferred_element_type=F32) + carry_ref[0:1, :]
    r1 = jnp.sum(jnp.where(hit1, seen, 0.0), axis=-1, keepdims=True)
    r2 = jnp.sum(jnp.where(hit2, seen, 0.0), axis=-1, keepdims=True)
    carry_ref[...] = carry_ref[...] + jnp.sum(onehot, axis=0, keepdims=True)
    cols = (i1.astype(F32), i2.astype(F32), w1, w2, r1, r2)
    route = jnp.zeros((rows, LANES), F32)
    for n, c in enumerate(cols):
        route = jnp.where(lane == n, c, route)
    route_ref[...] = route
    counts_ref[...] = carry_ref[...]


def _router(h, g, w_router, *, bm=1024):
    m, d = h.shape
    bm = min(bm, m)
    w = _pad_cols(w_router, LANES)
    w_hi = w.astype(BF16)
    w_split = _cat([w_hi, (w - w_hi.astype(F32)).astype(BF16)], 1)
    return pl.pallas_call(
        _router_kernel,
        grid=(m // bm,),
        in_specs=[pl.BlockSpec((bm, d), lambda i: (i, 0)), pl.BlockSpec((1, d), lambda i: (0, 0)),
                  pl.BlockSpec((d, 2 * LANES), lambda i: (0, 0)), pl.BlockSpec((bm, bm), lambda i: (0, 0))],
        out_specs=[pl.BlockSpec((bm, LANES), lambda i: (i, 0)), pl.BlockSpec((SUBLANES, LANES), lambda i: (0, 0))],
        out_shape=[jax.ShapeDtypeStruct((m, LANES), F32), jax.ShapeDtypeStruct((SUBLANES, LANES), F32)],
        scratch_shapes=[pltpu.VMEM((SUBLANES, LANES), F32)],
        compiler_params=_cparams(("arbitrary",)),
        name="router",
    )(h, g.reshape(1, d), w_split, jnp.tril(jnp.ones((bm, bm), BF16), -1))


def _rms(x, n, g):
    return x * lax.rsqrt(jnp.sum(x * x, axis=-1, keepdims=True) * (1.0 / n) + NORM_EPS) * g


def _mla_prep_kernel(u_ref, cos_ref, sin_ref, win_ref, gq_ref, gkv_ref, gqh_ref, gkh_ref, wq_ref, wqr_ref, wk_ref,
                     wv_ref, q_ref, k_ref, v_ref):
    z = jnp.dot(u_ref[...], win_ref[...], preferred_element_type=F32)
    cq = _rms(z[:, :Q_LORA], Q_LORA, gq_ref[...]).astype(BF16)
    ckv = _rms(z[:, Q_LORA:Q_LORA + KV_LORA], KV_LORA, gkv_ref[...]).astype(BF16)
    k_pe = z[:, Q_LORA + KV_LORA:Q_LORA + KV_LORA + LANES]
    k_pe_rot = z[:, Q_LORA + KV_LORA + LANES:]
    q_all = jnp.dot(cq, wq_ref[...], preferred_element_type=F32)
    q_rot = jnp.dot(cq, wqr_ref[...], preferred_element_type=F32)
    k_all = jnp.dot(ckv, wk_ref[...], preferred_element_type=F32)
    v_all = jnp.dot(ckv, wv_ref[...], preferred_element_type=F32)
    sin = sin_ref[...]
    cos_q = cos_ref[...] * gqh_ref[...]
    cos_k = cos_ref[...] * gkh_ref[...]

    def scale(x):
        return lax.rsqrt(jnp.sum(x * x, axis=-1, keepdims=True) * (1.0 / QK_DIM) + NORM_EPS)

    ones_lane = lax.broadcasted_iota(jnp.int32, sin.shape, 1) == HEAD_DIM
    for h in range(N_HEADS):
        sl = slice(h * LANES, (h + 1) * LANES)
        q = q_all[:, sl]
        k = k_all[:, sl] + k_pe
        q_ref[h] = (scale(q) * (QK_DIM ** -0.5) * (q * cos_q + q_rot[:, sl] * sin)).astype(BF16)
        k_ref[h] = (scale(k) * (k * cos_k + k_pe_rot * sin)).astype(BF16)
        v_ref[h] = jnp.where(ones_lane, 1.0, v_all[:, sl]).astype(BF16)


def _pad_heads(w, width):
    k = w.shape[0]
    w = w.reshape(k, N_HEADS, width)
    return jnp.pad(w, ((0, 0), (0, 0), (0, LANES - width))).reshape(k, N_HEADS * LANES)


def _mla_prep(u, w_in, rope_tabs, b, s, q_norm_g, kv_norm_g, w_uq, w_ukv, qk_g_q, qk_g_k, *, bm=1024):
    bm = min(bm, s)
    half = QK_ROPE // 2
    swap = lambda w: _cat([w[..., half:], w[..., :half]], -1)
    rope_slab = lambda w: jnp.pad(w, ((0, 0), (QK_NOPE, LANES - QK_DIM)))
    lora = Q_LORA + KV_LORA
    w_in = _cat([w_in[:, :lora], rope_slab(w_in[:, lora:]), rope_slab(swap(w_in[:, lora:] * qk_g_k[QK_NOPE:]))],
                1).astype(BF16)
    w_ukv = w_ukv.reshape(KV_LORA, N_HEADS, QK_NOPE + HEAD_DIM)
    w_uq3 = w_uq.reshape(Q_LORA, N_HEADS, QK_DIM)
    wq = _pad_heads(w_uq, QK_DIM).astype(BF16)
    wq_rot = jnp.pad(swap(w_uq3[..., QK_NOPE:] * qk_g_q[QK_NOPE:]), ((0, 0), (0, 0), (QK_NOPE, LANES - QK_DIM)))
    wq_rot = wq_rot.reshape(Q_LORA, N_HEADS * LANES).astype(BF16)
    wk = _pad_heads(w_ukv[..., :QK_NOPE].reshape(KV_LORA, -1), QK_NOPE).astype(BF16)
    wv = _pad_heads(w_ukv[..., QK_NOPE:].reshape(KV_LORA, -1), HEAD_DIM).astype(BF16)
    pad_g = lambda g: jnp.pad(g, (0, LANES - g.shape[0])).reshape(1, LANES)
    row = lambda width: pl.BlockSpec((bm, width), lambda bb, i: (bb * (s // bm) + i, 0))
    full = lambda a: pl.BlockSpec(a.shape, lambda bb, i: (0, 0))
    consts = [w_in, q_norm_g.reshape(1, -1), kv_norm_g.reshape(1, -1), pad_g(qk_g_q), pad_g(qk_g_k), wq, wq_rot,
              wk, wv]
    out_spec = pl.BlockSpec((None, N_HEADS, bm, LANES), lambda bb, i: (bb, 0, i, 0))
    out_shape = jax.ShapeDtypeStruct((b, N_HEADS, s, LANES), BF16)
    return pl.pallas_call(
        _mla_prep_kernel,
        grid=(b, s // bm),
        in_specs=[row(u.shape[1]), row(LANES), row(LANES)] + [full(a) for a in consts],
        out_specs=[out_spec] * 3,
        out_shape=[out_shape] * 3,
        compiler_params=_cparams(("arbitrary", "arbitrary")),
        name="mla_prep",
    )(u, *rope_tabs, *consts)


def _flash_kernel(qi_ref, kj_ref, q_ref, k_ref, v_ref, o_ref, m_ref, acc_ref, *, blk, sub):
    n = pl.program_id(2)
    i = qi_ref[n]
    j = kj_ref[n]
    hb = q_ref.shape[0]

    @pl.when(j == 0)
    def _():
        m_ref[...] = jnp.full_like(m_ref, -jnp.inf)
        acc_ref[...] = jnp.zeros_like(acc_ref)

    def update(masked):
        chains = [(h, r) for h in range(hb) for r in range(blk // sub)]

        def qk(h, r):
            nk = (r + 1) * sub if masked else blk
            s = lax.dot_general(q_ref[h, r * sub:(r + 1) * sub, :], k_ref[h, :nk, :], (((1,), (1,)), ((), ())),
                                preferred_element_type=F32)
            if masked:
                row = lax.broadcasted_iota(jnp.int32, (sub, nk), 0) + r * sub
                s = jnp.where(row >= lax.broadcasted_iota(jnp.int32, (sub, nk), 1), s, -jnp.inf)
            return s

        def softmax_pv(h, r, s):
            rows = slice(r * sub, (r + 1) * sub)
            m_old = m_ref[h, rows, :]
            m_new = jnp.maximum(m_old, jnp.max(s, axis=-1, keepdims=True))
            alpha = jnp.exp(m_old - m_new)
            p = jnp.exp((s - m_new[:, :1]).astype(BF16))
            m_ref[h, rows, :] = m_new
            pv = jnp.dot(p, v_ref[h, :s.shape[1], :], preferred_element_type=F32)
            acc_ref[h, rows, :] = alpha * acc_ref[h, rows, :] + pv

        s_prev = qk(*chains[0])
        for c in range(len(chains)):
            s_next = qk(*chains[c + 1]) if c + 1 < len(chains) else None
            softmax_pv(*chains[c], s_prev)
            s_prev = s_next

    @pl.when(j < i)
    def _():
        update(False)

    @pl.when(j == i)
    def _():
        update(True)
        acc = acc_ref[...]
        o_ref[...] = (acc / acc[:, :, HEAD_DIM:HEAD_DIM + 1]).astype(o_ref.dtype)


def _flash_attention(q, k, v, *, blk=2048, sub=256, hb=4):
    b, h, s, d = q.shape
    blk = min(blk, s)
    sub = min(sub, blk)
    nb = s // blk
    qi = np.array([i for i in range(nb) for j in range(i + 1)], np.int32)
    kj = np.array([j for i in range(nb) for j in range(i + 1)], np.int32)
    q_spec = pl.BlockSpec((None, hb, blk, d), lambda bb, hh, n, qi_r, kj_r: (bb, hh, qi_r[n], 0))
    kv_spec = pl.BlockSpec((None, hb, blk, d), lambda bb, hh, n, qi_r, kj_r: (bb, hh, kj_r[n], 0))
    return pl.pallas_call(
        functools.partial(_flash_kernel, blk=blk, sub=sub),
        grid_spec=pltpu.PrefetchScalarGridSpec(
            num_scalar_prefetch=2,
            grid=(b, h // hb, len(qi)),
            in_specs=[q_spec, kv_spec, kv_spec],
            out_specs=q_spec,
            scratch_shapes=[pltpu.VMEM((hb, blk, LANES), F32), pltpu.VMEM((hb, blk, d), F32)],
        ),
        out_shape=jax.ShapeDtypeStruct((b, h, s, d), BF16),
        compiler_params=_cparams(("arbitrary", "arbitrary", "arbitrary")),
        name="flash",
    )(jnp.asarray(qi), jnp.asarray(kj), q, k, v)


def _dot(a, b):
    return jnp.dot(a.astype(BF16), b.astype(BF16), preferred_element_type=F32)


def _dot_nt(a, b):
    return lax.dot_general(a.astype(BF16), b.astype(BF16), (((1,), (1,)), ((), ())), preferred_element_type=F32)


def _dot_tn(a, b):
    return jnp.dot(a.T.astype(BF16), b.astype(BF16), preferred_element_type=F32)


def _split3(x):
    hi = x.astype(BF16)
    r1 = x - hi.astype(F32)
    mid = r1.astype(BF16)
    return hi, mid, (r1 - mid.astype(F32)).astype(BF16)


def _cumsum_rows(tri, x):
    s = jnp.dot(tri, jnp.concatenate(_split3(x), axis=1), preferred_element_type=F32)
    n = x.shape[1]
    return s[:, :n] + s[:, n:2 * n] + s[:, 2 * n:]


def _head_sum(x):
    left = lax.broadcasted_iota(jnp.int32, (x.shape[0], LANES), 1) < HEAD_DIM
    outs = []
    for p in range(WIDTH // LANES):
        xs = x[:, p * LANES:(p + 1) * LANES]
        s_left = jnp.sum(jnp.where(left, xs, 0.0), axis=-1, keepdims=True)
        s_right = jnp.sum(jnp.where(left, 0.0, xs), axis=-1, keepdims=True)
        outs.append(jnp.where(left, s_left, s_right))
    return jnp.concatenate(outs, axis=1)


def _shift_rows(x, tail, j):
    rolled = pltpu.roll(x, j, 0)
    top = jnp.where(lax.broadcasted_iota(jnp.int32, (SUBLANES, x.shape[1]), 0) < j,
                    pltpu.roll(tail, j, 0), rolled[:SUBLANES])
    return jnp.concatenate([top, rolled[SUBLANES:]], axis=0)


def _softplus(x):
    return jnp.maximum(x, 0.0) + jnp.log(1.0 + jnp.exp(-jnp.abs(x)))


def _pair_masks():
    lane = lax.broadcasted_iota(jnp.int32, (CHUNK, LANES), 1)
    m0 = (lane < HEAD_DIM).astype(F32)
    return m0, 1.0 - m0


def _ext(x, m0, m1, dtype=BF16):
    return jnp.concatenate([x * m0, x * m1], axis=0).astype(dtype)


def _cat(xs, axis):
    return jnp.concatenate(xs, axis=axis)


SCAN_GROUP = 2


def _neumann_inverse(mats):
    n = mats[0].shape[0]
    eye = (lax.broadcasted_iota(jnp.int32, (n, n), 0) == lax.broadcasted_iota(jnp.int32, (n, n), 1)).astype(F32)
    ts = [eye + a for a in mats]
    ps = [_dot(a, a) for a in mats]
    yield
    for _ in range(int(math.log2(CHUNK)) - 2):
        pps = [_dot(_cat([p, t], 0), p) for p, t in zip(ps, ts)]
        ps = [pp[:n] for pp in pps]
        ts = [t + pp[n:] for t, pp in zip(ts, pps)]
        yield
    return [t + _dot(t, p) for t, p in zip(ts, ps)]


def _interleave(prepare, serial, n_groups):
    for _ in prepare(0):
        pass
    for g in range(n_groups):
        gens = [serial(g)] + ([prepare(g + 1)] if g + 1 < n_groups else [])
        while gens:
            for gen in list(gens):
                if next(gen, StopIteration) is StopIteration:
                    gens.remove(gen)


def _work_items(rows):
    return [(slice(c * CHUNK, (c + 1) * CHUNK), slice(p * LANES, (p + 1) * LANES))
            for c in range(rows // CHUNK) for p in range(WIDTH // LANES)]


def _tiles(x, items):
    return [x[rows, cols] for rows, cols in items]


def _rwkv_scan(r, k, v, a, b, ld, st_ref, y_ref):
    c2 = 2 * CHUNK
    strict = lax.broadcasted_iota(jnp.int32, (c2, c2), 0) > lax.broadcasted_iota(jnp.int32, (c2, c2), 1)
    trow = lax.broadcasted_iota(jnp.int32, (CHUNK, c2), 0)
    tcol = lax.broadcasted_iota(jnp.int32, (CHUNK, c2), 1) & (CHUNK - 1)
    incl_w = tcol <= trow
    tri = (lax.broadcasted_iota(jnp.int32, (CHUNK, CHUNK), 0)
           >= lax.broadcasted_iota(jnp.int32, (CHUNK, CHUNK), 1)).astype(BF16)
    m0, m1 = _pair_masks()
    items = _work_items(r.shape[0])
    n_pairs = WIDTH // LANES
    per_group = min(SCAN_GROUP, len(items) // n_pairs) * n_pairs
    full = (r, k, v, a, b, ld)
    ready = {}
    state = [st_ref[p] for p in range(n_pairs)]

    def prepare(g):
        sub = items[g * per_group:(g + 1) * per_group]
        r, k, v, a, b, ld = (_tiles(x, sub) for x in full)
        cum = [_cumsum_rows(tri, x) for x in ld]
        yield
        e_pos = [jnp.exp(x) for x in cum]
        e_neg = [jnp.exp(-x) for x in cum]
        e_end = [jnp.exp(x[CHUNK - 1:CHUNK, :] - x) for x in cum]
        a_ext = [_ext(ai * jnp.exp(ci - li), m0, m1) for ai, ci, li in zip(a, cum, ld)]
        v_ext = [_ext(x, m0, m1) for x in v]
        r_dec = [(ri * ei).astype(BF16) for ri, ei in zip(r, e_pos)]
        yield
        s1 = [_dot_nt(_cat([ae, rd], 0), _cat([_ext(bi * en, m0, m1), _ext(ki * en, m0, m1)], 0))
              for ae, rd, bi, ki, en in zip(a_ext, r_dec, b, k, e_neg)]
        yield
        a_ab = [jnp.where(strict, s[:c2, :c2], 0.0) for s in s1]
        a_ak = [jnp.where(strict, s[:c2, c2:], 0.0) for s in s1]
        a_rbk = [_cat([jnp.where(incl_w, s[c2:, :c2], 0.0), jnp.where(incl_w, s[c2:, c2:], 0.0)], 1).astype(BF16)
                 for s in s1]
        akv = [_dot(x, ve) for x, ve in zip(a_ak, v_ext)]
        t_inv = yield from _neumann_inverse(a_ab)
        yield
        wu0 = [_dot(t, _cat([ae, av.astype(BF16)], 1)) for t, ae, av in zip(t_inv, a_ext, akv)]
        bk_end = [_cat([_ext(bi * ee, m0, m1, F32), _ext(ki * ee, m0, m1, F32)], 0)
                  for bi, ki, ee in zip(b, k, e_end)]
        e_last_t = [jnp.broadcast_to(e[CHUNK - 1:CHUNK, :], (LANES, LANES)).T for e in e_pos]
        ready[g] = (sub, wu0, r_dec, v_ext, a_rbk, bk_end, e_last_t)

    def serial(g):
        sub, wu0, r_dec, v_ext, a_rbk, bk_end, e_last_t = ready.pop(g)
        for c in range(len(sub) // n_pairs):
            idx = range(c * n_pairs, (c + 1) * n_pairs)
            z = [_dot(_cat([wu0[i][:, :LANES].astype(BF16), r_dec[i]], 0), s) for i, s in zip(idx, state)]
            yield
            uv = [_cat([(wu0[i][:, LANES:] + zi[:c2]).astype(BF16), v_ext[i]], 0) for i, zi in zip(idx, z)]
            for i, zi, uvi in zip(idx, z, uv):
                rows, cols = sub[i]
                y_ref[rows, cols] = zi[c2:] + _dot(a_rbk[i], uvi)
            state[:] = [s * e_last_t[i] + _dot_tn(bk_end[i], uvi) for i, s, uvi in zip(idx, state, uv)]
            yield

    _interleave(prepare, serial, len(items) // per_group)
    for p in range(n_pairs):
        st_ref[p] = state[p]


def _gdn_scan(q, k, v, g, beta, st_ref, o_ref):
    c2 = 2 * CHUNK
    row = lax.broadcasted_iota(jnp.int32, (c2, c2), 0)
    col = lax.broadcasted_iota(jnp.int32, (c2, c2), 1)
    causal = ((row >= CHUNK) == (col >= CHUNK)) & (row >= col)
    strict = row > col
    tri = (lax.broadcasted_iota(jnp.int32, (CHUNK, CHUNK), 0)
           >= lax.broadcasted_iota(jnp.int32, (CHUNK, CHUNK), 1)).astype(BF16)
    m0, m1 = _pair_masks()
    items = _work_items(q.shape[0])
    n_pairs = WIDTH // LANES
    per_group = min(SCAN_GROUP, len(items) // n_pairs) * n_pairs
    full = (q, k, v, g, beta)
    ready = {}
    state = [st_ref[p] for p in range(n_pairs)]

    def prepare(grp):
        sub = items[grp * per_group:(grp + 1) * per_group]
        q, k, v, g, beta = (_tiles(x, sub) for x in full)
        gc = [_cumsum_rows(tri, x) for x in g]
        yield
        g_rows = [_cat([jnp.broadcast_to(x[:, 0:1], (CHUNK, LANES)),
                        jnp.broadcast_to(x[:, HEAD_DIM:HEAD_DIM + 1], (CHUNK, LANES))], 0) for x in gc]
        gamma = [jnp.exp(jnp.where(causal, x - x.T, -jnp.inf)) for x in g_rows]
        e_pos = [jnp.exp(x) for x in gc]
        kb = [ki * bi for ki, bi in zip(k, beta)]
        yield
        s1 = [_dot_nt(_cat([_ext(kbi, m0, m1), qi.astype(BF16)], 0), _ext(ki, m0, m1))
              for kbi, qi, ki in zip(kb, q, k)]
        yield
        a_neg = [-jnp.where(strict, s[:c2] * ga, 0.0) for s, ga in zip(s1, gamma)]
        a_in = [(s[c2:] * (ga[:CHUNK] + ga[CHUNK:])).astype(BF16) for s, ga in zip(s1, gamma)]
        t_inv = yield from _neumann_inverse(a_neg)
        yield
        uw = [_dot(t, _cat([_ext(vi * bi, m0, m1), _ext(kbi * ei, m0, m1)], 1))
              for t, vi, bi, kbi, ei in zip(t_inv, v, beta, kb, e_pos)]
        q_dec = [(qi * ei).astype(BF16) for qi, ei in zip(q, e_pos)]
        k_dec = [_ext(ki * jnp.exp(x[CHUNK - 1:CHUNK, :] - x), m0, m1, F32) for ki, x in zip(k, gc)]
        ready[grp] = (sub, uw, q_dec, k_dec, a_in, e_pos)

    def serial(grp):
        sub, uw, q_dec, k_dec, a_in, e_pos = ready.pop(grp)
        for c in range(len(sub) // n_pairs):
            idx = range(c * n_pairs, (c + 1) * n_pairs)
            z = [_dot(_cat([uw[i][:, LANES:].astype(BF16), q_dec[i]], 0), s) for i, s in zip(idx, state)]
            yield
            v_new = [(uw[i][:, :LANES] - zi[:c2]).astype(BF16) for i, zi in zip(idx, z)]
            for i, zi, vn in zip(idx, z, v_new):
                rows, cols = sub[i]
                o_ref[rows, cols] = zi[c2:] + _dot(a_in[i], vn)
            state[:] = [s * e_pos[i][CHUNK - 1:CHUNK, :] + _dot_tn(k_dec[i], vn)
                        for i, s, vn in zip(idx, state, v_new)]
            yield

    _interleave(prepare, serial, len(items) // per_group)
    for p in range(n_pairs):
        st_ref[p] = state[p]


def _rwkv_kernel(*refs, vres):
    if vres:
        (u_ref, vf_ref, win_ref, mu_ref, prm_ref, wup_ref, aup_ref, gup_ref, vup_ref, o_ref, st_ref, tail_ref,
         y_ref) = refs
    else:
        u_ref, win_ref, mu_ref, prm_ref, wup_ref, aup_ref, gup_ref, o_ref, vf_out_ref, st_ref, tail_ref, y_ref = refs

    @pl.when(pl.program_id(1) == 0)
    def _():
        st_ref[...] = jnp.zeros_like(st_ref)
        tail_ref[...] = jnp.zeros_like(tail_ref)

    z = jnp.dot(u_ref[...], win_ref[...], preferred_element_type=F32)
    rows = z.shape[0]
    zs = _shift_rows(z, tail_ref[...], 1)
    tail_ref[...] = z[rows - SUBLANES:]
    zl = z[:, :RWKV_IN] + (zs[:, :RWKV_IN] - z[:, :RWKV_IN]) * mu_ref[...]
    r, k, v = zl[:, :WIDTH], zl[:, WIDTH:2 * WIDTH], zl[:, 2 * WIDTH:3 * WIDTH]
    lo = zl[:, 3 * WIDTH:3 * WIDTH + LANES]
    g_lo = zl[:, 3 * WIDTH + LANES:]
    w0, a0, k_k, k_a, ln_g, ln_b, r_k, v_bias = (prm_ref[i:i + 1, :] for i in range(8))
    log_w = -_softplus(-(w0 + _dot(jnp.tanh(lo), wup_ref[...]))) - 0.5
    ld = -jnp.exp(log_w)
    iclr = jax.nn.sigmoid(a0 + _dot(lo, aup_ref[...]))
    gate = _dot(jax.nn.sigmoid(g_lo), gup_ref[...])
    if vres:
        x = z[:, RWKV_IN:] + pltpu.roll(zs[:, RWKV_IN:], LANES - VRES_LORA, 1)
        v = v + (vf_ref[...] - v) * jax.nn.sigmoid(v_bias + _dot(x, vup_ref[...]))
    else:
        vf_out_ref[...] = v
    kk = k * k_k
    kk = kk * lax.rsqrt(_head_sum(kk * kk) + L2_EPS)
    k = k * (1.0 + (iclr - 1.0) * k_a)
    _rwkv_scan(r, k, v, -kk, kk * iclr, ld, st_ref, y_ref)
    y = y_ref[...]
    d = y - _head_sum(y) * (1.0 / HEAD_DIM)
    y = d * lax.rsqrt(_head_sum(d * d) * (1.0 / HEAD_DIM) + RWKV_GN_EPS) * ln_g + ln_b
    y = y + _head_sum(r * k * r_k) * v
    o_ref[...] = (y * gate).astype(o_ref.dtype)


def _rwkv(u, w_in, v_first, b, s, mu, prm, w_up, a_up, g_up, v_up):
    vres = v_first is not None
    rows = min(SCAN_ROWS, s)
    zw = w_in.shape[1]
    row = lambda width: pl.BlockSpec((rows, width), lambda bb, c: (bb * (s // rows) + c, 0))
    full = lambda a: pl.BlockSpec(a.shape, lambda bb, c: (0, 0))
    zero = jnp.zeros((LANES // 2, WIDTH), F32)
    consts = [w_in.astype(BF16), mu.reshape(1, -1), prm, _cat([w_up, zero], 0).astype(BF16),
              _cat([zero, a_up], 0).astype(BF16), g_up.astype(BF16)]
    args, in_specs = [u], [row(u.shape[1])]
    if vres:
        args.append(v_first)
        in_specs.append(row(WIDTH))
        consts.append(jnp.pad(v_up, ((0, LANES - VRES_LORA), (0, 0))).astype(BF16))
    out_shape = [jax.ShapeDtypeStruct((b * s, WIDTH), BF16)]
    out_specs = [row(WIDTH)]
    if not vres:
        out_shape.append(jax.ShapeDtypeStruct((b * s, WIDTH), F32))
        out_specs.append(row(WIDTH))
    outs = pl.pallas_call(
        functools.partial(_rwkv_kernel, vres=vres),
        grid=(b, s // rows),
        in_specs=in_specs + [full(a) for a in consts],
        out_specs=out_specs,
        out_shape=out_shape,
        scratch_shapes=[pltpu.VMEM((WIDTH // LANES, LANES, LANES), F32), pltpu.VMEM((SUBLANES, zw), F32),
                        pltpu.VMEM((rows, WIDTH), F32)],
        compiler_params=_cparams(("arbitrary", "arbitrary")),
        name="rwkv",
    )(*args, *consts)
    return (outs[0], v_first) if vres else (outs[0], outs[1])


def _gdn_kernel(u_ref, win_ref, cw_ref, prm_ref, exp_ref, o_ref, st_ref, tail_ref, y_ref):
    @pl.when(pl.program_id(1) == 0)
    def _():
        st_ref[...] = jnp.zeros_like(st_ref)
        tail_ref[...] = jnp.zeros_like(tail_ref)

    z = jnp.dot(u_ref[...], win_ref[...], preferred_element_type=F32)
    rows = z.shape[0]
    x = z[:, :GDN_QKV]
    tail = tail_ref[...]
    conv = x * cw_ref[GDN_CONV - 1:GDN_CONV, :]
    for j in range(1, GDN_CONV):
        conv = conv + _shift_rows(x, tail, j) * cw_ref[GDN_CONV - 1 - j:GDN_CONV - j, :]
    tail_ref[...] = x[rows - SUBLANES:]
    qkv = conv * jax.nn.sigmoid(conv)
    q, k, v = qkv[:, :WIDTH], qkv[:, WIDTH:2 * WIDTH], qkv[:, 2 * WIDTH:]
    q = q * lax.rsqrt(_head_sum(q * q) + L2_EPS) * (HEAD_DIM ** -0.5)
    k = k * lax.rsqrt(_head_sum(k * k) + L2_EPS)
    logits = jnp.dot(_cat(_split3(z[:, GDN_QKV + WIDTH:]), 0), exp_ref[...], preferred_element_type=F32)
    logits = logits[:rows] + logits[rows:2 * rows] + logits[2 * rows:]
    neg_a, dt_bias, norm_g = (prm_ref[i:i + 1, :] for i in range(3))
    beta = jax.nn.sigmoid(logits[:, :WIDTH])
    g = neg_a * _softplus(logits[:, WIDTH:] + dt_bias)
    _gdn_scan(q, k, v, g, beta, st_ref, y_ref)
    o = y_ref[...]
    o = o * lax.rsqrt(_head_sum(o * o) * (1.0 / HEAD_DIM) + NORM_EPS) * norm_g
    gate = z[:, GDN_QKV:GDN_QKV + WIDTH]
    o_ref[...] = (o * (gate * jax.nn.sigmoid(gate))).astype(o_ref.dtype)


def _gdn(u, w_in, b, s, conv_w, a_log, dt_bias, norm_g):
    rows = min(SCAN_ROWS, s)
    row = lambda width: pl.BlockSpec((rows, width), lambda bb, c: (bb * (s // rows) + c, 0))
    full = lambda a: pl.BlockSpec(a.shape, lambda bb, c: (0, 0))
    per_lane = lambda t: jnp.repeat(t, HEAD_DIM)
    prm = jnp.stack([per_lane(-jnp.exp(a_log)), per_lane(dt_bias), jnp.tile(norm_g, N_HEADS)])
    head_of_lane = np.arange(WIDTH) // HEAD_DIM
    expand = np.zeros((LANES, 2 * WIDTH), np.float32)
    expand[head_of_lane, np.arange(WIDTH)] = 1.0
    expand[N_HEADS + head_of_lane, WIDTH + np.arange(WIDTH)] = 1.0
    consts = [w_in.astype(BF16), conv_w, prm, jnp.asarray(expand, BF16)]
    return pl.pallas_call(
        _gdn_kernel,
        grid=(b, s // rows),
        in_specs=[row(u.shape[1])] + [full(a) for a in consts],
        out_specs=row(WIDTH),
        out_shape=jax.ShapeDtypeStruct((b * s, WIDTH), BF16),
        scratch_shapes=[pltpu.VMEM((WIDTH // LANES, LANES, LANES), F32), pltpu.VMEM((SUBLANES, GDN_QKV), F32),
                        pltpu.VMEM((rows, WIDTH), F32)],
        compiler_params=_cparams(("arbitrary", "arbitrary")),
        name="gdn",
    )(u, *consts)


def _merge_kernel(h_ref, u_ref, oa_ref, ob_ref, oc_ref, wg_ref, wa_ref, wb_ref, wc_ref, wo_ref, out_ref):
    d = h_ref.shape[1]
    u = u_ref[...]
    ob = jnp.concatenate([ob_ref[h] for h in range(N_HEADS)], axis=1)
    merged = jnp.zeros(h_ref.shape, F32)
    for n, (o, w_ref) in enumerate(((oa_ref[...], wa_ref), (ob, wb_ref), (oc_ref[...], wc_ref))):
        proj = jnp.dot(o, w_ref[...], preferred_element_type=F32)
        zg = jnp.dot(u, wg_ref[:, n * d:(n + 1) * d], preferred_element_type=F32)
        merged = merged + jax.nn.sigmoid(zg) * proj
    out_ref[...] = h_ref[...] + jnp.dot(merged.astype(BF16), wo_ref[...], preferred_element_type=F32)


def _merge(h, u, oa, ob, oc, wg, wa, wb, wc, wo, s, *, bm=1024):
    m, d = h.shape
    bm = min(bm, s)
    row = lambda width: pl.BlockSpec((bm, width), lambda i: (i, 0))
    full = lambda w: pl.BlockSpec(w.shape, lambda i: (0, 0), pipeline_mode=pl.Buffered(1))
    wb = jnp.pad(wb.reshape(N_HEADS, HEAD_DIM, d), ((0, 0), (0, LANES - HEAD_DIM), (0, 0))).reshape(N_HEADS * LANES, d)
    ws = [w.astype(BF16) for w in (wg, wa, wb, wc, wo)]
    ob_spec = pl.BlockSpec((None, N_HEADS, bm, LANES), lambda i: (i // (s // bm), 0, i % (s // bm), 0))
    return pl.pallas_call(
        _merge_kernel,
        grid=(m // bm,),
        in_specs=[row(d), row(d), row(WIDTH), ob_spec, row(WIDTH)] + [full(w) for w in ws],
        out_specs=row(d),
        out_shape=jax.ShapeDtypeStruct((m, d), F32),
        compiler_params=_cparams(("arbitrary",)),
        name="merge",
    )(h, u, oa, ob, oc, *ws)


def _ffn_kernel(be_ref, nu_ref, x_ref, g_ref, wg_ref, wu_ref, wd_ref, o_ref, xs_ref, acc_ref, *, nf, dense):
    i = pl.program_id(0)
    f = pl.program_id(1)
    used = i < nu_ref[0]

    @pl.when(used & (f == 0))
    def _():
        x = x_ref[...]
        ms = jnp.mean(x * x, axis=-1, keepdims=True)
        xs_ref[...] = (x * lax.rsqrt(ms + NORM_EPS) * g_ref[...]).astype(BF16)

    @pl.when(f == 0)
    def _():
        acc_ref[...] = jnp.zeros_like(acc_ref)

    @pl.when(used)
    def _():
        xs = xs_ref[...]
        gate = jnp.dot(xs, wg_ref[...].astype(BF16), preferred_element_type=F32)
        up = jnp.dot(xs, wu_ref[...].astype(BF16), preferred_element_type=F32)
        act = (gate * jax.nn.sigmoid(gate) * up).astype(BF16)
        acc_ref[...] += jnp.dot(act, wd_ref[...].astype(BF16), preferred_element_type=F32)

    @pl.when(f == nf - 1)
    def _():
        o_ref[...] = (x_ref[...] + acc_ref[...]) if dense else acc_ref[...]


def _ffn(x, g, wg, wu, wd, block_e, n_used, *, bm, tf, dense):
    r, d = x.shape
    ff = wg.shape[2]
    nf = ff // tf

    def ff_idx(i, f, be, nu):
        return jnp.where(i < nu[0], f, nf - 1)

    return pl.pallas_call(
        functools.partial(_ffn_kernel, nf=nf, dense=dense),
        grid_spec=pltpu.PrefetchScalarGridSpec(
            num_scalar_prefetch=2,
            grid=(r // bm, nf),
            in_specs=[
                pl.BlockSpec((bm, d), lambda i, f, be, nu: (i, 0)),
                pl.BlockSpec((1, d), lambda i, f, be, nu: (0, 0)),
                pl.BlockSpec((None, d, tf), lambda i, f, be, nu: (be[i], 0, ff_idx(i, f, be, nu))),
                pl.BlockSpec((None, d, tf), lambda i, f, be, nu: (be[i], 0, ff_idx(i, f, be, nu))),
                pl.BlockSpec((None, tf, d), lambda i, f, be, nu: (be[i], ff_idx(i, f, be, nu), 0)),
            ],
            out_specs=pl.BlockSpec((bm, d), lambda i, f, be, nu: (i, 0)),
            scratch_shapes=[pltpu.VMEM((bm, d), BF16), pltpu.VMEM((bm, d), F32)],
        ),
        out_shape=jax.ShapeDtypeStruct((r, d), F32),
        compiler_params=_cparams(("arbitrary", "arbitrary")),
        name="ffn",
    )(block_e, n_used, x, g.reshape(1, d).astype(F32), wg, wu, wd)


def _ple_kernel(*refs, combine, emit_norm):
    refs = list(refs)
    h = refs.pop(0)[...]
    if combine:
        y0_ref, y1_ref, route_ref = refs[:3]
        del refs[:3]
        h = h + route_ref[:, 2:3] * y0_ref[...] + route_ref[:, 3:4] * y1_ref[...]
    p_ref, wg_ref, wp_ref, g_ref = refs[:4]
    gate = jnp.dot(h.astype(BF16), wg_ref[...], preferred_element_type=F32)
    e = jnp.dot(p_ref[...].astype(BF16), wp_ref[...], preferred_element_type=F32)
    ms = jnp.mean(e * e, axis=-1, keepdims=True)
    e = e * lax.rsqrt(ms + NORM_EPS) * g_ref[...]
    out = h + jax.nn.sigmoid(gate) * e
    if emit_norm:
        gn_ref, o_ref, u_ref = refs[4:]
        u_ref[...] = (out * lax.rsqrt(jnp.mean(out * out, axis=-1, keepdims=True) + NORM_EPS) * gn_ref[...]).astype(BF16)
    else:
        o_ref, = refs[4:]
    o_ref[...] = out


def _ple(h, p, w_gate, w_proj, g, expert_out=None, g_next=None, *, bm=512):
    m, d = h.shape
    bm = min(bm, m)
    row = lambda a: pl.BlockSpec((bm, a.shape[1]), lambda i: (i, 0))
    full = lambda a: pl.BlockSpec(a.shape, lambda i: (0, 0))
    rows = [h] + list(expert_out or ()) + [p]
    consts = [w_gate.astype(BF16), w_proj.astype(BF16), g.reshape(1, d)]
    out_shape = [jax.ShapeDtypeStruct((m, d), F32)]
    if g_next is not None:
        consts.append(g_next.reshape(1, d))
        out_shape.append(jax.ShapeDtypeStruct((m, d), BF16))
    outs = pl.pallas_call(
        functools.partial(_ple_kernel, combine=expert_out is not None, emit_norm=g_next is not None),
        grid=(m // bm,),
        in_specs=[row(a) for a in rows] + [full(a) for a in consts],
        out_specs=[row(h)] * len(out_shape),
        out_shape=out_shape,
        compiler_params=_cparams(("arbitrary",)),
        name="ple",
    )(*rows, *consts)
    return outs if g_next is not None else (outs[0], None)


def _pad_cols(w, n):
    return jnp.pad(w, ((0, 0), (0, n - w.shape[1])))


def _rope_tables(positions):
    t = positions.size
    inv_freq = 1.0 / (ROPE_THETA ** (jnp.arange(0, QK_ROPE, 2, dtype=F32) / QK_ROPE))
    ang = positions.astype(F32).reshape(t, 1) * inv_freq
    cos, sin = jnp.cos(ang), jnp.sin(ang)
    pad = ((0, 0), (QK_NOPE, LANES - QK_DIM))
    return jnp.pad(_cat([cos, cos], 1), pad, constant_values=1.0), jnp.pad(_cat([-sin, sin], 1), pad)


def _moe(h, g, w_router, wg, wu, wd, *, bm=1024, tf=512):
    t, d = h.shape
    n_assign = t * TOP_K
    n_blocks = -(-(n_assign + N_EXPERTS * (bm - 1)) // bm)
    route, counts = _router(h, g, w_router)
    counts = counts[0, :N_EXPERTS].astype(jnp.int32)
    padded = (counts + bm - 1) // bm * bm
    pad_end = jnp.cumsum(padded)
    top_e = route[:, :TOP_K].astype(jnp.int32)
    dest = (pad_end - padded)[top_e] + route[:, 4:4 + TOP_K].astype(jnp.int32)
    src_tok = (jnp.arange(n_blocks * bm, dtype=jnp.int32) % t).at[dest.reshape(-1)].set(
        jnp.arange(n_assign, dtype=jnp.int32) // TOP_K)
    block_start = jnp.arange(n_blocks, dtype=jnp.int32) * bm
    block_e = jnp.minimum(jnp.sum(pad_end[None, :] <= block_start[:, None], axis=1), N_EXPERTS - 1).astype(jnp.int32)
    n_used = (pad_end[-1] // bm).astype(jnp.int32).reshape(1)
    y_rows = _ffn(h[src_tok], g, wg, wu, wd, block_e, n_used, bm=bm, tf=tf, dense=False)
    return y_rows[dest[:, 0]], y_rows[dest[:, 1]], route


def kernel(x, p, positions, norm_mix_g, w_in, rwkv_mu, rwkv_w0, rwkv_w_up, rwkv_a0, rwkv_a_up, rwkv_g_up, rwkv_k_k, rwkv_k_a, rwkv_r_k, rwkv_ln_g, rwkv_ln_b, vres_mu, vres_down, vres_up, vres_b, mla_q_norm_g, mla_kv_norm_g, mla_w_uq, mla_w_ukv, mla_qk_norm_q, mla_qk_norm_k, gdn_conv_w, gdn_a_log, gdn_dt_bias, gdn_norm_g, w_br_rwkv, w_br_mla, w_br_gdn, w_out, norm_ffn_g, ffn_wg, ffn_wu, ffn_wd, moe_router, moe_wg, moe_wu, moe_wd, ple_proj, ple_gate, ple_norm_g):
    b, s, d = x.shape
    t = b * s
    depth = w_in.shape[0]
    rope_tabs = _rope_tables(positions)

    h = x.reshape(t, d)
    v_first = None
    for i in range(depth):
        w = w_in[i]
        g = norm_mix_g[i]
        w_rwkv = w[:, :RWKV_IN]
        if i > 0:
            mu_v = vres_mu[i - 1][:, None]
            vd = vres_down[i - 1]
            w_rwkv = _pad_cols(_cat([w_rwkv, (1.0 - mu_v) * vd, mu_v * vd], 1), RWKV_IN + LANES)
        w_gdn = w[:, RWKV_IN + MLA_IN:RWKV_IN + MLA_IN + GDN_IN]
        w_gdn = _pad_cols(_cat([w_gdn[:, :GDN_QKV], w_gdn[:, GDN_QKV + 2 * N_HEADS:],
                                w_gdn[:, GDN_QKV:GDN_QKV + 2 * N_HEADS]], 1), GDN_QKV + WIDTH + LANES)
        if i == 0:
            u = _norm(h, g)
        prm = jnp.stack([rwkv_w0[i], rwkv_a0[i], rwkv_k_k[i], rwkv_k_a[i], rwkv_ln_g[i], rwkv_ln_b[i],
                         rwkv_r_k[i].reshape(-1), vres_b[i - 1] if i > 0 else jnp.zeros((WIDTH,), F32)])
        o_a, v_first = _rwkv(u, w_rwkv, v_first, b, s, rwkv_mu[i], prm, rwkv_w_up[i], rwkv_a_up[i], rwkv_g_up[i],
                             vres_up[i - 1] if i > 0 else None)
        q, k, v = _mla_prep(u, w[:, RWKV_IN:RWKV_IN + MLA_IN], rope_tabs, b, s, mla_q_norm_g[i], mla_kv_norm_g[i],
                            mla_w_uq[i], mla_w_ukv[i], mla_qk_norm_q[i], mla_qk_norm_k[i])
        o_b = _flash_attention(q, k, v)
        o_c = _gdn(u, w_gdn, b, s, gdn_conv_w[i], gdn_a_log[i], gdn_dt_bias[i], gdn_norm_g[i])
        h = _merge(h, u, o_a, o_b, o_c, w[:, RWKV_IN + MLA_IN + GDN_IN:], w_br_rwkv[i], w_br_mla[i], w_br_gdn[i],
                   w_out[i], s)
        j = i // 2
        if i % 2 == 0:
            bm = min(1024, t)
            h = _ffn(h, norm_ffn_g[i], ffn_wg[j:j + 1], ffn_wu[j:j + 1], ffn_wd[j:j + 1],
                     jnp.zeros((t // bm,), jnp.int32), jnp.full((1,), t // bm, jnp.int32), bm=bm, tf=256, dense=True)
            expert_out = None
        else:
            expert_out = _moe(h, norm_ffn_g[i], moe_router[j], moe_wg[j], moe_wu[j], moe_wd[j])
        h, u = _ple(h, p[i].reshape(t, -1), ple_gate[i], ple_proj[i], ple_norm_g[i], expert_out,
                    norm_mix_g[i + 1] if i + 1 < depth else None)
    return h.reshape(b, s, d)
```
